```python
import math
import jax, jax.numpy as jnp
from jax import lax
import numpy as np

D_MODEL = 1024
BATCH = 2
SEQ = 8192
DEPTH = 2

N_EVEN = (DEPTH + 1) // 2
N_ODD = DEPTH // 2

LRU_WIDTH = D_MODEL // 2
LRU_BLOCKS = 8
LRU_BLOCK_DIM = LRU_WIDTH // LRU_BLOCKS
LRU_CONV = 4
LRU_C = 8.0

ATT_HEADS = 8
ATT_HEAD_DIM = (D_MODEL // 2) // ATT_HEADS
ATT_WIDTH = ATT_HEADS * ATT_HEAD_DIM
DILATED_PATTERNS = ((128, 1), (512, 4), (2048, 16))
ATT_SPAN = max(w for w, _ in DILATED_PATTERNS)
ROPE_THETA = 500000.0
ROPE_DIM = ATT_HEAD_DIM // 4

S5_WIDTH = D_MODEL // 2
S5_GROUP = 16
S5_GROUPS = S5_WIDTH // S5_GROUP
S5_STATE = 64

GLA_HEADS = 4
GLA_DK = (D_MODEL // 4) // GLA_HEADS
GLA_DV = (D_MODEL // 2) // GLA_HEADS
GLA_LOWRANK = 16
GLA_TAU = 16.0
GLA_CHUNK = 64

D_FF = 3 * D_MODEL
FFN_CONV = 3

EPS = 1e-6
NEG_INF = -1e30

EVEN_COLS = (LRU_WIDTH, LRU_WIDTH, ATT_WIDTH, ATT_WIDTH, ATT_WIDTH)
ODD_COLS = (S5_WIDTH, GLA_HEADS * GLA_DK, GLA_HEADS * GLA_DK,
            GLA_HEADS * GLA_DV, GLA_HEADS * GLA_DV, GLA_LOWRANK)

kernel_name = "hybrid_lru_dilated_s5_gla_block"


def split_cols(t, widths):
    idx = [int(i) for i in np.cumsum(widths)[:-1]]
    return jnp.split(t, idx, axis=-1)


def rmsnorm(x, g):
    xf = x.astype(jnp.float32)
    y = xf * lax.rsqrt(jnp.mean(xf * xf, axis=-1, keepdims=True) + EPS)
    return y.astype(x.dtype) * g


def causal_dwconv(x, w, b):
    K = w.shape[0]
    S = x.shape[1]
    xp = jnp.pad(x, ((0, 0), (K - 1, 0), (0, 0)))
    return sum(xp[:, j:j + S] * w[j] for j in range(K)) + b


def rope_tables(S):
    pos = jnp.arange(S, dtype=jnp.float32)
    inv = ROPE_THETA ** (-jnp.arange(0, ROPE_DIM, 2, dtype=jnp.float32) / ROPE_DIM)
    ang = pos[:, None] * inv[None, :]
    return jnp.cos(ang), jnp.sin(ang)


def apply_partial_rope(x, cos, sin):
    half = ROPE_DIM // 2
    c = cos[None, :, None, :].astype(x.dtype)
    s = sin[None, :, None, :].astype(x.dtype)
    x1, x2, rest = x[..., :half], x[..., half:ROPE_DIM], x[..., ROPE_DIM:]
    return jnp.concatenate([x1 * c - x2 * s, x2 * c + x1 * s, rest], axis=-1)


def _linear_combine(left, right):
    a1, b1 = left
    a2, b2 = right
    return a1 * a2, a2 * b1 + b2


def _complex_combine(left, right):
    ar1, ai1, br1, bi1 = left
    ar2, ai2, br2, bi2 = right
    return (ar1 * ar2 - ai1 * ai2, ar1 * ai2 + ai1 * ar2,
            ar2 * br1 - ai2 * bi1 + br2, ar2 * bi1 + ai2 * br1 + bi2)


def rg_lru(x, ga_w, ga_b, gx_w, gx_b, lam):
    B, S, W = x.shape
    xb = x.reshape(B, S, LRU_BLOCKS, LRU_BLOCK_DIM)
    r = jax.nn.sigmoid(jnp.einsum('bshi,hij->bshj', xb, ga_w).reshape(B, S, W) + ga_b)
    i = jax.nn.sigmoid(jnp.einsum('bshi,hij->bshj', xb, gx_w).reshape(B, S, W) + gx_b)
    log_a = (LRU_C * r.astype(jnp.float32)) * jax.nn.log_sigmoid(lam.astype(jnp.float32))
    a = jnp.exp(log_a)
    b = jnp.sqrt(-jnp.expm1(2.0 * log_a)) * (i * x).astype(jnp.float32)
    _, h = lax.associative_scan(_linear_combine, (a, b), axis=1)
    return h.astype(x.dtype)


def _dilated_pattern(q, k, v, dil, n_back):
    B, Sp, H, hd = q.shape
    L = Sp // dil
    nb = L // n_back
    n = n_back

    def to_sub(t):
        return t.reshape(B, L, dil, H, hd).transpose(0, 2, 1, 3, 4).reshape(B, dil, nb, n, H, hd)

    qs, ks, vs = to_sub(q), to_sub(k), to_sub(v)
    pad = ((0, 0), (0, 0), (1, 0), (0, 0), (0, 0), (0, 0))
    kb = jnp.concatenate([jnp.pad(ks, pad)[:, :, :-1], ks], axis=3)
    vb = jnp.concatenate([jnp.pad(vs, pad)[:, :, :-1], vs], axis=3)
    scores = jnp.einsum('brnqhd,brnkhd->brnhqk', qs, kb).astype(jnp.float32) * (hd ** -0.5)
    qi = jnp.arange(n)[:, None]
    ki = jnp.arange(2 * n)[None, :]
    dist = n + qi - ki
    band = (dist >= 0) & (dist <= n_back)
    valid = (jnp.arange(nb)[:, None, None] > 0) | (ki[None] >= n)
    mask = (band[None] & valid)[None, None, :, None, :, :]
    scores = jnp.where(mask, scores, NEG_INF)
    lse = jax.nn.logsumexp(scores, axis=-1)
    p = jnp.exp(scores - lse[..., None])
    out = jnp.einsum('brnhqk,brnkhd->brnqhd', p.astype(vb.dtype), vb)
    out = out.reshape(B, dil, L, H, hd).transpose(0, 2, 1, 3, 4).reshape(B, Sp, H, hd)
    lse = lse.transpose(0, 1, 2, 4, 3).reshape(B, dil, L, H).transpose(0, 2, 1, 3).reshape(B, Sp, H)
    return out, lse


def dilated_attention(q, k, v):
    B, S, H, hd = q.shape
    pad = (-S) % ATT_SPAN
    padw = ((0, 0), (0, pad), (0, 0), (0, 0))
    qp, kp, vp = jnp.pad(q, padw), jnp.pad(k, padw), jnp.pad(v, padw)
    outs, lses = [], []
    for window, dil in DILATED_PATTERNS:
        o, l = _dilated_pattern(qp, kp, vp, dil, window // dil)
        outs.append(o.astype(jnp.float32))
        lses.append(l)
    w = jax.nn.softmax(jnp.stack(lses, axis=0), axis=0)
    o = jnp.sum(w[..., None] * jnp.stack(outs, axis=0), axis=0)
    return o[:, :S].astype(q.dtype)


def s5(u, lam_re, lam_im, b_re, b_im, c_re, c_im, d, log_step):
    f32 = jnp.float32
    B, S, W = u.shape
    ug = u.astype(f32).reshape(B, S, S5_GROUPS, S5_GROUP)
    step = jnp.exp(log_step.astype(f32))[:, None]
    lr, li = lam_re.astype(f32), lam_im.astype(f32)
    mag = jnp.exp(lr * step)
    ar, ai = mag * jnp.cos(li * step), mag * jnp.sin(li * step)
    den = lr * lr + li * li
    cr = ((ar - 1.0) * lr + ai * li) / den
    ci = (ai * lr - (ar - 1.0) * li) / den
    bbr = cr[..., None] * b_re - ci[..., None] * b_im
    bbi = cr[..., None] * b_im + ci[..., None] * b_re
    bur = jnp.einsum('bsgp,gnp->bsgn', ug, bbr)
    bui = jnp.einsum('bsgp,gnp->bsgn', ug, bbi)
    shp = bur.shape
    _, _, xr, xi = lax.associative_scan(
        _complex_combine,
        (jnp.broadcast_to(ar, shp), jnp.broadcast_to(ai, shp), bur, bui), axis=1)
    y = jnp.einsum('bsgn,gpn->bsgp', xr, c_re) - jnp.einsum('bsgn,gpn->bsgp', xi, c_im)
    return (y.reshape(B, S, W) + d * u.astype(f32)).astype(u.dtype)


def gla(q, k, v, gk_lr, gk_w, gk_b, g, norm):
    f32 = jnp.float32
    B, S, H, _ = q.shape
    C = GLA_CHUNK
    nc = S // C
    log_a = jax.nn.log_sigmoid((gk_lr @ gk_w + gk_b).astype(f32)) / GLA_TAU

    def chunks(t, dim):
        return t.astype(f32).reshape(B, nc, C, H, dim).transpose(0, 3, 1, 2, 4)

    qc = chunks(q, GLA_DK) * (GLA_DK ** -0.5)
    kc = chunks(k, GLA_DK)
    vc = chunks(v, GLA_DV)
    bc = jnp.cumsum(chunks(log_a, GLA_DK), axis=3)
    b_last = bc[:, :, :, -1:, :]
    q_dec = qc * jnp.exp(bc)
    k_inv = kc * jnp.exp(-bc)
    causal = jnp.tril(jnp.ones((C, C), dtype=bool))
    att = jnp.where(causal, jnp.einsum('bhncd,bhnjd->bhncj', q_dec, k_inv), 0.0)
    o_intra = jnp.einsum('bhncj,bhnje->bhnce', att, vc)
    kv = jnp.einsum('bhncd,bhnce->nbhde', kc * jnp.exp(b_last - bc), vc)
    decay = jnp.exp(b_last[:, :, :, 0, :]).transpose(2, 0, 1, 3)

    def step(state, inp):
        dec, kv_c = inp
        return dec[..., None] * state + kv_c, state

    _, s_prev = lax.scan(step, jnp.zeros((B, H, GLA_DK, GLA_DV), f32), (decay, kv))
    o_inter = jnp.einsum('bhncd,nbhde->bhnce', q_dec, s_prev)
    o = (o_intra + o_inter).transpose(0, 2, 3, 1, 4).reshape(B, S, H, GLA_DV)
    o = rmsnorm(o, norm).reshape(B, S, H * GLA_DV).astype(g.dtype)
    return o * jax.nn.silu(g)


def even_layer(x, norm, w_in, conv_w, conv_b, ga_w, ga_b, gx_w, gx_b, lam, q_norm, k_norm, w_out, cos, sin):
    B, S, _ = x.shape
    h = rmsnorm(x, norm)
    xl, gl, q, k, v = split_cols(h @ w_in, EVEN_COLS)
    xl = causal_dwconv(xl, conv_w, conv_b)
    y_lru = rg_lru(xl, ga_w, ga_b, gx_w, gx_b, lam) * jax.nn.gelu(gl)
    q = apply_partial_rope(rmsnorm(q.reshape(B, S, ATT_HEADS, ATT_HEAD_DIM), q_norm), cos, sin)
    k = apply_partial_rope(rmsnorm(k.reshape(B, S, ATT_HEADS, ATT_HEAD_DIM), k_norm), cos, sin)
    v = v.reshape(B, S, ATT_HEADS, ATT_HEAD_DIM)
    y_att = dilated_attention(q, k, v).reshape(B, S, ATT_WIDTH)
    return x + jnp.concatenate([y_lru, y_att], axis=-1) @ w_out


def odd_layer(x, norm, w_in, lam_re, lam_im, b_re, b_im, c_re, c_im, d, log_step,
              glu_w, glu_b, gk_w, gk_b, gla_norm, w_out):
    B, S, _ = x.shape
    h = rmsnorm(x, norm)
    u, q, k, v, g, gk = split_cols(h @ w_in, ODD_COLS)
    ys = jax.nn.gelu(s5(u, lam_re, lam_im, b_re, b_im, c_re, c_im, d, log_step))
    y_s5 = ys * jax.nn.sigmoid(ys @ glu_w + glu_b)
    y_gla = gla(q.reshape(B, S, GLA_HEADS, GLA_DK), k.reshape(B, S, GLA_HEADS, GLA_DK),
                v.reshape(B, S, GLA_HEADS, GLA_DV), gk, gk_w, gk_b, g, gla_norm)
    return x + jnp.concatenate([y_s5, y_gla], axis=-1) @ w_out


def conv_ffn(x, norm, w_in, conv_w, conv_b, w_out):
    h = rmsnorm(x, norm)
    a, lin = jnp.split(h @ w_in, 2, axis=-1)
    return x + (jax.nn.gelu(causal_dwconv(a, conv_w, conv_b)) * lin) @ w_out


def setup_inputs(seed: int = 0) -> dict:
    key = jax.random.key(seed)
    keys = iter(jax.random.split(key, 64))
    f32 = jnp.float32

    def nrm(shape, scale):
        return jax.random.normal(next(keys), shape, f32) * scale

    def gain(shape):
        return 1.0 + nrm(shape, 0.02)

    NE, NO = N_EVEN, N_ODD
    inp = {}
    inp['x'] = nrm((BATCH, SEQ, D_MODEL), 1.0)
    inp['e_norm'] = gain((NE, D_MODEL))
    inp['e_w_in'] = nrm((NE, D_MODEL, sum(EVEN_COLS)), D_MODEL ** -0.5)
    inp['e_conv_w'] = nrm((NE, LRU_CONV, LRU_WIDTH), LRU_CONV ** -0.5)
    inp['e_conv_b'] = nrm((NE, LRU_WIDTH), 0.01)
    inp['e_gate_a_w'] = nrm((NE, LRU_BLOCKS, LRU_BLOCK_DIM, LRU_BLOCK_DIM), LRU_BLOCK_DIM ** -0.5)
    inp['e_gate_a_b'] = nrm((NE, LRU_WIDTH), 0.01)
    inp['e_gate_x_w'] = nrm((NE, LRU_BLOCKS, LRU_BLOCK_DIM, LRU_BLOCK_DIM), LRU_BLOCK_DIM ** -0.5)
    inp['e_gate_x_b'] = nrm((NE, LRU_WIDTH), 0.01)
    a_c = jax.random.uniform(next(keys), (NE, LRU_WIDTH), f32, 0.9, 0.999)
    p = a_c ** (1.0 / LRU_C)
    inp['e_lambda'] = jnp.log(p) - jnp.log1p(-p)
    inp['e_q_norm'] = gain((NE, ATT_HEAD_DIM))
    inp['e_k_norm'] = gain((NE, ATT_HEAD_DIM))
    inp['e_w_out'] = nrm((NE, LRU_WIDTH + ATT_WIDTH, D_MODEL), (LRU_WIDTH + ATT_WIDTH) ** -0.5)
    inp['o_norm'] = gain((NO, D_MODEL))
    inp['o_w_in'] = nrm((NO, D_MODEL, sum(ODD_COLS)), D_MODEL ** -0.5)
    inp['o_lambda_re'] = -0.5 + nrm((NO, S5_GROUPS, S5_STATE), 0.01)
    inp['o_lambda_im'] = (jnp.pi * jnp.arange(S5_STATE, dtype=f32))[None, None, :] + nrm((NO, S5_GROUPS, S5_STATE), 0.01)
    inp['o_b_re'] = nrm((NO, S5_GROUPS, S5_STATE, S5_GROUP), (2.0 * S5_GROUP) ** -0.5)
    inp['o_b_im'] = nrm((NO, S5_GROUPS, S5_STATE, S5_GROUP), (2.0 * S5_GROUP) ** -0.5)
    inp['o_c_re'] = nrm((NO, S5_GROUPS, S5_GROUP, S5_STATE), 0.5 ** 0.5)
    inp['o_c_im'] = nrm((NO, S5_GROUPS, S5_GROUP, S5_STATE), 0.5 ** 0.5)
    inp['o_d'] = nrm((NO, S5_WIDTH), 1.0)
    inp['o_log_step'] = jax.random.uniform(next(keys), (NO, S5_GROUPS), f32, math.log(0.001), math.log(0.1))
    inp['o_glu_w'] = nrm((NO, S5_WIDTH, S5_WIDTH), S5_WIDTH ** -0.5)
    inp['o_glu_b'] = nrm((NO, S5_WIDTH), 0.01)
    inp['o_gk_w'] = nrm((NO, GLA_LOWRANK, GLA_HEADS * GLA_DK), GLA_LOWRANK ** -0.5)
    inp['o_gk_b'] = nrm((NO, GLA_HEADS * GLA_DK), 0.01)
    inp['o_gla_norm'] = gain((NO, GLA_DV))
    inp['o_w_out'] = nrm((NO, S5_WIDTH + GLA_HEADS * GLA_DV, D_MODEL), (S5_WIDTH + GLA_HEADS * GLA_DV) ** -0.5)
    inp['f_norm'] = gain((DEPTH, D_MODEL))
    inp['f_w_in'] = nrm((DEPTH, D_MODEL, 2 * D_FF), D_MODEL ** -0.5)
    inp['f_conv_w'] = nrm((DEPTH, FFN_CONV, D_FF), FFN_CONV ** -0.5)
    inp['f_conv_b'] = nrm((DEPTH, D_FF), 0.01)
    inp['f_w_out'] = nrm((DEPTH, D_FF, D_MODEL), D_FF ** -0.5)
    return inp


def reference(x,
              e_norm, e_w_in, e_conv_w, e_conv_b, e_gate_a_w, e_gate_a_b, e_gate_x_w, e_gate_x_b,
              e_lambda, e_q_norm, e_k_norm, e_w_out,
              o_norm, o_w_in, o_lambda_re, o_lambda_im, o_b_re, o_b_im, o_c_re, o_c_im, o_d,
              o_log_step, o_glu_w, o_glu_b, o_gk_w, o_gk_b, o_gla_norm, o_w_out,
              f_norm, f_w_in, f_conv_w, f_conv_b, f_w_out):
    cos, sin = rope_tables(x.shape[1])
    for layer in range(DEPTH):
        i = layer // 2
        if layer % 2 == 0:
            x = even_layer(x, e_norm[i], e_w_in[i], e_conv_w[i], e_conv_b[i], e_gate_a_w[i], e_gate_a_b[i],
                           e_gate_x_w[i], e_gate_x_b[i], e_lambda[i], e_q_norm[i], e_k_norm[i], e_w_out[i],
                           cos, sin)
        else:
            x = odd_layer(x, o_norm[i], o_w_in[i], o_lambda_re[i], o_lambda_im[i], o_b_re[i], o_b_im[i],
                          o_c_re[i], o_c_im[i], o_d[i], o_log_step[i], o_glu_w[i], o_glu_b[i],
                          o_gk_w[i], o_gk_b[i], o_gla_norm[i], o_w_out[i])
        x = conv_ffn(x, f_norm[layer], f_w_in[layer], f_conv_w[layer], f_conv_b[layer], f_w_out[layer])
    return x
```

```python
import functools
import math

import numpy as np
import jax
import jax.numpy as jnp
from jax import lax
from jax.experimental import pallas as pl
from jax.experimental.pallas import tpu as pltpu

F32 = jnp.float32
BF16 = jnp.bfloat16

D_MODEL = 1024
LRU_WIDTH = 512
LRU_BLOCKS = 8
LRU_CONV = 4
LRU_C = 8.0
ATT_HEADS = 8
ATT_HEAD_DIM = 64
ATT_WIDTH = 512
DILATED_PATTERNS = ((128, 1), (512, 4), (2048, 16))
ATT_SPAN = 2048
ROPE_THETA = 500000.0
ROPE_DIM = 16
S5_WIDTH = 512
S5_GROUP = 16
S5_GROUPS = 32
S5_STATE = 64
S5_NSTATE = S5_GROUPS * S5_STATE
GLA_HEADS = 4
GLA_DK = 64
GLA_DV = 128
GLA_LOWRANK = 16
GLA_TAU = 16.0
D_FF = 3 * D_MODEL
EPS = 1e-6
NEG_INF = -1e30

LANES = 128
SUBLANES = 8
VMEM_LIMIT = 56 * 1024 * 1024

TM_PROJ = 512
TM_FFN = 512
TF_FFN = 512
TL_LRU = 256
TQ_ATT = 256
TK_ATT = 256
TL_S5 = 512
L_S5 = 128
TL_GLA = 512
C_GLA = 64
GK_PAD = 128


def _cparams(*sem):
    return pltpu.CompilerParams(dimension_semantics=sem, vmem_limit_bytes=VMEM_LIMIT)


def _rms(x, g):
    return x * lax.rsqrt(jnp.mean(x * x, axis=-1, keepdims=True) + EPS) * g


def _log_sigmoid(x):
    return -(jnp.maximum(-x, 0.0) + jnp.log1p(jnp.exp(-jnp.abs(x))))


def _split_bf16(x):
    hi = x.astype(BF16)
    lo = (x - hi.astype(F32)).astype(BF16)
    return hi, lo


def _dot(a, b):
    return jnp.dot(a, b, preferred_element_type=F32)


def _dot_nt(a, b):
    return lax.dot_general(a, b, (((1,), (1,)), ((), ())), preferred_element_type=F32)


def _dot_tn(a, b):
    return lax.dot_general(a, b, (((0,), (0,)), ((), ())), preferred_element_type=F32)


def _even_in_kernel(x_ref, g_ref, w_ref, qn_ref, kn_ref, seg_ref, cos_ref, s1_ref, s2_ref,
                    xg_ref, q_ref, k_ref, v_ref):
    h = _rms(x_ref[...], g_ref[...])
    y = _dot(h.astype(BF16), w_ref[...])
    xg_ref[...] = y[:, :2 * LRU_WIDTH]
    v_ref[...] = y[:, 2 * LRU_WIDTH + 2 * ATT_WIDTH:].astype(BF16)
    seg = seg_ref[...]
    cos, s1, s2 = cos_ref[...], s1_ref[...], s2_ref[...]
    half = ROPE_DIM // 2
    for off, n_ref, dst, scale in ((2 * LRU_WIDTH, qn_ref, q_ref, ATT_HEAD_DIM ** -0.5),
                                   (2 * LRU_WIDTH + ATT_WIDTH, kn_ref, k_ref, 1.0)):
        for c in range(ATT_WIDTH // LANES):
            t = y[:, off + c * LANES: off + (c + 1) * LANES]
            hi, lo = _split_bf16(t * t)
            ms = (_dot(hi, seg) + _dot(lo, seg)) * (1.0 / ATT_HEAD_DIM)
            tn = t * lax.rsqrt(ms + EPS) * n_ref[...]
            r = (tn * cos + pltpu.roll(tn, LANES - half, 1) * s1 + pltpu.roll(tn, half, 1) * s2)
            dst[:, c * LANES:(c + 1) * LANES] = (r * scale).astype(BF16)


def _even_in(x, g, w, qn, kn, seg, cos_t, s1_t, s2_t, seq):
    T = x.shape[0]
    tm = TM_PROJ
    n_seq = seq // tm
    ncol = w.shape[1]
    full = lambda i: (0, 0)
    tab = lambda i: (i % n_seq, 0)
    row = lambda i: (i, 0)
    return pl.pallas_call(
        _even_in_kernel,
        grid=(T // tm,),
        in_specs=[pl.BlockSpec((tm, D_MODEL), row),
                  pl.BlockSpec((1, D_MODEL), full),
                  pl.BlockSpec((D_MODEL, ncol), full),
                  pl.BlockSpec((1, LANES), full),
                  pl.BlockSpec((1, LANES), full),
                  pl.BlockSpec((LANES, LANES), full),
                  pl.BlockSpec((tm, LANES), tab),
                  pl.BlockSpec((tm, LANES), tab),
                  pl.BlockSpec((tm, LANES), tab)],
        out_specs=[pl.BlockSpec((tm, 2 * LRU_WIDTH), row),
                   pl.BlockSpec((tm, ATT_WIDTH), row),
                   pl.BlockSpec((tm, ATT_WIDTH), row),
                   pl.BlockSpec((tm, ATT_WIDTH), row)],
        out_shape=[jax.ShapeDtypeStruct((T, 2 * LRU_WIDTH), F32),
                   jax.ShapeDtypeStruct((T, ATT_WIDTH), BF16),
                   jax.ShapeDtypeStruct((T, ATT_WIDTH), BF16),
                   jax.ShapeDtypeStruct((T, ATT_WIDTH), BF16)],
        compiler_params=_cparams("arbitrary"),
        name="even_in_proj",
    )(x, g, w, qn, kn, seg, cos_t, s1_t, s2_t)


def _lru_kernel(xl_ref, gl_ref, cw_ref, cb_ref, wa_ref, ba_ref, wx_ref, bx_ref, lam_ref,
                o_ref, xbuf, hprev):
    tl = xl_ref.shape[0]

    @pl.when(pl.program_id(1) == 0)
    def _():
        xbuf[0:SUBLANES, :] = jnp.zeros((SUBLANES, LRU_WIDTH), F32)
        hprev[...] = jnp.zeros_like(hprev)

    x = xl_ref[...]
    xbuf[SUBLANES:SUBLANES + tl, :] = x
    conv = cb_ref[...] + cw_ref[LRU_CONV - 1:LRU_CONV, :] * x
    for j in range(LRU_CONV - 1):
        conv = conv + cw_ref[j:j + 1, :] * xbuf[pl.ds(SUBLANES - (LRU_CONV - 1) + j, tl), :]
    xbuf[0:SUBLANES, :] = x[tl - SUBLANES:tl, :]

    c16 = conv.astype(BF16)
    r = jax.nn.sigmoid(_dot(c16, wa_ref[...]) + ba_ref[...])
    ig = jax.nn.sigmoid(_dot(c16, wx_ref[...]) + bx_ref[...])
    log_a = (LRU_C * r) * _log_sigmoid(lam_ref[...])
    a = jnp.exp(log_a)
    b = jnp.sqrt(1.0 - a * a) * (ig * conv)

    rows = lax.broadcasted_iota(jnp.int32, (tl, 1), 0)
    d = 1
    while d < tl:
        keep = rows >= d
        a_sh = jnp.where(keep, pltpu.roll(a, d, 0), 1.0)
        b_sh = jnp.where(keep, pltpu.roll(b, d, 0), 0.0)
        b = a * b_sh + b
        a = a * a_sh
        d *= 2
    h = b + a * hprev[0:1, :]
    hprev[0:1, :] = h[tl - 1:tl, :]
    o_ref[...] = (h * jax.nn.gelu(gl_ref[...])).astype(BF16)


def _lru(xg, cw, cb, wa, ba, wx, bx, lam, batch, seq):
    T = xg.shape[0]
    tl = TL_LRU
    n_seq = seq // tl
    full = lambda b, i: (0, 0)
    return pl.pallas_call(
        _lru_kernel,
        grid=(batch, n_seq),
        in_specs=[pl.BlockSpec((tl, LRU_WIDTH), lambda b, i: (b * n_seq + i, 0)),
                  pl.BlockSpec((tl, LRU_WIDTH), lambda b, i: (b * n_seq + i, 1)),
                  pl.BlockSpec((LRU_CONV, LRU_WIDTH), full),
                  pl.BlockSpec((1, LRU_WIDTH), full),
                  pl.BlockSpec((LRU_WIDTH, LRU_WIDTH), full),
                  pl.BlockSpec((1, LRU_WIDTH), full),
                  pl.BlockSpec((LRU_WIDTH, LRU_WIDTH), full),
                  pl.BlockSpec((1, LRU_WIDTH), full),
                  pl.BlockSpec((1, LRU_WIDTH), full)],
        out_specs=pl.BlockSpec((tl, LRU_WIDTH), lambda b, i: (b * n_seq + i, 0)),
        out_shape=jax.ShapeDtypeStruct((T, LRU_WIDTH), BF16),
        scratch_shapes=[pltpu.VMEM((SUBLANES + tl, LRU_WIDTH), F32),
                        pltpu.VMEM((SUBLANES, LRU_WIDTH), F32)],
        compiler_params=_cparams("arbitrary", "arbitrary"),
        name="rg_lru",
    )(xg, xg, cw, cb, wa, ba, wx, bx, lam)


def _attention_bias(tq, tk):
    nb = ATT_SPAN // tk
    r = np.arange(tq)[:, None]
    c = np.arange(tk)[None, :]
    tiles = []
    for d in range(nb + 1):
        diff = d * tk + r - c
        mult = np.zeros((tq, tk), np.int64)
        for window, dil in DILATED_PATTERNS:
            mult += ((diff >= 0) & (diff <= window) & (diff % dil == 0)).astype(np.int64)
        tiles.append(np.where(mult > 0, np.log(np.maximum(mult, 1)), NEG_INF))
    tiles.append(np.full((tq, tk), NEG_INF))
    return np.stack(tiles).astype(np.float32)


def _attn_kernel(q_ref, k_ref, v_ref, bias_ref, o_ref):
    tq = q_ref.shape[1]
    tk = TK_ATT
    nb = ATT_SPAN // tk
    qi = pl.program_id(2)
    w0 = jnp.maximum(qi - nb, 0)
    q = q_ref[0]
    lane = lax.broadcasted_iota(jnp.int32, (1, LANES), 1)
    outs = []
    for h in range(LANES // ATT_HEAD_DIM):
        in_head = (lane >= h * ATT_HEAD_DIM) & (lane < (h + 1) * ATT_HEAD_DIM)
        qm = jnp.where(in_head, q, jnp.zeros_like(q))
        tiles = []
        m = jnp.full((tq, 1), NEG_INF, F32)
        for jj in range(nb + 1):
            kb = w0 + jj
            kt = k_ref[0, pl.ds(pl.multiple_of(kb * tk, tk), tk), :]
            dblk = qi - kb
            s = _dot_nt(qm, kt) + bias_ref[jnp.where(dblk < 0, nb + 1, dblk)]
            m = jnp.maximum(m, jnp.max(s, axis=-1, keepdims=True))
            tiles.append(s)
        l = jnp.zeros((tq, 1), F32)
        acc = jnp.zeros((tq, LANES), F32)
        for jj in range(nb + 1):
            kb = w0 + jj
            p = jnp.exp(tiles[jj] - m)
            l = l + jnp.sum(p, axis=-1, keepdims=True)
            vt = v_ref[0, pl.ds(pl.multiple_of(kb * tk, tk), tk), :]
            acc = acc + _dot(p.astype(BF16), vt)
        outs.append(acc / l)
    o_ref[0] = jnp.where(lane < ATT_HEAD_DIM, outs[0], outs[1]).astype(BF16)


def _attention(q, k, v, bias):
    B, S, W = q.shape
    tq = TQ_ATT
    return pl.pallas_call(
        _attn_kernel,
        grid=(B, W // LANES, S // tq),
        in_specs=[pl.BlockSpec((1, tq, LANES), lambda b, p, i: (b, i, p)),
                  pl.BlockSpec((1, S, LANES), lambda b, p, i: (b, 0, p)),
                  pl.BlockSpec((1, S, LANES), lambda b, p, i: (b, 0, p)),
                  pl.BlockSpec(bias.shape, lambda b, p, i: (0, 0, 0))],
        out_specs=pl.BlockSpec((1, tq, LANES), lambda b, p, i: (b, i, p)),
        out_shape=jax.ShapeDtypeStruct((B, S, W), BF16),
        compiler_params=_cparams("arbitrary", "arbitrary", "arbitrary"),
        name="dilated_attention",
    )(q, k, v, bias)


def _ffn_kernel(x_ref, ya_ref, yb_ref, wo_ref, g_ref, wa_ref, wl_ref, cw_ref, cb_ref, w2_ref,
                o_ref, x1_s, h_s, acc_s, abuf, carry_s, *, tiles_per_seq):
    i = pl.program_id(0)
    j = pl.program_id(1)
    tm = x_ref.shape[0]
    half = ya_ref.shape[1]

    @pl.when(j == 0)
    def _():
        x1 = (x_ref[...] + _dot(ya_ref[...], wo_ref[0:half, :])
              + _dot(yb_ref[...], wo_ref[half:2 * half, :]))
        x1_s[...] = x1
        h_s[...] = _rms(x1, g_ref[...]).astype(BF16)
        acc_s[...] = jnp.zeros_like(acc_s)

    h = h_s[...]
    a = _dot(h, wa_ref[...])
    lin = _dot(h, wl_ref[...])
    prev = jnp.where(i % tiles_per_seq == 0, jnp.zeros_like(carry_s[j]), carry_s[j])
    abuf[0:SUBLANES, :] = prev
    abuf[SUBLANES:SUBLANES + tm, :] = a
    carry_s[j] = a[tm - SUBLANES:tm, :]
    conv = (cb_ref[...] + cw_ref[2:3, :] * a
            + cw_ref[1:2, :] * abuf[pl.ds(SUBLANES - 1, tm), :]
            + cw_ref[0:1, :] * abuf[pl.ds(SUBLANES - 2, tm), :])
    act = jax.nn.gelu(conv) * lin
    acc_s[...] += _dot(act.astype(BF16), w2_ref[...])

    @pl.when(j == pl.num_programs(1) - 1)
    def _():
        o_ref[...] = x1_s[...] + acc_s[...]


def _ffn(x, ya, yb, wo, g, w_in, cw, cb, w2, seq):
    T = x.shape[0]
    tm, tf = TM_FFN, TF_FFN
    nj = D_FF // tf
    half = ya.shape[1]
    row = lambda i, j: (i, 0)
    full = lambda i, j: (0, 0)
    return pl.pallas_call(
        functools.partial(_ffn_kernel, tiles_per_seq=seq // tm),
        grid=(T // tm, nj),
        in_specs=[pl.BlockSpec((tm, D_MODEL), row),
                  pl.BlockSpec((tm, half), row),
                  pl.BlockSpec((tm, half), row),
                  pl.BlockSpec((2 * half, D_MODEL), full),
                  pl.BlockSpec((1, D_MODEL), full),
                  pl.BlockSpec((D_MODEL, tf), lambda i, j: (0, j)),
                  pl.BlockSpec((D_MODEL, tf), lambda i, j: (0, nj + j)),
                  pl.BlockSpec((3, tf), lambda i, j: (0, j)),
                  pl.BlockSpec((1, tf), lambda i, j: (0, j)),
                  pl.BlockSpec((tf, D_MODEL), lambda i, j: (j, 0))],
        out_specs=pl.BlockSpec((tm, D_MODEL), row),
        out_shape=jax.ShapeDtypeStruct((T, D_MODEL), F32),
        scratch_shapes=[pltpu.VMEM((tm, D_MODEL), F32),
                        pltpu.VMEM((tm, D_MODEL), BF16),
                        pltpu.VMEM((tm, D_MODEL), F32),
                        pltpu.VMEM((SUBLANES + tm, tf), F32),
                        pltpu.VMEM((nj, SUBLANES, tf), F32)],
        compiler_params=_cparams("arbitrary", "arbitrary"),
        name="outproj_conv_mlp",
    )(x, ya, yb, wo, g, w_in, w_in, cw, cb, w2)


def _odd_in_kernel(x_ref, g_ref, w_ref, o_ref):
    h = _rms(x_ref[...], g_ref[...])
    o_ref[...] = _dot(h.astype(BF16), w_ref[...])


def _odd_in(x, g, w):
    T = x.shape[0]
    tm = TM_PROJ
    ncol = w.shape[1]
    return pl.pallas_call(
        _odd_in_kernel,
        grid=(T // tm,),
        in_specs=[pl.BlockSpec((tm, D_MODEL), lambda i: (i, 0)),
                  pl.BlockSpec((1, D_MODEL), lambda i: (0, 0)),
                  pl.BlockSpec((D_MODEL, ncol), lambda i: (0, 0))],
        out_specs=pl.BlockSpec((tm, ncol), lambda i: (i, 0)),
        out_shape=jax.ShapeDtypeStruct((T, ncol), F32),
        compiler_params=_cparams("arbitrary"),
        name="odd_in_proj",
    )(x, g, w)


def _s5_kernel(u_ref, bm_ref, pr_ref, pi_ref, qr_ref, qi_ref, tri_ref, cre_ref, cim_ref,
               d_ref, gw_ref, gb_ref, o_ref, sr, si, ys_s):
    tl = u_ref.shape[0]
    L = L_S5
    slab_states = S5_NSTATE // (S5_WIDTH // LANES)

    @pl.when(pl.program_id(1) == 0)
    def _():
        sr[...] = jnp.zeros_like(sr)
        si[...] = jnp.zeros_like(si)

    tri = tri_ref[...]
    for c in range(tl // L):
        rows = slice(c * L, (c + 1) * L)
        for s in range(S5_WIDTH // LANES):
            cols = slice(s * slab_states, (s + 1) * slab_states)
            u = u_ref[rows, s * LANES:(s + 1) * LANES]
            bu = _dot(u.astype(BF16), bm_ref[s])
            bur, bui = bu[:, :slab_states], bu[:, slab_states:]
            qr, qi = qr_ref[:, cols], qi_ref[:, cols]
            zr = bur * qr - bui * qi
            zi = bur * qi + bui * qr
            zr_hi, zr_lo = _split_bf16(zr)
            zi_hi, zi_lo = _split_bf16(zi)
            wr = _dot(tri, zr_hi) + _dot(tri, zr_lo) + sr[0:1, cols]
            wi = _dot(tri, zi_hi) + _dot(tri, zi_lo) + si[0:1, cols]
            pr, pi = pr_ref[:, cols], pi_ref[:, cols]
            xr = pr * wr - pi * wi
            xi = pr * wi + pi * wr
            ar, ai = pr_ref[1:2, cols], pi_ref[1:2, cols]
            lr, li = xr[L - 1:L, :], xi[L - 1:L, :]
            sr[0:1, cols] = ar * lr - ai * li
            si[0:1, cols] = ar * li + ai * lr
            y = _dot(xr.astype(BF16), cre_ref[s]) + _dot(xi.astype(BF16), cim_ref[s])
            y = y + d_ref[:, s * LANES:(s + 1) * LANES] * u
            ys_s[rows, s * LANES:(s + 1) * LANES] = jax.nn.gelu(y)
    ys = ys_s[...]
    o_ref[...] = (ys * jax.nn.sigmoid(_dot(ys.astype(BF16), gw_ref[...]) + gb_ref[...])).astype(BF16)


def _s5(proj, bm, pr, pi, qr, qi, tri, cre, cim, d, gw, gb, batch, seq):
    T = proj.shape[0]
    tl = TL_S5
    n_seq = seq // tl
    c2 = lambda b, i: (0, 0)
    c3 = lambda b, i: (0, 0, 0)
    return pl.pallas_call(
        _s5_kernel,
        grid=(batch, n_seq),
        in_specs=[pl.BlockSpec((tl, S5_WIDTH), lambda b, i: (b * n_seq + i, 0)),
                  pl.BlockSpec(bm.shape, c3),
                  pl.BlockSpec(pr.shape, c2), pl.BlockSpec(pi.shape, c2),
                  pl.BlockSpec(qr.shape, c2), pl.BlockSpec(qi.shape, c2),
                  pl.BlockSpec(tri.shape, c2),
                  pl.BlockSpec(cre.shape, c3), pl.BlockSpec(cim.shape, c3),
                  pl.BlockSpec((1, S5_WIDTH), c2),
                  pl.BlockSpec((S5_WIDTH, S5_WIDTH), c2),
                  pl.BlockSpec((1, S5_WIDTH), c2)],
        out_specs=pl.BlockSpec((tl, S5_WIDTH), lambda b, i: (b * n_seq + i, 0)),
        out_shape=jax.ShapeDtypeStruct((T, S5_WIDTH), BF16),
        scratch_shapes=[pltpu.VMEM((SUBLANES, S5_NSTATE), F32),
                        pltpu.VMEM((SUBLANES, S5_NSTATE), F32),
                        pltpu.VMEM((tl, S5_WIDTH), F32)],
        compiler_params=_cparams("arbitrary", "arbitrary"),
        name="s5_glu",
    )(proj, bm, pr, pi, qr, qi, tri, cre, cim, d, gw, gb)


def _s5_params(lam_re, lam_im, b_re, b_im, c_re, c_im, log_step):
    G, N, P = S5_GROUPS, S5_STATE, S5_GROUP
    gs = LANES // P
    ns = S5_WIDTH // LANES
    step = jnp.exp(log_step.astype(F32))[:, None]
    lr, li = lam_re.astype(F32), lam_im.astype(F32)
    mag = jnp.exp(lr * step)
    ar, ai = mag * jnp.cos(li * step), mag * jnp.sin(li * step)
    den = lr * lr + li * li
    cr = ((ar - 1.0) * lr + ai * li) / den
    ci = (ai * lr - (ar - 1.0) * li) / den
    bbr = cr[..., None] * b_re - ci[..., None] * b_im
    bbi = cr[..., None] * b_im + ci[..., None] * b_re
    eye = jnp.eye(gs, dtype=F32)

    def in_blockdiag(t):
        t = t.reshape(ns, gs, N, P).transpose(0, 1, 3, 2)
        return jnp.einsum('ab,sapn->sapbn', eye, t).reshape(ns, gs * P, gs * N)

    def out_blockdiag(t):
        t = t.reshape(ns, gs, P, N).transpose(0, 1, 3, 2)
        return jnp.einsum('ab,sanp->sanbp', eye, t).reshape(ns, gs * N, gs * P)

    bm = jnp.concatenate([in_blockdiag(bbr), in_blockdiag(bbi)], axis=-1).astype(BF16)
    cre = out_blockdiag(c_re).astype(BF16)
    cim = out_blockdiag(-c_im).astype(BF16)
    j = jnp.arange(L_S5, dtype=F32)[:, None]
    la = (lr * step).reshape(1, G * N)
    th = (li * step).reshape(1, G * N)
    pmag, qmag = jnp.exp(j * la), jnp.exp(-(j * la))
    cs, sn = jnp.cos(j * th), jnp.sin(j * th)
    return bm, pmag * cs, pmag * sn, qmag * cs, -(qmag * sn), cre, cim


def _gla_kernel(q_ref, k_ref, v_ref, g_ref, gk_ref, gw_ref, gb_ref, nrm_ref, tri_ref, o_ref, st):
    tl = q_ref.shape[0]
    C = C_GLA
    pair = LANES // GLA_DK

    @pl.when(pl.program_id(1) == 0)
    def _():
        st[...] = jnp.zeros_like(st)

    z = jnp.dot(gk_ref[...], gw_ref[...], preferred_element_type=F32,
                precision=lax.Precision.HIGHEST) + gb_ref[...]
    log_a = _log_sigmoid(z) * (1.0 / GLA_TAU)
    tri = tri_ref[...]
    lane = lax.broadcasted_iota(jnp.int32, (1, LANES), 1)
    rc = lax.broadcasted_iota(jnp.int32, (C, C), 0)
    cc = lax.broadcasted_iota(jnp.int32, (C, C), 1)
    causal = rc >= cc
    for c in range(tl // C):
        rows = slice(c * C, (c + 1) * C)
        for hp in range(GLA_HEADS // pair):
            cols = slice(hp * LANES, (hp + 1) * LANES)
            bc = jnp.dot(tri, log_a[rows, cols], preferred_element_type=F32,
                         precision=lax.Precision.HIGHEST)
            b_last = bc[C - 1:C, :]
            q_dec = q_ref[rows, cols] * (GLA_DK ** -0.5) * jnp.exp(bc)
            k = k_ref[rows, cols]
            k_inv = (k * jnp.exp(-bc)).astype(BF16)
            k_dec = k * jnp.exp(b_last - bc)
            decay = jnp.exp(b_last)
            for hh in range(pair):
                h = hp * pair + hh
                in_head = (lane >= hh * GLA_DK) & (lane < (hh + 1) * GLA_DK)
                qd = jnp.where(in_head, q_dec, 0.0).astype(BF16)
                kd = jnp.where(in_head, k_dec, 0.0).astype(BF16)
                vh = v_ref[rows, h * GLA_DV:(h + 1) * GLA_DV].astype(BF16)
                att = jnp.where(causal, _dot_nt(qd, k_inv), 0.0)
                s_prev = st[h]
                o = _dot(att.astype(BF16), vh) + _dot_nt(qd, s_prev.astype(BF16))
                st[h] = s_prev * decay + _dot_tn(vh, kd)
                o = _rms(o, nrm_ref[...])
                gh = g_ref[rows, h * GLA_DV:(h + 1) * GLA_DV]
                o_ref[rows, h * GLA_DV:(h + 1) * GLA_DV] = (o * jax.nn.silu(gh)).astype(BF16)


def _gla(proj, gw, gb, nrm, tri, batch, seq):
    T = proj.shape[0]
    tl = TL_GLA
    n_seq = seq // tl
    hk = GLA_HEADS * GLA_DK
    hv = GLA_HEADS * GLA_DV
    c2 = lambda b, i: (0, 0)
    q0 = S5_WIDTH // hk
    v0 = (S5_WIDTH + 2 * hk) // hv
    gk0 = (S5_WIDTH + 2 * hk + 2 * hv) // GK_PAD
    return pl.pallas_call(
        _gla_kernel,
        grid=(batch, n_seq),
        in_specs=[pl.BlockSpec((tl, hk), lambda b, i: (b * n_seq + i, q0)),
                  pl.BlockSpec((tl, hk), lambda b, i: (b * n_seq + i, q0 + 1)),
                  pl.BlockSpec((tl, hv), lambda b, i: (b * n_seq + i, v0)),
                  pl.BlockSpec((tl, hv), lambda b, i: (b * n_seq + i, v0 + 1)),
                  pl.BlockSpec((tl, GK_PAD), lambda b, i: (b * n_seq + i, gk0)),
                  pl.BlockSpec((GK_PAD, hk), c2),
                  pl.BlockSpec((1, hk), c2),
                  pl.BlockSpec((1, GLA_DV), c2),
                  pl.BlockSpec((C_GLA, C_GLA), c2)],
        out_specs=pl.BlockSpec((tl, hv), lambda b, i: (b * n_seq + i, 0)),
        out_shape=jax.ShapeDtypeStruct((T, hv), BF16),
        scratch_shapes=[pltpu.VMEM((GLA_HEADS, GLA_DV, LANES), F32)],
        compiler_params=_cparams("arbitrary", "arbitrary"),
        name="gla",
    )(proj, proj, proj, proj, proj, gw, gb, nrm, tri)


def _block_diag(w):
    nb, a, b = w.shape
    return jnp.einsum('hk,hij->hikj', jnp.eye(nb, dtype=w.dtype), w).reshape(nb * a, nb * b)


def _rope_tables(seq):
    half = ROPE_DIM // 2
    pos = jnp.arange(seq, dtype=F32)
    inv = ROPE_THETA ** (-jnp.arange(0, ROPE_DIM, 2, dtype=F32) / ROPE_DIM)
    ang = pos[:, None] * inv[None, :]
    cos, sin = jnp.cos(ang), jnp.sin(ang)
    rest = ATT_HEAD_DIM - ROPE_DIM
    ones = jnp.ones((seq, rest), F32)
    zeros = jnp.zeros((seq, rest), F32)
    zh = jnp.zeros((seq, half), F32)
    per_head = lambda parts: jnp.tile(jnp.concatenate(parts, axis=1), (1, LANES // ATT_HEAD_DIM))
    return (per_head([cos, cos, ones]), per_head([-sin, zh, zeros]), per_head([zh, sin, zeros]))


def kernel(x, e_norm, e_w_in, e_conv_w, e_conv_b, e_gate_a_w, e_gate_a_b, e_gate_x_w, e_gate_x_b, e_lambda, e_q_norm, e_k_norm, e_w_out, o_norm, o_w_in, o_lambda_re, o_lambda_im, o_b_re, o_b_im, o_c_re, o_c_im, o_d, o_log_step, o_glu_w, o_glu_b, o_gk_w, o_gk_b, o_gla_norm, o_w_out, f_norm, f_w_in, f_conv_w, f_conv_b, f_w_out):
    B, S, D = x.shape
    T = B * S
    depth = f_norm.shape[0]
    row = lambda t: t.reshape(1, -1).astype(F32)
    xt = x.reshape(T, D)

    cos_t, s1_t, s2_t = _rope_tables(S)
    head_seg = jnp.asarray(np.kron(np.eye(LANES // ATT_HEAD_DIM), np.ones((ATT_HEAD_DIM, ATT_HEAD_DIM))), BF16)
    att_bias = jnp.asarray(_attention_bias(TQ_ATT, TK_ATT))
    tri_s5 = jnp.asarray(np.tril(np.ones((L_S5, L_S5))), BF16)
    tri_gla = jnp.asarray(np.tril(np.ones((C_GLA, C_GLA))), F32)
    two_heads = lambda t: jnp.tile(row(t), (1, LANES // ATT_HEAD_DIM))

    for layer in range(depth):
        i = layer // 2
        if layer % 2 == 0:
            xg, q, k, v = _even_in(xt, row(e_norm[i]), e_w_in[i].astype(BF16),
                                   two_heads(e_q_norm[i]), two_heads(e_k_norm[i]),
                                   head_seg, cos_t, s1_t, s2_t, S)
            ya = _lru(xg, e_conv_w[i], row(e_conv_b[i]),
                      _block_diag(e_gate_a_w[i]).astype(BF16), row(e_gate_a_b[i]),
                      _block_diag(e_gate_x_w[i]).astype(BF16), row(e_gate_x_b[i]),
                      row(e_lambda[i]), B, S)
            yb = _attention(q.reshape(B, S, ATT_WIDTH), k.reshape(B, S, ATT_WIDTH),
                            v.reshape(B, S, ATT_WIDTH), att_bias).reshape(T, ATT_WIDTH)
            w_out = e_w_out[i]
        else:
            w_in = jnp.pad(o_w_in[i], ((0, 0), (0, GK_PAD - GLA_LOWRANK))).astype(BF16)
            proj = _odd_in(xt, row(o_norm[i]), w_in)
            bm, pr, pi, qr, qi, cre, cim = _s5_params(o_lambda_re[i], o_lambda_im[i], o_b_re[i], o_b_im[i],
                                                     o_c_re[i], o_c_im[i], o_log_step[i])
            ya = _s5(proj, bm, pr, pi, qr, qi, tri_s5, cre, cim, row(o_d[i]),
                     o_glu_w[i].astype(BF16), row(o_glu_b[i]), B, S)
            gk_w = jnp.pad(o_gk_w[i], ((0, GK_PAD - GLA_LOWRANK), (0, 0)))
            yb = _gla(proj, gk_w, row(o_gk_b[i]), row(o_gla_norm[i]), tri_gla, B, S)
            w_out = o_w_out[i]
        xt = _ffn(xt, ya, yb, w_out.astype(BF16), row(f_norm[layer]), f_w_in[layer].astype(BF16),
                  f_conv_w[layer], row(f_conv_b[layer]), f_w_out[layer].astype(BF16), S)
    return xt.reshape(B, S, D)
```

```python
import functools
import math

import numpy as np
import jax
import jax.numpy as jnp
from jax import lax
from jax.experimental import pallas as pl
from jax.experimental.pallas import tpu as pltpu

F32 = jnp.float32
BF16 = jnp.bfloat16

D_MODEL = 1024
LRU_WIDTH = 512
LRU_BLOCKS = 8
LRU_CONV = 4
LRU_C = 8.0
ATT_HEADS = 8
ATT_HEAD_DIM = 64
ATT_WIDTH = 512
DILATED_PATTERNS = ((128, 1), (512, 4), (2048, 16))
ATT_SPAN = 2048
ROPE_THETA = 500000.0
ROPE_DIM = 16
S5_WIDTH = 512
S5_GROUP = 16
S5_GROUPS = 32
S5_STATE = 64
S5_NSTATE = S5_GROUPS * S5_STATE
GLA_HEADS = 4
GLA_DK = 64
GLA_DV = 128
GLA_LOWRANK = 16
GLA_TAU = 16.0
D_FF = 3 * D_MODEL
EPS = 1e-6
NEG_INF = -1e30

LANES = 128
SUBLANES = 8
VMEM_LIMIT = 56 * 1024 * 1024

TM_PROJ = 512
TM_FFN = 512
TF_FFN = 512
TL_LRU = 256
TQ_ATT = 256
TK_ATT = 256
TL_S5 = 512
L_S5 = 128
TL_GLA = 512
C_GLA = 64
GK_PAD = 128


def _cparams(*sem):
    return pltpu.CompilerParams(dimension_semantics=sem, vmem_limit_bytes=VMEM_LIMIT)


def _rms(x, g):
    return x * lax.rsqrt(jnp.mean(x * x, axis=-1, keepdims=True) + EPS) * g


def _log_sigmoid(x):
    return -(jnp.maximum(-x, 0.0) + jnp.log1p(jnp.exp(-jnp.abs(x))))


def _split_bf16(x):
    hi = x.astype(BF16)
    lo = (x - hi.astype(F32)).astype(BF16)
    return hi, lo


def _dot(a, b):
    return jnp.dot(a, b, preferred_element_type=F32)


def _dot_nt(a, b):
    return lax.dot_general(a, b, (((1,), (1,)), ((), ())), preferred_element_type=F32)


def _dot_tn(a, b):
    return lax.dot_general(a, b, (((0,), (0,)), ((), ())), preferred_element_type=F32)


def _even_in_kernel(x_ref, g_ref, w_ref, qn_ref, kn_ref, seg_ref, cos_ref, s1_ref, s2_ref,
                    xg_ref, q_ref, k_ref, v_ref):
    h = _rms(x_ref[...], g_ref[...])
    y = _dot(h.astype(BF16), w_ref[...])
    xg_ref[...] = y[:, :2 * LRU_WIDTH]
    v_ref[...] = y[:, 2 * LRU_WIDTH + 2 * ATT_WIDTH:].astype(BF16)
    seg = seg_ref[...]
    cos, s1, s2 = cos_ref[...], s1_ref[...], s2_ref[...]
    half = ROPE_DIM // 2
    for off, n_ref, dst, scale in ((2 * LRU_WIDTH, qn_ref, q_ref, ATT_HEAD_DIM ** -0.5),
                                   (2 * LRU_WIDTH + ATT_WIDTH, kn_ref, k_ref, 1.0)):
        for c in range(ATT_WIDTH // LANES):
            t = y[:, off + c * LANES: off + (c + 1) * LANES]
            hi, lo = _split_bf16(t * t)
            ms = (_dot(hi, seg) + _dot(lo, seg)) * (1.0 / ATT_HEAD_DIM)
            tn = t * lax.rsqrt(ms + EPS) * n_ref[...]
            r = (tn * cos + pltpu.roll(tn, LANES - half, 1) * s1 + pltpu.roll(tn, half, 1) * s2)
            dst[:, c * LANES:(c + 1) * LANES] = (r * scale).astype(BF16)


def _even_in(x, g, w, qn, kn, seg, cos_t, s1_t, s2_t, seq):
    T = x.shape[0]
    tm = TM_PROJ
    n_seq = seq // tm
    ncol = w.shape[1]
    full = lambda i: (0, 0)
    tab = lambda i: (i % n_seq, 0)
    row = lambda i: (i, 0)
    return pl.pallas_call(
        _even_in_kernel,
        grid=(T // tm,),
        in_specs=[pl.BlockSpec((tm, D_MODEL), row),
                  pl.BlockSpec((1, D_MODEL), full),
                  pl.BlockSpec((D_MODEL, ncol), full),
                  pl.BlockSpec((1, LANES), full),
                  pl.BlockSpec((1, LANES), full),
                  pl.BlockSpec((LANES, LANES), full),
                  pl.BlockSpec((tm, LANES), tab),
                  pl.BlockSpec((tm, LANES), tab),
                  pl.BlockSpec((tm, LANES), tab)],
        out_specs=[pl.BlockSpec((tm, 2 * LRU_WIDTH), row),
                   pl.BlockSpec((tm, ATT_WIDTH), row),
                   pl.BlockSpec((tm, ATT_WIDTH), row),
                   pl.BlockSpec((tm, ATT_WIDTH), row)],
        out_shape=[jax.ShapeDtypeStruct((T, 2 * LRU_WIDTH), F32),
                   jax.ShapeDtypeStruct((T, ATT_WIDTH), BF16),
                   jax.ShapeDtypeStruct((T, ATT_WIDTH), BF16),
                   jax.ShapeDtypeStruct((T, ATT_WIDTH), BF16)],
        compiler_params=_cparams("arbitrary"),
        name="even_in_proj",
    )(x, g, w, qn, kn, seg, cos_t, s1_t, s2_t)


def _lru_kernel(xl_ref, gl_ref, cw_ref, cb_ref, wa_ref, ba_ref, wx_ref, bx_ref, lam_ref,
                o_ref, xbuf, hprev):
    tl = xl_ref.shape[0]

    @pl.when(pl.program_id(1) == 0)
    def _():
        xbuf[0:SUBLANES, :] = jnp.zeros((SUBLANES, LRU_WIDTH), F32)
        hprev[...] = jnp.zeros_like(hprev)

    x = xl_ref[...]
    xbuf[SUBLANES:SUBLANES + tl, :] = x
    conv = cb_ref[...] + cw_ref[LRU_CONV - 1:LRU_CONV, :] * x
    for j in range(LRU_CONV - 1):
        conv = conv + cw_ref[j:j + 1, :] * xbuf[pl.ds(SUBLANES - (LRU_CONV - 1) + j, tl), :]
    xbuf[0:SUBLANES, :] = x[tl - SUBLANES:tl, :]

    c16 = conv.astype(BF16)
    r = jax.nn.sigmoid(_dot(c16, wa_ref[...]) + ba_ref[...])
    ig = jax.nn.sigmoid(_dot(c16, wx_ref[...]) + bx_ref[...])
    log_a = (LRU_C * r) * _log_sigmoid(lam_ref[...])
    a = jnp.exp(log_a)
    b = jnp.sqrt(1.0 - a * a) * (ig * conv)

    rows = lax.broadcasted_iota(jnp.int32, (tl, 1), 0)
    d = 1
    while d < tl:
        keep = rows >= d
        a_sh = jnp.where(keep, pltpu.roll(a, d, 0), 1.0)
        b_sh = jnp.where(keep, pltpu.roll(b, d, 0), 0.0)
        b = a * b_sh + b
        a = a * a_sh
        d *= 2
    h = b + a * hprev[0:1, :]
    hprev[0:1, :] = h[tl - 1:tl, :]
    o_ref[...] = (h * jax.nn.gelu(gl_ref[...])).astype(BF16)


def _lru(xg, cw, cb, wa, ba, wx, bx, lam, batch, seq):
    T = xg.shape[0]
    tl = TL_LRU
    n_seq = seq // tl
    full = lambda b, i: (0, 0)
    return pl.pallas_call(
        _lru_kernel,
        grid=(batch, n_seq),
        in_specs=[pl.BlockSpec((tl, LRU_WIDTH), lambda b, i: (b * n_seq + i, 0)),
                  pl.BlockSpec((tl, LRU_WIDTH), lambda b, i: (b * n_seq + i, 1)),
                  pl.BlockSpec((LRU_CONV, LRU_WIDTH), full),
                  pl.BlockSpec((1, LRU_WIDTH), full),
                  pl.BlockSpec((LRU_WIDTH, LRU_WIDTH), full),
                  pl.BlockSpec((1, LRU_WIDTH), full),
                  pl.BlockSpec((LRU_WIDTH, LRU_WIDTH), full),
                  pl.BlockSpec((1, LRU_WIDTH), full),
                  pl.BlockSpec((1, LRU_WIDTH), full)],
        out_specs=pl.BlockSpec((tl, LRU_WIDTH), lambda b, i: (b * n_seq + i, 0)),
        out_shape=jax.ShapeDtypeStruct((T, LRU_WIDTH), BF16),
        scratch_shapes=[pltpu.VMEM((SUBLANES + tl, LRU_WIDTH), F32),
                        pltpu.VMEM((SUBLANES, LRU_WIDTH), F32)],
        compiler_params=_cparams("arbitrary", "arbitrary"),
        name="rg_lru",
    )(xg, xg, cw, cb, wa, ba, wx, bx, lam)


def _attention_bias(tq, tk):
    nb = ATT_SPAN // tk
    r = np.arange(tq)[:, None]
    c = np.arange(tk)[None, :]
    tiles = []
    for d in range(nb + 1):
        diff = d * tk + r - c
        mult = np.zeros((tq, tk), np.int64)
        for window, dil in DILATED_PATTERNS:
            mult += ((diff >= 0) & (diff <= window) & (diff % dil == 0)).astype(np.int64)
        tiles.append(np.where(mult > 0, np.log(np.maximum(mult, 1)), NEG_INF))
    tiles.append(np.full((tq, tk), NEG_INF))
    return np.stack(tiles).astype(np.float32)


def _attn_kernel(q_ref, k_ref, v_ref, bias_ref, o_ref):
    tq = q_ref.shape[1]
    tk = TK_ATT
    nb = ATT_SPAN // tk
    qi = pl.program_id(2)
    w0 = jnp.maximum(qi - nb, 0)
    q = q_ref[0]
    lane = lax.broadcasted_iota(jnp.int32, (1, LANES), 1)
    outs = []
    for h in range(LANES // ATT_HEAD_DIM):
        in_head = (lane >= h * ATT_HEAD_DIM) & (lane < (h + 1) * ATT_HEAD_DIM)
        qm = jnp.where(in_head, q, jnp.zeros_like(q))
        tiles = []
        m = jnp.full((tq, 1), NEG_INF, F32)
        for jj in range(nb + 1):
            kb = w0 + jj
            kt = k_ref[0, pl.ds(pl.multiple_of(kb * tk, tk), tk), :]
            dblk = qi - kb
            s = _dot_nt(qm, kt) + bias_ref[jnp.where(dblk < 0, nb + 1, dblk)]
            m = jnp.maximum(m, jnp.max(s, axis=-1, keepdims=True))
            tiles.append(s)
        l = jnp.zeros((tq, 1), F32)
        acc = jnp.zeros((tq, LANES), F32)
        for jj in range(nb + 1):
            kb = w0 + jj
            p = jnp.exp(tiles[jj] - m)
            l = l + jnp.sum(p, axis=-1, keepdims=True)
            vt = v_ref[0, pl.ds(pl.multiple_of(kb * tk, tk), tk), :]
            acc = acc + _dot(p.astype(BF16), vt)
        outs.append(acc / l)
    o_ref[0] = jnp.where(lane < ATT_HEAD_DIM, outs[0], outs[1]).astype(BF16)


def _attention(q, k, v, bias):
    B, S, W = q.shape
    tq = TQ_ATT
    return pl.pallas_call(
        _attn_kernel,
        grid=(B, W // LANES, S // tq),
        in_specs=[pl.BlockSpec((1, tq, LANES), lambda b, p, i: (b, i, p)),
                  pl.BlockSpec((1, S, LANES), lambda b, p, i: (b, 0, p)),
                  pl.BlockSpec((1, S, LANES), lambda b, p, i: (b, 0, p)),
                  pl.BlockSpec(bias.shape, lambda b, p, i: (0, 0, 0))],
        out_specs=pl.BlockSpec((1, tq, LANES), lambda b, p, i: (b, i, p)),
        out_shape=jax.ShapeDtypeStruct((B, S, W), BF16),
        compiler_params=_cparams("arbitrary", "arbitrary", "arbitrary"),
        name="dilated_attention",
    )(q, k, v, bias)


def _ffn_kernel(x_ref, ya_ref, yb_ref, wo_ref, g_ref, w1_ref, cw_ref, cb_ref, w2_ref,
                o_ref, act_s, abuf, carry_s, *, tiles_per_seq):
    i = pl.program_id(0)
    tm = x_ref.shape[0]
    half = ya_ref.shape[1]
    tf = abuf.shape[1]

    @pl.when(i % tiles_per_seq == 0)
    def _():
        carry_s[...] = jnp.zeros_like(carry_s)

    x1 = (x_ref[...] + _dot(ya_ref[...], wo_ref[0:half, :])
          + _dot(yb_ref[...], wo_ref[half:2 * half, :]))
    h = _rms(x1, g_ref[...]).astype(BF16)
    for c in range(D_FF // tf):
        cols = slice(c * tf, (c + 1) * tf)
        a = _dot(h, w1_ref[:, cols])
        lin = _dot(h, w1_ref[:, D_FF + c * tf:D_FF + (c + 1) * tf])
        abuf[0:SUBLANES, :] = carry_s[:, cols]
        abuf[SUBLANES:SUBLANES + tm, :] = a
        carry_s[:, cols] = a[tm - SUBLANES:tm, :]
        conv = (cb_ref[:, cols] + cw_ref[2:3, cols] * a
                + cw_ref[1:2, cols] * abuf[pl.ds(SUBLANES - 1, tm), :]
                + cw_ref[0:1, cols] * abuf[pl.ds(SUBLANES - 2, tm), :])
        act_s[:, cols] = (jax.nn.gelu(conv) * lin).astype(BF16)
    o_ref[...] = x1 + _dot(act_s[...], w2_ref[...])


def _ffn(x, ya, yb, wo, g, w_in, cw, cb, w2, seq):
    T = x.shape[0]
    tm, tf = TM_FFN, TF_FFN
    half = ya.shape[1]
    row = lambda i: (i, 0)
    resident = lambda shape: pl.BlockSpec(shape, lambda i: (0, 0), pipeline_mode=pl.Buffered(1))
    return pl.pallas_call(
        functools.partial(_ffn_kernel, tiles_per_seq=seq // tm),
        grid=(T // tm,),
        in_specs=[pl.BlockSpec((tm, D_MODEL), row),
                  pl.BlockSpec((tm, half), row),
                  pl.BlockSpec((tm, half), row),
                  resident((2 * half, D_MODEL)),
                  resident((1, D_MODEL)),
                  resident((D_MODEL, 2 * D_FF)),
                  resident((3, D_FF)),
                  resident((1, D_FF)),
                  resident((D_FF, D_MODEL))],
        out_specs=pl.BlockSpec((tm, D_MODEL), row),
        out_shape=jax.ShapeDtypeStruct((T, D_MODEL), F32),
        scratch_shapes=[pltpu.VMEM((tm, D_FF), BF16),
                        pltpu.VMEM((SUBLANES + tm, tf), F32),
                        pltpu.VMEM((SUBLANES, D_FF), F32)],
        compiler_params=_cparams("arbitrary"),
        name="outproj_conv_mlp",
    )(x, ya, yb, wo, g, w_in, cw, cb, w2)


def _odd_in_kernel(x_ref, g_ref, w_ref, o_ref):
    h = _rms(x_ref[...], g_ref[...])
    o_ref[...] = _dot(h.astype(BF16), w_ref[...])


def _odd_in(x, g, w):
    T = x.shape[0]
    tm = TM_PROJ
    ncol = w.shape[1]
    return pl.pallas_call(
        _odd_in_kernel,
        grid=(T // tm,),
        in_specs=[pl.BlockSpec((tm, D_MODEL), lambda i: (i, 0)),
                  pl.BlockSpec((1, D_MODEL), lambda i: (0, 0)),
                  pl.BlockSpec((D_MODEL, ncol), lambda i: (0, 0))],
        out_specs=pl.BlockSpec((tm, ncol), lambda i: (i, 0)),
        out_shape=jax.ShapeDtypeStruct((T, ncol), F32),
        compiler_params=_cparams("arbitrary"),
        name="odd_in_proj",
    )(x, g, w)


def _s5_kernel(u_ref, bm_ref, pr_ref, pi_ref, qr_ref, qi_ref, tri_ref, cre_ref, cim_ref,
               d_ref, gw_ref, gb_ref, o_ref, sr, si, ys_s):
    tl = u_ref.shape[0]
    L = L_S5
    slab_states = S5_NSTATE // (S5_WIDTH // LANES)

    @pl.when(pl.program_id(1) == 0)
    def _():
        sr[...] = jnp.zeros_like(sr)
        si[...] = jnp.zeros_like(si)

    tri = tri_ref[...]
    for c in range(tl // L):
        rows = slice(c * L, (c + 1) * L)
        for s in range(S5_WIDTH // LANES):
            cols = slice(s * slab_states, (s + 1) * slab_states)
            u = u_ref[rows, s * LANES:(s + 1) * LANES]
            bu = _dot(u.astype(BF16), bm_ref[s])
            bur, bui = bu[:, :slab_states], bu[:, slab_states:]
            qr, qi = qr_ref[:, cols], qi_ref[:, cols]
            zr = bur * qr - bui * qi
            zi = bur * qi + bui * qr
            zr_hi, zr_lo = _split_bf16(zr)
            zi_hi, zi_lo = _split_bf16(zi)
            wr = _dot(tri, zr_hi) + _dot(tri, zr_lo) + sr[0:1, cols]
            wi = _dot(tri, zi_hi) + _dot(tri, zi_lo) + si[0:1, cols]
            pr, pi = pr_ref[:, cols], pi_ref[:, cols]
            xr = pr * wr - pi * wi
            xi = pr * wi + pi * wr
            ar, ai = pr_ref[1:2, cols], pi_ref[1:2, cols]
            lr, li = xr[L - 1:L, :], xi[L - 1:L, :]
            sr[0:1, cols] = ar * lr - ai * li
            si[0:1, cols] = ar * li + ai * lr
            y = _dot(xr.astype(BF16), cre_ref[s]) + _dot(xi.astype(BF16), cim_ref[s])
            y = y + d_ref[:, s * LANES:(s + 1) * LANES] * u
            ys_s[rows, s * LANES:(s + 1) * LANES] = jax.nn.gelu(y)
    ys = ys_s[...]
    o_ref[...] = (ys * jax.nn.sigmoid(_dot(ys.astype(BF16), gw_ref[...]) + gb_ref[...])).astype(BF16)


def _s5(proj, bm, pr, pi, qr, qi, tri, cre, cim, d, gw, gb, batch, seq):
    T = proj.shape[0]
    tl = TL_S5
    n_seq = seq // tl
    c2 = lambda b, i: (0, 0)
    c3 = lambda b, i: (0, 0, 0)
    return pl.pallas_call(
        _s5_kernel,
        grid=(batch, n_seq),
        in_specs=[pl.BlockSpec((tl, S5_WIDTH), lambda b, i: (b * n_seq + i, 0)),
                  pl.BlockSpec(bm.shape, c3),
                  pl.BlockSpec(pr.shape, c2), pl.BlockSpec(pi.shape, c2),
                  pl.BlockSpec(qr.shape, c2), pl.BlockSpec(qi.shape, c2),
                  pl.BlockSpec(tri.shape, c2),
                  pl.BlockSpec(cre.shape, c3), pl.BlockSpec(cim.shape, c3),
                  pl.BlockSpec((1, S5_WIDTH), c2),
                  pl.BlockSpec((S5_WIDTH, S5_WIDTH), c2),
                  pl.BlockSpec((1, S5_WIDTH), c2)],
        out_specs=pl.BlockSpec((tl, S5_WIDTH), lambda b, i: (b * n_seq + i, 0)),
        out_shape=jax.ShapeDtypeStruct((T, S5_WIDTH), BF16),
        scratch_shapes=[pltpu.VMEM((SUBLANES, S5_NSTATE), F32),
                        pltpu.VMEM((SUBLANES, S5_NSTATE), F32),
                        pltpu.VMEM((tl, S5_WIDTH), F32)],
        compiler_params=_cparams("arbitrary", "arbitrary"),
        name="s5_glu",
    )(proj, bm, pr, pi, qr, qi, tri, cre, cim, d, gw, gb)


def _s5_params(lam_re, lam_im, b_re, b_im, c_re, c_im, log_step):
    G, N, P = S5_GROUPS, S5_STATE, S5_GROUP
    gs = LANES // P
    ns = S5_WIDTH // LANES
    step = jnp.exp(log_step.astype(F32))[:, None]
    lr, li = lam_re.astype(F32), lam_im.astype(F32)
    mag = jnp.exp(lr * step)
    ar, ai = mag * jnp.cos(li * step), mag * jnp.sin(li * step)
    den = lr * lr + li * li
    cr = ((ar - 1.0) * lr + ai * li) / den
    ci = (ai * lr - (ar - 1.0) * li) / den
    bbr = cr[..., None] * b_re - ci[..., None] * b_im
    bbi = cr[..., None] * b_im + ci[..., None] * b_re
    eye = jnp.eye(gs, dtype=F32)

    def in_blockdiag(t):
        t = t.reshape(ns, gs, N, P).transpose(0, 1, 3, 2)
        return jnp.einsum('ab,sapn->sapbn', eye, t).reshape(ns, gs * P, gs * N)

    def out_blockdiag(t):
        t = t.reshape(ns, gs, P, N).transpose(0, 1, 3, 2)
        return jnp.einsum('ab,sanp->sanbp', eye, t).reshape(ns, gs * N, gs * P)

    bm = jnp.concatenate([in_blockdiag(bbr), in_blockdiag(bbi)], axis=-1).astype(BF16)
    cre = out_blockdiag(c_re).astype(BF16)
    cim = out_blockdiag(-c_im).astype(BF16)
    j = jnp.arange(L_S5, dtype=F32)[:, None]
    la = (lr * step).reshape(1, G * N)
    th = (li * step).reshape(1, G * N)
    pmag, qmag = jnp.exp(j * la), jnp.exp(-(j * la))
    cs, sn = jnp.cos(j * th), jnp.sin(j * th)
    return bm, pmag * cs, pmag * sn, qmag * cs, -(qmag * sn), cre, cim


def _gla_kernel(q_ref, k_ref, v_ref, g_ref, gk_ref, gw_ref, gb_ref, nrm_ref, tri_ref, o_ref, st):
    tl = q_ref.shape[0]
    C = C_GLA
    pair = LANES // GLA_DK

    @pl.when(pl.program_id(1) == 0)
    def _():
        st[...] = jnp.zeros_like(st)

    z = jnp.dot(gk_ref[...], gw_ref[...], preferred_element_type=F32,
                precision=lax.Precision.HIGHEST) + gb_ref[...]
    log_a = _log_sigmoid(z) * (1.0 / GLA_TAU)
    tri = tri_ref[...]
    lane = lax.broadcasted_iota(jnp.int32, (1, LANES), 1)
    rc = lax.broadcasted_iota(jnp.int32, (C, C), 0)
    cc = lax.broadcasted_iota(jnp.int32, (C, C), 1)
    causal = rc >= cc
    for c in range(tl // C):
        rows = slice(c * C, (c + 1) * C)
        for hp in range(GLA_HEADS // pair):
            cols = slice(hp * LANES, (hp + 1) * LANES)
            bc = jnp.dot(tri, log_a[rows, cols], preferred_element_type=F32,
                         precision=lax.Precision.HIGHEST)
            b_last = bc[C - 1:C, :]
            q_dec = q_ref[rows, cols] * (GLA_DK ** -0.5) * jnp.exp(bc)
            k = k_ref[rows, cols]
            k_inv = (k * jnp.exp(-bc)).astype(BF16)
            k_dec = k * jnp.exp(b_last - bc)
            decay = jnp.exp(b_last)
            for hh in range(pair):
                h = hp * pair + hh
                in_head = (lane >= hh * GLA_DK) & (lane < (hh + 1) * GLA_DK)
                qd = jnp.where(in_head, q_dec, 0.0).astype(BF16)
                kd = jnp.where(in_head, k_dec, 0.0).astype(BF16)
                vh = v_ref[rows, h * GLA_DV:(h + 1) * GLA_DV].astype(BF16)
                att = jnp.where(causal, _dot_nt(qd, k_inv), 0.0)
                s_prev = st[h]
                o = _dot(att.astype(BF16), vh) + _dot_nt(qd, s_prev.astype(BF16))
                st[h] = s_prev * decay + _dot_tn(vh, kd)
                o = _rms(o, nrm_ref[...])
                gh = g_ref[rows, h * GLA_DV:(h + 1) * GLA_DV]
                o_ref[rows, h * GLA_DV:(h + 1) * GLA_DV] = (o * jax.nn.silu(gh)).astype(BF16)


def _gla(proj, gw, gb, nrm, tri, batch, seq):
    T = proj.shape[0]
    tl = TL_GLA
    n_seq = seq // tl
    hk = GLA_HEADS * GLA_DK
    hv = GLA_HEADS * GLA_DV
    c2 = lambda b, i: (0, 0)
    q0 = S5_WIDTH // hk
    v0 = (S5_WIDTH + 2 * hk) // hv
    gk0 = (S5_WIDTH + 2 * hk + 2 * hv) // GK_PAD
    return pl.pallas_call(
        _gla_kernel,
        grid=(batch, n_seq),
        in_specs=[pl.BlockSpec((tl, hk), lambda b, i: (b * n_seq + i, q0)),
                  pl.BlockSpec((tl, hk), lambda b, i: (b * n_seq + i, q0 + 1)),
                  pl.BlockSpec((tl, hv), lambda b, i: (b * n_seq + i, v0)),
                  pl.BlockSpec((tl, hv), lambda b, i: (b * n_seq + i, v0 + 1)),
                  pl.BlockSpec((tl, GK_PAD), lambda b, i: (b * n_seq + i, gk0)),
                  pl.BlockSpec((GK_PAD, hk), c2),
                  pl.BlockSpec((1, hk), c2),
                  pl.BlockSpec((1, GLA_DV), c2),
                  pl.BlockSpec((C_GLA, C_GLA), c2)],
        out_specs=pl.BlockSpec((tl, hv), lambda b, i: (b * n_seq + i, 0)),
        out_shape=jax.ShapeDtypeStruct((T, hv), BF16),
        scratch_shapes=[pltpu.VMEM((GLA_HEADS, GLA_DV, LANES), F32)],
        compiler_params=_cparams("arbitrary", "arbitrary"),
        name="gla",
    )(proj, proj, proj, proj, proj, gw, gb, nrm, tri)


def _block_diag(w):
    nb, a, b = w.shape
    return jnp.einsum('hk,hij->hikj', jnp.eye(nb, dtype=w.dtype), w).reshape(nb * a, nb * b)


def _rope_tables(seq):
    half = ROPE_DIM // 2
    pos = jnp.arange(seq, dtype=F32)
    inv = ROPE_THETA ** (-jnp.arange(0, ROPE_DIM, 2, dtype=F32) / ROPE_DIM)
    ang = pos[:, None] * inv[None, :]
    cos, sin = jnp.cos(ang), jnp.sin(ang)
    rest = ATT_HEAD_DIM - ROPE_DIM
    ones = jnp.ones((seq, rest), F32)
    zeros = jnp.zeros((seq, rest), F32)
    zh = jnp.zeros((seq, half), F32)
    per_head = lambda parts: jnp.tile(jnp.concatenate(parts, axis=1), (1, LANES // ATT_HEAD_DIM))
    return (per_head([cos, cos, ones]), per_head([-sin, zh, zeros]), per_head([zh, sin, zeros]))


def kernel(x, e_norm, e_w_in, e_conv_w, e_conv_b, e_gate_a_w, e_gate_a_b, e_gate_x_w, e_gate_x_b, e_lambda, e_q_norm, e_k_norm, e_w_out, o_norm, o_w_in, o_lambda_re, o_lambda_im, o_b_re, o_b_im, o_c_re, o_c_im, o_d, o_log_step, o_glu_w, o_glu_b, o_gk_w, o_gk_b, o_gla_norm, o_w_out, f_norm, f_w_in, f_conv_w, f_conv_b, f_w_out):
    B, S, D = x.shape
    T = B * S
    depth = f_norm.shape[0]
    row = lambda t: t.reshape(1, -1).astype(F32)
    xt = x.reshape(T, D)

    cos_t, s1_t, s2_t = _rope_tables(S)
    head_seg = jnp.asarray(np.kron(np.eye(LANES // ATT_HEAD_DIM), np.ones((ATT_HEAD_DIM, ATT_HEAD_DIM))), BF16)
    att_bias = jnp.asarray(_attention_bias(TQ_ATT, TK_ATT))
    tri_s5 = jnp.asarray(np.tril(np.ones((L_S5, L_S5))), BF16)
    tri_gla = jnp.asarray(np.tril(np.ones((C_GLA, C_GLA))), F32)
    two_heads = lambda t: jnp.tile(row(t), (1, LANES // ATT_HEAD_DIM))

    for layer in range(depth):
        i = layer // 2
        if layer % 2 == 0:
            xg, q, k, v = _even_in(xt, row(e_norm[i]), e_w_in[i].astype(BF16),
                                   two_heads(e_q_norm[i]), two_heads(e_k_norm[i]),
                                   head_seg, cos_t, s1_t, s2_t, S)
            ya = _lru(xg, e_conv_w[i], row(e_conv_b[i]),
                      _block_diag(e_gate_a_w[i]).astype(BF16), row(e_gate_a_b[i]),
                      _block_diag(e_gate_x_w[i]).astype(BF16), row(e_gate_x_b[i]),
                      row(e_lambda[i]), B, S)
            yb = _attention(q.reshape(B, S, ATT_WIDTH), k.reshape(B, S, ATT_WIDTH),
                            v.reshape(B, S, ATT_WIDTH), att_bias).reshape(T, ATT_WIDTH)
            w_out = e_w_out[i]
        else:
            w_in = jnp.pad(o_w_in[i], ((0, 0), (0, GK_PAD - GLA_LOWRANK))).astype(BF16)
            proj = _odd_in(xt, row(o_norm[i]), w_in)
            bm, pr, pi, qr, qi, cre, cim = _s5_params(o_lambda_re[i], o_lambda_im[i], o_b_re[i], o_b_im[i],
                                                     o_c_re[i], o_c_im[i], o_log_step[i])
            ya = _s5(proj, bm, pr, pi, qr, qi, tri_s5, cre, cim, row(o_d[i]),
                     o_glu_w[i].astype(BF16), row(o_glu_b[i]), B, S)
            gk_w = jnp.pad(o_gk_w[i], ((0, GK_PAD - GLA_LOWRANK), (0, 0)))
            yb = _gla(proj, gk_w, row(o_gk_b[i]), row(o_gla_norm[i]), tri_gla, B, S)
            w_out = o_w_out[i]
        xt = _ffn(xt, ya, yb, w_out.astype(BF16), row(f_norm[layer]), f_w_in[layer].astype(BF16),
                  f_conv_w[layer], row(f_conv_b[layer]), f_w_out[layer].astype(BF16), S)
    return xt.reshape(B, S, D)
```

```python
import functools
import math

import numpy as np
import jax
import jax.numpy as jnp
from jax import lax
from jax.experimental import pallas as pl
from jax.experimental.pallas import tpu as pltpu

F32 = jnp.float32
BF16 = jnp.bfloat16

D_MODEL = 1024
LRU_WIDTH = 512
LRU_BLOCKS = 8
LRU_CONV = 4
LRU_C = 8.0
ATT_HEADS = 8
ATT_HEAD_DIM = 64
ATT_WIDTH = 512
DILATED_PATTERNS = ((128, 1), (512, 4), (2048, 16))
ATT_SPAN = 2048
ROPE_THETA = 500000.0
ROPE_DIM = 16
S5_WIDTH = 512
S5_GROUP = 16
S5_GROUPS = 32
S5_STATE = 64
S5_NSTATE = S5_GROUPS * S5_STATE
GLA_HEADS = 4
GLA_DK = 64
GLA_DV = 128
GLA_LOWRANK = 16
GLA_TAU = 16.0
D_FF = 3 * D_MODEL
EPS = 1e-6
NEG_INF = -1e30

LANES = 128
SUBLANES = 8
VMEM_LIMIT = 56 * 1024 * 1024

TM_PROJ = 512
TM_FFN = 512
TF_FFN = 512
TL_LRU = 256
TL_S5 = 512
L_S5 = 128
TL_GLA = 512
C_GLA = 64
GK_PAD = 128


def _cparams(*sem):
    return pltpu.CompilerParams(dimension_semantics=sem, vmem_limit_bytes=VMEM_LIMIT)


def _rms(x, g):
    return x * lax.rsqrt(jnp.mean(x * x, axis=-1, keepdims=True) + EPS) * g


def _log_sigmoid(x):
    return -(jnp.maximum(-x, 0.0) + jnp.log1p(jnp.exp(-jnp.abs(x))))


def _split_bf16(x):
    hi = x.astype(BF16)
    lo = (x - hi.astype(F32)).astype(BF16)
    return hi, lo


def _dot(a, b):
    return jnp.dot(a, b, preferred_element_type=F32)


def _dot_nt(a, b):
    return lax.dot_general(a, b, (((1,), (1,)), ((), ())), preferred_element_type=F32)


def _dot_tn(a, b):
    return lax.dot_general(a, b, (((0,), (0,)), ((), ())), preferred_element_type=F32)


def _even_in_kernel(x_ref, g_ref, w_ref, qn_ref, kn_ref, seg_ref, cos_ref, s1_ref, s2_ref,
                    xg_ref, q_ref, k_ref, v_ref):
    h = _rms(x_ref[...], g_ref[...])
    y = _dot(h.astype(BF16), w_ref[...])
    xg_ref[...] = y[:, :2 * LRU_WIDTH]
    v_ref[...] = y[:, 2 * LRU_WIDTH + 2 * ATT_WIDTH:]
    seg = seg_ref[...]
    cos, s1, s2 = cos_ref[...], s1_ref[...], s2_ref[...]
    half = ROPE_DIM // 2
    for off, n_ref, dst, scale in ((2 * LRU_WIDTH, qn_ref, q_ref, ATT_HEAD_DIM ** -0.5),
                                   (2 * LRU_WIDTH + ATT_WIDTH, kn_ref, k_ref, 1.0)):
        for c in range(ATT_WIDTH // LANES):
            t = y[:, off + c * LANES: off + (c + 1) * LANES]
            hi, lo = _split_bf16(t * t)
            ms = (_dot(hi, seg) + _dot(lo, seg)) * (1.0 / ATT_HEAD_DIM)
            tn = t * lax.rsqrt(ms + EPS) * n_ref[...]
            r = (tn * cos + pltpu.roll(tn, LANES - half, 1) * s1 + pltpu.roll(tn, half, 1) * s2)
            dst[:, c * LANES:(c + 1) * LANES] = r * scale


def _even_in(x, g, w, qn, kn, seg, cos_t, s1_t, s2_t, seq):
    T = x.shape[0]
    tm = TM_PROJ
    n_seq = seq // tm
    ncol = w.shape[1]
    full = lambda i: (0, 0)
    tab = lambda i: (i % n_seq, 0)
    row = lambda i: (i, 0)
    return pl.pallas_call(
        _even_in_kernel,
        grid=(T // tm,),
        in_specs=[pl.BlockSpec((tm, D_MODEL), row),
                  pl.BlockSpec((1, D_MODEL), full),
                  pl.BlockSpec((D_MODEL, ncol), full),
                  pl.BlockSpec((1, LANES), full),
                  pl.BlockSpec((1, LANES), full),
                  pl.BlockSpec((LANES, LANES), full),
                  pl.BlockSpec((tm, LANES), tab),
                  pl.BlockSpec((tm, LANES), tab),
                  pl.BlockSpec((tm, LANES), tab)],
        out_specs=[pl.BlockSpec((tm, 2 * LRU_WIDTH), row),
                   pl.BlockSpec((tm, ATT_WIDTH), row),
                   pl.BlockSpec((tm, ATT_WIDTH), row),
                   pl.BlockSpec((tm, ATT_WIDTH), row)],
        out_shape=[jax.ShapeDtypeStruct((T, 2 * LRU_WIDTH), F32),
                   jax.ShapeDtypeStruct((T, ATT_WIDTH), F32),
                   jax.ShapeDtypeStruct((T, ATT_WIDTH), F32),
                   jax.ShapeDtypeStruct((T, ATT_WIDTH), F32)],
        compiler_params=_cparams("arbitrary"),
        name="even_in_proj",
    )(x, g, w, qn, kn, seg, cos_t, s1_t, s2_t)


def _lru_kernel(xl_ref, gl_ref, cw_ref, cb_ref, wa_ref, ba_ref, wx_ref, bx_ref, lam_ref,
                o_ref, xbuf, hprev):
    tl = xl_ref.shape[0]

    @pl.when(pl.program_id(1) == 0)
    def _():
        xbuf[0:SUBLANES, :] = jnp.zeros((SUBLANES, LRU_WIDTH), F32)
        hprev[...] = jnp.zeros_like(hprev)

    x = xl_ref[...]
    xbuf[SUBLANES:SUBLANES + tl, :] = x
    conv = cb_ref[...] + cw_ref[LRU_CONV - 1:LRU_CONV, :] * x
    for j in range(LRU_CONV - 1):
        conv = conv + cw_ref[j:j + 1, :] * xbuf[pl.ds(SUBLANES - (LRU_CONV - 1) + j, tl), :]
    xbuf[0:SUBLANES, :] = x[tl - SUBLANES:tl, :]

    c16 = conv.astype(BF16)
    r = jax.nn.sigmoid(_dot(c16, wa_ref[...]) + ba_ref[...])
    ig = jax.nn.sigmoid(_dot(c16, wx_ref[...]) + bx_ref[...])
    log_a = (LRU_C * r) * _log_sigmoid(lam_ref[...])
    a = jnp.exp(log_a)
    b = jnp.sqrt(1.0 - a * a) * (ig * conv)

    rows = lax.broadcasted_iota(jnp.int32, (tl, 1), 0)
    d = 1
    while d < tl:
        keep = rows >= d
        a_sh = jnp.where(keep, pltpu.roll(a, d, 0), 1.0)
        b_sh = jnp.where(keep, pltpu.roll(b, d, 0), 0.0)
        b = a * b_sh + b
        a = a * a_sh
        d *= 2
    h = b + a * hprev[0:1, :]
    hprev[0:1, :] = h[tl - 1:tl, :]
    o_ref[...] = (h * jax.nn.gelu(gl_ref[...])).astype(BF16)


def _lru(xg, cw, cb, wa, ba, wx, bx, lam, batch, seq):
    T = xg.shape[0]
    tl = TL_LRU
    n_seq = seq // tl
    full = lambda b, i: (0, 0)
    return pl.pallas_call(
        _lru_kernel,
        grid=(batch, n_seq),
        in_specs=[pl.BlockSpec((tl, LRU_WIDTH), lambda b, i: (b * n_seq + i, 0)),
                  pl.BlockSpec((tl, LRU_WIDTH), lambda b, i: (b * n_seq + i, 1)),
                  pl.BlockSpec((LRU_CONV, LRU_WIDTH), full),
                  pl.BlockSpec((1, LRU_WIDTH), full),
                  pl.BlockSpec((LRU_WIDTH, LRU_WIDTH), full),
                  pl.BlockSpec((1, LRU_WIDTH), full),
                  pl.BlockSpec((LRU_WIDTH, LRU_WIDTH), full),
                  pl.BlockSpec((1, LRU_WIDTH), full),
                  pl.BlockSpec((1, LRU_WIDTH), full)],
        out_specs=pl.BlockSpec((tl, LRU_WIDTH), lambda b, i: (b * n_seq + i, 0)),
        out_shape=jax.ShapeDtypeStruct((T, LRU_WIDTH), BF16),
        scratch_shapes=[pltpu.VMEM((SUBLANES + tl, LRU_WIDTH), F32),
                        pltpu.VMEM((SUBLANES, LRU_WIDTH), F32)],
        compiler_params=_cparams("arbitrary", "arbitrary"),
        name="rg_lru",
    )(xg, xg, cw, cb, wa, ba, wx, bx, lam)


N_BACK = DILATED_PATTERNS[0][0] // DILATED_PATTERNS[0][1]
Q_BLOCKS = ATT_SPAN // N_BACK


def _attention_bias():
    qi = np.arange(N_BACK)[:, None]
    ki = np.arange(2 * N_BACK)[None, :]
    dist = N_BACK + qi - ki
    band = (dist >= 0) & (dist <= N_BACK)
    first = band & (ki >= N_BACK)
    return np.where(np.stack([band, first]), 0.0, NEG_INF).astype(np.float32)


def _attn_kernel(q_ref, k_ref, v_ref, bias_ref, o_ref, *scratch):
    kv_s = scratch[:6]
    o_s, m_s, l_s = scratch[6:]
    sb = pl.program_id(2)
    lane = lax.broadcasted_iota(jnp.int32, (1, LANES), 1)
    head0 = lane < ATT_HEAD_DIM

    for p, (window, dil) in enumerate(DILATED_PATTERNS):
        per_res = Q_BLOCKS // dil
        span = N_BACK * per_res
        for src, dst in ((k_ref, kv_s[2 * p]), (v_ref, kv_s[2 * p + 1])):
            @pl.when(sb == 0)
            def _():
                dst[:, 0:N_BACK, :] = jnp.zeros((dil, N_BACK, LANES), BF16)

            @pl.when(sb > 0)
            def _():
                dst[:, 0:N_BACK, :] = dst[:, span:span + N_BACK, :]

            for r in range(dil):
                rows = pl.ds(r, span, stride=dil) if dil > 1 else pl.ds(0, span)
                dst[r, N_BACK:N_BACK + span, :] = src[0, rows, :].astype(BF16)

    for p, (window, dil) in enumerate(DILATED_PATTERNS):
        per_res = Q_BLOCKS // dil
        k_s, v_s = kv_s[2 * p], kv_s[2 * p + 1]

        for n in range(Q_BLOCKS):
            r, m = n % dil, n // dil
            if dil > 1:
                rows = pl.ds(m * (N_BACK * dil) + r, N_BACK, stride=dil)
            else:
                rows = pl.ds(m * N_BACK, N_BACK)
            q = q_ref[0, rows, :].astype(BF16)
            kc = k_s[r, m * N_BACK:(m + 2) * N_BACK, :]
            vc = v_s[r, m * N_BACK:(m + 2) * N_BACK, :]
            bias = bias_ref[jnp.where(sb == 0, 1, 0)] if m == 0 else bias_ref[0]
            res = []
            for h in range(LANES // ATT_HEAD_DIM):
                qm = jnp.where(head0 if h == 0 else ~head0, q, jnp.zeros_like(q))
                s = _dot_nt(qm, kc) + bias
                mx = jnp.max(s, axis=-1, keepdims=True)
                e = jnp.exp(s - mx)
                res.append((_dot(e.astype(BF16), vc), mx, jnp.sum(e, axis=-1, keepdims=True)))
            for dst, idx in ((o_s, 0), (m_s, 1), (l_s, 2)):
                dst[p, rows, :] = jnp.where(head0, res[0][idx], res[1][idx])

    mx = jnp.maximum(jnp.maximum(m_s[0], m_s[1]), m_s[2])
    num = jnp.zeros_like(mx)
    den = jnp.zeros_like(mx)
    for p in range(len(DILATED_PATTERNS)):
        w = jnp.exp(m_s[p] - mx)
        num = num + w * o_s[p]
        den = den + w * l_s[p]
    o_ref[0] = (num / den).astype(BF16)


def _attention(q, k, v, bias):
    B, S, W = q.shape
    blk = pl.BlockSpec((1, ATT_SPAN, LANES), lambda b, p, i: (b, i, p))
    kv_scratch = []
    for window, dil in DILATED_PATTERNS:
        shape = (dil, N_BACK * (1 + Q_BLOCKS // dil), LANES)
        kv_scratch += [pltpu.VMEM(shape, BF16), pltpu.VMEM(shape, BF16)]
    acc = pltpu.VMEM((len(DILATED_PATTERNS), ATT_SPAN, LANES), F32)
    return pl.pallas_call(
        _attn_kernel,
        grid=(B, W // LANES, S // ATT_SPAN),
        in_specs=[blk, blk, blk, pl.BlockSpec(bias.shape, lambda b, p, i: (0, 0, 0))],
        out_specs=blk,
        out_shape=jax.ShapeDtypeStruct((B, S, W), BF16),
        scratch_shapes=kv_scratch + [acc, acc, acc],
        compiler_params=_cparams("arbitrary", "arbitrary", "arbitrary"),
        name="dilated_attention",
    )(q, k, v, bias)


def _ffn_kernel(x_ref, ya_ref, yb_ref, wo_ref, g_ref, w1_ref, cw_ref, cb_ref, w2_ref,
                o_ref, act_s, abuf, carry_s, *, tiles_per_seq):
    i = pl.program_id(0)
    tm = x_ref.shape[0]
    half = ya_ref.shape[1]
    tf = abuf.shape[1]

    @pl.when(i % tiles_per_seq == 0)
    def _():
        carry_s[...] = jnp.zeros_like(carry_s)

    x1 = (x_ref[...] + _dot(ya_ref[...], wo_ref[0:half, :])
          + _dot(yb_ref[...], wo_ref[half:2 * half, :]))
    h = _rms(x1, g_ref[...]).astype(BF16)
    for c in range(D_FF // tf):
        cols = slice(c * tf, (c + 1) * tf)
        a = _dot(h, w1_ref[:, cols])
        lin = _dot(h, w1_ref[:, D_FF + c * tf:D_FF + (c + 1) * tf])
        abuf[0:SUBLANES, :] = carry_s[:, cols]
        abuf[SUBLANES:SUBLANES + tm, :] = a
        carry_s[:, cols] = a[tm - SUBLANES:tm, :]
        conv = (cb_ref[:, cols] + cw_ref[2:3, cols] * a
                + cw_ref[1:2, cols] * abuf[pl.ds(SUBLANES - 1, tm), :]
                + cw_ref[0:1, cols] * abuf[pl.ds(SUBLANES - 2, tm), :])
        act_s[:, cols] = (jax.nn.gelu(conv) * lin).astype(BF16)
    o_ref[...] = x1 + _dot(act_s[...], w2_ref[...])


def _ffn(x, ya, yb, wo, g, w_in, cw, cb, w2, seq):
    T = x.shape[0]
    tm, tf = TM_FFN, TF_FFN
    half = ya.shape[1]
    row = lambda i: (i, 0)
    resident = lambda shape: pl.BlockSpec(shape, lambda i: (0, 0), pipeline_mode=pl.Buffered(1))
    return pl.pallas_call(
        functools.partial(_ffn_kernel, tiles_per_seq=seq // tm),
        grid=(T // tm,),
        in_specs=[pl.BlockSpec((tm, D_MODEL), row),
                  pl.BlockSpec((tm, half), row),
                  pl.BlockSpec((tm, half), row),
                  resident((2 * half, D_MODEL)),
                  resident((1, D_MODEL)),
                  resident((D_MODEL, 2 * D_FF)),
                  resident((3, D_FF)),
                  resident((1, D_FF)),
                  resident((D_FF, D_MODEL))],
        out_specs=pl.BlockSpec((tm, D_MODEL), row),
        out_shape=jax.ShapeDtypeStruct((T, D_MODEL), F32),
        scratch_shapes=[pltpu.VMEM((tm, D_FF), BF16),
                        pltpu.VMEM((SUBLANES + tm, tf), F32),
                        pltpu.VMEM((SUBLANES, D_FF), F32)],
        compiler_params=_cparams("arbitrary"),
        name="outproj_conv_mlp",
    )(x, ya, yb, wo, g, w_in, cw, cb, w2)


def _odd_in_kernel(x_ref, g_ref, w_ref, o_ref):
    h = _rms(x_ref[...], g_ref[...])
    o_ref[...] = _dot(h.astype(BF16), w_ref[...])


def _odd_in(x, g, w):
    T = x.shape[0]
    tm = TM_PROJ
    ncol = w.shape[1]
    return pl.pallas_call(
        _odd_in_kernel,
        grid=(T // tm,),
        in_specs=[pl.BlockSpec((tm, D_MODEL), lambda i: (i, 0)),
                  pl.BlockSpec((1, D_MODEL), lambda i: (0, 0)),
                  pl.BlockSpec((D_MODEL, ncol), lambda i: (0, 0))],
        out_specs=pl.BlockSpec((tm, ncol), lambda i: (i, 0)),
        out_shape=jax.ShapeDtypeStruct((T, ncol), F32),
        compiler_params=_cparams("arbitrary"),
        name="odd_in_proj",
    )(x, g, w)


def _s5_kernel(u_ref, bm_ref, pr_ref, pi_ref, qr_ref, qi_ref, tri_ref, cre_ref, cim_ref,
               d_ref, gw_ref, gb_ref, o_ref, sr, si, ys_s):
    tl = u_ref.shape[0]
    L = L_S5
    slab_states = S5_NSTATE // (S5_WIDTH // LANES)

    @pl.when(pl.program_id(1) == 0)
    def _():
        sr[...] = jnp.zeros_like(sr)
        si[...] = jnp.zeros_like(si)

    tri = tri_ref[...]
    for c in range(tl // L):
        rows = slice(c * L, (c + 1) * L)
        for s in range(S5_WIDTH // LANES):
            cols = slice(s * slab_states, (s + 1) * slab_states)
            u = u_ref[rows, s * LANES:(s + 1) * LANES]
            bu = _dot(u.astype(BF16), bm_ref[s])
            bur, bui = bu[:, :slab_states], bu[:, slab_states:]
            qr, qi = qr_ref[:, cols], qi_ref[:, cols]
            zr = bur * qr - bui * qi
            zi = bur * qi + bui * qr
            zr_hi, zr_lo = _split_bf16(zr)
            zi_hi, zi_lo = _split_bf16(zi)
            wr = _dot(tri, zr_hi) + _dot(tri, zr_lo) + sr[0:1, cols]
            wi = _dot(tri, zi_hi) + _dot(tri, zi_lo) + si[0:1, cols]
            pr, pi = pr_ref[:, cols], pi_ref[:, cols]
            xr = pr * wr - pi * wi
            xi = pr * wi + pi * wr
            ar, ai = pr_ref[1:2, cols], pi_ref[1:2, cols]
            lr, li = xr[L - 1:L, :], xi[L - 1:L, :]
            sr[0:1, cols] = ar * lr - ai * li
            si[0:1, cols] = ar * li + ai * lr
            y = _dot(xr.astype(BF16), cre_ref[s]) + _dot(xi.astype(BF16), cim_ref[s])
            y = y + d_ref[:, s * LANES:(s + 1) * LANES] * u
            ys_s[rows, s * LANES:(s + 1) * LANES] = jax.nn.gelu(y)
    ys = ys_s[...]
    o_ref[...] = (ys * jax.nn.sigmoid(_dot(ys.astype(BF16), gw_ref[...]) + gb_ref[...])).astype(BF16)


def _s5(proj, bm, pr, pi, qr, qi, tri, cre, cim, d, gw, gb, batch, seq):
    T = proj.shape[0]
    tl = TL_S5
    n_seq = seq // tl
    c2 = lambda b, i: (0, 0)
    c3 = lambda b, i: (0, 0, 0)
    return pl.pallas_call(
        _s5_kernel,
        grid=(batch, n_seq),
        in_specs=[pl.BlockSpec((tl, S5_WIDTH), lambda b, i: (b * n_seq + i, 0)),
                  pl.BlockSpec(bm.shape, c3),
                  pl.BlockSpec(pr.shape, c2), pl.BlockSpec(pi.shape, c2),
                  pl.BlockSpec(qr.shape, c2), pl.BlockSpec(qi.shape, c2),
                  pl.BlockSpec(tri.shape, c2),
                  pl.BlockSpec(cre.shape, c3), pl.BlockSpec(cim.shape, c3),
                  pl.BlockSpec((1, S5_WIDTH), c2),
                  pl.BlockSpec((S5_WIDTH, S5_WIDTH), c2),
                  pl.BlockSpec((1, S5_WIDTH), c2)],
        out_specs=pl.BlockSpec((tl, S5_WIDTH), lambda b, i: (b * n_seq + i, 0)),
        out_shape=jax.ShapeDtypeStruct((T, S5_WIDTH), BF16),
        scratch_shapes=[pltpu.VMEM((SUBLANES, S5_NSTATE), F32),
                        pltpu.VMEM((SUBLANES, S5_NSTATE), F32),
                        pltpu.VMEM((tl, S5_WIDTH), F32)],
        compiler_params=_cparams("arbitrary", "arbitrary"),
        name="s5_glu",
    )(proj, bm, pr, pi, qr, qi, tri, cre, cim, d, gw, gb)


def _s5_params(lam_re, lam_im, b_re, b_im, c_re, c_im, log_step):
    G, N, P = S5_GROUPS, S5_STATE, S5_GROUP
    gs = LANES // P
    ns = S5_WIDTH // LANES
    step = jnp.exp(log_step.astype(F32))[:, None]
    lr, li = lam_re.astype(F32), lam_im.astype(F32)
    mag = jnp.exp(lr * step)
    ar, ai = mag * jnp.cos(li * step), mag * jnp.sin(li * step)
    den = lr * lr + li * li
    cr = ((ar - 1.0) * lr + ai * li) / den
    ci = (ai * lr - (ar - 1.0) * li) / den
    bbr = cr[..., None] * b_re - ci[..., None] * b_im
    bbi = cr[..., None] * b_im + ci[..., None] * b_re
    eye = jnp.eye(gs, dtype=F32)

    def in_blockdiag(t):
        t = t.reshape(ns, gs, N, P).transpose(0, 1, 3, 2)
        return jnp.einsum('ab,sapn->sapbn', eye, t).reshape(ns, gs * P, gs * N)

    def out_blockdiag(t):
        t = t.reshape(ns, gs, P, N).transpose(0, 1, 3, 2)
        return jnp.einsum('ab,sanp->sanbp', eye, t).reshape(ns, gs * N, gs * P)

    bm = jnp.concatenate([in_blockdiag(bbr), in_blockdiag(bbi)], axis=-1).astype(BF16)
    cre = out_blockdiag(c_re).astype(BF16)
    cim = out_blockdiag(-c_im).astype(BF16)
    j = jnp.arange(L_S5, dtype=F32)[:, None]
    la = (lr * step).reshape(1, G * N)
    th = (li * step).reshape(1, G * N)
    pmag, qmag = jnp.exp(j * la), jnp.exp(-(j * la))
    cs, sn = jnp.cos(j * th), jnp.sin(j * th)
    return bm, pmag * cs, pmag * sn, qmag * cs, -(qmag * sn), cre, cim


def _gla_kernel(q_ref, k_ref, v_ref, g_ref, gk_ref, gw_ref, gb_ref, nrm_ref, tri_ref, o_ref, st):
    tl = q_ref.shape[0]
    C = C_GLA
    pair = LANES // GLA_DK

    @pl.when(pl.program_id(1) == 0)
    def _():
        st[...] = jnp.zeros_like(st)

    z = jnp.dot(gk_ref[...], gw_ref[...], preferred_element_type=F32,
                precision=lax.Precision.HIGHEST) + gb_ref[...]
    log_a = _log_sigmoid(z) * (1.0 / GLA_TAU)
    tri = tri_ref[...]
    lane = lax.broadcasted_iota(jnp.int32, (1, LANES), 1)
    rc = lax.broadcasted_iota(jnp.int32, (C, C), 0)
    cc = lax.broadcasted_iota(jnp.int32, (C, C), 1)
    causal = rc >= cc
    for c in range(tl // C):
        rows = slice(c * C, (c + 1) * C)
        for hp in range(GLA_HEADS // pair):
            cols = slice(hp * LANES, (hp + 1) * LANES)
            bc = jnp.dot(tri, log_a[rows, cols], preferred_element_type=F32,
                         precision=lax.Precision.HIGHEST)
            b_last = bc[C - 1:C, :]
            q_dec = q_ref[rows, cols] * (GLA_DK ** -0.5) * jnp.exp(bc)
            k = k_ref[rows, cols]
            k_inv = (k * jnp.exp(-bc)).astype(BF16)
            k_dec = k * jnp.exp(b_last - bc)
            decay = jnp.exp(b_last)
            for hh in range(pair):
                h = hp * pair + hh
                in_head = (lane >= hh * GLA_DK) & (lane < (hh + 1) * GLA_DK)
                qd = jnp.where(in_head, q_dec, 0.0).astype(BF16)
                kd = jnp.where(in_head, k_dec, 0.0).astype(BF16)
                vh = v_ref[rows, h * GLA_DV:(h + 1) * GLA_DV].astype(BF16)
                att = jnp.where(causal, _dot_nt(qd, k_inv), 0.0)
                s_prev = st[h]
                o = _dot(att.astype(BF16), vh) + _dot_nt(qd, s_prev.astype(BF16))
                st[h] = s_prev * decay + _dot_tn(vh, kd)
                o = _rms(o, nrm_ref[...])
                gh = g_ref[rows, h * GLA_DV:(h + 1) * GLA_DV]
                o_ref[rows, h * GLA_DV:(h + 1) * GLA_DV] = (o * jax.nn.silu(gh)).astype(BF16)


def _gla(proj, gw, gb, nrm, tri, batch, seq):
    T = proj.shape[0]
    tl = TL_GLA
    n_seq = seq // tl
    hk = GLA_HEADS * GLA_DK
    hv = GLA_HEADS * GLA_DV
    c2 = lambda b, i: (0, 0)
    q0 = S5_WIDTH // hk
    v0 = (S5_WIDTH + 2 * hk) // hv
    gk0 = (S5_WIDTH + 2 * hk + 2 * hv) // GK_PAD
    return pl.pallas_call(
        _gla_kernel,
        grid=(batch, n_seq),
        in_specs=[pl.BlockSpec((tl, hk), lambda b, i: (b * n_seq + i, q0)),
                  pl.BlockSpec((tl, hk), lambda b, i: (b * n_seq + i, q0 + 1)),
                  pl.BlockSpec((tl, hv), lambda b, i: (b * n_seq + i, v0)),
                  pl.BlockSpec((tl, hv), lambda b, i: (b * n_seq + i, v0 + 1)),
                  pl.BlockSpec((tl, GK_PAD), lambda b, i: (b * n_seq + i, gk0)),
                  pl.BlockSpec((GK_PAD, hk), c2),
                  pl.BlockSpec((1, hk), c2),
                  pl.BlockSpec((1, GLA_DV), c2),
                  pl.BlockSpec((C_GLA, C_GLA), c2)],
        out_specs=pl.BlockSpec((tl, hv), lambda b, i: (b * n_seq + i, 0)),
        out_shape=jax.ShapeDtypeStruct((T, hv), BF16),
        scratch_shapes=[pltpu.VMEM((GLA_HEADS, GLA_DV, LANES), F32)],
        compiler_params=_cparams("arbitrary", "arbitrary"),
        name="gla",
    )(proj, proj, proj, proj, proj, gw, gb, nrm, tri)


def _block_diag(w):
    nb, a, b = w.shape
    return jnp.einsum('hk,hij->hikj', jnp.eye(nb, dtype=w.dtype), w).reshape(nb * a, nb * b)


def _rope_tables(seq):
    half = ROPE_DIM // 2
    pos = jnp.arange(seq, dtype=F32)
    inv = ROPE_THETA ** (-jnp.arange(0, ROPE_DIM, 2, dtype=F32) / ROPE_DIM)
    ang = pos[:, None] * inv[None, :]
    cos, sin = jnp.cos(ang), jnp.sin(ang)
    rest = ATT_HEAD_DIM - ROPE_DIM
    ones = jnp.ones((seq, rest), F32)
    zeros = jnp.zeros((seq, rest), F32)
    zh = jnp.zeros((seq, half), F32)
    per_head = lambda parts: jnp.tile(jnp.concatenate(parts, axis=1), (1, LANES // ATT_HEAD_DIM))
    return (per_head([cos, cos, ones]), per_head([-sin, zh, zeros]), per_head([zh, sin, zeros]))


def kernel(x, e_norm, e_w_in, e_conv_w, e_conv_b, e_gate_a_w, e_gate_a_b, e_gate_x_w, e_gate_x_b, e_lambda, e_q_norm, e_k_norm, e_w_out, o_norm, o_w_in, o_lambda_re, o_lambda_im, o_b_re, o_b_im, o_c_re, o_c_im, o_d, o_log_step, o_glu_w, o_glu_b, o_gk_w, o_gk_b, o_gla_norm, o_w_out, f_norm, f_w_in, f_conv_w, f_conv_b, f_w_out):
    B, S, D = x.shape
    T = B * S
    depth = f_norm.shape[0]
    row = lambda t: t.reshape(1, -1).astype(F32)
    xt = x.reshape(T, D)

    cos_t, s1_t, s2_t = _rope_tables(S)
    head_seg = jnp.asarray(np.kron(np.eye(LANES // ATT_HEAD_DIM), np.ones((ATT_HEAD_DIM, ATT_HEAD_DIM))), BF16)
    att_bias = jnp.asarray(_attention_bias())
    tri_s5 = jnp.asarray(np.tril(np.ones((L_S5, L_S5))), BF16)
    tri_gla = jnp.asarray(np.tril(np.ones((C_GLA, C_GLA))), F32)
    two_heads = lambda t: jnp.tile(row(t), (1, LANES // ATT_HEAD_DIM))

    for layer in range(depth):
        i = layer // 2
        if layer % 2 == 0:
            xg, q, k, v = _even_in(xt, row(e_norm[i]), e_w_in[i].astype(BF16),
                                   two_heads(e_q_norm[i]), two_heads(e_k_norm[i]),
                                   head_seg, cos_t, s1_t, s2_t, S)
            ya = _lru(xg, e_conv_w[i], row(e_conv_b[i]),
                      _block_diag(e_gate_a_w[i]).astype(BF16), row(e_gate_a_b[i]),
                      _block_diag(e_gate_x_w[i]).astype(BF16), row(e_gate_x_b[i]),
                      row(e_lambda[i]), B, S)
            yb = _attention(q.reshape(B, S, ATT_WIDTH), k.reshape(B, S, ATT_WIDTH),
                            v.reshape(B, S, ATT_WIDTH), att_bias).reshape(T, ATT_WIDTH)
            w_out = e_w_out[i]
        else:
            w_in = jnp.pad(o_w_in[i], ((0, 0), (0, GK_PAD - GLA_LOWRANK))).astype(BF16)
            proj = _odd_in(xt, row(o_norm[i]), w_in)
            bm, pr, pi, qr, qi, cre, cim = _s5_params(o_lambda_re[i], o_lambda_im[i], o_b_re[i], o_b_im[i],
                                                     o_c_re[i], o_c_im[i], o_log_step[i])
            ya = _s5(proj, bm, pr, pi, qr, qi, tri_s5, cre, cim, row(o_d[i]),
                     o_glu_w[i].astype(BF16), row(o_glu_b[i]), B, S)
            gk_w = jnp.pad(o_gk_w[i], ((0, GK_PAD - GLA_LOWRANK), (0, 0)))
            yb = _gla(proj, gk_w, row(o_gk_b[i]), row(o_gla_norm[i]), tri_gla, B, S)
            w_out = o_w_out[i]
        xt = _ffn(xt, ya, yb, w_out.astype(BF16), row(f_norm[layer]), f_w_in[layer].astype(BF16),
                  f_conv_w[layer], row(f_conv_b[layer]), f_w_out[layer].astype(BF16), S)
    return xt.reshape(B, S, D)
```

```python
import functools
import math

import numpy as np
import jax
import jax.numpy as jnp
from jax import lax
from jax.experimental import pallas as pl
from jax.experimental.pallas import tpu as pltpu

F32 = jnp.float32
BF16 = jnp.bfloat16

D_MODEL = 1024
LRU_WIDTH = 512
LRU_BLOCKS = 8
LRU_CONV = 4
LRU_C = 8.0
ATT_HEADS = 8
ATT_HEAD_DIM = 64
ATT_WIDTH = 512
DILATED_PATTERNS = ((128, 1), (512, 4), (2048, 16))
ATT_SPAN = 2048
ROPE_THETA = 500000.0
ROPE_DIM = 16
S5_WIDTH = 512
S5_GROUP = 16
S5_GROUPS = 32
S5_STATE = 64
S5_NSTATE = S5_GROUPS * S5_STATE
GLA_HEADS = 4
GLA_DK = 64
GLA_DV = 128
GLA_LOWRANK = 16
GLA_TAU = 16.0
D_FF = 3 * D_MODEL
EPS = 1e-6
NEG_INF = -1e30

LANES = 128
SUBLANES = 8
VMEM_LIMIT = 56 * 1024 * 1024

TM_PROJ = 512
TM_FFN = 512
TF_FFN = 512
TL_LRU = 256
TL_S5 = 512
L_S5 = 128
TL_GLA = 512
C_GLA = 64
GK_PAD = 128


def _cparams(*sem):
    return pltpu.CompilerParams(dimension_semantics=sem, vmem_limit_bytes=VMEM_LIMIT)


def _rms(x, g):
    return x * lax.rsqrt(jnp.mean(x * x, axis=-1, keepdims=True) + EPS) * g


def _log_sigmoid(x):
    return -(jnp.maximum(-x, 0.0) + jnp.log1p(jnp.exp(-jnp.abs(x))))


def _split_bf16(x):
    hi = x.astype(BF16)
    lo = (x - hi.astype(F32)).astype(BF16)
    return hi, lo


def _dot(a, b):
    return jnp.dot(a, b, preferred_element_type=F32)


def _dot_nt(a, b):
    return lax.dot_general(a, b, (((1,), (1,)), ((), ())), preferred_element_type=F32)


def _dot_tn(a, b):
    return lax.dot_general(a, b, (((0,), (0,)), ((), ())), preferred_element_type=F32)


def _even_in_kernel(x_ref, g_ref, w_ref, qn_ref, kn_ref, seg_ref, cos_ref, s1_ref, s2_ref,
                    xg_ref, q_ref, k_ref, v_ref):
    h = _rms(x_ref[...], g_ref[...])
    y = _dot(h.astype(BF16), w_ref[...])
    xg_ref[...] = y[:, :2 * LRU_WIDTH]
    v_ref[...] = y[:, 2 * LRU_WIDTH + 2 * ATT_WIDTH:]
    seg = seg_ref[...]
    cos, s1, s2 = cos_ref[...], s1_ref[...], s2_ref[...]
    half = ROPE_DIM // 2
    for off, n_ref, dst, scale in ((2 * LRU_WIDTH, qn_ref, q_ref, ATT_HEAD_DIM ** -0.5),
                                   (2 * LRU_WIDTH + ATT_WIDTH, kn_ref, k_ref, 1.0)):
        for c in range(ATT_WIDTH // LANES):
            t = y[:, off + c * LANES: off + (c + 1) * LANES]
            hi, lo = _split_bf16(t * t)
            ms = (_dot(hi, seg) + _dot(lo, seg)) * (1.0 / ATT_HEAD_DIM)
            tn = t * lax.rsqrt(ms + EPS) * n_ref[...]
            r = (tn * cos + pltpu.roll(tn, LANES - half, 1) * s1 + pltpu.roll(tn, half, 1) * s2)
            dst[:, c * LANES:(c + 1) * LANES] = r * scale


def _even_in(x, g, w, qn, kn, seg, cos_t, s1_t, s2_t, seq):
    T = x.shape[0]
    tm = TM_PROJ
    n_seq = seq // tm
    ncol = w.shape[1]
    full = lambda i: (0, 0)
    tab = lambda i: (i % n_seq, 0)
    row = lambda i: (i, 0)
    return pl.pallas_call(
        _even_in_kernel,
        grid=(T // tm,),
        in_specs=[pl.BlockSpec((tm, D_MODEL), row),
                  pl.BlockSpec((1, D_MODEL), full),
                  pl.BlockSpec((D_MODEL, ncol), full),
                  pl.BlockSpec((1, LANES), full),
                  pl.BlockSpec((1, LANES), full),
                  pl.BlockSpec((LANES, LANES), full),
                  pl.BlockSpec((tm, LANES), tab),
                  pl.BlockSpec((tm, LANES), tab),
                  pl.BlockSpec((tm, LANES), tab)],
        out_specs=[pl.BlockSpec((tm, 2 * LRU_WIDTH), row),
                   pl.BlockSpec((tm, ATT_WIDTH), row),
                   pl.BlockSpec((tm, ATT_WIDTH), row),
                   pl.BlockSpec((tm, ATT_WIDTH), row)],
        out_shape=[jax.ShapeDtypeStruct((T, 2 * LRU_WIDTH), F32),
                   jax.ShapeDtypeStruct((T, ATT_WIDTH), F32),
                   jax.ShapeDtypeStruct((T, ATT_WIDTH), F32),
                   jax.ShapeDtypeStruct((T, ATT_WIDTH), F32)],
        compiler_params=_cparams("arbitrary"),
        name="even_in_proj",
    )(x, g, w, qn, kn, seg, cos_t, s1_t, s2_t)


def _lru_kernel(xl_ref, gl_ref, cw_ref, cb_ref, wa_ref, ba_ref, wx_ref, bx_ref, lam_ref,
                o_ref, xbuf, hprev):
    tl = xl_ref.shape[0]

    @pl.when(pl.program_id(1) == 0)
    def _():
        xbuf[0:SUBLANES, :] = jnp.zeros((SUBLANES, LRU_WIDTH), F32)
        hprev[...] = jnp.zeros_like(hprev)

    x = xl_ref[...]
    xbuf[SUBLANES:SUBLANES + tl, :] = x
    conv = cb_ref[...] + cw_ref[LRU_CONV - 1:LRU_CONV, :] * x
    for j in range(LRU_CONV - 1):
        conv = conv + cw_ref[j:j + 1, :] * xbuf[pl.ds(SUBLANES - (LRU_CONV - 1) + j, tl), :]
    xbuf[0:SUBLANES, :] = x[tl - SUBLANES:tl, :]

    c16 = conv.astype(BF16)
    r = jax.nn.sigmoid(_dot(c16, wa_ref[...]) + ba_ref[...])
    ig = jax.nn.sigmoid(_dot(c16, wx_ref[...]) + bx_ref[...])
    log_a = (LRU_C * r) * _log_sigmoid(lam_ref[...])
    a = jnp.exp(log_a)
    b = jnp.sqrt(1.0 - a * a) * (ig * conv)

    rows = lax.broadcasted_iota(jnp.int32, (tl, 1), 0)
    d = 1
    while d < tl:
        keep = rows >= d
        a_sh = jnp.where(keep, pltpu.roll(a, d, 0), 1.0)
        b_sh = jnp.where(keep, pltpu.roll(b, d, 0), 0.0)
        b = a * b_sh + b
        a = a * a_sh
        d *= 2
    h = b + a * hprev[0:1, :]
    hprev[0:1, :] = h[tl - 1:tl, :]
    o_ref[...] = (h * jax.nn.gelu(gl_ref[...])).astype(BF16)


def _lru(xg, cw, cb, wa, ba, wx, bx, lam, batch, seq):
    T = xg.shape[0]
    tl = TL_LRU
    n_seq = seq // tl
    full = lambda b, i: (0, 0)
    return pl.pallas_call(
        _lru_kernel,
        grid=(batch, n_seq),
        in_specs=[pl.BlockSpec((tl, LRU_WIDTH), lambda b, i: (b * n_seq + i, 0)),
                  pl.BlockSpec((tl, LRU_WIDTH), lambda b, i: (b * n_seq + i, 1)),
                  pl.BlockSpec((LRU_CONV, LRU_WIDTH), full),
                  pl.BlockSpec((1, LRU_WIDTH), full),
                  pl.BlockSpec((LRU_WIDTH, LRU_WIDTH), full),
                  pl.BlockSpec((1, LRU_WIDTH), full),
                  pl.BlockSpec((LRU_WIDTH, LRU_WIDTH), full),
                  pl.BlockSpec((1, LRU_WIDTH), full),
                  pl.BlockSpec((1, LRU_WIDTH), full)],
        out_specs=pl.BlockSpec((tl, LRU_WIDTH), lambda b, i: (b * n_seq + i, 0)),
        out_shape=jax.ShapeDtypeStruct((T, LRU_WIDTH), BF16),
        scratch_shapes=[pltpu.VMEM((SUBLANES + tl, LRU_WIDTH), F32),
                        pltpu.VMEM((SUBLANES, LRU_WIDTH), F32)],
        compiler_params=_cparams("arbitrary", "arbitrary"),
        name="rg_lru",
    )(xg, xg, cw, cb, wa, ba, wx, bx, lam)


N_BACK = DILATED_PATTERNS[0][0] // DILATED_PATTERNS[0][1]
Q_BLOCKS = ATT_SPAN // N_BACK


def _attention_bias():
    qi = np.arange(N_BACK)[:, None]
    ki = np.arange(2 * N_BACK)[None, :]
    dist = N_BACK + qi - ki
    band = (dist >= 0) & (dist <= N_BACK)
    first = band & (ki >= N_BACK)
    return np.where(np.stack([band, first]), 0.0, NEG_INF).astype(np.float32)


def _attn_kernel(q_ref, k_ref, v_ref, bias_ref, o_ref, *scratch):
    kv_s = scratch[:6]
    o_s, m_s, l_s = scratch[6:]
    sb = pl.program_id(2)
    lane = lax.broadcasted_iota(jnp.int32, (1, LANES), 1)
    head0 = lane < ATT_HEAD_DIM

    for p, (window, dil) in enumerate(DILATED_PATTERNS):
        per_res = Q_BLOCKS // dil
        span = N_BACK * per_res
        for src, dst in ((k_ref, kv_s[2 * p]), (v_ref, kv_s[2 * p + 1])):
            @pl.when(sb == 0)
            def _():
                dst[:, 0:N_BACK, :] = jnp.zeros((dil, N_BACK, LANES), BF16)

            @pl.when(sb > 0)
            def _():
                dst[:, 0:N_BACK, :] = dst[:, span:span + N_BACK, :]

            for r in range(dil):
                rows = pl.ds(r, span, stride=dil) if dil > 1 else pl.ds(0, span)
                dst[r, N_BACK:N_BACK + span, :] = src[0, rows, :].astype(BF16)

    for p, (window, dil) in enumerate(DILATED_PATTERNS):
        per_res = Q_BLOCKS // dil
        k_s, v_s = kv_s[2 * p], kv_s[2 * p + 1]

        for n in range(Q_BLOCKS):
            r, m = n % dil, n // dil
            if dil > 1:
                rows = pl.ds(m * (N_BACK * dil) + r, N_BACK, stride=dil)
            else:
                rows = pl.ds(m * N_BACK, N_BACK)
            q = q_ref[0, rows, :].astype(BF16)
            kc = k_s[r, m * N_BACK:(m + 2) * N_BACK, :]
            vc = v_s[r, m * N_BACK:(m + 2) * N_BACK, :]
            bias = bias_ref[jnp.where(sb == 0, 1, 0)] if m == 0 else bias_ref[0]
            res = []
            for h in range(LANES // ATT_HEAD_DIM):
                qm = jnp.where(head0 if h == 0 else ~head0, q, jnp.zeros_like(q))
                s = _dot_nt(qm, kc) + bias
                mx = jnp.max(s, axis=-1, keepdims=True)
                e = jnp.exp(s - mx)
                res.append((_dot(e.astype(BF16), vc), mx, jnp.sum(e, axis=-1, keepdims=True)))
            for dst, idx in ((o_s, 0), (m_s, 1), (l_s, 2)):
                dst[p, rows, :] = jnp.where(head0, res[0][idx], res[1][idx])

    mx = jnp.maximum(jnp.maximum(m_s[0], m_s[1]), m_s[2])
    num = jnp.zeros_like(mx)
    den = jnp.zeros_like(mx)
    for p in range(len(DILATED_PATTERNS)):
        w = jnp.exp(m_s[p] - mx)
        num = num + w * o_s[p]
        den = den + w * l_s[p]
    o_ref[0] = (num / den).astype(BF16)


def _attention(q, k, v, bias):
    B, S, W = q.shape
    blk = pl.BlockSpec((1, ATT_SPAN, LANES), lambda b, p, i: (b, i, p))
    kv_scratch = []
    for window, dil in DILATED_PATTERNS:
        shape = (dil, N_BACK * (1 + Q_BLOCKS // dil), LANES)
        kv_scratch += [pltpu.VMEM(shape, BF16), pltpu.VMEM(shape, BF16)]
    acc = pltpu.VMEM((len(DILATED_PATTERNS), ATT_SPAN, LANES), F32)
    return pl.pallas_call(
        _attn_kernel,
        grid=(B, W // LANES, S // ATT_SPAN),
        in_specs=[blk, blk, blk, pl.BlockSpec(bias.shape, lambda b, p, i: (0, 0, 0))],
        out_specs=blk,
        out_shape=jax.ShapeDtypeStruct((B, S, W), BF16),
        scratch_shapes=kv_scratch + [acc, acc, acc],
        compiler_params=_cparams("arbitrary", "arbitrary", "arbitrary"),
        name="dilated_attention",
    )(q, k, v, bias)


def _ffn_kernel(x_ref, ya_ref, yb_ref, wo_ref, g_ref, w1_ref, cw_ref, cb_ref, w2_ref,
                o_ref, act_s, abuf, carry_s, *, tiles_per_seq):
    i = pl.program_id(0)
    tm = x_ref.shape[0]
    half = ya_ref.shape[1]
    tf = abuf.shape[1]

    @pl.when(i % tiles_per_seq == 0)
    def _():
        carry_s[...] = jnp.zeros_like(carry_s)

    x1 = (x_ref[...] + _dot(ya_ref[...], wo_ref[0:half, :])
          + _dot(yb_ref[...], wo_ref[half:2 * half, :]))
    h = _rms(x1, g_ref[...]).astype(BF16)
    for c in range(D_FF // tf):
        cols = slice(c * tf, (c + 1) * tf)
        a = _dot(h, w1_ref[:, cols])
        lin = _dot(h, w1_ref[:, D_FF + c * tf:D_FF + (c + 1) * tf])
        abuf[0:SUBLANES, :] = carry_s[:, cols]
        abuf[SUBLANES:SUBLANES + tm, :] = a
        carry_s[:, cols] = a[tm - SUBLANES:tm, :]
        conv = (cb_ref[:, cols] + cw_ref[2:3, cols] * a
                + cw_ref[1:2, cols] * abuf[pl.ds(SUBLANES - 1, tm), :]
                + cw_ref[0:1, cols] * abuf[pl.ds(SUBLANES - 2, tm), :])
        act_s[:, cols] = (jax.nn.gelu(conv) * lin).astype(BF16)
    o_ref[...] = x1 + _dot(act_s[...], w2_ref[...])


def _ffn(x, ya, yb, wo, g, w_in, cw, cb, w2, seq):
    T = x.shape[0]
    tm, tf = TM_FFN, TF_FFN
    half = ya.shape[1]
    row = lambda i: (i, 0)
    resident = lambda shape: pl.BlockSpec(shape, lambda i: (0, 0), pipeline_mode=pl.Buffered(1))
    return pl.pallas_call(
        functools.partial(_ffn_kernel, tiles_per_seq=seq // tm),
        grid=(T // tm,),
        in_specs=[pl.BlockSpec((tm, D_MODEL), row),
                  pl.BlockSpec((tm, half), row),
                  pl.BlockSpec((tm, half), row),
                  resident((2 * half, D_MODEL)),
                  resident((1, D_MODEL)),
                  resident((D_MODEL, 2 * D_FF)),
                  resident((3, D_FF)),
                  resident((1, D_FF)),
                  resident((D_FF, D_MODEL))],
        out_specs=pl.BlockSpec((tm, D_MODEL), row),
        out_shape=jax.ShapeDtypeStruct((T, D_MODEL), F32),
        scratch_shapes=[pltpu.VMEM((tm, D_FF), BF16),
                        pltpu.VMEM((SUBLANES + tm, tf), F32),
                        pltpu.VMEM((SUBLANES, D_FF), F32)],
        compiler_params=_cparams("arbitrary"),
        name="outproj_conv_mlp",
    )(x, ya, yb, wo, g, w_in, cw, cb, w2)


def _odd_in_kernel(x_ref, g_ref, w_ref, o_ref):
    h = _rms(x_ref[...], g_ref[...])
    o_ref[...] = _dot(h.astype(BF16), w_ref[...])


def _odd_in(x, g, w):
    T = x.shape[0]
    tm = TM_PROJ
    ncol = w.shape[1]
    return pl.pallas_call(
        _odd_in_kernel,
        grid=(T // tm,),
        in_specs=[pl.BlockSpec((tm, D_MODEL), lambda i: (i, 0)),
                  pl.BlockSpec((1, D_MODEL), lambda i: (0, 0)),
                  pl.BlockSpec((D_MODEL, ncol), lambda i: (0, 0))],
        out_specs=pl.BlockSpec((tm, ncol), lambda i: (i, 0)),
        out_shape=jax.ShapeDtypeStruct((T, ncol), F32),
        compiler_params=_cparams("arbitrary"),
        name="odd_in_proj",
    )(x, g, w)


def _cswap(v):
    return jnp.concatenate([v[:, LANES:], v[:, :LANES]], axis=1)


def _s5_kernel(u_ref, bm_ref, p1_ref, p2_ref, q1_ref, q2_ref, tri_ref, cm_ref,
               d_ref, gw_ref, gb_ref, o_ref, carry, x_s, ys_s):
    tl = u_ref.shape[0]
    L = L_S5
    blk = 2 * LANES
    slab = 2 * S5_NSTATE // (S5_WIDTH // LANES)

    @pl.when(pl.program_id(1) == 0)
    def _():
        carry[...] = jnp.zeros_like(carry)

    tri = tri_ref[...]
    for s in range(S5_WIDTH // LANES):
        u = u_ref[:, s * LANES:(s + 1) * LANES]
        u16 = u.astype(BF16)
        for jb in range(slab // blk):
            cols = slice(s * slab + jb * blk, s * slab + (jb + 1) * blk)
            bu = _dot(u16, bm_ref[s, :, jb * blk:(jb + 1) * blk])
            for c in range(tl // L):
                rows = slice(c * L, (c + 1) * L)
                v = bu[rows, :]
                z = v * q1_ref[:, cols] + _cswap(v) * q2_ref[:, cols]
                w = _dot(tri, z.astype(BF16)) + carry[0:1, cols]
                x = w * p1_ref[:, cols] + _cswap(w) * p2_ref[:, cols]
                xl = x[L - 1:L, :]
                carry[0:1, cols] = xl * p1_ref[1:2, cols] + _cswap(xl) * p2_ref[1:2, cols]
                x_s[rows, jb * blk:(jb + 1) * blk] = x.astype(BF16)
        y = _dot(x_s[...], cm_ref[s]) + d_ref[:, s * LANES:(s + 1) * LANES] * u
        ys_s[:, s * LANES:(s + 1) * LANES] = jax.nn.gelu(y)
    ys = ys_s[...]
    o_ref[...] = (ys * jax.nn.sigmoid(_dot(ys.astype(BF16), gw_ref[...]) + gb_ref[...])).astype(BF16)


def _s5(proj, bm, pr, pi, qr, qi, tri, cm, d, gw, gb, batch, seq):
    T = proj.shape[0]
    tl = TL_S5
    n_seq = seq // tl
    c2 = lambda b, i: (0, 0)
    c3 = lambda b, i: (0, 0, 0)
    return pl.pallas_call(
        _s5_kernel,
        grid=(batch, n_seq),
        in_specs=[pl.BlockSpec((tl, S5_WIDTH), lambda b, i: (b * n_seq + i, 0)),
                  pl.BlockSpec(bm.shape, c3),
                  pl.BlockSpec(pr.shape, c2), pl.BlockSpec(pi.shape, c2),
                  pl.BlockSpec(qr.shape, c2), pl.BlockSpec(qi.shape, c2),
                  pl.BlockSpec(tri.shape, c2),
                  pl.BlockSpec(cm.shape, c3),
                  pl.BlockSpec((1, S5_WIDTH), c2),
                  pl.BlockSpec((S5_WIDTH, S5_WIDTH), c2),
                  pl.BlockSpec((1, S5_WIDTH), c2)],
        out_specs=pl.BlockSpec((tl, S5_WIDTH), lambda b, i: (b * n_seq + i, 0)),
        out_shape=jax.ShapeDtypeStruct((T, S5_WIDTH), BF16),
        scratch_shapes=[pltpu.VMEM((SUBLANES, 2 * S5_NSTATE), F32),
                        pltpu.VMEM((tl, 2 * S5_NSTATE // (S5_WIDTH // LANES)), BF16),
                        pltpu.VMEM((tl, S5_WIDTH), F32)],
        compiler_params=_cparams("arbitrary", "arbitrary"),
        name="s5_glu",
    )(proj, bm, pr, pi, qr, qi, tri, cm, d, gw, gb)


def _s5_params(lam_re, lam_im, b_re, b_im, c_re, c_im, log_step):
    G, N, P = S5_GROUPS, S5_STATE, S5_GROUP
    gs = LANES // P
    ns = S5_WIDTH // LANES
    step = jnp.exp(log_step.astype(F32))[:, None]
    lr, li = lam_re.astype(F32), lam_im.astype(F32)
    mag = jnp.exp(lr * step)
    ar, ai = mag * jnp.cos(li * step), mag * jnp.sin(li * step)
    den = lr * lr + li * li
    cr = ((ar - 1.0) * lr + ai * li) / den
    ci = (ai * lr - (ar - 1.0) * li) / den
    bbr = cr[..., None] * b_re - ci[..., None] * b_im
    bbi = cr[..., None] * b_im + ci[..., None] * b_re
    eye = jnp.eye(gs, dtype=F32)

    def in_blockdiag(t):
        t = t.reshape(ns, gs, N, P).transpose(0, 1, 3, 2)
        return jnp.einsum('ab,sapn->sapbn', eye, t).reshape(ns, gs * P, gs * N)

    def out_blockdiag(t):
        t = t.reshape(ns, gs, P, N).transpose(0, 1, 3, 2)
        return jnp.einsum('ab,sanp->sanbp', eye, t).reshape(ns, gs * N, gs * P)

    def interleave(re, im, axis):
        shp = list(re.shape)
        blocked = shp[:axis] + [shp[axis] // LANES, LANES] + shp[axis + 1:]
        both = jnp.stack([re.reshape(blocked), im.reshape(blocked)], axis=axis + 1)
        return both.reshape(shp[:axis] + [2 * shp[axis]] + shp[axis + 1:])

    bm = interleave(in_blockdiag(bbr), in_blockdiag(bbi), 2).astype(BF16)
    cm = interleave(out_blockdiag(c_re), out_blockdiag(-c_im), 1).astype(BF16)
    j = jnp.arange(L_S5, dtype=F32)[:, None]
    la = (lr * step).reshape(1, G * N)
    th = (li * step).reshape(1, G * N)
    pmag, qmag = jnp.exp(j * la), jnp.exp(-(j * la))
    cs, sn = jnp.cos(j * th), jnp.sin(j * th)
    pr, pi, qr, qi = pmag * cs, pmag * sn, qmag * cs, -(qmag * sn)
    return (bm, interleave(pr, pr, 1), interleave(-pi, pi, 1),
            interleave(qr, qr, 1), interleave(-qi, qi, 1), cm)


def _gla_cumsum_matrix(tl):
    r = np.arange(tl)[:, None]
    c = np.arange(tl)[None, :]
    same = (r // C_GLA) == (c // C_GLA)
    return np.concatenate([same & (c <= r), same & (c > r)], axis=0).astype(np.float32)


def _gla_kernel(q_ref, k_ref, v_ref, g_ref, gk_ref, gw_ref, gb_ref, nrm_ref, cum_ref, o_ref, st):
    tl = q_ref.shape[0]
    C = C_GLA
    blk = 2 * C
    pair = LANES // GLA_DK

    @pl.when(pl.program_id(1) == 0)
    def _():
        st[...] = jnp.zeros_like(st)

    z = _dot(gk_ref[...].astype(BF16), gw_ref[...]) + gb_ref[...]
    log_a = _log_sigmoid(z) * (1.0 / GLA_TAU)
    hi, lo = _split_bf16(log_a)
    sums = _dot(cum_ref[...], hi) + _dot(cum_ref[...], lo)
    bc, suffix = sums[:tl], sums[tl:]
    eb = jnp.exp(bc)
    q_dec = q_ref[...] * (GLA_DK ** -0.5) * eb
    k = k_ref[...]
    k_inv = (k * jnp.exp(-bc)).astype(BF16)
    k_dec = k * jnp.exp(suffix)

    lane = lax.broadcasted_iota(jnp.int32, (1, LANES), 1)
    rb = lax.broadcasted_iota(jnp.int32, (blk, blk), 0)
    cb = lax.broadcasted_iota(jnp.int32, (blk, blk), 1)
    causal = (rb >= cb) & ((rb < C) | (cb >= C))
    heads = []
    for h in range(GLA_HEADS):
        hp, hh = divmod(h, pair)
        cols = slice(hp * LANES, (hp + 1) * LANES)
        in_head = (lane >= hh * GLA_DK) & (lane < (hh + 1) * GLA_DK)
        heads.append(dict(
            cols=cols,
            qd=jnp.where(in_head, q_dec[:, cols], 0.0).astype(BF16),
            kd=jnp.where(in_head, k_dec[:, cols], 0.0).astype(BF16),
            ki=k_inv[:, cols],
            vh=v_ref[:, h * GLA_DV:(h + 1) * GLA_DV].astype(BF16),
            state=st[h]))
    for b in range(tl // blk):
        rows = slice(b * blk, (b + 1) * blk)
        for h, hd in enumerate(heads):
            att = jnp.where(causal, _dot_nt(hd["qd"][rows], hd["ki"][rows]), 0.0)
            o = _dot(att.astype(BF16), hd["vh"][rows])
            inter = []
            for c in range(b * blk // C, (b + 1) * blk // C):
                crow = slice(c * C, (c + 1) * C)
                inter.append(_dot_nt(hd["qd"][crow], hd["state"].astype(BF16)))
                decay = eb[(c + 1) * C - 1:(c + 1) * C, hd["cols"]]
                hd["state"] = hd["state"] * decay + _dot_tn(hd["vh"][crow], hd["kd"][crow])
            o = _rms(o + jnp.concatenate(inter, axis=0), nrm_ref[...])
            gh = g_ref[rows, h * GLA_DV:(h + 1) * GLA_DV]
            o_ref[rows, h * GLA_DV:(h + 1) * GLA_DV] = (o * jax.nn.silu(gh)).astype(BF16)
    for h, hd in enumerate(heads):
        st[h] = hd["state"]


def _gla(proj, gw, gb, nrm, tri, batch, seq):
    T = proj.shape[0]
    tl = TL_GLA
    n_seq = seq // tl
    hk = GLA_HEADS * GLA_DK
    hv = GLA_HEADS * GLA_DV
    c2 = lambda b, i: (0, 0)
    q0 = S5_WIDTH // hk
    v0 = (S5_WIDTH + 2 * hk) // hv
    gk0 = (S5_WIDTH + 2 * hk + 2 * hv) // GK_PAD
    return pl.pallas_call(
        _gla_kernel,
        grid=(batch, n_seq),
        in_specs=[pl.BlockSpec((tl, hk), lambda b, i: (b * n_seq + i, q0)),
                  pl.BlockSpec((tl, hk), lambda b, i: (b * n_seq + i, q0 + 1)),
                  pl.BlockSpec((tl, hv), lambda b, i: (b * n_seq + i, v0)),
                  pl.BlockSpec((tl, hv), lambda b, i: (b * n_seq + i, v0 + 1)),
                  pl.BlockSpec((tl, GK_PAD), lambda b, i: (b * n_seq + i, gk0)),
                  pl.BlockSpec((GK_PAD, hk), c2),
                  pl.BlockSpec((1, hk), c2),
                  pl.BlockSpec((1, GLA_DV), c2),
                  pl.BlockSpec((2 * tl, tl), c2)],
        out_specs=pl.BlockSpec((tl, hv), lambda b, i: (b * n_seq + i, 0)),
        out_shape=jax.ShapeDtypeStruct((T, hv), BF16),
        scratch_shapes=[pltpu.VMEM((GLA_HEADS, GLA_DV, LANES), F32)],
        compiler_params=_cparams("arbitrary", "arbitrary"),
        name="gla",
    )(proj, proj, proj, proj, proj, gw, gb, nrm, tri)


def _block_diag(w):
    nb, a, b = w.shape
    return jnp.einsum('hk,hij->hikj', jnp.eye(nb, dtype=w.dtype), w).reshape(nb * a, nb * b)


def _rope_tables(seq):
    half = ROPE_DIM // 2
    pos = jnp.arange(seq, dtype=F32)
    inv = ROPE_THETA ** (-jnp.arange(0, ROPE_DIM, 2, dtype=F32) / ROPE_DIM)
    ang = pos[:, None] * inv[None, :]
    cos, sin = jnp.cos(ang), jnp.sin(ang)
    rest = ATT_HEAD_DIM - ROPE_DIM
    ones = jnp.ones((seq, rest), F32)
    zeros = jnp.zeros((seq, rest), F32)
    zh = jnp.zeros((seq, half), F32)
    per_head = lambda parts: jnp.tile(jnp.concatenate(parts, axis=1), (1, LANES // ATT_HEAD_DIM))
    return (per_head([cos, cos, ones]), per_head([-sin, zh, zeros]), per_head([zh, sin, zeros]))


def kernel(x, e_norm, e_w_in, e_conv_w, e_conv_b, e_gate_a_w, e_gate_a_b, e_gate_x_w, e_gate_x_b, e_lambda, e_q_norm, e_k_norm, e_w_out, o_norm, o_w_in, o_lambda_re, o_lambda_im, o_b_re, o_b_im, o_c_re, o_c_im, o_d, o_log_step, o_glu_w, o_glu_b, o_gk_w, o_gk_b, o_gla_norm, o_w_out, f_norm, f_w_in, f_conv_w, f_conv_b, f_w_out):
    B, S, D = x.shape
    T = B * S
    depth = f_norm.shape[0]
    row = lambda t: t.reshape(1, -1).astype(F32)
    xt = x.reshape(T, D)

    cos_t, s1_t, s2_t = _rope_tables(S)
    head_seg = jnp.asarray(np.kron(np.eye(LANES // ATT_HEAD_DIM), np.ones((ATT_HEAD_DIM, ATT_HEAD_DIM))), BF16)
    att_bias = jnp.asarray(_attention_bias())
    tri_s5 = jnp.asarray(np.tril(np.ones((L_S5, L_S5))), BF16)
    tri_gla = jnp.asarray(_gla_cumsum_matrix(TL_GLA), BF16)
    two_heads = lambda t: jnp.tile(row(t), (1, LANES // ATT_HEAD_DIM))

    for layer in range(depth):
        i = layer // 2
        if layer % 2 == 0:
            xg, q, k, v = _even_in(xt, row(e_norm[i]), e_w_in[i].astype(BF16),
                                   two_heads(e_q_norm[i]), two_heads(e_k_norm[i]),
                                   head_seg, cos_t, s1_t, s2_t, S)
            ya = _lru(xg, e_conv_w[i], row(e_conv_b[i]),
                      _block_diag(e_gate_a_w[i]).astype(BF16), row(e_gate_a_b[i]),
                      _block_diag(e_gate_x_w[i]).astype(BF16), row(e_gate_x_b[i]),
                      row(e_lambda[i]), B, S)
            yb = _attention(q.reshape(B, S, ATT_WIDTH), k.reshape(B, S, ATT_WIDTH),
                            v.reshape(B, S, ATT_WIDTH), att_bias).reshape(T, ATT_WIDTH)
            w_out = e_w_out[i]
        else:
            w_in = jnp.pad(o_w_in[i], ((0, 0), (0, GK_PAD - GLA_LOWRANK))).astype(BF16)
            proj = _odd_in(xt, row(o_norm[i]), w_in)
            bm, pr, pi, qr, qi, cm = _s5_params(o_lambda_re[i], o_lambda_im[i], o_b_re[i], o_b_im[i],
                                                     o_c_re[i], o_c_im[i], o_log_step[i])
            ya = _s5(proj, bm, pr, pi, qr, qi, tri_s5, cm, row(o_d[i]),
                     o_glu_w[i].astype(BF16), row(o_glu_b[i]), B, S)
            gk_w = jnp.pad(o_gk_w[i], ((0, GK_PAD - GLA_LOWRANK), (0, 0))).astype(BF16)
            yb = _gla(proj, gk_w, row(o_gk_b[i]), row(o_gla_norm[i]), tri_gla, B, S)
            w_out = o_w_out[i]
        xt = _ffn(xt, ya, yb, w_out.astype(BF16), row(f_norm[layer]), f_w_in[layer].astype(BF16),
                  f_conv_w[layer], row(f_conv_b[layer]), f_w_out[layer].astype(BF16), S)
    return xt.reshape(B, S, D)
```

```python
import functools
import math

import numpy as np
import jax
import jax.numpy as jnp
from jax import lax
from jax.experimental import pallas as pl
from jax.experimental.pallas import tpu as pltpu

F32 = jnp.float32
BF16 = jnp.bfloat16

D_MODEL = 1024
LRU_WIDTH = 512
LRU_BLOCKS = 8
LRU_CONV = 4
LRU_C = 8.0
ATT_HEADS = 8
ATT_HEAD_DIM = 64
ATT_WIDTH = 512
DILATED_PATTERNS = ((128, 1), (512, 4), (2048, 16))
ATT_SPAN = 2048
ROPE_THETA = 500000.0
ROPE_DIM = 16
S5_WIDTH = 512
S5_GROUP = 16
S5_GROUPS = 32
S5_STATE = 64
S5_NSTATE = S5_GROUPS * S5_STATE
GLA_HEADS = 4
GLA_DK = 64
GLA_DV = 128
GLA_LOWRANK = 16
GLA_TAU = 16.0
D_FF = 3 * D_MODEL
EPS = 1e-6
NEG_INF = -1e30

LANES = 128
SUBLANES = 8
VMEM_LIMIT = 56 * 1024 * 1024

TM_PROJ = 512
TM_FFN = 512
TF_FFN = 512
TL_LRU = 256
TL_S5 = 512
L_S5 = 128
TL_GLA = 512
C_GLA = 64
GK_PAD = 128


def _cparams(*sem):
    return pltpu.CompilerParams(dimension_semantics=sem, vmem_limit_bytes=VMEM_LIMIT)


def _rms(x, g):
    return x * lax.rsqrt(jnp.mean(x * x, axis=-1, keepdims=True) + EPS) * g


def _log_sigmoid(x):
    return -(jnp.maximum(-x, 0.0) + jnp.log1p(jnp.exp(-jnp.abs(x))))


def _split_bf16(x):
    hi = x.astype(BF16)
    lo = (x - hi.astype(F32)).astype(BF16)
    return hi, lo


def _dot(a, b):
    return jnp.dot(a, b, preferred_element_type=F32)


def _dot_nt(a, b):
    return lax.dot_general(a, b, (((1,), (1,)), ((), ())), preferred_element_type=F32)


def _dot_tn(a, b):
    return lax.dot_general(a, b, (((0,), (0,)), ((), ())), preferred_element_type=F32)


def _even_in_kernel(x_ref, g_ref, w_ref, qn_ref, kn_ref, seg_ref, cos_ref, s1_ref, s2_ref,
                    xg_ref, q_ref, k_ref, v_ref):
    h = _rms(x_ref[...], g_ref[...]).astype(BF16)
    seg = seg_ref[...]
    cos, s1, s2 = cos_ref[...], s1_ref[...], s2_ref[...]
    half = ROPE_DIM // 2
    for off, n_ref, dst, scale in ((2 * LRU_WIDTH, qn_ref, q_ref, ATT_HEAD_DIM ** -0.5),
                                   (2 * LRU_WIDTH + ATT_WIDTH, kn_ref, k_ref, 1.0)):
        y = _dot(h, w_ref[:, off:off + ATT_WIDTH])
        for c in range(ATT_WIDTH // LANES):
            t = y[:, c * LANES:(c + 1) * LANES]
            ms = _dot((t * t).astype(BF16), seg) * (1.0 / ATT_HEAD_DIM)
            tn = t * lax.rsqrt(ms + EPS) * n_ref[...]
            r = (tn * cos + pltpu.roll(tn, LANES - half, 1) * s1 + pltpu.roll(tn, half, 1) * s2)
            dst[:, c * LANES:(c + 1) * LANES] = r * scale
    xg_ref[...] = _dot(h, w_ref[:, :2 * LRU_WIDTH])
    v_ref[...] = _dot(h, w_ref[:, 2 * LRU_WIDTH + 2 * ATT_WIDTH:])


def _even_in(x, g, w, qn, kn, seg, cos_t, s1_t, s2_t, seq):
    T = x.shape[0]
    tm = TM_PROJ
    n_seq = seq // tm
    ncol = w.shape[1]
    full = lambda i: (0, 0)
    tab = lambda i: (i % n_seq, 0)
    row = lambda i: (i, 0)
    return pl.pallas_call(
        _even_in_kernel,
        grid=(T // tm,),
        in_specs=[pl.BlockSpec((tm, D_MODEL), row),
                  pl.BlockSpec((1, D_MODEL), full),
                  pl.BlockSpec((D_MODEL, ncol), full),
                  pl.BlockSpec((1, LANES), full),
                  pl.BlockSpec((1, LANES), full),
                  pl.BlockSpec((LANES, LANES), full),
                  pl.BlockSpec((tm, LANES), tab),
                  pl.BlockSpec((tm, LANES), tab),
                  pl.BlockSpec((tm, LANES), tab)],
        out_specs=[pl.BlockSpec((tm, 2 * LRU_WIDTH), row),
                   pl.BlockSpec((tm, ATT_WIDTH), row),
                   pl.BlockSpec((tm, ATT_WIDTH), row),
                   pl.BlockSpec((tm, ATT_WIDTH), row)],
        out_shape=[jax.ShapeDtypeStruct((T, 2 * LRU_WIDTH), F32),
                   jax.ShapeDtypeStruct((T, ATT_WIDTH), F32),
                   jax.ShapeDtypeStruct((T, ATT_WIDTH), F32),
                   jax.ShapeDtypeStruct((T, ATT_WIDTH), F32)],
        compiler_params=_cparams("arbitrary"),
        name="even_in_proj",
    )(x, g, w, qn, kn, seg, cos_t, s1_t, s2_t)


def _lru_kernel(xl_ref, gl_ref, cw_ref, cb_ref, wa_ref, ba_ref, wx_ref, bx_ref, lam_ref,
                o_ref, xbuf, hprev):
    tl = xl_ref.shape[0]

    @pl.when(pl.program_id(1) == 0)
    def _():
        xbuf[0:SUBLANES, :] = jnp.zeros((SUBLANES, LRU_WIDTH), F32)
        hprev[...] = jnp.zeros_like(hprev)

    x = xl_ref[...]
    xbuf[SUBLANES:SUBLANES + tl, :] = x
    conv = cb_ref[...] + cw_ref[LRU_CONV - 1:LRU_CONV, :] * x
    for j in range(LRU_CONV - 1):
        conv = conv + cw_ref[j:j + 1, :] * xbuf[pl.ds(SUBLANES - (LRU_CONV - 1) + j, tl), :]
    xbuf[0:SUBLANES, :] = x[tl - SUBLANES:tl, :]

    c16 = conv.astype(BF16)
    r = jax.nn.sigmoid(_dot(c16, wa_ref[...]) + ba_ref[...])
    ig = jax.nn.sigmoid(_dot(c16, wx_ref[...]) + bx_ref[...])
    log_a = (LRU_C * r) * _log_sigmoid(lam_ref[...])
    a = jnp.exp(log_a)
    b = jnp.sqrt(1.0 - a * a) * (ig * conv)

    a = a.reshape(tl // SUBLANES, SUBLANES, LRU_WIDTH)
    b = b.reshape(tl // SUBLANES, SUBLANES, LRU_WIDTH)
    sub = lax.broadcasted_iota(jnp.int32, (1, SUBLANES, 1), 1)
    d = 1
    while d < SUBLANES:
        keep = sub >= d
        a_sh = jnp.where(keep, pltpu.roll(a, d, 1), 1.0)
        b_sh = jnp.where(keep, pltpu.roll(b, d, 1), 0.0)
        b = a * b_sh + b
        a = a * a_sh
        d *= 2
    last = hprev[0:1, :]
    groups = []
    for t in range(tl // SUBLANES):
        ht = b[t] + a[t] * last
        groups.append(ht)
        last = ht[SUBLANES - 1:SUBLANES, :]
    hprev[0:1, :] = last
    h = jnp.concatenate(groups, axis=0)
    o_ref[...] = (h * jax.nn.gelu(gl_ref[...])).astype(BF16)


def _lru(xg, cw, cb, wa, ba, wx, bx, lam, batch, seq):
    T = xg.shape[0]
    tl = TL_LRU
    n_seq = seq // tl
    full = lambda b, i: (0, 0)
    return pl.pallas_call(
        _lru_kernel,
        grid=(batch, n_seq),
        in_specs=[pl.BlockSpec((tl, LRU_WIDTH), lambda b, i: (b * n_seq + i, 0)),
                  pl.BlockSpec((tl, LRU_WIDTH), lambda b, i: (b * n_seq + i, 1)),
                  pl.BlockSpec((LRU_CONV, LRU_WIDTH), full),
                  pl.BlockSpec((1, LRU_WIDTH), full),
                  pl.BlockSpec((LRU_WIDTH, LRU_WIDTH), full),
                  pl.BlockSpec((1, LRU_WIDTH), full),
                  pl.BlockSpec((LRU_WIDTH, LRU_WIDTH), full),
                  pl.BlockSpec((1, LRU_WIDTH), full),
                  pl.BlockSpec((1, LRU_WIDTH), full)],
        out_specs=pl.BlockSpec((tl, LRU_WIDTH), lambda b, i: (b * n_seq + i, 0)),
        out_shape=jax.ShapeDtypeStruct((T, LRU_WIDTH), BF16),
        scratch_shapes=[pltpu.VMEM((SUBLANES + tl, LRU_WIDTH), F32),
                        pltpu.VMEM((SUBLANES, LRU_WIDTH), F32)],
        compiler_params=_cparams("arbitrary", "arbitrary"),
        name="rg_lru",
    )(xg, xg, cw, cb, wa, ba, wx, bx, lam)


N_BACK = DILATED_PATTERNS[0][0] // DILATED_PATTERNS[0][1]
Q_BLOCKS = ATT_SPAN // N_BACK


def _attention_bias():
    qi = np.arange(N_BACK)[:, None]
    ki = np.arange(2 * N_BACK)[None, :]
    dist = N_BACK + qi - ki
    band = (dist >= 0) & (dist <= N_BACK)
    first = band & (ki >= N_BACK)
    return np.where(np.stack([band, first]), 0.0, NEG_INF).astype(np.float32)


def _attn_kernel(q_ref, k_ref, v_ref, bias_ref, o_ref, *scratch):
    kv_s = scratch[:6]
    o_s, m_s, l_s = scratch[6:]
    sb = pl.program_id(2)
    lane = lax.broadcasted_iota(jnp.int32, (1, LANES), 1)
    head0 = lane < ATT_HEAD_DIM

    for p, (window, dil) in enumerate(DILATED_PATTERNS):
        per_res = Q_BLOCKS // dil
        span = N_BACK * per_res
        for src, dst in ((k_ref, kv_s[2 * p]), (v_ref, kv_s[2 * p + 1])):
            @pl.when(sb == 0)
            def _():
                dst[:, 0:N_BACK, :] = jnp.zeros((dil, N_BACK, LANES), BF16)

            @pl.when(sb > 0)
            def _():
                dst[:, 0:N_BACK, :] = dst[:, span:span + N_BACK, :]

            for r in range(dil):
                rows = pl.ds(r, span, stride=dil) if dil > 1 else pl.ds(0, span)
                dst[r, N_BACK:N_BACK + span, :] = src[0, rows, :].astype(BF16)

    for p, (window, dil) in enumerate(DILATED_PATTERNS):
        per_res = Q_BLOCKS // dil
        k_s, v_s = kv_s[2 * p], kv_s[2 * p + 1]

        for n in range(Q_BLOCKS):
            r, m = n % dil, n // dil
            if dil > 1:
                rows = pl.ds(m * (N_BACK * dil) + r, N_BACK, stride=dil)
            else:
                rows = pl.ds(m * N_BACK, N_BACK)
            q = q_ref[0, rows, :].astype(BF16)
            kc = k_s[r, m * N_BACK:(m + 2) * N_BACK, :]
            vc = v_s[r, m * N_BACK:(m + 2) * N_BACK, :]
            bias = bias_ref[jnp.where(sb == 0, 1, 0)] if m == 0 else bias_ref[0]
            res = []
            for h in range(LANES // ATT_HEAD_DIM):
                qm = jnp.where(head0 if h == 0 else ~head0, q, jnp.zeros_like(q))
                s = _dot_nt(qm, kc) + bias
                mx = jnp.max(s, axis=-1, keepdims=True)
                e = jnp.exp(s - mx)
                res.append((_dot(e.astype(BF16), vc), mx, jnp.sum(e, axis=-1, keepdims=True)))
            for dst, idx in ((o_s, 0), (m_s, 1), (l_s, 2)):
                dst[p, rows, :] = jnp.where(head0, res[0][idx], res[1][idx])

    mx = jnp.maximum(jnp.maximum(m_s[0], m_s[1]), m_s[2])
    num = jnp.zeros_like(mx)
    den = jnp.zeros_like(mx)
    for p in range(len(DILATED_PATTERNS)):
        w = jnp.exp(m_s[p] - mx)
        num = num + w * o_s[p]
        den = den + w * l_s[p]
    o_ref[0] = (num / den).astype(BF16)


def _attention(q, k, v, bias):
    B, S, W = q.shape
    blk = pl.BlockSpec((1, ATT_SPAN, LANES), lambda b, p, i: (b, i, p))
    kv_scratch = []
    for window, dil in DILATED_PATTERNS:
        shape = (dil, N_BACK * (1 + Q_BLOCKS // dil), LANES)
        kv_scratch += [pltpu.VMEM(shape, BF16), pltpu.VMEM(shape, BF16)]
    acc = pltpu.VMEM((len(DILATED_PATTERNS), ATT_SPAN, LANES), F32)
    return pl.pallas_call(
        _attn_kernel,
        grid=(B, W // LANES, S // ATT_SPAN),
        in_specs=[blk, blk, blk, pl.BlockSpec(bias.shape, lambda b, p, i: (0, 0, 0))],
        out_specs=blk,
        out_shape=jax.ShapeDtypeStruct((B, S, W), BF16),
        scratch_shapes=kv_scratch + [acc, acc, acc],
        compiler_params=_cparams("arbitrary", "arbitrary", "arbitrary"),
        name="dilated_attention",
    )(q, k, v, bias)


def _ffn_kernel(x_ref, ya_ref, yb_ref, wo_ref, g_ref, w1_ref, cw_ref, cb_ref, w2_ref,
                o_ref, act_s, abuf, carry_s, *, tiles_per_seq):
    i = pl.program_id(0)
    tm = x_ref.shape[0]
    half = ya_ref.shape[1]
    tf = abuf.shape[1]

    @pl.when(i % tiles_per_seq == 0)
    def _():
        carry_s[...] = jnp.zeros_like(carry_s)

    x1 = (x_ref[...] + _dot(ya_ref[...], wo_ref[0:half, :])
          + _dot(yb_ref[...], wo_ref[half:2 * half, :]))
    h = _rms(x1, g_ref[...]).astype(BF16)
    for c in range(D_FF // tf):
        cols = slice(c * tf, (c + 1) * tf)
        a = _dot(h, w1_ref[:, cols])
        lin = _dot(h, w1_ref[:, D_FF + c * tf:D_FF + (c + 1) * tf])
        abuf[0:SUBLANES, :] = carry_s[:, cols]
        abuf[SUBLANES:SUBLANES + tm, :] = a
        carry_s[:, cols] = a[tm - SUBLANES:tm, :]
        conv = (cb_ref[:, cols] + cw_ref[2:3, cols] * a
                + cw_ref[1:2, cols] * abuf[pl.ds(SUBLANES - 1, tm), :]
                + cw_ref[0:1, cols] * abuf[pl.ds(SUBLANES - 2, tm), :])
        act_s[:, cols] = (jax.nn.gelu(conv) * lin).astype(BF16)
    o_ref[...] = x1 + _dot(act_s[...], w2_ref[...])


def _ffn(x, ya, yb, wo, g, w_in, cw, cb, w2, seq):
    T = x.shape[0]
    tm, tf = TM_FFN, TF_FFN
    half = ya.shape[1]
    row = lambda i: (i, 0)
    resident = lambda shape: pl.BlockSpec(shape, lambda i: (0, 0), pipeline_mode=pl.Buffered(1))
    return pl.pallas_call(
        functools.partial(_ffn_kernel, tiles_per_seq=seq // tm),
        grid=(T // tm,),
        in_specs=[pl.BlockSpec((tm, D_MODEL), row),
                  pl.BlockSpec((tm, half), row),
                  pl.BlockSpec((tm, half), row),
                  resident((2 * half, D_MODEL)),
                  resident((1, D_MODEL)),
                  resident((D_MODEL, 2 * D_FF)),
                  resident((3, D_FF)),
                  resident((1, D_FF)),
                  resident((D_FF, D_MODEL))],
        out_specs=pl.BlockSpec((tm, D_MODEL), row),
        out_shape=jax.ShapeDtypeStruct((T, D_MODEL), F32),
        scratch_shapes=[pltpu.VMEM((tm, D_FF), BF16),
                        pltpu.VMEM((SUBLANES + tm, tf), F32),
                        pltpu.VMEM((SUBLANES, D_FF), F32)],
        compiler_params=_cparams("arbitrary"),
        name="outproj_conv_mlp",
    )(x, ya, yb, wo, g, w_in, cw, cb, w2)


def _odd_in_kernel(x_ref, g_ref, w_ref, o_ref):
    h = _rms(x_ref[...], g_ref[...])
    o_ref[...] = _dot(h.astype(BF16), w_ref[...])


def _odd_in(x, g, w):
    T = x.shape[0]
    tm = TM_PROJ
    ncol = w.shape[1]
    return pl.pallas_call(
        _odd_in_kernel,
        grid=(T // tm,),
        in_specs=[pl.BlockSpec((tm, D_MODEL), lambda i: (i, 0)),
                  pl.BlockSpec((1, D_MODEL), lambda i: (0, 0)),
                  pl.BlockSpec((D_MODEL, ncol), lambda i: (0, 0))],
        out_specs=pl.BlockSpec((tm, ncol), lambda i: (i, 0)),
        out_shape=jax.ShapeDtypeStruct((T, ncol), F32),
        compiler_params=_cparams("arbitrary"),
        name="odd_in_proj",
    )(x, g, w)


def _cswap(v):
    return jnp.concatenate([v[:, LANES:], v[:, :LANES]], axis=1)


def _s5_kernel(u_ref, bm_ref, p1_ref, p2_ref, q1_ref, q2_ref, tri_ref, cm_ref,
               d_ref, gw_ref, gb_ref, o_ref, carry, x_s, ys_s):
    tl = u_ref.shape[0]
    L = L_S5
    blk = 2 * LANES
    slab = 2 * S5_NSTATE // (S5_WIDTH // LANES)

    @pl.when(pl.program_id(1) == 0)
    def _():
        carry[...] = jnp.zeros_like(carry)

    tri = tri_ref[...]
    for s in range(S5_WIDTH // LANES):
        u = u_ref[:, s * LANES:(s + 1) * LANES]
        u16 = u.astype(BF16)
        for jb in range(slab // blk):
            cols = slice(s * slab + jb * blk, s * slab + (jb + 1) * blk)
            bu = _dot(u16, bm_ref[s, :, jb * blk:(jb + 1) * blk])
            for c in range(tl // L):
                rows = slice(c * L, (c + 1) * L)
                v = bu[rows, :]
                z = v * q1_ref[:, cols] + _cswap(v) * q2_ref[:, cols]
                w = _dot(tri, z.astype(BF16)) + carry[0:1, cols]
                x = w * p1_ref[:, cols] + _cswap(w) * p2_ref[:, cols]
                xl = x[L - 1:L, :]
                carry[0:1, cols] = xl * p1_ref[1:2, cols] + _cswap(xl) * p2_ref[1:2, cols]
                x_s[rows, jb * blk:(jb + 1) * blk] = x.astype(BF16)
        y = _dot(x_s[...], cm_ref[s]) + d_ref[:, s * LANES:(s + 1) * LANES] * u
        ys_s[:, s * LANES:(s + 1) * LANES] = jax.nn.gelu(y)
    ys = ys_s[...]
    o_ref[...] = (ys * jax.nn.sigmoid(_dot(ys.astype(BF16), gw_ref[...]) + gb_ref[...])).astype(BF16)


def _s5(proj, bm, pr, pi, qr, qi, tri, cm, d, gw, gb, batch, seq):
    T = proj.shape[0]
    tl = TL_S5
    n_seq = seq // tl
    c2 = lambda b, i: (0, 0)
    c3 = lambda b, i: (0, 0, 0)
    return pl.pallas_call(
        _s5_kernel,
        grid=(batch, n_seq),
        in_specs=[pl.BlockSpec((tl, S5_WIDTH), lambda b, i: (b * n_seq + i, 0)),
                  pl.BlockSpec(bm.shape, c3),
                  pl.BlockSpec(pr.shape, c2), pl.BlockSpec(pi.shape, c2),
                  pl.BlockSpec(qr.shape, c2), pl.BlockSpec(qi.shape, c2),
                  pl.BlockSpec(tri.shape, c2),
                  pl.BlockSpec(cm.shape, c3),
                  pl.BlockSpec((1, S5_WIDTH), c2),
                  pl.BlockSpec((S5_WIDTH, S5_WIDTH), c2),
                  pl.BlockSpec((1, S5_WIDTH), c2)],
        out_specs=pl.BlockSpec((tl, S5_WIDTH), lambda b, i: (b * n_seq + i, 0)),
        out_shape=jax.ShapeDtypeStruct((T, S5_WIDTH), BF16),
        scratch_shapes=[pltpu.VMEM((SUBLANES, 2 * S5_NSTATE), F32),
                        pltpu.VMEM((tl, 2 * S5_NSTATE // (S5_WIDTH // LANES)), BF16),
                        pltpu.VMEM((tl, S5_WIDTH), F32)],
        compiler_params=_cparams("arbitrary", "arbitrary"),
        name="s5_glu",
    )(proj, bm, pr, pi, qr, qi, tri, cm, d, gw, gb)


def _s5_params(lam_re, lam_im, b_re, b_im, c_re, c_im, log_step):
    G, N, P = S5_GROUPS, S5_STATE, S5_GROUP
    gs = LANES // P
    ns = S5_WIDTH // LANES
    step = jnp.exp(log_step.astype(F32))[:, None]
    lr, li = lam_re.astype(F32), lam_im.astype(F32)
    mag = jnp.exp(lr * step)
    ar, ai = mag * jnp.cos(li * step), mag * jnp.sin(li * step)
    den = lr * lr + li * li
    cr = ((ar - 1.0) * lr + ai * li) / den
    ci = (ai * lr - (ar - 1.0) * li) / den
    bbr = cr[..., None] * b_re - ci[..., None] * b_im
    bbi = cr[..., None] * b_im + ci[..., None] * b_re
    eye = jnp.eye(gs, dtype=F32)

    def in_blockdiag(t):
        t = t.reshape(ns, gs, N, P).transpose(0, 1, 3, 2)
        return jnp.einsum('ab,sapn->sapbn', eye, t).reshape(ns, gs * P, gs * N)

    def out_blockdiag(t):
        t = t.reshape(ns, gs, P, N).transpose(0, 1, 3, 2)
        return jnp.einsum('ab,sanp->sanbp', eye, t).reshape(ns, gs * N, gs * P)

    def interleave(re, im, axis):
        shp = list(re.shape)
        blocked = shp[:axis] + [shp[axis] // LANES, LANES] + shp[axis + 1:]
        both = jnp.stack([re.reshape(blocked), im.reshape(blocked)], axis=axis + 1)
        return both.reshape(shp[:axis] + [2 * shp[axis]] + shp[axis + 1:])

    bm = interleave(in_blockdiag(bbr), in_blockdiag(bbi), 2).astype(BF16)
    cm = interleave(out_blockdiag(c_re), out_blockdiag(-c_im), 1).astype(BF16)
    j = jnp.arange(L_S5, dtype=F32)[:, None]
    la = (lr * step).reshape(1, G * N)
    th = (li * step).reshape(1, G * N)
    pmag, qmag = jnp.exp(j * la), jnp.exp(-(j * la))
    cs, sn = jnp.cos(j * th), jnp.sin(j * th)
    pr, pi, qr, qi = pmag * cs, pmag * sn, qmag * cs, -(qmag * sn)
    return (bm, interleave(pr, pr, 1), interleave(-pi, pi, 1),
            interleave(qr, qr, 1), interleave(-qi, qi, 1), cm)


def _gla_cumsum_matrix(tl):
    r = np.arange(tl)[:, None]
    c = np.arange(tl)[None, :]
    same = (r // C_GLA) == (c // C_GLA)
    return np.concatenate([same & (c <= r), same & (c > r)], axis=0).astype(np.float32)


def _gla_kernel(q_ref, k_ref, v_ref, g_ref, gk_ref, gw_ref, gb_ref, nrm_ref, cum_ref, o_ref, st):
    tl = q_ref.shape[0]
    C = C_GLA
    blk = 2 * C
    pair = LANES // GLA_DK

    @pl.when(pl.program_id(1) == 0)
    def _():
        st[...] = jnp.zeros_like(st)

    z = _dot(gk_ref[...].astype(BF16), gw_ref[...]) + gb_ref[...]
    log_a = _log_sigmoid(z) * (1.0 / GLA_TAU)
    hi, lo = _split_bf16(log_a)
    sums = _dot(cum_ref[...], hi) + _dot(cum_ref[...], lo)
    bc, suffix = sums[:tl], sums[tl:]
    eb = jnp.exp(bc)
    q_dec = q_ref[...] * (GLA_DK ** -0.5) * eb
    k = k_ref[...]
    k_inv = (k * jnp.exp(-bc)).astype(BF16)
    k_dec = k * jnp.exp(suffix)

    lane = lax.broadcasted_iota(jnp.int32, (1, LANES), 1)
    rb = lax.broadcasted_iota(jnp.int32, (blk, blk), 0)
    cb = lax.broadcasted_iota(jnp.int32, (blk, blk), 1)
    causal = (rb >= cb) & ((rb < C) | (cb >= C))
    heads = []
    for h in range(GLA_HEADS):
        hp, hh = divmod(h, pair)
        cols = slice(hp * LANES, (hp + 1) * LANES)
        in_head = (lane >= hh * GLA_DK) & (lane < (hh + 1) * GLA_DK)
        heads.append(dict(
            cols=cols,
            qd=jnp.where(in_head, q_dec[:, cols], 0.0).astype(BF16),
            kd=jnp.where(in_head, k_dec[:, cols], 0.0).astype(BF16),
            ki=k_inv[:, cols],
            vh=v_ref[:, h * GLA_DV:(h + 1) * GLA_DV].astype(BF16),
            state=st[h]))
    for b in range(tl // blk):
        rows = slice(b * blk, (b + 1) * blk)
        for h, hd in enumerate(heads):
            att = jnp.where(causal, _dot_nt(hd["qd"][rows], hd["ki"][rows]), 0.0)
            o = _dot(att.astype(BF16), hd["vh"][rows])
            inter = []
            for c in range(b * blk // C, (b + 1) * blk // C):
                crow = slice(c * C, (c + 1) * C)
                inter.append(_dot_nt(hd["qd"][crow], hd["state"].astype(BF16)))
                decay = eb[(c + 1) * C - 1:(c + 1) * C, hd["cols"]]
                hd["state"] = hd["state"] * decay + _dot_tn(hd["vh"][crow], hd["kd"][crow])
            o = _rms(o + jnp.concatenate(inter, axis=0), nrm_ref[...])
            gh = g_ref[rows, h * GLA_DV:(h + 1) * GLA_DV]
            o_ref[rows, h * GLA_DV:(h + 1) * GLA_DV] = (o * jax.nn.silu(gh)).astype(BF16)
    for h, hd in enumerate(heads):
        st[h] = hd["state"]


def _gla(proj, gw, gb, nrm, tri, batch, seq):
    T = proj.shape[0]
    tl = TL_GLA
    n_seq = seq // tl
    hk = GLA_HEADS * GLA_DK
    hv = GLA_HEADS * GLA_DV
    c2 = lambda b, i: (0, 0)
    q0 = S5_WIDTH // hk
    v0 = (S5_WIDTH + 2 * hk) // hv
    gk0 = (S5_WIDTH + 2 * hk + 2 * hv) // GK_PAD
    return pl.pallas_call(
        _gla_kernel,
        grid=(batch, n_seq),
        in_specs=[pl.BlockSpec((tl, hk), lambda b, i: (b * n_seq + i, q0)),
                  pl.BlockSpec((tl, hk), lambda b, i: (b * n_seq + i, q0 + 1)),
                  pl.BlockSpec((tl, hv), lambda b, i: (b * n_seq + i, v0)),
                  pl.BlockSpec((tl, hv), lambda b, i: (b * n_seq + i, v0 + 1)),
                  pl.BlockSpec((tl, GK_PAD), lambda b, i: (b * n_seq + i, gk0)),
                  pl.BlockSpec((GK_PAD, hk), c2),
                  pl.BlockSpec((1, hk), c2),
                  pl.BlockSpec((1, GLA_DV), c2),
                  pl.BlockSpec((2 * tl, tl), c2)],
        out_specs=pl.BlockSpec((tl, hv), lambda b, i: (b * n_seq + i, 0)),
        out_shape=jax.ShapeDtypeStruct((T, hv), BF16),
        scratch_shapes=[pltpu.VMEM((GLA_HEADS, GLA_DV, LANES), F32)],
        compiler_params=_cparams("arbitrary", "arbitrary"),
        name="gla",
    )(proj, proj, proj, proj, proj, gw, gb, nrm, tri)


def _block_diag(w):
    nb, a, b = w.shape
    return jnp.einsum('hk,hij->hikj', jnp.eye(nb, dtype=w.dtype), w).reshape(nb * a, nb * b)


def _rope_tables(seq):
    half = ROPE_DIM // 2
    pos = jnp.arange(seq, dtype=F32)
    inv = ROPE_THETA ** (-jnp.arange(0, ROPE_DIM, 2, dtype=F32) / ROPE_DIM)
    ang = pos[:, None] * inv[None, :]
    cos, sin = jnp.cos(ang), jnp.sin(ang)
    rest = ATT_HEAD_DIM - ROPE_DIM
    ones = jnp.ones((seq, rest), F32)
    zeros = jnp.zeros((seq, rest), F32)
    zh = jnp.zeros((seq, half), F32)
    per_head = lambda parts: jnp.tile(jnp.concatenate(parts, axis=1), (1, LANES // ATT_HEAD_DIM))
    return (per_head([cos, cos, ones]), per_head([-sin, zh, zeros]), per_head([zh, sin, zeros]))


def kernel(x, e_norm, e_w_in, e_conv_w, e_conv_b, e_gate_a_w, e_gate_a_b, e_gate_x_w, e_gate_x_b, e_lambda, e_q_norm, e_k_norm, e_w_out, o_norm, o_w_in, o_lambda_re, o_lambda_im, o_b_re, o_b_im, o_c_re, o_c_im, o_d, o_log_step, o_glu_w, o_glu_b, o_gk_w, o_gk_b, o_gla_norm, o_w_out, f_norm, f_w_in, f_conv_w, f_conv_b, f_w_out):
    B, S, D = x.shape
    T = B * S
    depth = f_norm.shape[0]
    row = lambda t: t.reshape(1, -1).astype(F32)
    xt = x.reshape(T, D)

    cos_t, s1_t, s2_t = _rope_tables(S)
    head_seg = jnp.asarray(np.kron(np.eye(LANES // ATT_HEAD_DIM), np.ones((ATT_HEAD_DIM, ATT_HEAD_DIM))), BF16)
    att_bias = jnp.asarray(_attention_bias())
    tri_s5 = jnp.asarray(np.tril(np.ones((L_S5, L_S5))), BF16)
    tri_gla = jnp.asarray(_gla_cumsum_matrix(TL_GLA), BF16)
    two_heads = lambda t: jnp.tile(row(t), (1, LANES // ATT_HEAD_DIM))

    for layer in range(depth):
        i = layer // 2
        if layer % 2 == 0:
            xg, q, k, v = _even_in(xt, row(e_norm[i]), e_w_in[i].astype(BF16),
                                   two_heads(e_q_norm[i]), two_heads(e_k_norm[i]),
                                   head_seg, cos_t, s1_t, s2_t, S)
            ya = _lru(xg, e_conv_w[i], row(e_conv_b[i]),
                      _block_diag(e_gate_a_w[i]).astype(BF16), row(e_gate_a_b[i]),
                      _block_diag(e_gate_x_w[i]).astype(BF16), row(e_gate_x_b[i]),
                      row(e_lambda[i]), B, S)
            yb = _attention(q.reshape(B, S, ATT_WIDTH), k.reshape(B, S, ATT_WIDTH),
                            v.reshape(B, S, ATT_WIDTH), att_bias).reshape(T, ATT_WIDTH)
            w_out = e_w_out[i]
        else:
            w_in = jnp.pad(o_w_in[i], ((0, 0), (0, GK_PAD - GLA_LOWRANK))).astype(BF16)
            proj = _odd_in(xt, row(o_norm[i]), w_in)
            bm, pr, pi, qr, qi, cm = _s5_params(o_lambda_re[i], o_lambda_im[i], o_b_re[i], o_b_im[i],
                                                     o_c_re[i], o_c_im[i], o_log_step[i])
            ya = _s5(proj, bm, pr, pi, qr, qi, tri_s5, cm, row(o_d[i]),
                     o_glu_w[i].astype(BF16), row(o_glu_b[i]), B, S)
            gk_w = jnp.pad(o_gk_w[i], ((0, GK_PAD - GLA_LOWRANK), (0, 0))).astype(BF16)
            yb = _gla(proj, gk_w, row(o_gk_b[i]), row(o_gla_norm[i]), tri_gla, B, S)
            w_out = o_w_out[i]
        xt = _ffn(xt, ya, yb, w_out.astype(BF16), row(f_norm[layer]), f_w_in[layer].astype(BF16),
                  f_conv_w[layer], row(f_conv_b[layer]), f_w_out[layer].astype(BF16), S)
    return xt.reshape(B, S, D)
```

```python
import functools
import math

import numpy as np
import jax
import jax.numpy as jnp
from jax import lax
from jax.experimental import pallas as pl
from jax.experimental.pallas import tpu as pltpu

F32 = jnp.float32
BF16 = jnp.bfloat16

D_MODEL = 1024
LRU_WIDTH = 512
LRU_BLOCKS = 8
LRU_CONV = 4
LRU_C = 8.0
ATT_HEADS = 8
ATT_HEAD_DIM = 64
ATT_WIDTH = 512
DILATED_PATTERNS = ((128, 1), (512, 4), (2048, 16))
ATT_SPAN = 2048
ROPE_THETA = 500000.0
ROPE_DIM = 16
S5_WIDTH = 512
S5_GROUP = 16
S5_GROUPS = 32
S5_STATE = 64
S5_NSTATE = S5_GROUPS * S5_STATE
GLA_HEADS = 4
GLA_DK = 64
GLA_DV = 128
GLA_LOWRANK = 16
GLA_TAU = 16.0
D_FF = 3 * D_MODEL
EPS = 1e-6
NEG_INF = -1e30

LANES = 128
SUBLANES = 8
VMEM_LIMIT = 56 * 1024 * 1024

TM_PROJ = 512
TM_FFN = 512
TF_FFN = 512
TL_LRU = 256
TL_S5 = 512
L_S5 = 128
TL_GLA = 512
C_GLA = 64
GK_PAD = 128


def _cparams(*sem):
    return pltpu.CompilerParams(dimension_semantics=sem, vmem_limit_bytes=VMEM_LIMIT)


def _rms(x, g):
    return x * lax.rsqrt(jnp.mean(x * x, axis=-1, keepdims=True) + EPS) * g


def _log_sigmoid(x):
    return -(jnp.maximum(-x, 0.0) + jnp.log1p(jnp.exp(-jnp.abs(x))))


def _split_bf16(x):
    hi = x.astype(BF16)
    lo = (x - hi.astype(F32)).astype(BF16)
    return hi, lo


def _dot(a, b):
    return jnp.dot(a, b, preferred_element_type=F32)


def _dot_nt(a, b):
    return lax.dot_general(a, b, (((1,), (1,)), ((), ())), preferred_element_type=F32)


def _dot_tn(a, b):
    return lax.dot_general(a, b, (((0,), (0,)), ((), ())), preferred_element_type=F32)


def _even_in_kernel(x_ref, g_ref, w_ref, qn_ref, kn_ref, seg_ref, cos_ref, s1_ref, s2_ref,
                    xg_ref, q_ref, k_ref, v_ref):
    h = _rms(x_ref[...], g_ref[...]).astype(BF16)
    seg = seg_ref[...]
    cos, s1, s2 = cos_ref[...], s1_ref[...], s2_ref[...]
    half = ROPE_DIM // 2
    for off, n_ref, dst, scale in ((2 * LRU_WIDTH, qn_ref, q_ref, ATT_HEAD_DIM ** -0.5),
                                   (2 * LRU_WIDTH + ATT_WIDTH, kn_ref, k_ref, 1.0)):
        y = _dot(h, w_ref[:, off:off + ATT_WIDTH])
        for c in range(ATT_WIDTH // LANES):
            t = y[:, c * LANES:(c + 1) * LANES]
            ms = _dot((t * t).astype(BF16), seg) * (1.0 / ATT_HEAD_DIM)
            tn = t * lax.rsqrt(ms + EPS) * n_ref[...]
            r = (tn * cos + pltpu.roll(tn, LANES - half, 1) * s1 + pltpu.roll(tn, half, 1) * s2)
            dst[:, c * LANES:(c + 1) * LANES] = r * scale
    xg_ref[...] = _dot(h, w_ref[:, :2 * LRU_WIDTH])
    v_ref[...] = _dot(h, w_ref[:, 2 * LRU_WIDTH + 2 * ATT_WIDTH:])


def _even_in(x, g, w, qn, kn, seg, cos_t, s1_t, s2_t, seq):
    T = x.shape[0]
    tm = TM_PROJ
    n_seq = seq // tm
    ncol = w.shape[1]
    full = lambda i: (0, 0)
    tab = lambda i: (i % n_seq, 0)
    row = lambda i: (i, 0)
    return pl.pallas_call(
        _even_in_kernel,
        grid=(T // tm,),
        in_specs=[pl.BlockSpec((tm, D_MODEL), row),
                  pl.BlockSpec((1, D_MODEL), full),
                  pl.BlockSpec((D_MODEL, ncol), full),
                  pl.BlockSpec((1, LANES), full),
                  pl.BlockSpec((1, LANES), full),
                  pl.BlockSpec((LANES, LANES), full),
                  pl.BlockSpec((tm, LANES), tab),
                  pl.BlockSpec((tm, LANES), tab),
                  pl.BlockSpec((tm, LANES), tab)],
        out_specs=[pl.BlockSpec((tm, 2 * LRU_WIDTH), row),
                   pl.BlockSpec((tm, ATT_WIDTH), row),
                   pl.BlockSpec((tm, ATT_WIDTH), row),
                   pl.BlockSpec((tm, ATT_WIDTH), row)],
        out_shape=[jax.ShapeDtypeStruct((T, 2 * LRU_WIDTH), F32),
                   jax.ShapeDtypeStruct((T, ATT_WIDTH), F32),
                   jax.ShapeDtypeStruct((T, ATT_WIDTH), F32),
                   jax.ShapeDtypeStruct((T, ATT_WIDTH), F32)],
        compiler_params=_cparams("arbitrary"),
        name="even_in_proj",
    )(x, g, w, qn, kn, seg, cos_t, s1_t, s2_t)


def _lru_kernel(xl_ref, gl_ref, cw_ref, cb_ref, wa_ref, ba_ref, wx_ref, bx_ref, lam_ref,
                o_ref, xbuf, hprev):
    tl = xl_ref.shape[0]

    @pl.when(pl.program_id(1) == 0)
    def _():
        xbuf[0:SUBLANES, :] = jnp.zeros((SUBLANES, LRU_WIDTH), F32)
        hprev[...] = jnp.zeros_like(hprev)

    x = xl_ref[...]
    xbuf[SUBLANES:SUBLANES + tl, :] = x
    conv = cb_ref[...] + cw_ref[LRU_CONV - 1:LRU_CONV, :] * x
    for j in range(LRU_CONV - 1):
        conv = conv + cw_ref[j:j + 1, :] * xbuf[pl.ds(SUBLANES - (LRU_CONV - 1) + j, tl), :]
    xbuf[0:SUBLANES, :] = x[tl - SUBLANES:tl, :]

    c16 = conv.astype(BF16)
    r = jax.nn.sigmoid(_dot(c16, wa_ref[...]) + ba_ref[...])
    ig = jax.nn.sigmoid(_dot(c16, wx_ref[...]) + bx_ref[...])
    log_a = (LRU_C * r) * _log_sigmoid(lam_ref[...])
    a = jnp.exp(log_a)
    b = jnp.sqrt(1.0 - a * a) * (ig * conv)

    a = a.reshape(tl // SUBLANES, SUBLANES, LRU_WIDTH)
    b = b.reshape(tl // SUBLANES, SUBLANES, LRU_WIDTH)
    sub = lax.broadcasted_iota(jnp.int32, (1, SUBLANES, 1), 1)
    d = 1
    while d < SUBLANES:
        keep = sub >= d
        a_sh = jnp.where(keep, pltpu.roll(a, d, 1), 1.0)
        b_sh = jnp.where(keep, pltpu.roll(b, d, 1), 0.0)
        b = a * b_sh + b
        a = a * a_sh
        d *= 2
    last = hprev[0:1, :]
    groups = []
    for t in range(tl // SUBLANES):
        ht = b[t] + a[t] * last
        groups.append(ht)
        last = ht[SUBLANES - 1:SUBLANES, :]
    hprev[0:1, :] = last
    h = jnp.concatenate(groups, axis=0)
    o_ref[...] = (h * jax.nn.gelu(gl_ref[...])).astype(BF16)


def _lru(xg, cw, cb, wa, ba, wx, bx, lam, batch, seq):
    T = xg.shape[0]
    tl = TL_LRU
    n_seq = seq // tl
    full = lambda b, i: (0, 0)
    return pl.pallas_call(
        _lru_kernel,
        grid=(batch, n_seq),
        in_specs=[pl.BlockSpec((tl, LRU_WIDTH), lambda b, i: (b * n_seq + i, 0)),
                  pl.BlockSpec((tl, LRU_WIDTH), lambda b, i: (b * n_seq + i, 1)),
                  pl.BlockSpec((LRU_CONV, LRU_WIDTH), full),
                  pl.BlockSpec((1, LRU_WIDTH), full),
                  pl.BlockSpec((LRU_WIDTH, LRU_WIDTH), full),
                  pl.BlockSpec((1, LRU_WIDTH), full),
                  pl.BlockSpec((LRU_WIDTH, LRU_WIDTH), full),
                  pl.BlockSpec((1, LRU_WIDTH), full),
                  pl.BlockSpec((1, LRU_WIDTH), full)],
        out_specs=pl.BlockSpec((tl, LRU_WIDTH), lambda b, i: (b * n_seq + i, 0)),
        out_shape=jax.ShapeDtypeStruct((T, LRU_WIDTH), BF16),
        scratch_shapes=[pltpu.VMEM((SUBLANES + tl, LRU_WIDTH), F32),
                        pltpu.VMEM((SUBLANES, LRU_WIDTH), F32)],
        compiler_params=_cparams("arbitrary", "arbitrary"),
        name="rg_lru",
    )(xg, xg, cw, cb, wa, ba, wx, bx, lam)


N_BACK = DILATED_PATTERNS[0][0] // DILATED_PATTERNS[0][1]
Q_BLOCKS = ATT_SPAN // N_BACK


def _attention_bias():
    qi = np.arange(N_BACK)[:, None]
    ki = np.arange(2 * N_BACK)[None, :]
    dist = N_BACK + qi - ki
    band = (dist >= 0) & (dist <= N_BACK)
    first = band & (ki >= N_BACK)
    return np.where(np.stack([band, first]), 0.0, NEG_INF).astype(np.float32)


def _attn_kernel(q_ref, k_ref, v_ref, bias_ref, o_ref, *scratch):
    kv_s = scratch[:6]
    o_s, m_s, l_s = scratch[6:]
    sb = pl.program_id(2)
    lane = lax.broadcasted_iota(jnp.int32, (1, LANES), 1)
    head0 = lane < ATT_HEAD_DIM

    for p, (window, dil) in enumerate(DILATED_PATTERNS):
        per_res = Q_BLOCKS // dil
        span = N_BACK * per_res
        for src, dst in ((k_ref, kv_s[2 * p]), (v_ref, kv_s[2 * p + 1])):
            @pl.when(sb == 0)
            def _():
                dst[:, 0:N_BACK, :] = jnp.zeros((dil, N_BACK, LANES), BF16)

            @pl.when(sb > 0)
            def _():
                dst[:, 0:N_BACK, :] = dst[:, span:span + N_BACK, :]

            for r in range(dil):
                rows = pl.ds(r, span, stride=dil) if dil > 1 else pl.ds(0, span)
                dst[r, N_BACK:N_BACK + span, :] = src[0, rows, :].astype(BF16)

    for p, (window, dil) in enumerate(DILATED_PATTERNS):
        per_res = Q_BLOCKS // dil
        k_s, v_s = kv_s[2 * p], kv_s[2 * p + 1]

        for n in range(Q_BLOCKS):
            r, m = n % dil, n // dil
            if dil > 1:
                rows = pl.ds(m * (N_BACK * dil) + r, N_BACK, stride=dil)
            else:
                rows = pl.ds(m * N_BACK, N_BACK)
            q = q_ref[0, rows, :].astype(BF16)
            kc = k_s[r, m * N_BACK:(m + 2) * N_BACK, :]
            vc = v_s[r, m * N_BACK:(m + 2) * N_BACK, :]
            bias = bias_ref[jnp.where(sb == 0, 1, 0)] if m == 0 else bias_ref[0]
            res = []
            for h in range(LANES // ATT_HEAD_DIM):
                qm = jnp.where(head0 if h == 0 else ~head0, q, jnp.zeros_like(q))
                s = _dot_nt(qm, kc) + bias
                mx = jnp.max(s, axis=-1, keepdims=True)
                e = jnp.exp(s - mx)
                res.append((_dot(e.astype(BF16), vc), mx, jnp.sum(e, axis=-1, keepdims=True)))
            for dst, idx in ((o_s, 0), (m_s, 1), (l_s, 2)):
                dst[p, rows, :] = jnp.where(head0, res[0][idx], res[1][idx])

    mx = jnp.maximum(jnp.maximum(m_s[0], m_s[1]), m_s[2])
    num = jnp.zeros_like(mx)
    den = jnp.zeros_like(mx)
    for p in range(len(DILATED_PATTERNS)):
        w = jnp.exp(m_s[p] - mx)
        num = num + w * o_s[p]
        den = den + w * l_s[p]
    o_ref[0] = (num / den).astype(BF16)


def _attention(q, k, v, bias):
    B, S, W = q.shape
    blk = pl.BlockSpec((1, ATT_SPAN, LANES), lambda b, p, i: (b, i, p))
    kv_scratch = []
    for window, dil in DILATED_PATTERNS:
        shape = (dil, N_BACK * (1 + Q_BLOCKS // dil), LANES)
        kv_scratch += [pltpu.VMEM(shape, BF16), pltpu.VMEM(shape, BF16)]
    acc = pltpu.VMEM((len(DILATED_PATTERNS), ATT_SPAN, LANES), F32)
    return pl.pallas_call(
        _attn_kernel,
        grid=(B, W // LANES, S // ATT_SPAN),
        in_specs=[blk, blk, blk, pl.BlockSpec(bias.shape, lambda b, p, i: (0, 0, 0))],
        out_specs=blk,
        out_shape=jax.ShapeDtypeStruct((B, S, W), BF16),
        scratch_shapes=kv_scratch + [acc, acc, acc],
        compiler_params=_cparams("arbitrary", "arbitrary", "arbitrary"),
        name="dilated_attention",
    )(q, k, v, bias)


def _ffn_kernel(x_ref, ya_ref, yb_ref, wo_ref, g_ref, w1_ref, cw_ref, cb_ref, w2_ref,
                o_ref, act_s, abuf, carry_s, *, tiles_per_seq):
    i = pl.program_id(0)
    tm = x_ref.shape[0]
    half = ya_ref.shape[1]
    tf = abuf.shape[1]

    @pl.when(i % tiles_per_seq == 0)
    def _():
        carry_s[...] = jnp.zeros_like(carry_s)

    x1 = (x_ref[...] + _dot(ya_ref[...], wo_ref[0:half, :])
          + _dot(yb_ref[...], wo_ref[half:2 * half, :]))
    h = _rms(x1, g_ref[...]).astype(BF16)
    for c in range(D_FF // tf):
        cols = slice(c * tf, (c + 1) * tf)
        a = _dot(h, w1_ref[:, cols])
        lin = _dot(h, w1_ref[:, D_FF + c * tf:D_FF + (c + 1) * tf])
        abuf[0:SUBLANES, :] = carry_s[:, cols]
        abuf[SUBLANES:SUBLANES + tm, :] = a
        carry_s[:, cols] = a[tm - SUBLANES:tm, :]
        conv = (cb_ref[:, cols] + cw_ref[2:3, cols] * a
                + cw_ref[1:2, cols] * abuf[pl.ds(SUBLANES - 1, tm), :]
                + cw_ref[0:1, cols] * abuf[pl.ds(SUBLANES - 2, tm), :])
        act_s[:, cols] = (jax.nn.gelu(conv) * lin).astype(BF16)
    o_ref[...] = x1 + _dot(act_s[...], w2_ref[...])


def _ffn(x, ya, yb, wo, g, w_in, cw, cb, w2, seq):
    T = x.shape[0]
    tm, tf = TM_FFN, TF_FFN
    half = ya.shape[1]
    row = lambda i: (i, 0)
    resident = lambda shape: pl.BlockSpec(shape, lambda i: (0, 0), pipeline_mode=pl.Buffered(1))
    return pl.pallas_call(
        functools.partial(_ffn_kernel, tiles_per_seq=seq // tm),
        grid=(T // tm,),
        in_specs=[pl.BlockSpec((tm, D_MODEL), row),
                  pl.BlockSpec((tm, half), row),
                  pl.BlockSpec((tm, half), row),
                  resident((2 * half, D_MODEL)),
                  resident((1, D_MODEL)),
                  resident((D_MODEL, 2 * D_FF)),
                  resident((3, D_FF)),
                  resident((1, D_FF)),
                  resident((D_FF, D_MODEL))],
        out_specs=pl.BlockSpec((tm, D_MODEL), row),
        out_shape=jax.ShapeDtypeStruct((T, D_MODEL), F32),
        scratch_shapes=[pltpu.VMEM((tm, D_FF), BF16),
                        pltpu.VMEM((SUBLANES + tm, tf), F32),
                        pltpu.VMEM((SUBLANES, D_FF), F32)],
        compiler_params=_cparams("arbitrary"),
        name="outproj_conv_mlp",
    )(x, ya, yb, wo, g, w_in, cw, cb, w2)


def _odd_in_kernel(x_ref, g_ref, w_ref, o_ref):
    h = _rms(x_ref[...], g_ref[...])
    o_ref[...] = _dot(h.astype(BF16), w_ref[...])


def _odd_in(x, g, w):
    T = x.shape[0]
    tm = TM_PROJ
    ncol = w.shape[1]
    return pl.pallas_call(
        _odd_in_kernel,
        grid=(T // tm,),
        in_specs=[pl.BlockSpec((tm, D_MODEL), lambda i: (i, 0)),
                  pl.BlockSpec((1, D_MODEL), lambda i: (0, 0)),
                  pl.BlockSpec((D_MODEL, ncol), lambda i: (0, 0))],
        out_specs=pl.BlockSpec((tm, ncol), lambda i: (i, 0)),
        out_shape=jax.ShapeDtypeStruct((T, ncol), F32),
        compiler_params=_cparams("arbitrary"),
        name="odd_in_proj",
    )(x, g, w)


def _cswap(v):
    return jnp.concatenate([v[:, LANES:], v[:, :LANES]], axis=1)


def _s5_kernel(u_ref, bm_ref, p1_ref, p2_ref, q1_ref, q2_ref, tri_ref, cm_ref,
               d_ref, gw_ref, gb_ref, o_ref, carry, x_s, ys_s):
    tl = u_ref.shape[0]
    L = L_S5
    blk = 2 * LANES
    slab = 2 * S5_NSTATE // (S5_WIDTH // LANES)

    @pl.when(pl.program_id(1) == 0)
    def _():
        carry[...] = jnp.zeros_like(carry)

    tri = tri_ref[...]
    for s in range(S5_WIDTH // LANES):
        u = u_ref[:, s * LANES:(s + 1) * LANES]
        u16 = u.astype(BF16)
        for jb in range(slab // blk):
            cols = slice(s * slab + jb * blk, s * slab + (jb + 1) * blk)
            bu = _dot(u16, bm_ref[s, :, jb * blk:(jb + 1) * blk])
            for c in range(tl // L):
                rows = slice(c * L, (c + 1) * L)
                v = bu[rows, :]
                z = v * q1_ref[:, cols] + _cswap(v) * q2_ref[:, cols]
                w = _dot(tri, z.astype(BF16)) + carry[0:1, cols]
                x = w * p1_ref[:, cols] + _cswap(w) * p2_ref[:, cols]
                xl = x[L - 1:L, :]
                carry[0:1, cols] = xl * p1_ref[1:2, cols] + _cswap(xl) * p2_ref[1:2, cols]
                x_s[rows, jb * blk:(jb + 1) * blk] = x.astype(BF16)
        y = _dot(x_s[...], cm_ref[s]) + d_ref[:, s * LANES:(s + 1) * LANES] * u
        ys_s[:, s * LANES:(s + 1) * LANES] = jax.nn.gelu(y)
    ys = ys_s[...]
    o_ref[...] = (ys * jax.nn.sigmoid(_dot(ys.astype(BF16), gw_ref[...]) + gb_ref[...])).astype(BF16)


def _s5(proj, bm, pr, pi, qr, qi, tri, cm, d, gw, gb, batch, seq):
    T = proj.shape[0]
    tl = TL_S5
    n_seq = seq // tl
    c2 = lambda b, i: (0, 0)
    c3 = lambda b, i: (0, 0, 0)
    return pl.pallas_call(
        _s5_kernel,
        grid=(batch, n_seq),
        in_specs=[pl.BlockSpec((tl, S5_WIDTH), lambda b, i: (b * n_seq + i, 0)),
                  pl.BlockSpec(bm.shape, c3),
                  pl.BlockSpec(pr.shape, c2), pl.BlockSpec(pi.shape, c2),
                  pl.BlockSpec(qr.shape, c2), pl.BlockSpec(qi.shape, c2),
                  pl.BlockSpec(tri.shape, c2),
                  pl.BlockSpec(cm.shape, c3),
                  pl.BlockSpec((1, S5_WIDTH), c2),
                  pl.BlockSpec((S5_WIDTH, S5_WIDTH), c2),
                  pl.BlockSpec((1, S5_WIDTH), c2)],
        out_specs=pl.BlockSpec((tl, S5_WIDTH), lambda b, i: (b * n_seq + i, 0)),
        out_shape=jax.ShapeDtypeStruct((T, S5_WIDTH), BF16),
        scratch_shapes=[pltpu.VMEM((SUBLANES, 2 * S5_NSTATE), F32),
                        pltpu.VMEM((tl, 2 * S5_NSTATE // (S5_WIDTH // LANES)), BF16),
                        pltpu.VMEM((tl, S5_WIDTH), F32)],
        compiler_params=_cparams("arbitrary", "arbitrary"),
        name="s5_glu",
    )(proj, bm, pr, pi, qr, qi, tri, cm, d, gw, gb)


def _s5_params(lam_re, lam_im, b_re, b_im, c_re, c_im, log_step):
    G, N, P = S5_GROUPS, S5_STATE, S5_GROUP
    gs = LANES // P
    ns = S5_WIDTH // LANES
    step = jnp.exp(log_step.astype(F32))[:, None]
    lr, li = lam_re.astype(F32), lam_im.astype(F32)
    mag = jnp.exp(lr * step)
    ar, ai = mag * jnp.cos(li * step), mag * jnp.sin(li * step)
    den = lr * lr + li * li
    cr = ((ar - 1.0) * lr + ai * li) / den
    ci = (ai * lr - (ar - 1.0) * li) / den
    bbr = cr[..., None] * b_re - ci[..., None] * b_im
    bbi = cr[..., None] * b_im + ci[..., None] * b_re
    eye = jnp.eye(gs, dtype=F32)

    def in_blockdiag(t):
        t = t.reshape(ns, gs, N, P).transpose(0, 1, 3, 2)
        return jnp.einsum('ab,sapn->sapbn', eye, t).reshape(ns, gs * P, gs * N)

    def out_blockdiag(t):
        t = t.reshape(ns, gs, P, N).transpose(0, 1, 3, 2)
        return jnp.einsum('ab,sanp->sanbp', eye, t).reshape(ns, gs * N, gs * P)

    def interleave(re, im, axis):
        shp = list(re.shape)
        blocked = shp[:axis] + [shp[axis] // LANES, LANES] + shp[axis + 1:]
        both = jnp.stack([re.reshape(blocked), im.reshape(blocked)], axis=axis + 1)
        return both.reshape(shp[:axis] + [2 * shp[axis]] + shp[axis + 1:])

    bm = interleave(in_blockdiag(bbr), in_blockdiag(bbi), 2).astype(BF16)
    cm = interleave(out_blockdiag(c_re), out_blockdiag(-c_im), 1).astype(BF16)
    j = jnp.arange(L_S5, dtype=F32)[:, None]
    la = (lr * step).reshape(1, G * N)
    th = (li * step).reshape(1, G * N)
    pmag, qmag = jnp.exp(j * la), jnp.exp(-(j * la))
    cs, sn = jnp.cos(j * th), jnp.sin(j * th)
    pr, pi, qr, qi = pmag * cs, pmag * sn, qmag * cs, -(qmag * sn)
    return (bm, interleave(pr, pr, 1), interleave(-pi, pi, 1),
            interleave(qr, qr, 1), interleave(-qi, qi, 1), cm)


def _gla_cumsum_matrix(tl):
    r = np.arange(tl)[:, None]
    c = np.arange(tl)[None, :]
    same = (r // C_GLA) == (c // C_GLA)
    return np.concatenate([same & (c <= r), same & (c > r)], axis=0).astype(np.float32)


def _gla_kernel(q_ref, k_ref, v_ref, g_ref, gk_ref, gw_ref, gb_ref, nrm_ref, cum_ref, o_ref, st):
    nbatch, tl = q_ref.shape[0], q_ref.shape[1]
    C = C_GLA
    blk = 2 * C
    pair = LANES // GLA_DK

    @pl.when(pl.program_id(0) == 0)
    def _():
        st[...] = jnp.zeros_like(st)

    lane = lax.broadcasted_iota(jnp.int32, (1, LANES), 1)
    rb = lax.broadcasted_iota(jnp.int32, (blk, blk), 0)
    cb = lax.broadcasted_iota(jnp.int32, (blk, blk), 1)
    causal = (rb >= cb) & ((rb < C) | (cb >= C))
    streams = []
    for n in range(nbatch):
        z = _dot(gk_ref[n].astype(BF16), gw_ref[...]) + gb_ref[...]
        log_a = _log_sigmoid(z) * (1.0 / GLA_TAU)
        hi, lo = _split_bf16(log_a)
        sums = _dot(cum_ref[...], hi) + _dot(cum_ref[...], lo)
        bc, suffix = sums[:tl], sums[tl:]
        eb = jnp.exp(bc)
        q_dec = q_ref[n] * (GLA_DK ** -0.5) * eb
        k = k_ref[n]
        k_inv = (k * jnp.exp(-bc)).astype(BF16)
        k_dec = k * jnp.exp(suffix)
        for h in range(GLA_HEADS):
            hp, hh = divmod(h, pair)
            cols = slice(hp * LANES, (hp + 1) * LANES)
            in_head = (lane >= hh * GLA_DK) & (lane < (hh + 1) * GLA_DK)
            streams.append(dict(
                n=n, h=h, eb=eb[:, cols],
                qd=jnp.where(in_head, q_dec[:, cols], 0.0).astype(BF16),
                kd=jnp.where(in_head, k_dec[:, cols], 0.0).astype(BF16),
                ki=k_inv[:, cols],
                vh=v_ref[n, :, h * GLA_DV:(h + 1) * GLA_DV].astype(BF16),
                state=st[n * GLA_HEADS + h]))
    for b in range(tl // blk):
        rows = slice(b * blk, (b + 1) * blk)
        for sd in streams:
            n, h = sd["n"], sd["h"]
            att = jnp.where(causal, _dot_nt(sd["qd"][rows], sd["ki"][rows]), 0.0)
            o = _dot(att.astype(BF16), sd["vh"][rows])
            inter = []
            for c in range(b * blk // C, (b + 1) * blk // C):
                crow = slice(c * C, (c + 1) * C)
                inter.append(_dot_nt(sd["qd"][crow], sd["state"].astype(BF16)))
                decay = sd["eb"][(c + 1) * C - 1:(c + 1) * C, :]
                sd["state"] = sd["state"] * decay + _dot_tn(sd["vh"][crow], sd["kd"][crow])
            o = _rms(o + jnp.concatenate(inter, axis=0), nrm_ref[...])
            gh = g_ref[n, rows, h * GLA_DV:(h + 1) * GLA_DV]
            o_ref[n, rows, h * GLA_DV:(h + 1) * GLA_DV] = (o * jax.nn.silu(gh)).astype(BF16)
    for sd in streams:
        st[sd["n"] * GLA_HEADS + sd["h"]] = sd["state"]


def _gla(proj, gw, gb, nrm, tri, batch, seq):
    tl = TL_GLA
    hk = GLA_HEADS * GLA_DK
    hv = GLA_HEADS * GLA_DV
    c2 = lambda i: (0, 0)
    q0 = S5_WIDTH // hk
    v0 = (S5_WIDTH + 2 * hk) // hv
    gk0 = (S5_WIDTH + 2 * hk + 2 * hv) // GK_PAD
    proj = proj.reshape(batch, seq, proj.shape[-1])
    out = pl.pallas_call(
        _gla_kernel,
        grid=(seq // tl,),
        in_specs=[pl.BlockSpec((batch, tl, hk), lambda i: (0, i, q0)),
                  pl.BlockSpec((batch, tl, hk), lambda i: (0, i, q0 + 1)),
                  pl.BlockSpec((batch, tl, hv), lambda i: (0, i, v0)),
                  pl.BlockSpec((batch, tl, hv), lambda i: (0, i, v0 + 1)),
                  pl.BlockSpec((batch, tl, GK_PAD), lambda i: (0, i, gk0)),
                  pl.BlockSpec((GK_PAD, hk), c2),
                  pl.BlockSpec((1, hk), c2),
                  pl.BlockSpec((1, GLA_DV), c2),
                  pl.BlockSpec((2 * tl, tl), c2)],
        out_specs=pl.BlockSpec((batch, tl, hv), lambda i: (0, i, 0)),
        out_shape=jax.ShapeDtypeStruct((batch, seq, hv), BF16),
        scratch_shapes=[pltpu.VMEM((batch * GLA_HEADS, GLA_DV, LANES), F32)],
        compiler_params=_cparams("arbitrary"),
        name="gla",
    )(proj, proj, proj, proj, proj, gw, gb, nrm, tri)
    return out.reshape(batch * seq, hv)


def _block_diag(w):
    nb, a, b = w.shape
    return jnp.einsum('hk,hij->hikj', jnp.eye(nb, dtype=w.dtype), w).reshape(nb * a, nb * b)


def _rope_tables(seq):
    half = ROPE_DIM // 2
    pos = jnp.arange(seq, dtype=F32)
    inv = ROPE_THETA ** (-jnp.arange(0, ROPE_DIM, 2, dtype=F32) / ROPE_DIM)
    ang = pos[:, None] * inv[None, :]
    cos, sin = jnp.cos(ang), jnp.sin(ang)
    rest = ATT_HEAD_DIM - ROPE_DIM
    ones = jnp.ones((seq, rest), F32)
    zeros = jnp.zeros((seq, rest), F32)
    zh = jnp.zeros((seq, half), F32)
    per_head = lambda parts: jnp.tile(jnp.concatenate(parts, axis=1), (1, LANES // ATT_HEAD_DIM))
    return (per_head([cos, cos, ones]), per_head([-sin, zh, zeros]), per_head([zh, sin, zeros]))


def kernel(x, e_norm, e_w_in, e_conv_w, e_conv_b, e_gate_a_w, e_gate_a_b, e_gate_x_w, e_gate_x_b, e_lambda, e_q_norm, e_k_norm, e_w_out, o_norm, o_w_in, o_lambda_re, o_lambda_im, o_b_re, o_b_im, o_c_re, o_c_im, o_d, o_log_step, o_glu_w, o_glu_b, o_gk_w, o_gk_b, o_gla_norm, o_w_out, f_norm, f_w_in, f_conv_w, f_conv_b, f_w_out):
    B, S, D = x.shape
    T = B * S
    depth = f_norm.shape[0]
    row = lambda t: t.reshape(1, -1).astype(F32)
    xt = x.reshape(T, D)

    cos_t, s1_t, s2_t = _rope_tables(S)
    head_seg = jnp.asarray(np.kron(np.eye(LANES // ATT_HEAD_DIM), np.ones((ATT_HEAD_DIM, ATT_HEAD_DIM))), BF16)
    att_bias = jnp.asarray(_attention_bias())
    tri_s5 = jnp.asarray(np.tril(np.ones((L_S5, L_S5))), BF16)
    tri_gla = jnp.asarray(_gla_cumsum_matrix(TL_GLA), BF16)
    two_heads = lambda t: jnp.tile(row(t), (1, LANES // ATT_HEAD_DIM))

    for layer in range(depth):
        i = layer // 2
        if layer % 2 == 0:
            xg, q, k, v = _even_in(xt, row(e_norm[i]), e_w_in[i].astype(BF16),
                                   two_heads(e_q_norm[i]), two_heads(e_k_norm[i]),
                                   head_seg, cos_t, s1_t, s2_t, S)
            ya = _lru(xg, e_conv_w[i], row(e_conv_b[i]),
                      _block_diag(e_gate_a_w[i]).astype(BF16), row(e_gate_a_b[i]),
                      _block_diag(e_gate_x_w[i]).astype(BF16), row(e_gate_x_b[i]),
                      row(e_lambda[i]), B, S)
            yb = _attention(q.reshape(B, S, ATT_WIDTH), k.reshape(B, S, ATT_WIDTH),
                            v.reshape(B, S, ATT_WIDTH), att_bias).reshape(T, ATT_WIDTH)
            w_out = e_w_out[i]
        else:
            w_in = jnp.pad(o_w_in[i], ((0, 0), (0, GK_PAD - GLA_LOWRANK))).astype(BF16)
            proj = _odd_in(xt, row(o_norm[i]), w_in)
            bm, pr, pi, qr, qi, cm = _s5_params(o_lambda_re[i], o_lambda_im[i], o_b_re[i], o_b_im[i],
                                                     o_c_re[i], o_c_im[i], o_log_step[i])
            ya = _s5(proj, bm, pr, pi, qr, qi, tri_s5, cm, row(o_d[i]),
                     o_glu_w[i].astype(BF16), row(o_glu_b[i]), B, S)
            gk_w = jnp.pad(o_gk_w[i], ((0, GK_PAD - GLA_LOWRANK), (0, 0))).astype(BF16)
            yb = _gla(proj, gk_w, row(o_gk_b[i]), row(o_gla_norm[i]), tri_gla, B, S)
            w_out = o_w_out[i]
        xt = _ffn(xt, ya, yb, w_out.astype(BF16), row(f_norm[layer]), f_w_in[layer].astype(BF16),
                  f_conv_w[layer], row(f_conv_b[layer]), f_w_out[layer].astype(BF16), S)
    return xt.reshape(B, S, D)
```

```python
import functools
import math

import numpy as np
import jax
import jax.numpy as jnp
from jax import lax
from jax.experimental import pallas as pl
from jax.experimental.pallas import tpu as pltpu

F32 = jnp.float32
BF16 = jnp.bfloat16

D_MODEL = 1024
LRU_WIDTH = 512
LRU_BLOCKS = 8
LRU_CONV = 4
LRU_C = 8.0
ATT_HEADS = 8
ATT_HEAD_DIM = 64
ATT_WIDTH = 512
DILATED_PATTERNS = ((128, 1), (512, 4), (2048, 16))
ATT_SPAN = 2048
ROPE_THETA = 500000.0
ROPE_DIM = 16
S5_WIDTH = 512
S5_GROUP = 16
S5_GROUPS = 32
S5_STATE = 64
S5_NSTATE = S5_GROUPS * S5_STATE
GLA_HEADS = 4
GLA_DK = 64
GLA_DV = 128
GLA_LOWRANK = 16
GLA_TAU = 16.0
D_FF = 3 * D_MODEL
EPS = 1e-6
NEG_INF = -1e30

LANES = 128
SUBLANES = 8
VMEM_LIMIT = 56 * 1024 * 1024

TM_PROJ = 512
TM_FFN = 512
TF_FFN = 512
TL_LRU = 256
TL_S5 = 512
L_S5 = 128
TL_GLA = 512
C_GLA = 64
GK_PAD = 128


def _cparams(*sem):
    return pltpu.CompilerParams(dimension_semantics=sem, vmem_limit_bytes=VMEM_LIMIT)


def _rms(x, g):
    return x * lax.rsqrt(jnp.mean(x * x, axis=-1, keepdims=True) + EPS) * g


def _log_sigmoid(x):
    return -(jnp.maximum(-x, 0.0) + jnp.log1p(jnp.exp(-jnp.abs(x))))


def _split_bf16(x):
    hi = x.astype(BF16)
    lo = (x - hi.astype(F32)).astype(BF16)
    return hi, lo


def _dot(a, b):
    return jnp.dot(a, b, preferred_element_type=F32)


def _dot_nt(a, b):
    return lax.dot_general(a, b, (((1,), (1,)), ((), ())), preferred_element_type=F32)


def _dot_tn(a, b):
    return lax.dot_general(a, b, (((0,), (0,)), ((), ())), preferred_element_type=F32)


def _even_in_kernel(x_ref, g_ref, w_ref, qn_ref, kn_ref, seg_ref, cos_ref, s1_ref, s2_ref,
                    xg_ref, q_ref, k_ref, v_ref):
    h = _rms(x_ref[...], g_ref[...]).astype(BF16)
    seg = seg_ref[...]
    cos, s1, s2 = cos_ref[...], s1_ref[...], s2_ref[...]
    half = ROPE_DIM // 2
    for off, n_ref, dst, scale in ((2 * LRU_WIDTH, qn_ref, q_ref, ATT_HEAD_DIM ** -0.5),
                                   (2 * LRU_WIDTH + ATT_WIDTH, kn_ref, k_ref, 1.0)):
        y = _dot(h, w_ref[:, off:off + ATT_WIDTH])
        for c in range(ATT_WIDTH // LANES):
            t = y[:, c * LANES:(c + 1) * LANES]
            ms = _dot((t * t).astype(BF16), seg) * (1.0 / ATT_HEAD_DIM)
            tn = t * lax.rsqrt(ms + EPS) * n_ref[...]
            r = (tn * cos + pltpu.roll(tn, LANES - half, 1) * s1 + pltpu.roll(tn, half, 1) * s2)
            dst[:, c * LANES:(c + 1) * LANES] = r * scale
    xg_ref[...] = _dot(h, w_ref[:, :2 * LRU_WIDTH])
    v_ref[...] = _dot(h, w_ref[:, 2 * LRU_WIDTH + 2 * ATT_WIDTH:])


def _even_in(x, g, w, qn, kn, seg, cos_t, s1_t, s2_t, seq):
    T = x.shape[0]
    tm = TM_PROJ
    n_seq = seq // tm
    ncol = w.shape[1]
    full = lambda i: (0, 0)
    tab = lambda i: (i % n_seq, 0)
    row = lambda i: (i, 0)
    return pl.pallas_call(
        _even_in_kernel,
        grid=(T // tm,),
        in_specs=[pl.BlockSpec((tm, D_MODEL), row),
                  pl.BlockSpec((1, D_MODEL), full),
                  pl.BlockSpec((D_MODEL, ncol), full),
                  pl.BlockSpec((1, LANES), full),
                  pl.BlockSpec((1, LANES), full),
                  pl.BlockSpec((LANES, LANES), full),
                  pl.BlockSpec((tm, LANES), tab),
                  pl.BlockSpec((tm, LANES), tab),
                  pl.BlockSpec((tm, LANES), tab)],
        out_specs=[pl.BlockSpec((tm, 2 * LRU_WIDTH), row),
                   pl.BlockSpec((tm, ATT_WIDTH), row),
                   pl.BlockSpec((tm, ATT_WIDTH), row),
                   pl.BlockSpec((tm, ATT_WIDTH), row)],
        out_shape=[jax.ShapeDtypeStruct((T, 2 * LRU_WIDTH), F32),
                   jax.ShapeDtypeStruct((T, ATT_WIDTH), F32),
                   jax.ShapeDtypeStruct((T, ATT_WIDTH), F32),
                   jax.ShapeDtypeStruct((T, ATT_WIDTH), F32)],
        compiler_params=_cparams("arbitrary"),
        name="even_in_proj",
    )(x, g, w, qn, kn, seg, cos_t, s1_t, s2_t)


def _lru_kernel(xl_ref, gl_ref, cw_ref, cb_ref, wa_ref, ba_ref, wx_ref, bx_ref, lam_ref,
                o_ref, xbuf, hprev):
    tl = xl_ref.shape[0]

    @pl.when(pl.program_id(1) == 0)
    def _():
        xbuf[...] = jnp.zeros_like(xbuf)
        hprev[...] = jnp.zeros_like(hprev)

    groups = tl // SUBLANES
    sub = lax.broadcasted_iota(jnp.int32, (1, SUBLANES, 1), 1)
    x = xl_ref[...]
    x3 = x.reshape(groups, SUBLANES, LRU_WIDTH)
    prev = xbuf[...].reshape(1, SUBLANES, LRU_WIDTH)
    xbuf[...] = x[tl - SUBLANES:tl, :]
    conv = cb_ref[...] + cw_ref[LRU_CONV - 1:LRU_CONV, :] * x3
    for k in range(1, LRU_CONV):
        r = pltpu.roll(x3, k, 1)
        rp = jnp.concatenate([pltpu.roll(prev, k, 1), r[:groups - 1]], axis=0)
        conv = conv + cw_ref[LRU_CONV - 1 - k:LRU_CONV - k, :] * jnp.where(sub >= k, r, rp)
    conv = conv.reshape(tl, LRU_WIDTH)

    c16 = conv.astype(BF16)
    r = jax.nn.sigmoid(_dot(c16, wa_ref[...]) + ba_ref[...])
    ig = jax.nn.sigmoid(_dot(c16, wx_ref[...]) + bx_ref[...])
    log_a = (LRU_C * r) * _log_sigmoid(lam_ref[...])
    a = jnp.exp(log_a)
    b = jnp.sqrt(1.0 - a * a) * (ig * conv)

    a = a.reshape(groups, SUBLANES, LRU_WIDTH)
    b = b.reshape(groups, SUBLANES, LRU_WIDTH)
    d = 1
    while d < SUBLANES:
        keep = sub >= d
        a_sh = jnp.where(keep, pltpu.roll(a, d, 1), 1.0)
        b_sh = jnp.where(keep, pltpu.roll(b, d, 1), 0.0)
        b = a * b_sh + b
        a = a * a_sh
        d *= 2
    last = hprev[0:1, :]
    hs = []
    for t in range(groups):
        ht = b[t] + a[t] * last
        hs.append(ht)
        last = ht[SUBLANES - 1:SUBLANES, :]
    hprev[0:1, :] = last
    h = jnp.concatenate(hs, axis=0)
    o_ref[...] = (h * jax.nn.gelu(gl_ref[...])).astype(BF16)


def _lru(xg, cw, cb, wa, ba, wx, bx, lam, batch, seq):
    T = xg.shape[0]
    tl = TL_LRU
    n_seq = seq // tl
    full = lambda b, i: (0, 0)
    return pl.pallas_call(
        _lru_kernel,
        grid=(batch, n_seq),
        in_specs=[pl.BlockSpec((tl, LRU_WIDTH), lambda b, i: (b * n_seq + i, 0)),
                  pl.BlockSpec((tl, LRU_WIDTH), lambda b, i: (b * n_seq + i, 1)),
                  pl.BlockSpec((LRU_CONV, LRU_WIDTH), full),
                  pl.BlockSpec((1, LRU_WIDTH), full),
                  pl.BlockSpec((LRU_WIDTH, LRU_WIDTH), full),
                  pl.BlockSpec((1, LRU_WIDTH), full),
                  pl.BlockSpec((LRU_WIDTH, LRU_WIDTH), full),
                  pl.BlockSpec((1, LRU_WIDTH), full),
                  pl.BlockSpec((1, LRU_WIDTH), full)],
        out_specs=pl.BlockSpec((tl, LRU_WIDTH), lambda b, i: (b * n_seq + i, 0)),
        out_shape=jax.ShapeDtypeStruct((T, LRU_WIDTH), BF16),
        scratch_shapes=[pltpu.VMEM((SUBLANES, LRU_WIDTH), F32),
                        pltpu.VMEM((SUBLANES, LRU_WIDTH), F32)],
        compiler_params=_cparams("arbitrary", "arbitrary"),
        name="rg_lru",
    )(xg, xg, cw, cb, wa, ba, wx, bx, lam)


N_BACK = DILATED_PATTERNS[0][0] // DILATED_PATTERNS[0][1]
Q_BLOCKS = ATT_SPAN // N_BACK


def _attention_bias():
    qi = np.arange(N_BACK)[:, None]
    ki = np.arange(2 * N_BACK)[None, :]
    dist = N_BACK + qi - ki
    band = (dist >= 0) & (dist <= N_BACK)
    first = band & (ki >= N_BACK)
    return np.where(np.stack([band, first]), 0.0, NEG_INF).astype(np.float32)


def _attn_kernel(q_ref, k_ref, v_ref, bias_ref, o_ref, *scratch):
    kv_s = scratch[:6]
    o_s, m_s, l_s = scratch[6:]
    sb = pl.program_id(2)
    lane = lax.broadcasted_iota(jnp.int32, (1, LANES), 1)
    head0 = lane < ATT_HEAD_DIM

    for p, (window, dil) in enumerate(DILATED_PATTERNS):
        per_res = Q_BLOCKS // dil
        span = N_BACK * per_res
        for src, dst in ((k_ref, kv_s[2 * p]), (v_ref, kv_s[2 * p + 1])):
            @pl.when(sb == 0)
            def _():
                dst[:, 0:N_BACK, :] = jnp.zeros((dil, N_BACK, LANES), BF16)

            @pl.when(sb > 0)
            def _():
                dst[:, 0:N_BACK, :] = dst[:, span:span + N_BACK, :]

            for r in range(dil):
                rows = pl.ds(r, span, stride=dil) if dil > 1 else pl.ds(0, span)
                dst[r, N_BACK:N_BACK + span, :] = src[0, rows, :].astype(BF16)

    for p, (window, dil) in enumerate(DILATED_PATTERNS):
        per_res = Q_BLOCKS // dil
        k_s, v_s = kv_s[2 * p], kv_s[2 * p + 1]

        for n in range(Q_BLOCKS):
            r, m = n % dil, n // dil
            if dil > 1:
                rows = pl.ds(m * (N_BACK * dil) + r, N_BACK, stride=dil)
            else:
                rows = pl.ds(m * N_BACK, N_BACK)
            q = q_ref[0, rows, :].astype(BF16)
            kc = k_s[r, m * N_BACK:(m + 2) * N_BACK, :]
            vc = v_s[r, m * N_BACK:(m + 2) * N_BACK, :]
            bias = bias_ref[jnp.where(sb == 0, 1, 0)] if m == 0 else bias_ref[0]
            res = []
            for h in range(LANES // ATT_HEAD_DIM):
                qm = jnp.where(head0 if h == 0 else ~head0, q, jnp.zeros_like(q))
                s = _dot_nt(qm, kc) + bias
                mx = jnp.max(s, axis=-1, keepdims=True)
                e = jnp.exp(s - mx)
                res.append((_dot(e.astype(BF16), vc), mx, jnp.sum(e, axis=-1, keepdims=True)))
            for dst, idx in ((o_s, 0), (m_s, 1), (l_s, 2)):
                dst[p, rows, :] = jnp.where(head0, res[0][idx], res[1][idx])

    mx = jnp.maximum(jnp.maximum(m_s[0], m_s[1]), m_s[2])
    num = jnp.zeros_like(mx)
    den = jnp.zeros_like(mx)
    for p in range(len(DILATED_PATTERNS)):
        w = jnp.exp(m_s[p] - mx)
        num = num + w * o_s[p]
        den = den + w * l_s[p]
    o_ref[0] = (num / den).astype(BF16)


def _attention(q, k, v, bias):
    B, S, W = q.shape
    blk = pl.BlockSpec((1, ATT_SPAN, LANES), lambda b, p, i: (b, i, p))
    kv_scratch = []
    for window, dil in DILATED_PATTERNS:
        shape = (dil, N_BACK * (1 + Q_BLOCKS // dil), LANES)
        kv_scratch += [pltpu.VMEM(shape, BF16), pltpu.VMEM(shape, BF16)]
    acc = pltpu.VMEM((len(DILATED_PATTERNS), ATT_SPAN, LANES), F32)
    return pl.pallas_call(
        _attn_kernel,
        grid=(B, W // LANES, S // ATT_SPAN),
        in_specs=[blk, blk, blk, pl.BlockSpec(bias.shape, lambda b, p, i: (0, 0, 0))],
        out_specs=blk,
        out_shape=jax.ShapeDtypeStruct((B, S, W), BF16),
        scratch_shapes=kv_scratch + [acc, acc, acc],
        compiler_params=_cparams("arbitrary", "arbitrary", "arbitrary"),
        name="dilated_attention",
    )(q, k, v, bias)


def _ffn_kernel(x_ref, ya_ref, yb_ref, wo_ref, g_ref, w1_ref, cw_ref, cb_ref, w2_ref,
                o_ref, act_s, carry_s, *, tiles_per_seq):
    i = pl.program_id(0)
    tm = x_ref.shape[0]
    half = ya_ref.shape[1]
    tf = TF_FFN

    @pl.when(i % tiles_per_seq == 0)
    def _():
        carry_s[...] = jnp.zeros_like(carry_s)

    x1 = (x_ref[...] + _dot(ya_ref[...], wo_ref[0:half, :])
          + _dot(yb_ref[...], wo_ref[half:2 * half, :]))
    h = _rms(x1, g_ref[...]).astype(BF16)
    groups = tm // SUBLANES
    sub = lax.broadcasted_iota(jnp.int32, (1, SUBLANES, 1), 1)
    for c in range(D_FF // tf):
        cols = slice(c * tf, (c + 1) * tf)
        a = _dot(h, w1_ref[:, cols])
        lin = _dot(h, w1_ref[:, D_FF + c * tf:D_FF + (c + 1) * tf])
        a3 = a.reshape(groups, SUBLANES, tf)
        prev = carry_s[:, cols].reshape(1, SUBLANES, tf)
        carry_s[:, cols] = a[tm - SUBLANES:tm, :]

        def delayed(k):
            r = pltpu.roll(a3, k, 1)
            rp = jnp.concatenate([pltpu.roll(prev, k, 1), r[:groups - 1]], axis=0)
            return jnp.where(sub >= k, r, rp)

        conv = (cb_ref[:, cols] + cw_ref[2:3, cols] * a3
                + cw_ref[1:2, cols] * delayed(1) + cw_ref[0:1, cols] * delayed(2))
        act = jax.nn.gelu(conv).reshape(tm, tf) * lin
        act_s[:, cols] = act.astype(BF16)
    o_ref[...] = x1 + _dot(act_s[...], w2_ref[...])


def _ffn(x, ya, yb, wo, g, w_in, cw, cb, w2, seq):
    T = x.shape[0]
    tm = TM_FFN
    half = ya.shape[1]
    row = lambda i: (i, 0)
    resident = lambda shape: pl.BlockSpec(shape, lambda i: (0, 0), pipeline_mode=pl.Buffered(1))
    return pl.pallas_call(
        functools.partial(_ffn_kernel, tiles_per_seq=seq // tm),
        grid=(T // tm,),
        in_specs=[pl.BlockSpec((tm, D_MODEL), row),
                  pl.BlockSpec((tm, half), row),
                  pl.BlockSpec((tm, half), row),
                  resident((2 * half, D_MODEL)),
                  resident((1, D_MODEL)),
                  resident((D_MODEL, 2 * D_FF)),
                  resident((3, D_FF)),
                  resident((1, D_FF)),
                  resident((D_FF, D_MODEL))],
        out_specs=pl.BlockSpec((tm, D_MODEL), row),
        out_shape=jax.ShapeDtypeStruct((T, D_MODEL), F32),
        scratch_shapes=[pltpu.VMEM((tm, D_FF), BF16),
                        pltpu.VMEM((SUBLANES, D_FF), F32)],
        compiler_params=_cparams("arbitrary"),
        name="outproj_conv_mlp",
    )(x, ya, yb, wo, g, w_in, cw, cb, w2)


def _odd_in_kernel(x_ref, g_ref, w_ref, o_ref):
    h = _rms(x_ref[...], g_ref[...])
    o_ref[...] = _dot(h.astype(BF16), w_ref[...])


def _odd_in(x, g, w):
    T = x.shape[0]
    tm = TM_PROJ
    ncol = w.shape[1]
    return pl.pallas_call(
        _odd_in_kernel,
        grid=(T // tm,),
        in_specs=[pl.BlockSpec((tm, D_MODEL), lambda i: (i, 0)),
                  pl.BlockSpec((1, D_MODEL), lambda i: (0, 0)),
                  pl.BlockSpec((D_MODEL, ncol), lambda i: (0, 0))],
        out_specs=pl.BlockSpec((tm, ncol), lambda i: (i, 0)),
        out_shape=jax.ShapeDtypeStruct((T, ncol), F32),
        compiler_params=_cparams("arbitrary"),
        name="odd_in_proj",
    )(x, g, w)


def _cswap(v):
    return jnp.concatenate([v[:, LANES:], v[:, :LANES]], axis=1)


def _s5_kernel(u_ref, bm_ref, p1_ref, p2_ref, q1_ref, q2_ref, tri_ref, cm_ref,
               d_ref, gw_ref, gb_ref, o_ref, carry, x_s, ys_s):
    tl = u_ref.shape[0]
    L = L_S5
    blk = 2 * LANES
    slab = 2 * S5_NSTATE // (S5_WIDTH // LANES)

    @pl.when(pl.program_id(1) == 0)
    def _():
        carry[...] = jnp.zeros_like(carry)

    tri = tri_ref[...]
    for s in range(S5_WIDTH // LANES):
        u = u_ref[:, s * LANES:(s + 1) * LANES]
        u16 = u.astype(BF16)
        for jb in range(slab // blk):
            cols = slice(s * slab + jb * blk, s * slab + (jb + 1) * blk)
            bu = _dot(u16, bm_ref[s, :, jb * blk:(jb + 1) * blk])
            for c in range(tl // L):
                rows = slice(c * L, (c + 1) * L)
                v = bu[rows, :]
                z = v * q1_ref[:, cols] + _cswap(v) * q2_ref[:, cols]
                w = _dot(tri, z.astype(BF16)) + carry[0:1, cols]
                x = w * p1_ref[:, cols] + _cswap(w) * p2_ref[:, cols]
                xl = x[L - 1:L, :]
                carry[0:1, cols] = xl * p1_ref[1:2, cols] + _cswap(xl) * p2_ref[1:2, cols]
                x_s[rows, jb * blk:(jb + 1) * blk] = x.astype(BF16)
        y = _dot(x_s[...], cm_ref[s]) + d_ref[:, s * LANES:(s + 1) * LANES] * u
        ys_s[:, s * LANES:(s + 1) * LANES] = jax.nn.gelu(y)
    ys = ys_s[...]
    o_ref[...] = (ys * jax.nn.sigmoid(_dot(ys.astype(BF16), gw_ref[...]) + gb_ref[...])).astype(BF16)


def _s5(proj, bm, pr, pi, qr, qi, tri, cm, d, gw, gb, batch, seq):
    T = proj.shape[0]
    tl = TL_S5
    n_seq = seq // tl
    c2 = lambda b, i: (0, 0)
    c3 = lambda b, i: (0, 0, 0)
    return pl.pallas_call(
        _s5_kernel,
        grid=(batch, n_seq),
        in_specs=[pl.BlockSpec((tl, S5_WIDTH), lambda b, i: (b * n_seq + i, 0)),
                  pl.BlockSpec(bm.shape, c3),
                  pl.BlockSpec(pr.shape, c2), pl.BlockSpec(pi.shape, c2),
                  pl.BlockSpec(qr.shape, c2), pl.BlockSpec(qi.shape, c2),
                  pl.BlockSpec(tri.shape, c2),
                  pl.BlockSpec(cm.shape, c3),
                  pl.BlockSpec((1, S5_WIDTH), c2),
                  pl.BlockSpec((S5_WIDTH, S5_WIDTH), c2),
                  pl.BlockSpec((1, S5_WIDTH), c2)],
        out_specs=pl.BlockSpec((tl, S5_WIDTH), lambda b, i: (b * n_seq + i, 0)),
        out_shape=jax.ShapeDtypeStruct((T, S5_WIDTH), BF16),
        scratch_shapes=[pltpu.VMEM((SUBLANES, 2 * S5_NSTATE), F32),
                        pltpu.VMEM((tl, 2 * S5_NSTATE // (S5_WIDTH // LANES)), BF16),
                        pltpu.VMEM((tl, S5_WIDTH), F32)],
        compiler_params=_cparams("arbitrary", "arbitrary"),
        name="s5_glu",
    )(proj, bm, pr, pi, qr, qi, tri, cm, d, gw, gb)


def _s5_params(lam_re, lam_im, b_re, b_im, c_re, c_im, log_step):
    G, N, P = S5_GROUPS, S5_STATE, S5_GROUP
    gs = LANES // P
    ns = S5_WIDTH // LANES
    step = jnp.exp(log_step.astype(F32))[:, None]
    lr, li = lam_re.astype(F32), lam_im.astype(F32)
    mag = jnp.exp(lr * step)
    ar, ai = mag * jnp.cos(li * step), mag * jnp.sin(li * step)
    den = lr * lr + li * li
    cr = ((ar - 1.0) * lr + ai * li) / den
    ci = (ai * lr - (ar - 1.0) * li) / den
    bbr = cr[..., None] * b_re - ci[..., None] * b_im
    bbi = cr[..., None] * b_im + ci[..., None] * b_re
    eye = jnp.eye(gs, dtype=F32)

    def in_blockdiag(t):
        t = t.reshape(ns, gs, N, P).transpose(0, 1, 3, 2)
        return jnp.einsum('ab,sapn->sapbn', eye, t).reshape(ns, gs * P, gs * N)

    def out_blockdiag(t):
        t = t.reshape(ns, gs, P, N).transpose(0, 1, 3, 2)
        return jnp.einsum('ab,sanp->sanbp', eye, t).reshape(ns, gs * N, gs * P)

    def interleave(re, im, axis):
        shp = list(re.shape)
        blocked = shp[:axis] + [shp[axis] // LANES, LANES] + shp[axis + 1:]
        both = jnp.stack([re.reshape(blocked), im.reshape(blocked)], axis=axis + 1)
        return both.reshape(shp[:axis] + [2 * shp[axis]] + shp[axis + 1:])

    bm = interleave(in_blockdiag(bbr), in_blockdiag(bbi), 2).astype(BF16)
    cm = interleave(out_blockdiag(c_re), out_blockdiag(-c_im), 1).astype(BF16)
    j = jnp.arange(L_S5, dtype=F32)[:, None]
    la = (lr * step).reshape(1, G * N)
    th = (li * step).reshape(1, G * N)
    pmag, qmag = jnp.exp(j * la), jnp.exp(-(j * la))
    cs, sn = jnp.cos(j * th), jnp.sin(j * th)
    pr, pi, qr, qi = pmag * cs, pmag * sn, qmag * cs, -(qmag * sn)
    return (bm, interleave(pr, pr, 1), interleave(-pi, pi, 1),
            interleave(qr, qr, 1), interleave(-qi, qi, 1), cm)


def _gla_cumsum_matrix(tl):
    r = np.arange(tl)[:, None]
    c = np.arange(tl)[None, :]
    same = (r // C_GLA) == (c // C_GLA)
    return np.concatenate([same & (c <= r), same & (c > r)], axis=0).astype(np.float32)


def _gla_kernel(q_ref, k_ref, v_ref, g_ref, gk_ref, gw_ref, gb_ref, nrm_ref, cum_ref, o_ref, st):
    nbatch, tl = q_ref.shape[0], q_ref.shape[1]
    C = C_GLA
    blk = 2 * C
    pair = LANES // GLA_DK

    @pl.when(pl.program_id(0) == 0)
    def _():
        st[...] = jnp.zeros_like(st)

    lane = lax.broadcasted_iota(jnp.int32, (1, LANES), 1)
    rb = lax.broadcasted_iota(jnp.int32, (blk, blk), 0)
    cb = lax.broadcasted_iota(jnp.int32, (blk, blk), 1)
    causal = (rb >= cb) & ((rb < C) | (cb >= C))
    streams = []
    for n in range(nbatch):
        z = _dot(gk_ref[n].astype(BF16), gw_ref[...]) + gb_ref[...]
        log_a = _log_sigmoid(z) * (1.0 / GLA_TAU)
        hi, lo = _split_bf16(log_a)
        sums = _dot(cum_ref[...], hi) + _dot(cum_ref[...], lo)
        bc, suffix = sums[:tl], sums[tl:]
        eb = jnp.exp(bc)
        q_dec = q_ref[n] * (GLA_DK ** -0.5) * eb
        k = k_ref[n]
        k_inv = (k * jnp.exp(-bc)).astype(BF16)
        k_dec = k * jnp.exp(suffix)
        for h in range(GLA_HEADS):
            hp, hh = divmod(h, pair)
            cols = slice(hp * LANES, (hp + 1) * LANES)
            in_head = (lane >= hh * GLA_DK) & (lane < (hh + 1) * GLA_DK)
            streams.append(dict(
                n=n, h=h, eb=eb[:, cols],
                qd=jnp.where(in_head, q_dec[:, cols], 0.0).astype(BF16),
                kd=jnp.where(in_head, k_dec[:, cols], 0.0).astype(BF16),
                ki=k_inv[:, cols],
                vh=v_ref[n, :, h * GLA_DV:(h + 1) * GLA_DV].astype(BF16),
                state=st[n * GLA_HEADS + h]))
    for b in range(tl // blk):
        rows = slice(b * blk, (b + 1) * blk)
        for sd in streams:
            n, h = sd["n"], sd["h"]
            att = jnp.where(causal, _dot_nt(sd["qd"][rows], sd["ki"][rows]), 0.0)
            o = _dot(att.astype(BF16), sd["vh"][rows])
            inter = []
            for c in range(b * blk // C, (b + 1) * blk // C):
                crow = slice(c * C, (c + 1) * C)
                inter.append(_dot_nt(sd["qd"][crow], sd["state"].astype(BF16)))
                decay = sd["eb"][(c + 1) * C - 1:(c + 1) * C, :]
                sd["state"] = sd["state"] * decay + _dot_tn(sd["vh"][crow], sd["kd"][crow])
            o = _rms(o + jnp.concatenate(inter, axis=0), nrm_ref[...])
            gh = g_ref[n, rows, h * GLA_DV:(h + 1) * GLA_DV]
            o_ref[n, rows, h * GLA_DV:(h + 1) * GLA_DV] = (o * jax.nn.silu(gh)).astype(BF16)
    for sd in streams:
        st[sd["n"] * GLA_HEADS + sd["h"]] = sd["state"]


def _gla(proj, gw, gb, nrm, tri, batch, seq):
    tl = TL_GLA
    hk = GLA_HEADS * GLA_DK
    hv = GLA_HEADS * GLA_DV
    c2 = lambda i: (0, 0)
    q0 = S5_WIDTH // hk
    v0 = (S5_WIDTH + 2 * hk) // hv
    gk0 = (S5_WIDTH + 2 * hk + 2 * hv) // GK_PAD
    proj = proj.reshape(batch, seq, proj.shape[-1])
    out = pl.pallas_call(
        _gla_kernel,
        grid=(seq // tl,),
        in_specs=[pl.BlockSpec((batch, tl, hk), lambda i: (0, i, q0)),
                  pl.BlockSpec((batch, tl, hk), lambda i: (0, i, q0 + 1)),
                  pl.BlockSpec((batch, tl, hv), lambda i: (0, i, v0)),
                  pl.BlockSpec((batch, tl, hv), lambda i: (0, i, v0 + 1)),
                  pl.BlockSpec((batch, tl, GK_PAD), lambda i: (0, i, gk0)),
                  pl.BlockSpec((GK_PAD, hk), c2),
                  pl.BlockSpec((1, hk), c2),
                  pl.BlockSpec((1, GLA_DV), c2),
                  pl.BlockSpec((2 * tl, tl), c2)],
        out_specs=pl.BlockSpec((batch, tl, hv), lambda i: (0, i, 0)),
        out_shape=jax.ShapeDtypeStruct((batch, seq, hv), BF16),
        scratch_shapes=[pltpu.VMEM((batch * GLA_HEADS, GLA_DV, LANES), F32)],
        compiler_params=_cparams("arbitrary"),
        name="gla",
    )(proj, proj, proj, proj, proj, gw, gb, nrm, tri)
    return out.reshape(batch * seq, hv)


def _block_diag(w):
    nb, a, b = w.shape
    return jnp.einsum('hk,hij->hikj', jnp.eye(nb, dtype=w.dtype), w).reshape(nb * a, nb * b)


def _rope_tables(seq):
    half = ROPE_DIM // 2
    pos = jnp.arange(seq, dtype=F32)
    inv = ROPE_THETA ** (-jnp.arange(0, ROPE_DIM, 2, dtype=F32) / ROPE_DIM)
    ang = pos[:, None] * inv[None, :]
    cos, sin = jnp.cos(ang), jnp.sin(ang)
    rest = ATT_HEAD_DIM - ROPE_DIM
    ones = jnp.ones((seq, rest), F32)
    zeros = jnp.zeros((seq, rest), F32)
    zh = jnp.zeros((seq, half), F32)
    per_head = lambda parts: jnp.tile(jnp.concatenate(parts, axis=1), (1, LANES // ATT_HEAD_DIM))
    return (per_head([cos, cos, ones]), per_head([-sin, zh, zeros]), per_head([zh, sin, zeros]))


def kernel(x, e_norm, e_w_in, e_conv_w, e_conv_b, e_gate_a_w, e_gate_a_b, e_gate_x_w, e_gate_x_b, e_lambda, e_q_norm, e_k_norm, e_w_out, o_norm, o_w_in, o_lambda_re, o_lambda_im, o_b_re, o_b_im, o_c_re, o_c_im, o_d, o_log_step, o_glu_w, o_glu_b, o_gk_w, o_gk_b, o_gla_norm, o_w_out, f_norm, f_w_in, f_conv_w, f_conv_b, f_w_out):
    B, S, D = x.shape
    T = B * S
    depth = f_norm.shape[0]
    row = lambda t: t.reshape(1, -1).astype(F32)
    xt = x.reshape(T, D)

    cos_t, s1_t, s2_t = _rope_tables(S)
    head_seg = jnp.asarray(np.kron(np.eye(LANES // ATT_HEAD_DIM), np.ones((ATT_HEAD_DIM, ATT_HEAD_DIM))), BF16)
    att_bias = jnp.asarray(_attention_bias())
    tri_s5 = jnp.asarray(np.tril(np.ones((L_S5, L_S5))), BF16)
    tri_gla = jnp.asarray(_gla_cumsum_matrix(TL_GLA), BF16)
    two_heads = lambda t: jnp.tile(row(t), (1, LANES // ATT_HEAD_DIM))

    for layer in range(depth):
        i = layer // 2
        if layer % 2 == 0:
            xg, q, k, v = _even_in(xt, row(e_norm[i]), e_w_in[i].astype(BF16),
                                   two_heads(e_q_norm[i]), two_heads(e_k_norm[i]),
                                   head_seg, cos_t, s1_t, s2_t, S)
            ya = _lru(xg, e_conv_w[i], row(e_conv_b[i]),
                      _block_diag(e_gate_a_w[i]).astype(BF16), row(e_gate_a_b[i]),
                      _block_diag(e_gate_x_w[i]).astype(BF16), row(e_gate_x_b[i]),
                      row(e_lambda[i]), B, S)
            yb = _attention(q.reshape(B, S, ATT_WIDTH), k.reshape(B, S, ATT_WIDTH),
                            v.reshape(B, S, ATT_WIDTH), att_bias).reshape(T, ATT_WIDTH)
            w_out = e_w_out[i]
        else:
            w_in = jnp.pad(o_w_in[i], ((0, 0), (0, GK_PAD - GLA_LOWRANK))).astype(BF16)
            proj = _odd_in(xt, row(o_norm[i]), w_in)
            bm, pr, pi, qr, qi, cm = _s5_params(o_lambda_re[i], o_lambda_im[i], o_b_re[i], o_b_im[i],
                                                     o_c_re[i], o_c_im[i], o_log_step[i])
            ya = _s5(proj, bm, pr, pi, qr, qi, tri_s5, cm, row(o_d[i]),
                     o_glu_w[i].astype(BF16), row(o_glu_b[i]), B, S)
            gk_w = jnp.pad(o_gk_w[i], ((0, GK_PAD - GLA_LOWRANK), (0, 0))).astype(BF16)
            yb = _gla(proj, gk_w, row(o_gk_b[i]), row(o_gla_norm[i]), tri_gla, B, S)
            w_out = o_w_out[i]
        xt = _ffn(xt, ya, yb, w_out.astype(BF16), row(f_norm[layer]), f_w_in[layer].astype(BF16),
                  f_conv_w[layer], row(f_conv_b[layer]), f_w_out[layer].astype(BF16), S)
    return xt.reshape(B, S, D)
```

```python
import functools
import math

import numpy as np
import jax
import jax.numpy as jnp
from jax import lax
from jax.experimental import pallas as pl
from jax.experimental.pallas import tpu as pltpu

F32 = jnp.float32
BF16 = jnp.bfloat16

D_MODEL = 1024
LRU_WIDTH = 512
LRU_BLOCKS = 8
LRU_CONV = 4
LRU_C = 8.0
ATT_HEADS = 8
ATT_HEAD_DIM = 64
ATT_WIDTH = 512
DILATED_PATTERNS = ((128, 1), (512, 4), (2048, 16))
ATT_SPAN = 2048
ROPE_THETA = 500000.0
ROPE_DIM = 16
S5_WIDTH = 512
S5_GROUP = 16
S5_GROUPS = 32
S5_STATE = 64
S5_NSTATE = S5_GROUPS * S5_STATE
GLA_HEADS = 4
GLA_DK = 64
GLA_DV = 128
GLA_LOWRANK = 16
GLA_TAU = 16.0
D_FF = 3 * D_MODEL
EPS = 1e-6
NEG_INF = -1e30

LANES = 128
SUBLANES = 8
VMEM_LIMIT = 56 * 1024 * 1024

TM_PROJ = 512
TM_FFN = 512
TF_FFN = 512
TL_LRU = 256
TL_S5 = 512
L_S5 = 128
TL_GLA = 512
C_GLA = 64
GK_PAD = 128


def _cparams(*sem):
    return pltpu.CompilerParams(dimension_semantics=sem, vmem_limit_bytes=VMEM_LIMIT)


def _rms(x, g):
    return x * lax.rsqrt(jnp.mean(x * x, axis=-1, keepdims=True) + EPS) * g


def _log_sigmoid(x):
    return -(jnp.maximum(-x, 0.0) + jnp.log1p(jnp.exp(-jnp.abs(x))))


def _split_bf16(x):
    hi = x.astype(BF16)
    lo = (x - hi.astype(F32)).astype(BF16)
    return hi, lo


def _dot(a, b):
    return jnp.dot(a, b, preferred_element_type=F32)


def _dot_nt(a, b):
    return lax.dot_general(a, b, (((1,), (1,)), ((), ())), preferred_element_type=F32)


def _dot_tn(a, b):
    return lax.dot_general(a, b, (((0,), (0,)), ((), ())), preferred_element_type=F32)


def _even_in_kernel(x_ref, g_ref, w_ref, qn_ref, kn_ref, seg_ref, cos_ref, s1_ref, s2_ref,
                    xg_ref, q_ref, k_ref, v_ref):
    h = _rms(x_ref[...], g_ref[...]).astype(BF16)
    seg = seg_ref[...]
    cos, s1, s2 = cos_ref[...], s1_ref[...], s2_ref[...]
    half = ROPE_DIM // 2
    for off, n_ref, dst, scale in ((2 * LRU_WIDTH, qn_ref, q_ref, ATT_HEAD_DIM ** -0.5),
                                   (2 * LRU_WIDTH + ATT_WIDTH, kn_ref, k_ref, 1.0)):
        y = _dot(h, w_ref[:, off:off + ATT_WIDTH])
        for c in range(ATT_WIDTH // LANES):
            t = y[:, c * LANES:(c + 1) * LANES]
            ms = _dot((t * t).astype(BF16), seg) * (1.0 / ATT_HEAD_DIM)
            tn = t * lax.rsqrt(ms + EPS) * n_ref[...]
            r = (tn * cos + pltpu.roll(tn, LANES - half, 1) * s1 + pltpu.roll(tn, half, 1) * s2)
            dst[:, c * LANES:(c + 1) * LANES] = r * scale
    xg_ref[...] = _dot(h, w_ref[:, :2 * LRU_WIDTH])
    v_ref[...] = _dot(h, w_ref[:, 2 * LRU_WIDTH + 2 * ATT_WIDTH:])


def _even_in(x, g, w, qn, kn, seg, cos_t, s1_t, s2_t, seq):
    T = x.shape[0]
    tm = TM_PROJ
    n_seq = seq // tm
    ncol = w.shape[1]
    full = lambda i: (0, 0)
    tab = lambda i: (i % n_seq, 0)
    row = lambda i: (i, 0)
    return pl.pallas_call(
        _even_in_kernel,
        grid=(T // tm,),
        in_specs=[pl.BlockSpec((tm, D_MODEL), row),
                  pl.BlockSpec((1, D_MODEL), full),
                  pl.BlockSpec((D_MODEL, ncol), full),
                  pl.BlockSpec((1, LANES), full),
                  pl.BlockSpec((1, LANES), full),
                  pl.BlockSpec((LANES, LANES), full),
                  pl.BlockSpec((tm, LANES), tab),
                  pl.BlockSpec((tm, LANES), tab),
                  pl.BlockSpec((tm, LANES), tab)],
        out_specs=[pl.BlockSpec((tm, 2 * LRU_WIDTH), row),
                   pl.BlockSpec((tm, ATT_WIDTH), row),
                   pl.BlockSpec((tm, ATT_WIDTH), row),
                   pl.BlockSpec((tm, ATT_WIDTH), row)],
        out_shape=[jax.ShapeDtypeStruct((T, 2 * LRU_WIDTH), F32),
                   jax.ShapeDtypeStruct((T, ATT_WIDTH), F32),
                   jax.ShapeDtypeStruct((T, ATT_WIDTH), F32),
                   jax.ShapeDtypeStruct((T, ATT_WIDTH), F32)],
        compiler_params=_cparams("arbitrary"),
        name="even_in_proj",
    )(x, g, w, qn, kn, seg, cos_t, s1_t, s2_t)


def _lru_kernel(xl_ref, gl_ref, cw_ref, cb_ref, wa_ref, ba_ref, wx_ref, bx_ref, lam_ref,
                o_ref, xbuf, hprev):
    tl = xl_ref.shape[0]

    @pl.when(pl.program_id(1) == 0)
    def _():
        xbuf[...] = jnp.zeros_like(xbuf)
        hprev[...] = jnp.zeros_like(hprev)

    groups = tl // SUBLANES
    sub = lax.broadcasted_iota(jnp.int32, (1, SUBLANES, 1), 1)
    x = xl_ref[...]
    x3 = x.reshape(groups, SUBLANES, LRU_WIDTH)
    prev = xbuf[...].reshape(1, SUBLANES, LRU_WIDTH)
    xbuf[...] = x[tl - SUBLANES:tl, :]
    conv = cb_ref[...] + cw_ref[LRU_CONV - 1:LRU_CONV, :] * x3
    for k in range(1, LRU_CONV):
        r = pltpu.roll(x3, k, 1)
        rp = jnp.concatenate([pltpu.roll(prev, k, 1), r[:groups - 1]], axis=0)
        conv = conv + cw_ref[LRU_CONV - 1 - k:LRU_CONV - k, :] * jnp.where(sub >= k, r, rp)
    conv = conv.reshape(tl, LRU_WIDTH)

    c16 = conv.astype(BF16)
    r = jax.nn.sigmoid(_dot(c16, wa_ref[...]) + ba_ref[...])
    ig = jax.nn.sigmoid(_dot(c16, wx_ref[...]) + bx_ref[...])
    log_a = (LRU_C * r) * _log_sigmoid(lam_ref[...])
    a = jnp.exp(log_a)
    b = jnp.sqrt(1.0 - a * a) * (ig * conv)

    a = a.reshape(groups, SUBLANES, LRU_WIDTH)
    b = b.reshape(groups, SUBLANES, LRU_WIDTH)
    d = 1
    while d < SUBLANES:
        keep = sub >= d
        a_sh = jnp.where(keep, pltpu.roll(a, d, 1), 1.0)
        b_sh = jnp.where(keep, pltpu.roll(b, d, 1), 0.0)
        b = a * b_sh + b
        a = a * a_sh
        d *= 2
    last = hprev[0:1, :]
    hs = []
    for t in range(groups):
        ht = b[t] + a[t] * last
        hs.append(ht)
        last = ht[SUBLANES - 1:SUBLANES, :]
    hprev[0:1, :] = last
    h = jnp.concatenate(hs, axis=0)
    o_ref[...] = (h * jax.nn.gelu(gl_ref[...])).astype(BF16)


def _lru(xg, cw, cb, wa, ba, wx, bx, lam, batch, seq):
    T = xg.shape[0]
    tl = TL_LRU
    n_seq = seq // tl
    full = lambda b, i: (0, 0)
    return pl.pallas_call(
        _lru_kernel,
        grid=(batch, n_seq),
        in_specs=[pl.BlockSpec((tl, LRU_WIDTH), lambda b, i: (b * n_seq + i, 0)),
                  pl.BlockSpec((tl, LRU_WIDTH), lambda b, i: (b * n_seq + i, 1)),
                  pl.BlockSpec((LRU_CONV, LRU_WIDTH), full),
                  pl.BlockSpec((1, LRU_WIDTH), full),
                  pl.BlockSpec((LRU_WIDTH, LRU_WIDTH), full),
                  pl.BlockSpec((1, LRU_WIDTH), full),
                  pl.BlockSpec((LRU_WIDTH, LRU_WIDTH), full),
                  pl.BlockSpec((1, LRU_WIDTH), full),
                  pl.BlockSpec((1, LRU_WIDTH), full)],
        out_specs=pl.BlockSpec((tl, LRU_WIDTH), lambda b, i: (b * n_seq + i, 0)),
        out_shape=jax.ShapeDtypeStruct((T, LRU_WIDTH), BF16),
        scratch_shapes=[pltpu.VMEM((SUBLANES, LRU_WIDTH), F32),
                        pltpu.VMEM((SUBLANES, LRU_WIDTH), F32)],
        compiler_params=_cparams("arbitrary", "arbitrary"),
        name="rg_lru",
    )(xg, xg, cw, cb, wa, ba, wx, bx, lam)


N_BACK = DILATED_PATTERNS[0][0] // DILATED_PATTERNS[0][1]
Q_BLOCKS = ATT_SPAN // N_BACK


def _attention_bias():
    qi = np.arange(N_BACK)[:, None]
    ki = np.arange(2 * N_BACK)[None, :]
    dist = N_BACK + qi - ki
    band = (dist >= 0) & (dist <= N_BACK)
    first = band & (ki >= N_BACK)
    return np.where(np.stack([band, first]), 0.0, NEG_INF).astype(np.float32)


def _attn_kernel(q_ref, k_ref, v_ref, bias_ref, o_ref, *scratch):
    kv_s = scratch[:6]
    o_s, m_s, l_s = scratch[6:]
    sb = pl.program_id(2)
    lane = lax.broadcasted_iota(jnp.int32, (1, LANES), 1)
    head0 = lane < ATT_HEAD_DIM

    for p, (window, dil) in enumerate(DILATED_PATTERNS):
        per_res = Q_BLOCKS // dil
        span = N_BACK * per_res
        for src, dst in ((k_ref, kv_s[2 * p]), (v_ref, kv_s[2 * p + 1])):
            @pl.when(sb == 0)
            def _():
                dst[:, 0:N_BACK, :] = jnp.zeros((dil, N_BACK, LANES), BF16)

            @pl.when(sb > 0)
            def _():
                dst[:, 0:N_BACK, :] = dst[:, span:span + N_BACK, :]

            for r in range(dil):
                rows = pl.ds(r, span, stride=dil) if dil > 1 else pl.ds(0, span)
                dst[r, N_BACK:N_BACK + span, :] = src[0, rows, :].astype(BF16)

    for p, (window, dil) in enumerate(DILATED_PATTERNS):
        per_res = Q_BLOCKS // dil
        k_s, v_s = kv_s[2 * p], kv_s[2 * p + 1]

        for n in range(Q_BLOCKS):
            r, m = n % dil, n // dil
            if dil > 1:
                rows = pl.ds(m * (N_BACK * dil) + r, N_BACK, stride=dil)
            else:
                rows = pl.ds(m * N_BACK, N_BACK)
            q = q_ref[0, rows, :].astype(BF16)
            kc = k_s[r, m * N_BACK:(m + 2) * N_BACK, :]
            vc = v_s[r, m * N_BACK:(m + 2) * N_BACK, :]
            bias = bias_ref[jnp.where(sb == 0, 1, 0)] if m == 0 else bias_ref[0]
            res = []
            for h in range(LANES // ATT_HEAD_DIM):
                qm = jnp.where(head0 if h == 0 else ~head0, q, jnp.zeros_like(q))
                s = _dot_nt(qm, kc) + bias
                mx = jnp.max(s, axis=-1, keepdims=True)
                e = jnp.exp(s - mx)
                res.append((_dot(e.astype(BF16), vc), mx, jnp.sum(e, axis=-1, keepdims=True)))
            for dst, idx in ((o_s, 0), (m_s, 1), (l_s, 2)):
                dst[p, rows, :] = jnp.where(head0, res[0][idx], res[1][idx])

    mx = jnp.maximum(jnp.maximum(m_s[0], m_s[1]), m_s[2])
    num = jnp.zeros_like(mx)
    den = jnp.zeros_like(mx)
    for p in range(len(DILATED_PATTERNS)):
        w = jnp.exp(m_s[p] - mx)
        num = num + w * o_s[p]
        den = den + w * l_s[p]
    o_ref[0] = (num / den).astype(BF16)


def _attention(q, k, v, bias):
    B, S, W = q.shape
    blk = pl.BlockSpec((1, ATT_SPAN, LANES), lambda b, p, i: (b, i, p))
    kv_scratch = []
    for window, dil in DILATED_PATTERNS:
        shape = (dil, N_BACK * (1 + Q_BLOCKS // dil), LANES)
        kv_scratch += [pltpu.VMEM(shape, BF16), pltpu.VMEM(shape, BF16)]
    acc = pltpu.VMEM((len(DILATED_PATTERNS), ATT_SPAN, LANES), F32)
    return pl.pallas_call(
        _attn_kernel,
        grid=(B, W // LANES, S // ATT_SPAN),
        in_specs=[blk, blk, blk, pl.BlockSpec(bias.shape, lambda b, p, i: (0, 0, 0))],
        out_specs=blk,
        out_shape=jax.ShapeDtypeStruct((B, S, W), BF16),
        scratch_shapes=kv_scratch + [acc, acc, acc],
        compiler_params=_cparams("arbitrary", "arbitrary", "arbitrary"),
        name="dilated_attention",
    )(q, k, v, bias)


def _ffn_kernel(x_ref, ya_ref, yb_ref, wo_ref, g_ref, w1_ref, cw_ref, cb_ref, w2_ref,
                o_ref, act_s, carry_s, *, tiles_per_seq):
    i = pl.program_id(0)
    tm = x_ref.shape[0]
    half = ya_ref.shape[1]
    tf = TF_FFN

    @pl.when(i % tiles_per_seq == 0)
    def _():
        carry_s[...] = jnp.zeros_like(carry_s)

    x1 = (x_ref[...] + _dot(ya_ref[...], wo_ref[0:half, :])
          + _dot(yb_ref[...], wo_ref[half:2 * half, :]))
    h = _rms(x1, g_ref[...]).astype(BF16)
    groups = tm // SUBLANES
    sub = lax.broadcasted_iota(jnp.int32, (1, SUBLANES, 1), 1)
    for c in range(D_FF // tf):
        cols = slice(c * tf, (c + 1) * tf)
        a = _dot(h, w1_ref[:, cols])
        lin = _dot(h, w1_ref[:, D_FF + c * tf:D_FF + (c + 1) * tf])
        a3 = a.reshape(groups, SUBLANES, tf)
        prev = carry_s[:, cols].reshape(1, SUBLANES, tf)
        carry_s[:, cols] = a[tm - SUBLANES:tm, :]

        def delayed(k):
            r = pltpu.roll(a3, k, 1)
            rp = jnp.concatenate([pltpu.roll(prev, k, 1), r[:groups - 1]], axis=0)
            return jnp.where(sub >= k, r, rp)

        conv = (cb_ref[:, cols] + cw_ref[2:3, cols] * a3
                + cw_ref[1:2, cols] * delayed(1) + cw_ref[0:1, cols] * delayed(2))
        act = jax.nn.gelu(conv).reshape(tm, tf) * lin
        act_s[:, cols] = act.astype(BF16)
    o_ref[...] = x1 + _dot(act_s[...], w2_ref[...])


def _ffn(x, ya, yb, wo, g, w_in, cw, cb, w2, seq):
    T = x.shape[0]
    tm = TM_FFN
    half = ya.shape[1]
    row = lambda i: (i, 0)
    resident = lambda shape: pl.BlockSpec(shape, lambda i: (0, 0), pipeline_mode=pl.Buffered(1))
    return pl.pallas_call(
        functools.partial(_ffn_kernel, tiles_per_seq=seq // tm),
        grid=(T // tm,),
        in_specs=[pl.BlockSpec((tm, D_MODEL), row),
                  pl.BlockSpec((tm, half), row),
                  pl.BlockSpec((tm, half), row),
                  resident((2 * half, D_MODEL)),
                  resident((1, D_MODEL)),
                  resident((D_MODEL, 2 * D_FF)),
                  resident((3, D_FF)),
                  resident((1, D_FF)),
                  resident((D_FF, D_MODEL))],
        out_specs=pl.BlockSpec((tm, D_MODEL), row),
        out_shape=jax.ShapeDtypeStruct((T, D_MODEL), F32),
        scratch_shapes=[pltpu.VMEM((tm, D_FF), BF16),
                        pltpu.VMEM((SUBLANES, D_FF), F32)],
        compiler_params=_cparams("arbitrary"),
        name="outproj_conv_mlp",
    )(x, ya, yb, wo, g, w_in, cw, cb, w2)


def _odd_in_kernel(x_ref, g_ref, w_ref, o_ref):
    h = _rms(x_ref[...], g_ref[...])
    o_ref[...] = _dot(h.astype(BF16), w_ref[...])


def _odd_in(x, g, w):
    T = x.shape[0]
    tm = TM_PROJ
    ncol = w.shape[1]
    return pl.pallas_call(
        _odd_in_kernel,
        grid=(T // tm,),
        in_specs=[pl.BlockSpec((tm, D_MODEL), lambda i: (i, 0)),
                  pl.BlockSpec((1, D_MODEL), lambda i: (0, 0)),
                  pl.BlockSpec((D_MODEL, ncol), lambda i: (0, 0))],
        out_specs=pl.BlockSpec((tm, ncol), lambda i: (i, 0)),
        out_shape=jax.ShapeDtypeStruct((T, ncol), F32),
        compiler_params=_cparams("arbitrary"),
        name="odd_in_proj",
    )(x, g, w)


def _cswap(v):
    return jnp.concatenate([v[:, LANES:], v[:, :LANES]], axis=1)


def _s5_kernel(u_ref, bm_ref, p1_ref, p2_ref, q1_ref, q2_ref, tri_ref, cm_ref,
               d_ref, gw_ref, gb_ref, o_ref, carry, x_s, ys_s):
    tl = u_ref.shape[0]
    L = L_S5
    blk = 2 * LANES
    slab = 2 * S5_NSTATE // (S5_WIDTH // LANES)

    @pl.when(pl.program_id(1) == 0)
    def _():
        carry[...] = jnp.zeros_like(carry)

    tri = tri_ref[...]
    for s in range(S5_WIDTH // LANES):
        u = u_ref[:, s * LANES:(s + 1) * LANES]
        u16 = u.astype(BF16)
        for jb in range(slab // blk):
            cols = slice(s * slab + jb * blk, s * slab + (jb + 1) * blk)
            bu = _dot(u16, bm_ref[s, :, jb * blk:(jb + 1) * blk])
            for c in range(tl // L):
                rows = slice(c * L, (c + 1) * L)
                v = bu[rows, :]
                z = v * q1_ref[:, cols] + _cswap(v) * q2_ref[:, cols]
                w = _dot(tri, z.astype(BF16)) + carry[0:1, cols]
                x = w * p1_ref[:, cols] + _cswap(w) * p2_ref[:, cols]
                xl = x[L - 1:L, :]
                carry[0:1, cols] = xl * p1_ref[1:2, cols] + _cswap(xl) * p2_ref[1:2, cols]
                x_s[rows, jb * blk:(jb + 1) * blk] = x.astype(BF16)
        y = _dot(x_s[...], cm_ref[s]) + d_ref[:, s * LANES:(s + 1) * LANES] * u
        ys_s[:, s * LANES:(s + 1) * LANES] = jax.nn.gelu(y)
    ys = ys_s[...]
    o_ref[...] = (ys * jax.nn.sigmoid(_dot(ys.astype(BF16), gw_ref[...]) + gb_ref[...])).astype(BF16)


def _s5(proj, bm, pr, pi, qr, qi, tri, cm, d, gw, gb, batch, seq):
    T = proj.shape[0]
    tl = TL_S5
    n_seq = seq // tl
    c2 = lambda b, i: (0, 0)
    c3 = lambda b, i: (0, 0, 0)
    return pl.pallas_call(
        _s5_kernel,
        grid=(batch, n_seq),
        in_specs=[pl.BlockSpec((tl, S5_WIDTH), lambda b, i: (b * n_seq + i, 0)),
                  pl.BlockSpec(bm.shape, c3),
                  pl.BlockSpec(pr.shape, c2), pl.BlockSpec(pi.shape, c2),
                  pl.BlockSpec(qr.shape, c2), pl.BlockSpec(qi.shape, c2),
                  pl.BlockSpec(tri.shape, c2),
                  pl.BlockSpec(cm.shape, c3),
                  pl.BlockSpec((1, S5_WIDTH), c2),
                  pl.BlockSpec((S5_WIDTH, S5_WIDTH), c2),
                  pl.BlockSpec((1, S5_WIDTH), c2)],
        out_specs=pl.BlockSpec((tl, S5_WIDTH), lambda b, i: (b * n_seq + i, 0)),
        out_shape=jax.ShapeDtypeStruct((T, S5_WIDTH), BF16),
        scratch_shapes=[pltpu.VMEM((SUBLANES, 2 * S5_NSTATE), F32),
                        pltpu.VMEM((tl, 2 * S5_NSTATE // (S5_WIDTH // LANES)), BF16),
                        pltpu.VMEM((tl, S5_WIDTH), F32)],
        compiler_params=_cparams("arbitrary", "arbitrary"),
        name="s5_glu",
    )(proj, bm, pr, pi, qr, qi, tri, cm, d, gw, gb)


def _s5_params(lam_re, lam_im, b_re, b_im, c_re, c_im, log_step):
    G, N, P = S5_GROUPS, S5_STATE, S5_GROUP
    gs = LANES // P
    ns = S5_WIDTH // LANES
    step = jnp.exp(log_step.astype(F32))[:, None]
    lr, li = lam_re.astype(F32), lam_im.astype(F32)
    mag = jnp.exp(lr * step)
    ar, ai = mag * jnp.cos(li * step), mag * jnp.sin(li * step)
    den = lr * lr + li * li
    cr = ((ar - 1.0) * lr + ai * li) / den
    ci = (ai * lr - (ar - 1.0) * li) / den
    bbr = cr[..., None] * b_re - ci[..., None] * b_im
    bbi = cr[..., None] * b_im + ci[..., None] * b_re
    eye = jnp.eye(gs, dtype=F32)

    def in_blockdiag(t):
        t = t.reshape(ns, gs, N, P).transpose(0, 1, 3, 2)
        return jnp.einsum('ab,sapn->sapbn', eye, t).reshape(ns, gs * P, gs * N)

    def out_blockdiag(t):
        t = t.reshape(ns, gs, P, N).transpose(0, 1, 3, 2)
        return jnp.einsum('ab,sanp->sanbp', eye, t).reshape(ns, gs * N, gs * P)

    def interleave(re, im, axis):
        shp = list(re.shape)
        blocked = shp[:axis] + [shp[axis] // LANES, LANES] + shp[axis + 1:]
        both = jnp.stack([re.reshape(blocked), im.reshape(blocked)], axis=axis + 1)
        return both.reshape(shp[:axis] + [2 * shp[axis]] + shp[axis + 1:])

    bm = interleave(in_blockdiag(bbr), in_blockdiag(bbi), 2).astype(BF16)
    cm = interleave(out_blockdiag(c_re), out_blockdiag(-c_im), 1).astype(BF16)
    def twice(v):
        v = v.reshape(G * N // LANES, 1, LANES)
        return jnp.broadcast_to(v, (G * N // LANES, 2, LANES)).reshape(1, 2 * G * N)

    sign = jnp.asarray(np.tile(np.repeat([-1.0, 1.0], LANES), G * N // LANES)[None, :], F32)
    j = jnp.arange(L_S5, dtype=F32)[:, None]
    la, th = twice(lr * step), twice(li * step)
    pmag, qmag = jnp.exp(j * la), jnp.exp(-(j * la))
    cs, sn = jnp.cos(j * th), jnp.sin(j * th)
    return bm, pmag * cs, sign * (pmag * sn), qmag * cs, -(sign * (qmag * sn)), cm


def _gla_cumsum_matrix(tl):
    r = np.arange(tl)[:, None]
    c = np.arange(tl)[None, :]
    same = (r // C_GLA) == (c // C_GLA)
    return np.concatenate([same & (c <= r), same & (c > r)], axis=0).astype(np.float32)


def _gla_kernel(q_ref, k_ref, v_ref, g_ref, gk_ref, gw_ref, gb_ref, nrm_ref, cum_ref, o_ref, st):
    nbatch, tl = q_ref.shape[0], q_ref.shape[1]
    C = C_GLA
    blk = 2 * C
    pair = LANES // GLA_DK

    @pl.when(pl.program_id(0) == 0)
    def _():
        st[...] = jnp.zeros_like(st)

    lane = lax.broadcasted_iota(jnp.int32, (1, LANES), 1)
    rb = lax.broadcasted_iota(jnp.int32, (blk, blk), 0)
    cb = lax.broadcasted_iota(jnp.int32, (blk, blk), 1)
    causal = (rb >= cb) & ((rb < C) | (cb >= C))
    streams = []
    for n in range(nbatch):
        z = _dot(gk_ref[n].astype(BF16), gw_ref[...]) + gb_ref[...]
        log_a = _log_sigmoid(z) * (1.0 / GLA_TAU)
        hi, lo = _split_bf16(log_a)
        sums = _dot(cum_ref[...], hi) + _dot(cum_ref[...], lo)
        bc, suffix = sums[:tl], sums[tl:]
        eb = jnp.exp(bc)
        q_dec = q_ref[n] * (GLA_DK ** -0.5) * eb
        k = k_ref[n]
        k_inv = (k * jnp.exp(-bc)).astype(BF16)
        k_dec = k * jnp.exp(suffix)
        for h in range(GLA_HEADS):
            hp, hh = divmod(h, pair)
            cols = slice(hp * LANES, (hp + 1) * LANES)
            in_head = (lane >= hh * GLA_DK) & (lane < (hh + 1) * GLA_DK)
            streams.append(dict(
                n=n, h=h, eb=eb[:, cols],
                qd=jnp.where(in_head, q_dec[:, cols], 0.0).astype(BF16),
                kd=jnp.where(in_head, k_dec[:, cols], 0.0).astype(BF16),
                ki=k_inv[:, cols],
                vh=v_ref[n, :, h * GLA_DV:(h + 1) * GLA_DV].astype(BF16),
                state=st[n * GLA_HEADS + h]))
    for b in range(tl // blk):
        rows = slice(b * blk, (b + 1) * blk)
        for sd in streams:
            n, h = sd["n"], sd["h"]
            att = jnp.where(causal, _dot_nt(sd["qd"][rows], sd["ki"][rows]), 0.0)
            o = _dot(att.astype(BF16), sd["vh"][rows])
            inter = []
            for c in range(b * blk // C, (b + 1) * blk // C):
                crow = slice(c * C, (c + 1) * C)
                inter.append(_dot_nt(sd["qd"][crow], sd["state"].astype(BF16)))
                decay = sd["eb"][(c + 1) * C - 1:(c + 1) * C, :]
                sd["state"] = sd["state"] * decay + _dot_tn(sd["vh"][crow], sd["kd"][crow])
            o = _rms(o + jnp.concatenate(inter, axis=0), nrm_ref[...])
            gh = g_ref[n, rows, h * GLA_DV:(h + 1) * GLA_DV]
            o_ref[n, rows, h * GLA_DV:(h + 1) * GLA_DV] = (o * jax.nn.silu(gh)).astype(BF16)
    for sd in streams:
        st[sd["n"] * GLA_HEADS + sd["h"]] = sd["state"]


def _gla(proj, gw, gb, nrm, tri, batch, seq):
    tl = TL_GLA
    hk = GLA_HEADS * GLA_DK
    hv = GLA_HEADS * GLA_DV
    c2 = lambda i: (0, 0)
    q0 = S5_WIDTH // hk
    v0 = (S5_WIDTH + 2 * hk) // hv
    gk0 = (S5_WIDTH + 2 * hk + 2 * hv) // GK_PAD
    proj = proj.reshape(batch, seq, proj.shape[-1])
    out = pl.pallas_call(
        _gla_kernel,
        grid=(seq // tl,),
        in_specs=[pl.BlockSpec((batch, tl, hk), lambda i: (0, i, q0)),
                  pl.BlockSpec((batch, tl, hk), lambda i: (0, i, q0 + 1)),
                  pl.BlockSpec((batch, tl, hv), lambda i: (0, i, v0)),
                  pl.BlockSpec((batch, tl, hv), lambda i: (0, i, v0 + 1)),
                  pl.BlockSpec((batch, tl, GK_PAD), lambda i: (0, i, gk0)),
                  pl.BlockSpec((GK_PAD, hk), c2),
                  pl.BlockSpec((1, hk), c2),
                  pl.BlockSpec((1, GLA_DV), c2),
                  pl.BlockSpec((2 * tl, tl), c2)],
        out_specs=pl.BlockSpec((batch, tl, hv), lambda i: (0, i, 0)),
        out_shape=jax.ShapeDtypeStruct((batch, seq, hv), BF16),
        scratch_shapes=[pltpu.VMEM((batch * GLA_HEADS, GLA_DV, LANES), F32)],
        compiler_params=_cparams("arbitrary"),
        name="gla",
    )(proj, proj, proj, proj, proj, gw, gb, nrm, tri)
    return out.reshape(batch * seq, hv)


def _block_diag(w):
    nb, a, b = w.shape
    return jnp.einsum('hk,hij->hikj', jnp.eye(nb, dtype=w.dtype), w).reshape(nb * a, nb * b)


def _rope_tables(seq):
    half = ROPE_DIM // 2
    pos = np.arange(seq, dtype=np.float64)
    inv = ROPE_THETA ** (-np.arange(0, ROPE_DIM, 2, dtype=np.float64) / ROPE_DIM)
    ang = pos[:, None] * inv[None, :]
    cos, sin = np.cos(ang), np.sin(ang)
    rest = ATT_HEAD_DIM - ROPE_DIM
    ones = np.ones((seq, rest))
    zeros = np.zeros((seq, rest))
    zh = np.zeros((seq, half))
    per_head = lambda parts: jnp.asarray(
        np.tile(np.concatenate(parts, axis=1), (1, LANES // ATT_HEAD_DIM)), F32)
    return (per_head([cos, cos, ones]), per_head([-sin, zh, zeros]), per_head([zh, sin, zeros]))


def _cast_kernel(w_ref, o_ref):
    o_ref[...] = w_ref[0].astype(BF16)


def _to_bf16(w, idx):
    _, rows, cols = w.shape
    tr = min(rows, 256)
    return pl.pallas_call(
        _cast_kernel,
        grid=(rows // tr,),
        in_specs=[pl.BlockSpec((1, tr, cols), lambda r: (idx, r, 0))],
        out_specs=pl.BlockSpec((tr, cols), lambda r: (r, 0)),
        out_shape=jax.ShapeDtypeStruct((rows, cols), BF16),
        compiler_params=_cparams("arbitrary"),
        name="weight_to_bf16",
    )(w)


def kernel(x, e_norm, e_w_in, e_conv_w, e_conv_b, e_gate_a_w, e_gate_a_b, e_gate_x_w, e_gate_x_b, e_lambda, e_q_norm, e_k_norm, e_w_out, o_norm, o_w_in, o_lambda_re, o_lambda_im, o_b_re, o_b_im, o_c_re, o_c_im, o_d, o_log_step, o_glu_w, o_glu_b, o_gk_w, o_gk_b, o_gla_norm, o_w_out, f_norm, f_w_in, f_conv_w, f_conv_b, f_w_out):
    B, S, D = x.shape
    T = B * S
    depth = f_norm.shape[0]
    row = lambda t: t.reshape(1, -1).astype(F32)
    xt = x.reshape(T, D)

    cos_t, s1_t, s2_t = _rope_tables(S)
    head_seg = jnp.asarray(np.kron(np.eye(LANES // ATT_HEAD_DIM), np.ones((ATT_HEAD_DIM, ATT_HEAD_DIM))), BF16)
    att_bias = jnp.asarray(_attention_bias())
    tri_s5 = jnp.asarray(np.tril(np.ones((L_S5, L_S5))), BF16)
    tri_gla = jnp.asarray(_gla_cumsum_matrix(TL_GLA), BF16)
    two_heads = lambda t: jnp.tile(row(t), (1, LANES // ATT_HEAD_DIM))

    for layer in range(depth):
        i = layer // 2
        if layer % 2 == 0:
            xg, q, k, v = _even_in(xt, row(e_norm[i]), _to_bf16(e_w_in, i),
                                   two_heads(e_q_norm[i]), two_heads(e_k_norm[i]),
                                   head_seg, cos_t, s1_t, s2_t, S)
            ya = _lru(xg, e_conv_w[i], row(e_conv_b[i]),
                      _block_diag(e_gate_a_w[i]).astype(BF16), row(e_gate_a_b[i]),
                      _block_diag(e_gate_x_w[i]).astype(BF16), row(e_gate_x_b[i]),
                      row(e_lambda[i]), B, S)
            yb = _attention(q.reshape(B, S, ATT_WIDTH), k.reshape(B, S, ATT_WIDTH),
                            v.reshape(B, S, ATT_WIDTH), att_bias).reshape(T, ATT_WIDTH)
            w_out = _to_bf16(e_w_out, i)
        else:
            w_in = jnp.pad(o_w_in[i], ((0, 0), (0, GK_PAD - GLA_LOWRANK))).astype(BF16)
            proj = _odd_in(xt, row(o_norm[i]), w_in)
            bm, pr, pi, qr, qi, cm = _s5_params(o_lambda_re[i], o_lambda_im[i], o_b_re[i], o_b_im[i],
                                                     o_c_re[i], o_c_im[i], o_log_step[i])
            ya = _s5(proj, bm, pr, pi, qr, qi, tri_s5, cm, row(o_d[i]),
                     _to_bf16(o_glu_w, i), row(o_glu_b[i]), B, S)
            gk_w = jnp.pad(o_gk_w[i], ((0, GK_PAD - GLA_LOWRANK), (0, 0))).astype(BF16)
            yb = _gla(proj, gk_w, row(o_gk_b[i]), row(o_gla_norm[i]), tri_gla, B, S)
            w_out = _to_bf16(o_w_out, i)
        xt = _ffn(xt, ya, yb, w_out, row(f_norm[layer]), _to_bf16(f_w_in, layer),
                  f_conv_w[layer], row(f_conv_b[layer]), _to_bf16(f_w_out, layer), S)
    return xt.reshape(B, S, D)
```

```python
import functools
import math

import numpy as np
import jax
import jax.numpy as jnp
from jax import lax
from jax.experimental import pallas as pl
from jax.experimental.pallas import tpu as pltpu

F32 = jnp.float32
BF16 = jnp.bfloat16

D_MODEL = 1024
LRU_WIDTH = 512
LRU_BLOCKS = 8
LRU_CONV = 4
LRU_C = 8.0
ATT_HEADS = 8
ATT_HEAD_DIM = 64
ATT_WIDTH = 512
DILATED_PATTERNS = ((128, 1), (512, 4), (2048, 16))
ATT_SPAN = 2048
ROPE_THETA = 500000.0
ROPE_DIM = 16
S5_WIDTH = 512
S5_GROUP = 16
S5_GROUPS = 32
S5_STATE = 64
S5_NSTATE = S5_GROUPS * S5_STATE
GLA_HEADS = 4
GLA_DK = 64
GLA_DV = 128
GLA_LOWRANK = 16
GLA_TAU = 16.0
D_FF = 3 * D_MODEL
EPS = 1e-6
NEG_INF = -1e30

LANES = 128
SUBLANES = 8
VMEM_LIMIT = 56 * 1024 * 1024

TM_PROJ = 512
TM_FFN = 512
TF_FFN = 512
TL_LRU = 256
TL_S5 = 512
L_S5 = 128
TL_GLA = 512
C_GLA = 64
GK_PAD = 128


def _cparams(*sem):
    return pltpu.CompilerParams(dimension_semantics=sem, vmem_limit_bytes=VMEM_LIMIT)


def _rms(x, g):
    return x * lax.rsqrt(jnp.mean(x * x, axis=-1, keepdims=True) + EPS) * g


def _log_sigmoid(x):
    return -(jnp.maximum(-x, 0.0) + jnp.log1p(jnp.exp(-jnp.abs(x))))


def _split_bf16(x):
    hi = x.astype(BF16)
    lo = (x - hi.astype(F32)).astype(BF16)
    return hi, lo


def _dot(a, b):
    return jnp.dot(a, b, preferred_element_type=F32)


def _dot_nt(a, b):
    return lax.dot_general(a, b, (((1,), (1,)), ((), ())), preferred_element_type=F32)


def _dot_tn(a, b):
    return lax.dot_general(a, b, (((0,), (0,)), ((), ())), preferred_element_type=F32)


def _even_in_kernel(x_ref, g_ref, w_ref, qn_ref, kn_ref, seg_ref, cos_ref, s1_ref, s2_ref,
                    xg_ref, q_ref, k_ref, v_ref):
    h = _rms(x_ref[...], g_ref[...]).astype(BF16)
    seg = seg_ref[...]
    cos, s1, s2 = cos_ref[...], s1_ref[...], s2_ref[...]
    half = ROPE_DIM // 2
    for off, n_ref, dst, scale in ((2 * LRU_WIDTH, qn_ref, q_ref, ATT_HEAD_DIM ** -0.5),
                                   (2 * LRU_WIDTH + ATT_WIDTH, kn_ref, k_ref, 1.0)):
        y = _dot(h, w_ref[:, off:off + ATT_WIDTH])
        for c in range(ATT_WIDTH // LANES):
            t = y[:, c * LANES:(c + 1) * LANES]
            ms = _dot((t * t).astype(BF16), seg) * (1.0 / ATT_HEAD_DIM)
            tn = t * lax.rsqrt(ms + EPS) * n_ref[...]
            r = (tn * cos + pltpu.roll(tn, LANES - half, 1) * s1 + pltpu.roll(tn, half, 1) * s2)
            dst[:, c * LANES:(c + 1) * LANES] = r * scale
    xg_ref[...] = _dot(h, w_ref[:, :2 * LRU_WIDTH])
    v_ref[...] = _dot(h, w_ref[:, 2 * LRU_WIDTH + 2 * ATT_WIDTH:])


def _even_in(x, g, w, qn, kn, seg, cos_t, s1_t, s2_t, seq):
    T = x.shape[0]
    tm = TM_PROJ
    n_seq = seq // tm
    ncol = w.shape[1]
    full = lambda i: (0, 0)
    tab = lambda i: (i % n_seq, 0)
    row = lambda i: (i, 0)
    return pl.pallas_call(
        _even_in_kernel,
        grid=(T // tm,),
        in_specs=[pl.BlockSpec((tm, D_MODEL), row),
                  pl.BlockSpec((1, D_MODEL), full),
                  pl.BlockSpec((D_MODEL, ncol), full),
                  pl.BlockSpec((1, LANES), full),
                  pl.BlockSpec((1, LANES), full),
                  pl.BlockSpec((LANES, LANES), full),
                  pl.BlockSpec((tm, LANES), tab),
                  pl.BlockSpec((tm, LANES), tab),
                  pl.BlockSpec((tm, LANES), tab)],
        out_specs=[pl.BlockSpec((tm, 2 * LRU_WIDTH), row),
                   pl.BlockSpec((tm, ATT_WIDTH), row),
                   pl.BlockSpec((tm, ATT_WIDTH), row),
                   pl.BlockSpec((tm, ATT_WIDTH), row)],
        out_shape=[jax.ShapeDtypeStruct((T, 2 * LRU_WIDTH), F32),
                   jax.ShapeDtypeStruct((T, ATT_WIDTH), F32),
                   jax.ShapeDtypeStruct((T, ATT_WIDTH), F32),
                   jax.ShapeDtypeStruct((T, ATT_WIDTH), F32)],
        compiler_params=_cparams("arbitrary"),
        name="even_in_proj",
    )(x, g, w, qn, kn, seg, cos_t, s1_t, s2_t)


def _lru_kernel(xl_ref, gl_ref, cw_ref, cb_ref, wa_ref, ba_ref, wx_ref, bx_ref, lam_ref,
                o_ref, xbuf, hprev):
    tl = xl_ref.shape[0]

    @pl.when(pl.program_id(1) == 0)
    def _():
        xbuf[...] = jnp.zeros_like(xbuf)
        hprev[...] = jnp.zeros_like(hprev)

    groups = tl // SUBLANES
    sub = lax.broadcasted_iota(jnp.int32, (1, SUBLANES, 1), 1)
    x = xl_ref[...]
    x3 = x.reshape(groups, SUBLANES, LRU_WIDTH)
    prev = xbuf[...].reshape(1, SUBLANES, LRU_WIDTH)
    xbuf[...] = x[tl - SUBLANES:tl, :]
    conv = cb_ref[...] + cw_ref[LRU_CONV - 1:LRU_CONV, :] * x3
    for k in range(1, LRU_CONV):
        r = pltpu.roll(x3, k, 1)
        rp = jnp.concatenate([pltpu.roll(prev, k, 1), r[:groups - 1]], axis=0)
        conv = conv + cw_ref[LRU_CONV - 1 - k:LRU_CONV - k, :] * jnp.where(sub >= k, r, rp)
    conv = conv.reshape(tl, LRU_WIDTH)

    c16 = conv.astype(BF16)
    r = jax.nn.sigmoid(_dot(c16, wa_ref[...]) + ba_ref[...])
    ig = jax.nn.sigmoid(_dot(c16, wx_ref[...]) + bx_ref[...])
    log_a = (LRU_C * r) * _log_sigmoid(lam_ref[...])
    a = jnp.exp(log_a)
    b = jnp.sqrt(1.0 - a * a) * (ig * conv)

    a = a.reshape(groups, SUBLANES, LRU_WIDTH)
    b = b.reshape(groups, SUBLANES, LRU_WIDTH)
    d = 1
    while d < SUBLANES:
        keep = sub >= d
        a_sh = jnp.where(keep, pltpu.roll(a, d, 1), 1.0)
        b_sh = jnp.where(keep, pltpu.roll(b, d, 1), 0.0)
        b = a * b_sh + b
        a = a * a_sh
        d *= 2
    last = hprev[0:1, :]
    hs = []
    for t in range(groups):
        ht = b[t] + a[t] * last
        hs.append(ht)
        last = ht[SUBLANES - 1:SUBLANES, :]
    hprev[0:1, :] = last
    h = jnp.concatenate(hs, axis=0)
    o_ref[...] = (h * jax.nn.gelu(gl_ref[...])).astype(BF16)


def _lru(xg, cw, cb, wa, ba, wx, bx, lam, batch, seq):
    T = xg.shape[0]
    tl = TL_LRU
    n_seq = seq // tl
    full = lambda b, i: (0, 0)
    return pl.pallas_call(
        _lru_kernel,
        grid=(batch, n_seq),
        in_specs=[pl.BlockSpec((tl, LRU_WIDTH), lambda b, i: (b * n_seq + i, 0)),
                  pl.BlockSpec((tl, LRU_WIDTH), lambda b, i: (b * n_seq + i, 1)),
                  pl.BlockSpec((LRU_CONV, LRU_WIDTH), full),
                  pl.BlockSpec((1, LRU_WIDTH), full),
                  pl.BlockSpec((LRU_WIDTH, LRU_WIDTH), full),
                  pl.BlockSpec((1, LRU_WIDTH), full),
                  pl.BlockSpec((LRU_WIDTH, LRU_WIDTH), full),
                  pl.BlockSpec((1, LRU_WIDTH), full),
                  pl.BlockSpec((1, LRU_WIDTH), full)],
        out_specs=pl.BlockSpec((tl, LRU_WIDTH), lambda b, i: (b * n_seq + i, 0)),
        out_shape=jax.ShapeDtypeStruct((T, LRU_WIDTH), BF16),
        scratch_shapes=[pltpu.VMEM((SUBLANES, LRU_WIDTH), F32),
                        pltpu.VMEM((SUBLANES, LRU_WIDTH), F32)],
        compiler_params=_cparams("arbitrary", "arbitrary"),
        name="rg_lru",
    )(xg, xg, cw, cb, wa, ba, wx, bx, lam)


N_BACK = DILATED_PATTERNS[0][0] // DILATED_PATTERNS[0][1]
Q_BLOCKS = ATT_SPAN // N_BACK


def _attention_bias():
    qi = np.arange(N_BACK)[:, None]
    ki = np.arange(2 * N_BACK)[None, :]
    dist = N_BACK + qi - ki
    band = (dist >= 0) & (dist <= N_BACK)
    first = band & (ki >= N_BACK)
    return np.where(np.stack([band, first]), 0.0, NEG_INF).astype(np.float32)


def _attn_kernel(q_ref, k_ref, v_ref, bias_ref, o_ref, *scratch):
    kv_s = scratch[:6]
    o_s, m_s, l_s = scratch[6:]
    sb = pl.program_id(2)
    lane = lax.broadcasted_iota(jnp.int32, (1, LANES), 1)
    head0 = lane < ATT_HEAD_DIM

    for p, (window, dil) in enumerate(DILATED_PATTERNS):
        per_res = Q_BLOCKS // dil
        span = N_BACK * per_res
        for src, dst in ((k_ref, kv_s[2 * p]), (v_ref, kv_s[2 * p + 1])):
            @pl.when(sb == 0)
            def _():
                dst[:, 0:N_BACK, :] = jnp.zeros((dil, N_BACK, LANES), BF16)

            @pl.when(sb > 0)
            def _():
                dst[:, 0:N_BACK, :] = dst[:, span:span + N_BACK, :]

            for r in range(dil):
                rows = pl.ds(r, span, stride=dil) if dil > 1 else pl.ds(0, span)
                dst[r, N_BACK:N_BACK + span, :] = src[0, rows, :].astype(BF16)

    for p, (window, dil) in enumerate(DILATED_PATTERNS):
        per_res = Q_BLOCKS // dil
        k_s, v_s = kv_s[2 * p], kv_s[2 * p + 1]

        for n in range(Q_BLOCKS):
            r, m = n % dil, n // dil
            if dil > 1:
                rows = pl.ds(m * (N_BACK * dil) + r, N_BACK, stride=dil)
            else:
                rows = pl.ds(m * N_BACK, N_BACK)
            q = q_ref[0, rows, :].astype(BF16)
            kc = k_s[r, m * N_BACK:(m + 2) * N_BACK, :]
            vc = v_s[r, m * N_BACK:(m + 2) * N_BACK, :]
            bias = bias_ref[jnp.where(sb == 0, 1, 0)] if m == 0 else bias_ref[0]
            res = []
            for h in range(LANES // ATT_HEAD_DIM):
                qm = jnp.where(head0 if h == 0 else ~head0, q, jnp.zeros_like(q))
                s = _dot_nt(qm, kc) + bias
                mx = jnp.max(s, axis=-1, keepdims=True)
                e = jnp.exp(s - mx)
                res.append((_dot(e.astype(BF16), vc), mx, jnp.sum(e, axis=-1, keepdims=True)))
            for dst, idx in ((o_s, 0), (m_s, 1), (l_s, 2)):
                dst[p, rows, :] = jnp.where(head0, res[0][idx], res[1][idx])

    mx = jnp.maximum(jnp.maximum(m_s[0], m_s[1]), m_s[2])
    num = jnp.zeros_like(mx)
    den = jnp.zeros_like(mx)
    for p in range(len(DILATED_PATTERNS)):
        w = jnp.exp(m_s[p] - mx)
        num = num + w * o_s[p]
        den = den + w * l_s[p]
    o_ref[0] = (num / den).astype(BF16)


def _attention(q, k, v, bias):
    B, S, W = q.shape
    blk = pl.BlockSpec((1, ATT_SPAN, LANES), lambda b, p, i: (b, i, p))
    kv_scratch = []
    for window, dil in DILATED_PATTERNS:
        shape = (dil, N_BACK * (1 + Q_BLOCKS // dil), LANES)
        kv_scratch += [pltpu.VMEM(shape, BF16), pltpu.VMEM(shape, BF16)]
    acc = pltpu.VMEM((len(DILATED_PATTERNS), ATT_SPAN, LANES), F32)
    return pl.pallas_call(
        _attn_kernel,
        grid=(B, W // LANES, S // ATT_SPAN),
        in_specs=[blk, blk, blk, pl.BlockSpec(bias.shape, lambda b, p, i: (0, 0, 0))],
        out_specs=blk,
        out_shape=jax.ShapeDtypeStruct((B, S, W), BF16),
        scratch_shapes=kv_scratch + [acc, acc, acc],
        compiler_params=_cparams("arbitrary", "arbitrary", "arbitrary"),
        name="dilated_attention",
    )(q, k, v, bias)


def _ffn_kernel(x_ref, ya_ref, yb_ref, wo_ref, g_ref, w1_ref, cw_ref, cb_ref, w2_ref,
                o_ref, act_s, carry_s, *, tiles_per_seq):
    i = pl.program_id(0)
    tm = x_ref.shape[0]
    half = ya_ref.shape[1]
    tf = TF_FFN

    @pl.when(i % tiles_per_seq == 0)
    def _():
        carry_s[...] = jnp.zeros_like(carry_s)

    x1 = (x_ref[...] + _dot(ya_ref[...], wo_ref[0:half, :])
          + _dot(yb_ref[...], wo_ref[half:2 * half, :]))
    h = _rms(x1, g_ref[...]).astype(BF16)
    groups = tm // SUBLANES
    sub = lax.broadcasted_iota(jnp.int32, (1, SUBLANES, 1), 1)
    for c in range(D_FF // tf):
        cols = slice(c * tf, (c + 1) * tf)
        a = _dot(h, w1_ref[:, cols])
        lin = _dot(h, w1_ref[:, D_FF + c * tf:D_FF + (c + 1) * tf])
        a3 = a.reshape(groups, SUBLANES, tf)
        prev = carry_s[:, cols].reshape(1, SUBLANES, tf)
        carry_s[:, cols] = a[tm - SUBLANES:tm, :]

        def delayed(k):
            r = pltpu.roll(a3, k, 1)
            rp = jnp.concatenate([pltpu.roll(prev, k, 1), r[:groups - 1]], axis=0)
            return jnp.where(sub >= k, r, rp)

        conv = (cb_ref[:, cols] + cw_ref[2:3, cols] * a3
                + cw_ref[1:2, cols] * delayed(1) + cw_ref[0:1, cols] * delayed(2))
        act = jax.nn.gelu(conv).reshape(tm, tf) * lin
        act_s[:, cols] = act.astype(BF16)
    o_ref[...] = x1 + _dot(act_s[...], w2_ref[...])


def _ffn(x, ya, yb, wo, g, w_in, cw, cb, w2, seq):
    T = x.shape[0]
    tm = TM_FFN
    half = ya.shape[1]
    row = lambda i: (i, 0)
    resident = lambda shape: pl.BlockSpec(shape, lambda i: (0, 0), pipeline_mode=pl.Buffered(1))
    return pl.pallas_call(
        functools.partial(_ffn_kernel, tiles_per_seq=seq // tm),
        grid=(T // tm,),
        in_specs=[pl.BlockSpec((tm, D_MODEL), row),
                  pl.BlockSpec((tm, half), row),
                  pl.BlockSpec((tm, half), row),
                  resident((2 * half, D_MODEL)),
                  resident((1, D_MODEL)),
                  resident((D_MODEL, 2 * D_FF)),
                  resident((3, D_FF)),
                  resident((1, D_FF)),
                  resident((D_FF, D_MODEL))],
        out_specs=pl.BlockSpec((tm, D_MODEL), row),
        out_shape=jax.ShapeDtypeStruct((T, D_MODEL), F32),
        scratch_shapes=[pltpu.VMEM((tm, D_FF), BF16),
                        pltpu.VMEM((SUBLANES, D_FF), F32)],
        compiler_params=_cparams("arbitrary"),
        name="outproj_conv_mlp",
    )(x, ya, yb, wo, g, w_in, cw, cb, w2)


def _odd_in_kernel(x_ref, g_ref, w_ref, o_ref):
    h = _rms(x_ref[...], g_ref[...])
    o_ref[...] = _dot(h.astype(BF16), w_ref[...])


def _odd_in(x, g, w):
    T = x.shape[0]
    tm = TM_PROJ
    ncol = w.shape[1]
    return pl.pallas_call(
        _odd_in_kernel,
        grid=(T // tm,),
        in_specs=[pl.BlockSpec((tm, D_MODEL), lambda i: (i, 0)),
                  pl.BlockSpec((1, D_MODEL), lambda i: (0, 0)),
                  pl.BlockSpec((D_MODEL, ncol), lambda i: (0, 0))],
        out_specs=pl.BlockSpec((tm, ncol), lambda i: (i, 0)),
        out_shape=jax.ShapeDtypeStruct((T, ncol), F32),
        compiler_params=_cparams("arbitrary"),
        name="odd_in_proj",
    )(x, g, w)


def _cswap(v):
    return jnp.concatenate([v[:, LANES:], v[:, :LANES]], axis=1)


def _s5_kernel(u_ref, bm_ref, p1_ref, p2_ref, q1_ref, q2_ref, pc_ref, tri_ref, cm_ref,
               d_ref, gw_ref, gb_ref, o_ref, carry, x_s, ys_s):
    tl = u_ref.shape[0]
    L = L_S5
    blk = 2 * LANES
    slab = 2 * S5_NSTATE // (S5_WIDTH // LANES)

    @pl.when(pl.program_id(1) == 0)
    def _():
        carry[...] = jnp.zeros_like(carry)

    tri = tri_ref[...]
    for s in range(S5_WIDTH // LANES):
        u = u_ref[:, s * LANES:(s + 1) * LANES]
        u16 = u.astype(BF16)
        for jb in range(slab // blk):
            cols = slice(s * slab + jb * blk, s * slab + (jb + 1) * blk)
            bu = _dot(u16, bm_ref[s, :, jb * blk:(jb + 1) * blk])
            for c in range(tl // L):
                rows = slice(c * L, (c + 1) * L)
                v = bu[rows, :].astype(BF16)
                z = v * q1_ref[:, cols] + _cswap(v) * q2_ref[:, cols]
                w = _dot(tri, z) + carry[0:1, cols]
                w16 = w.astype(BF16)
                x_s[rows, jb * blk:(jb + 1) * blk] = (
                    w16 * p1_ref[:, cols] + _cswap(w16) * p2_ref[:, cols])
                wl = w[L - 1:L, :]
                xl = wl * pc_ref[2:3, cols] + _cswap(wl) * pc_ref[3:4, cols]
                carry[0:1, cols] = xl * pc_ref[0:1, cols] + _cswap(xl) * pc_ref[1:2, cols]
        y = _dot(x_s[...], cm_ref[s]) + d_ref[:, s * LANES:(s + 1) * LANES] * u
        ys_s[:, s * LANES:(s + 1) * LANES] = jax.nn.gelu(y)
    ys = ys_s[...]
    o_ref[...] = (ys * jax.nn.sigmoid(_dot(ys.astype(BF16), gw_ref[...]) + gb_ref[...])).astype(BF16)


def _s5(proj, bm, p1, p2, q1, q2, pc, tri, cm, d, gw, gb, batch, seq):
    T = proj.shape[0]
    tl = TL_S5
    n_seq = seq // tl
    c2 = lambda b, i: (0, 0)
    c3 = lambda b, i: (0, 0, 0)
    return pl.pallas_call(
        _s5_kernel,
        grid=(batch, n_seq),
        in_specs=[pl.BlockSpec((tl, S5_WIDTH), lambda b, i: (b * n_seq + i, 0)),
                  pl.BlockSpec(bm.shape, c3),
                  pl.BlockSpec(p1.shape, c2), pl.BlockSpec(p2.shape, c2),
                  pl.BlockSpec(q1.shape, c2), pl.BlockSpec(q2.shape, c2),
                  pl.BlockSpec(pc.shape, c2),
                  pl.BlockSpec(tri.shape, c2),
                  pl.BlockSpec(cm.shape, c3),
                  pl.BlockSpec((1, S5_WIDTH), c2),
                  pl.BlockSpec((S5_WIDTH, S5_WIDTH), c2),
                  pl.BlockSpec((1, S5_WIDTH), c2)],
        out_specs=pl.BlockSpec((tl, S5_WIDTH), lambda b, i: (b * n_seq + i, 0)),
        out_shape=jax.ShapeDtypeStruct((T, S5_WIDTH), BF16),
        scratch_shapes=[pltpu.VMEM((SUBLANES, 2 * S5_NSTATE), F32),
                        pltpu.VMEM((tl, 2 * S5_NSTATE // (S5_WIDTH // LANES)), BF16),
                        pltpu.VMEM((tl, S5_WIDTH), F32)],
        compiler_params=_cparams("arbitrary", "arbitrary"),
        name="s5_glu",
    )(proj, bm, p1, p2, q1, q2, pc, tri, cm, d, gw, gb)


def _s5_params(lam_re, lam_im, b_re, b_im, c_re, c_im, log_step):
    G, N, P = S5_GROUPS, S5_STATE, S5_GROUP
    gs = LANES // P
    ns = S5_WIDTH // LANES
    step = jnp.exp(log_step.astype(F32))[:, None]
    lr, li = lam_re.astype(F32), lam_im.astype(F32)
    mag = jnp.exp(lr * step)
    ar, ai = mag * jnp.cos(li * step), mag * jnp.sin(li * step)
    den = lr * lr + li * li
    cr = ((ar - 1.0) * lr + ai * li) / den
    ci = (ai * lr - (ar - 1.0) * li) / den
    bbr = cr[..., None] * b_re - ci[..., None] * b_im
    bbi = cr[..., None] * b_im + ci[..., None] * b_re
    eye = jnp.eye(gs, dtype=F32)

    def in_blockdiag(t):
        t = t.reshape(ns, gs, N, P).transpose(0, 1, 3, 2)
        return jnp.einsum('ab,sapn->sapbn', eye, t).reshape(ns, gs * P, gs * N)

    def out_blockdiag(t):
        t = t.reshape(ns, gs, P, N).transpose(0, 1, 3, 2)
        return jnp.einsum('ab,sanp->sanbp', eye, t).reshape(ns, gs * N, gs * P)

    def interleave(re, im, axis):
        shp = list(re.shape)
        blocked = shp[:axis] + [shp[axis] // LANES, LANES] + shp[axis + 1:]
        both = jnp.stack([re.reshape(blocked), im.reshape(blocked)], axis=axis + 1)
        return both.reshape(shp[:axis] + [2 * shp[axis]] + shp[axis + 1:])

    bm = interleave(in_blockdiag(bbr), in_blockdiag(bbi), 2).astype(BF16)
    cm = interleave(out_blockdiag(c_re), out_blockdiag(-c_im), 1).astype(BF16)
    def twice(v):
        v = v.reshape(G * N // LANES, 1, LANES)
        return jnp.broadcast_to(v, (G * N // LANES, 2, LANES)).reshape(1, 2 * G * N)

    sign = jnp.asarray(np.tile(np.repeat([-1.0, 1.0], LANES), G * N // LANES)[None, :], F32)
    j = jnp.arange(L_S5, dtype=F32)[:, None]
    la, th = twice(lr * step), twice(li * step)
    pmag, qmag = jnp.exp(j * la), jnp.exp(-(j * la))
    cs, sn = jnp.cos(j * th), jnp.sin(j * th)
    p1, p2 = pmag * cs, sign * (pmag * sn)
    q1, q2 = qmag * cs, -(sign * (qmag * sn))
    pc = jnp.concatenate([p1[1:2], p2[1:2], p1[L_S5 - 1:], p2[L_S5 - 1:]], axis=0)
    return bm, p1.astype(BF16), p2.astype(BF16), q1.astype(BF16), q2.astype(BF16), pc, cm


def _gla_cumsum_matrix(tl):
    r = np.arange(tl)[:, None]
    c = np.arange(tl)[None, :]
    same = (r // C_GLA) == (c // C_GLA)
    return np.concatenate([same & (c <= r), same & (c > r)], axis=0).astype(np.float32)


def _gla_kernel(q_ref, k_ref, v_ref, g_ref, gk_ref, gw_ref, gb_ref, nrm_ref, cum_ref, o_ref, st):
    nbatch, tl = q_ref.shape[0], q_ref.shape[1]
    C = C_GLA
    blk = 2 * C
    pair = LANES // GLA_DK

    @pl.when(pl.program_id(0) == 0)
    def _():
        st[...] = jnp.zeros_like(st)

    lane = lax.broadcasted_iota(jnp.int32, (1, LANES), 1)
    rb = lax.broadcasted_iota(jnp.int32, (blk, blk), 0)
    cb = lax.broadcasted_iota(jnp.int32, (blk, blk), 1)
    causal = (rb >= cb) & ((rb < C) | (cb >= C))
    streams = []
    for n in range(nbatch):
        z = _dot(gk_ref[n].astype(BF16), gw_ref[...]) + gb_ref[...]
        log_a = _log_sigmoid(z) * (1.0 / GLA_TAU)
        hi, lo = _split_bf16(log_a)
        sums = _dot(cum_ref[...], hi) + _dot(cum_ref[...], lo)
        bc, suffix = sums[:tl], sums[tl:]
        eb = jnp.exp(bc)
        q_dec = q_ref[n] * (GLA_DK ** -0.5) * eb
        k = k_ref[n]
        k_inv = (k * jnp.exp(-bc)).astype(BF16)
        k_dec = k * jnp.exp(suffix)
        for h in range(GLA_HEADS):
            hp, hh = divmod(h, pair)
            cols = slice(hp * LANES, (hp + 1) * LANES)
            in_head = (lane >= hh * GLA_DK) & (lane < (hh + 1) * GLA_DK)
            streams.append(dict(
                n=n, h=h, eb=eb[:, cols],
                qd=jnp.where(in_head, q_dec[:, cols], 0.0).astype(BF16),
                kd=jnp.where(in_head, k_dec[:, cols], 0.0).astype(BF16),
                ki=k_inv[:, cols],
                vh=v_ref[n, :, h * GLA_DV:(h + 1) * GLA_DV].astype(BF16),
                state=st[n * GLA_HEADS + h]))
    for b in range(tl // blk):
        rows = slice(b * blk, (b + 1) * blk)
        for sd in streams:
            n, h = sd["n"], sd["h"]
            att = jnp.where(causal, _dot_nt(sd["qd"][rows], sd["ki"][rows]), 0.0)
            o = _dot(att.astype(BF16), sd["vh"][rows])
            inter = []
            for c in range(b * blk // C, (b + 1) * blk // C):
                crow = slice(c * C, (c + 1) * C)
                inter.append(_dot_nt(sd["qd"][crow], sd["state"].astype(BF16)))
                decay = sd["eb"][(c + 1) * C - 1:(c + 1) * C, :]
                sd["state"] = sd["state"] * decay + _dot_tn(sd["vh"][crow], sd["kd"][crow])
            o = _rms(o + jnp.concatenate(inter, axis=0), nrm_ref[...])
            gh = g_ref[n, rows, h * GLA_DV:(h + 1) * GLA_DV]
            o_ref[n, rows, h * GLA_DV:(h + 1) * GLA_DV] = (o * jax.nn.silu(gh)).astype(BF16)
    for sd in streams:
        st[sd["n"] * GLA_HEADS + sd["h"]] = sd["state"]


def _gla(proj, gw, gb, nrm, tri, batch, seq):
    tl = TL_GLA
    hk = GLA_HEADS * GLA_DK
    hv = GLA_HEADS * GLA_DV
    c2 = lambda i: (0, 0)
    q0 = S5_WIDTH // hk
    v0 = (S5_WIDTH + 2 * hk) // hv
    gk0 = (S5_WIDTH + 2 * hk + 2 * hv) // GK_PAD
    proj = proj.reshape(batch, seq, proj.shape[-1])
    out = pl.pallas_call(
        _gla_kernel,
        grid=(seq // tl,),
        in_specs=[pl.BlockSpec((batch, tl, hk), lambda i: (0, i, q0)),
                  pl.BlockSpec((batch, tl, hk), lambda i: (0, i, q0 + 1)),
                  pl.BlockSpec((batch, tl, hv), lambda i: (0, i, v0)),
                  pl.BlockSpec((batch, tl, hv), lambda i: (0, i, v0 + 1)),
                  pl.BlockSpec((batch, tl, GK_PAD), lambda i: (0, i, gk0)),
                  pl.BlockSpec((GK_PAD, hk), c2),
                  pl.BlockSpec((1, hk), c2),
                  pl.BlockSpec((1, GLA_DV), c2),
                  pl.BlockSpec((2 * tl, tl), c2)],
        out_specs=pl.BlockSpec((batch, tl, hv), lambda i: (0, i, 0)),
        out_shape=jax.ShapeDtypeStruct((batch, seq, hv), BF16),
        scratch_shapes=[pltpu.VMEM((batch * GLA_HEADS, GLA_DV, LANES), F32)],
        compiler_params=_cparams("arbitrary"),
        name="gla",
    )(proj, proj, proj, proj, proj, gw, gb, nrm, tri)
    return out.reshape(batch * seq, hv)


def _block_diag(w):
    nb, a, b = w.shape
    return jnp.einsum('hk,hij->hikj', jnp.eye(nb, dtype=w.dtype), w).reshape(nb * a, nb * b)


def _rope_tables(seq):
    half = ROPE_DIM // 2
    pos = np.arange(seq, dtype=np.float64)
    inv = ROPE_THETA ** (-np.arange(0, ROPE_DIM, 2, dtype=np.float64) / ROPE_DIM)
    ang = pos[:, None] * inv[None, :]
    cos, sin = np.cos(ang), np.sin(ang)
    rest = ATT_HEAD_DIM - ROPE_DIM
    ones = np.ones((seq, rest))
    zeros = np.zeros((seq, rest))
    zh = np.zeros((seq, half))
    per_head = lambda parts: jnp.asarray(
        np.tile(np.concatenate(parts, axis=1), (1, LANES // ATT_HEAD_DIM)), F32)
    return (per_head([cos, cos, ones]), per_head([-sin, zh, zeros]), per_head([zh, sin, zeros]))


def _cast_kernel(w_ref, o_ref):
    o_ref[...] = w_ref[0].astype(BF16)


def _to_bf16(w, idx):
    _, rows, cols = w.shape
    tr = rows // 4
    return pl.pallas_call(
        _cast_kernel,
        grid=(rows // tr,),
        in_specs=[pl.BlockSpec((1, tr, cols), lambda r: (idx, r, 0))],
        out_specs=pl.BlockSpec((tr, cols), lambda r: (r, 0)),
        out_shape=jax.ShapeDtypeStruct((rows, cols), BF16),
        compiler_params=_cparams("arbitrary"),
        name="weight_to_bf16",
    )(w)


def kernel(x, e_norm, e_w_in, e_conv_w, e_conv_b, e_gate_a_w, e_gate_a_b, e_gate_x_w, e_gate_x_b, e_lambda, e_q_norm, e_k_norm, e_w_out, o_norm, o_w_in, o_lambda_re, o_lambda_im, o_b_re, o_b_im, o_c_re, o_c_im, o_d, o_log_step, o_glu_w, o_glu_b, o_gk_w, o_gk_b, o_gla_norm, o_w_out, f_norm, f_w_in, f_conv_w, f_conv_b, f_w_out):
    B, S, D = x.shape
    T = B * S
    depth = f_norm.shape[0]
    row = lambda t: t.reshape(1, -1).astype(F32)
    xt = x.reshape(T, D)

    cos_t, s1_t, s2_t = _rope_tables(S)
    head_seg = jnp.asarray(np.kron(np.eye(LANES // ATT_HEAD_DIM), np.ones((ATT_HEAD_DIM, ATT_HEAD_DIM))), BF16)
    att_bias = jnp.asarray(_attention_bias())
    tri_s5 = jnp.asarray(np.tril(np.ones((L_S5, L_S5))), BF16)
    tri_gla = jnp.asarray(_gla_cumsum_matrix(TL_GLA), BF16)
    two_heads = lambda t: jnp.tile(row(t), (1, LANES // ATT_HEAD_DIM))

    for layer in range(depth):
        i = layer // 2
        if layer % 2 == 0:
            xg, q, k, v = _even_in(xt, row(e_norm[i]), _to_bf16(e_w_in, i),
                                   two_heads(e_q_norm[i]), two_heads(e_k_norm[i]),
                                   head_seg, cos_t, s1_t, s2_t, S)
            ya = _lru(xg, e_conv_w[i], row(e_conv_b[i]),
                      _block_diag(e_gate_a_w[i]).astype(BF16), row(e_gate_a_b[i]),
                      _block_diag(e_gate_x_w[i]).astype(BF16), row(e_gate_x_b[i]),
                      row(e_lambda[i]), B, S)
            yb = _attention(q.reshape(B, S, ATT_WIDTH), k.reshape(B, S, ATT_WIDTH),
                            v.reshape(B, S, ATT_WIDTH), att_bias).reshape(T, ATT_WIDTH)
            w_out = _to_bf16(e_w_out, i)
        else:
            w_in = jnp.pad(o_w_in[i], ((0, 0), (0, GK_PAD - GLA_LOWRANK))).astype(BF16)
            proj = _odd_in(xt, row(o_norm[i]), w_in)
            bm, p1, p2, q1, q2, pc, cm = _s5_params(o_lambda_re[i], o_lambda_im[i], o_b_re[i], o_b_im[i],
                                                    o_c_re[i], o_c_im[i], o_log_step[i])
            ya = _s5(proj, bm, p1, p2, q1, q2, pc, tri_s5, cm, row(o_d[i]),
                     _to_bf16(o_glu_w, i), row(o_glu_b[i]), B, S)
            gk_w = jnp.pad(o_gk_w[i], ((0, GK_PAD - GLA_LOWRANK), (0, 0))).astype(BF16)
            yb = _gla(proj, gk_w, row(o_gk_b[i]), row(o_gla_norm[i]), tri_gla, B, S)
            w_out = _to_bf16(o_w_out, i)
        xt = _ffn(xt, ya, yb, w_out, row(f_norm[layer]), _to_bf16(f_w_in, layer),
                  f_conv_w[layer], row(f_conv_b[layer]), _to_bf16(f_w_out, layer), S)
    return xt.reshape(B, S, D)
```

```python
import functools
import math

import numpy as np
import jax
import jax.numpy as jnp
from jax import lax
from jax.experimental import pallas as pl
from jax.experimental.pallas import tpu as pltpu

F32 = jnp.float32
BF16 = jnp.bfloat16

D_MODEL = 1024
LRU_WIDTH = 512
LRU_BLOCKS = 8
LRU_CONV = 4
LRU_C = 8.0
ATT_HEADS = 8
ATT_HEAD_DIM = 64
ATT_WIDTH = 512
DILATED_PATTERNS = ((128, 1), (512, 4), (2048, 16))
ATT_SPAN = 2048
ROPE_THETA = 500000.0
ROPE_DIM = 16
S5_WIDTH = 512
S5_GROUP = 16
S5_GROUPS = 32
S5_STATE = 64
S5_NSTATE = S5_GROUPS * S5_STATE
GLA_HEADS = 4
GLA_DK = 64
GLA_DV = 128
GLA_LOWRANK = 16
GLA_TAU = 16.0
D_FF = 3 * D_MODEL
EPS = 1e-6
NEG_INF = -1e30

LANES = 128
SUBLANES = 8
VMEM_LIMIT = 60 * 1024 * 1024

TM_PROJ = 512
TM_FFN = 1024
TF_FFN = 512
TL_LRU = 256
TL_S5 = 512
L_S5 = 128
TL_GLA = 512
C_GLA = 64
GK_PAD = 128


def _cparams(*sem):
    return pltpu.CompilerParams(dimension_semantics=sem, vmem_limit_bytes=VMEM_LIMIT)


def _rms(x, g):
    return x * lax.rsqrt(jnp.mean(x * x, axis=-1, keepdims=True) + EPS) * g


def _log_sigmoid(x):
    return -(jnp.maximum(-x, 0.0) + jnp.log(1.0 + jnp.exp(-jnp.abs(x))))


def _split_bf16(x):
    hi = x.astype(BF16)
    lo = (x - hi.astype(F32)).astype(BF16)
    return hi, lo


def _dot(a, b):
    return jnp.dot(a, b, preferred_element_type=F32)


def _dot_nt(a, b):
    return lax.dot_general(a, b, (((1,), (1,)), ((), ())), preferred_element_type=F32)


def _dot_tn(a, b):
    return lax.dot_general(a, b, (((0,), (0,)), ((), ())), preferred_element_type=F32)


def _even_in_kernel(x_ref, g_ref, w_ref, qn_ref, kn_ref, seg_ref, cos_ref, s1_ref, s2_ref,
                    xg_ref, q_ref, k_ref, v_ref):
    h = _rms(x_ref[...], g_ref[...]).astype(BF16)
    seg = seg_ref[...]
    cos, s1, s2 = cos_ref[...], s1_ref[...], s2_ref[...]
    half = ROPE_DIM // 2
    for off, n_ref, dst, scale in ((2 * LRU_WIDTH, qn_ref, q_ref, ATT_HEAD_DIM ** -0.5),
                                   (2 * LRU_WIDTH + ATT_WIDTH, kn_ref, k_ref, 1.0)):
        y = _dot(h, w_ref[:, off:off + ATT_WIDTH])
        for c in range(ATT_WIDTH // LANES):
            t = y[:, c * LANES:(c + 1) * LANES]
            ms = _dot((t * t).astype(BF16), seg) * (1.0 / ATT_HEAD_DIM)
            tn = t * lax.rsqrt(ms + EPS) * n_ref[...]
            r = (tn * cos + pltpu.roll(tn, LANES - half, 1) * s1 + pltpu.roll(tn, half, 1) * s2)
            dst[:, c * LANES:(c + 1) * LANES] = r * scale
    xg_ref[...] = _dot(h, w_ref[:, :2 * LRU_WIDTH])
    v_ref[...] = _dot(h, w_ref[:, 2 * LRU_WIDTH + 2 * ATT_WIDTH:])


def _even_in(x, g, w, qn, kn, seg, cos_t, s1_t, s2_t, seq):
    T = x.shape[0]
    tm = TM_PROJ
    n_seq = seq // tm
    ncol = w.shape[1]
    full = lambda i: (0, 0)
    tab = lambda i: (i % n_seq, 0)
    row = lambda i: (i, 0)
    return pl.pallas_call(
        _even_in_kernel,
        grid=(T // tm,),
        in_specs=[pl.BlockSpec((tm, D_MODEL), row),
                  pl.BlockSpec((1, D_MODEL), full),
                  pl.BlockSpec((D_MODEL, ncol), full),
                  pl.BlockSpec((1, LANES), full),
                  pl.BlockSpec((1, LANES), full),
                  pl.BlockSpec((LANES, LANES), full),
                  pl.BlockSpec((tm, LANES), tab),
                  pl.BlockSpec((tm, LANES), tab),
                  pl.BlockSpec((tm, LANES), tab)],
        out_specs=[pl.BlockSpec((tm, 2 * LRU_WIDTH), row),
                   pl.BlockSpec((tm, ATT_WIDTH), row),
                   pl.BlockSpec((tm, ATT_WIDTH), row),
                   pl.BlockSpec((tm, ATT_WIDTH), row)],
        out_shape=[jax.ShapeDtypeStruct((T, 2 * LRU_WIDTH), F32),
                   jax.ShapeDtypeStruct((T, ATT_WIDTH), F32),
                   jax.ShapeDtypeStruct((T, ATT_WIDTH), F32),
                   jax.ShapeDtypeStruct((T, ATT_WIDTH), F32)],
        compiler_params=_cparams("arbitrary"),
        name="even_in_proj",
    )(x, g, w, qn, kn, seg, cos_t, s1_t, s2_t)


def _lru_kernel(xl_ref, gl_ref, cw_ref, cb_ref, wa_ref, ba_ref, wx_ref, bx_ref, lam_ref,
                o_ref, xbuf, hprev):
    tl = xl_ref.shape[0]

    @pl.when(pl.program_id(1) == 0)
    def _():
        xbuf[...] = jnp.zeros_like(xbuf)
        hprev[...] = jnp.zeros_like(hprev)

    groups = tl // SUBLANES
    sub = lax.broadcasted_iota(jnp.int32, (1, SUBLANES, 1), 1)
    x = xl_ref[...]
    x3 = x.reshape(groups, SUBLANES, LRU_WIDTH)
    prev = xbuf[...].reshape(1, SUBLANES, LRU_WIDTH)
    xbuf[...] = x[tl - SUBLANES:tl, :]
    conv = cb_ref[...] + cw_ref[LRU_CONV - 1:LRU_CONV, :] * x3
    for k in range(1, LRU_CONV):
        r = pltpu.roll(x3, k, 1)
        rp = jnp.concatenate([pltpu.roll(prev, k, 1), r[:groups - 1]], axis=0)
        conv = conv + cw_ref[LRU_CONV - 1 - k:LRU_CONV - k, :] * jnp.where(sub >= k, r, rp)
    conv = conv.reshape(tl, LRU_WIDTH)

    c16 = conv.astype(BF16)
    r = jax.nn.sigmoid(_dot(c16, wa_ref[...]) + ba_ref[...])
    ig = jax.nn.sigmoid(_dot(c16, wx_ref[...]) + bx_ref[...])
    log_a = (LRU_C * r) * _log_sigmoid(lam_ref[...])
    a = jnp.exp(log_a)
    b = jnp.sqrt(1.0 - a * a) * (ig * conv)

    a = a.reshape(groups, SUBLANES, LRU_WIDTH)
    b = b.reshape(groups, SUBLANES, LRU_WIDTH)
    d = 1
    while d < SUBLANES:
        keep = sub >= d
        a_sh = jnp.where(keep, pltpu.roll(a, d, 1), 1.0)
        b_sh = jnp.where(keep, pltpu.roll(b, d, 1), 0.0)
        b = a * b_sh + b
        a = a * a_sh
        d *= 2
    last = hprev[0:1, :]
    hs = []
    for t in range(groups):
        ht = b[t] + a[t] * last
        hs.append(ht)
        last = ht[SUBLANES - 1:SUBLANES, :]
    hprev[0:1, :] = last
    h = jnp.concatenate(hs, axis=0)
    o_ref[...] = (h * jax.nn.gelu(gl_ref[...])).astype(BF16)


def _lru(xg, cw, cb, wa, ba, wx, bx, lam, batch, seq):
    T = xg.shape[0]
    tl = TL_LRU
    n_seq = seq // tl
    full = lambda b, i: (0, 0)
    return pl.pallas_call(
        _lru_kernel,
        grid=(batch, n_seq),
        in_specs=[pl.BlockSpec((tl, LRU_WIDTH), lambda b, i: (b * n_seq + i, 0)),
                  pl.BlockSpec((tl, LRU_WIDTH), lambda b, i: (b * n_seq + i, 1)),
                  pl.BlockSpec((LRU_CONV, LRU_WIDTH), full),
                  pl.BlockSpec((1, LRU_WIDTH), full),
                  pl.BlockSpec((LRU_WIDTH, LRU_WIDTH), full),
                  pl.BlockSpec((1, LRU_WIDTH), full),
                  pl.BlockSpec((LRU_WIDTH, LRU_WIDTH), full),
                  pl.BlockSpec((1, LRU_WIDTH), full),
                  pl.BlockSpec((1, LRU_WIDTH), full)],
        out_specs=pl.BlockSpec((tl, LRU_WIDTH), lambda b, i: (b * n_seq + i, 0)),
        out_shape=jax.ShapeDtypeStruct((T, LRU_WIDTH), BF16),
        scratch_shapes=[pltpu.VMEM((SUBLANES, LRU_WIDTH), F32),
                        pltpu.VMEM((SUBLANES, LRU_WIDTH), F32)],
        compiler_params=_cparams("arbitrary", "arbitrary"),
        name="rg_lru",
    )(xg, xg, cw, cb, wa, ba, wx, bx, lam)


N_BACK = DILATED_PATTERNS[0][0] // DILATED_PATTERNS[0][1]
Q_BLOCKS = ATT_SPAN // N_BACK


def _attention_bias():
    qi = np.arange(N_BACK)[:, None]
    ki = np.arange(2 * N_BACK)[None, :]
    dist = N_BACK + qi - ki
    band = (dist >= 0) & (dist <= N_BACK)
    first = band & (ki >= N_BACK)
    return np.where(np.stack([band, first]), 0.0, NEG_INF).astype(np.float32)


def _attn_kernel(q_ref, k_ref, v_ref, bias_ref, o_ref, *scratch):
    kv_s = scratch[:6]
    o_s, m_s, l_s = scratch[6:]
    sb = pl.program_id(2)
    lane = lax.broadcasted_iota(jnp.int32, (1, LANES), 1)
    head0 = lane < ATT_HEAD_DIM

    for p, (window, dil) in enumerate(DILATED_PATTERNS):
        per_res = Q_BLOCKS // dil
        span = N_BACK * per_res
        for src, dst in ((k_ref, kv_s[2 * p]), (v_ref, kv_s[2 * p + 1])):
            @pl.when(sb == 0)
            def _():
                dst[:, 0:N_BACK, :] = jnp.zeros((dil, N_BACK, LANES), BF16)

            @pl.when(sb > 0)
            def _():
                dst[:, 0:N_BACK, :] = dst[:, span:span + N_BACK, :]

            for r in range(dil):
                rows = pl.ds(r, span, stride=dil) if dil > 1 else pl.ds(0, span)
                dst[r, N_BACK:N_BACK + span, :] = src[0, rows, :].astype(BF16)

    for p, (window, dil) in enumerate(DILATED_PATTERNS):
        per_res = Q_BLOCKS // dil
        k_s, v_s = kv_s[2 * p], kv_s[2 * p + 1]

        for n in range(Q_BLOCKS):
            r, m = n % dil, n // dil
            if dil > 1:
                rows = pl.ds(m * (N_BACK * dil) + r, N_BACK, stride=dil)
            else:
                rows = pl.ds(m * N_BACK, N_BACK)
            q = q_ref[0, rows, :].astype(BF16)
            kc = k_s[r, m * N_BACK:(m + 2) * N_BACK, :]
            vc = v_s[r, m * N_BACK:(m + 2) * N_BACK, :]
            bias = bias_ref[jnp.where(sb == 0, 1, 0)] if m == 0 else bias_ref[0]
            res = []
            for h in range(LANES // ATT_HEAD_DIM):
                qm = jnp.where(head0 if h == 0 else ~head0, q, jnp.zeros_like(q))
                s = _dot_nt(qm, kc) + bias
                mx = jnp.max(s, axis=-1, keepdims=True)
                e = jnp.exp(s - mx)
                res.append((_dot(e.astype(BF16), vc), mx, jnp.sum(e, axis=-1, keepdims=True)))
            for dst, idx in ((o_s, 0), (m_s, 1), (l_s, 2)):
                dst[p, rows, :] = jnp.where(head0, res[0][idx], res[1][idx])

    mx = jnp.maximum(jnp.maximum(m_s[0], m_s[1]), m_s[2])
    num = jnp.zeros_like(mx)
    den = jnp.zeros_like(mx)
    for p in range(len(DILATED_PATTERNS)):
        w = jnp.exp(m_s[p] - mx)
        num = num + w * o_s[p]
        den = den + w * l_s[p]
    o_ref[0] = (num / den).astype(BF16)


def _attention(q, k, v, bias):
    B, S, W = q.shape
    blk = pl.BlockSpec((1, ATT_SPAN, LANES), lambda b, p, i: (b, i, p))
    kv_scratch = []
    for window, dil in DILATED_PATTERNS:
        shape = (dil, N_BACK * (1 + Q_BLOCKS // dil), LANES)
        kv_scratch += [pltpu.VMEM(shape, BF16), pltpu.VMEM(shape, BF16)]
    acc = pltpu.VMEM((len(DILATED_PATTERNS), ATT_SPAN, LANES), F32)
    return pl.pallas_call(
        _attn_kernel,
        grid=(B, W // LANES, S // ATT_SPAN),
        in_specs=[blk, blk, blk, pl.BlockSpec(bias.shape, lambda b, p, i: (0, 0, 0))],
        out_specs=blk,
        out_shape=jax.ShapeDtypeStruct((B, S, W), BF16),
        scratch_shapes=kv_scratch + [acc, acc, acc],
        compiler_params=_cparams("arbitrary", "arbitrary", "arbitrary"),
        name="dilated_attention",
    )(q, k, v, bias)


def _ffn_kernel(x_ref, ya_ref, yb_ref, wo_ref, g_ref, w1_ref, cw_ref, cb_ref, w2_ref,
                o_ref, act_s, carry_s, *, tiles_per_seq):
    i = pl.program_id(0)
    tm = x_ref.shape[0]
    half = ya_ref.shape[1]
    tf = TF_FFN

    @pl.when(i % tiles_per_seq == 0)
    def _():
        carry_s[...] = jnp.zeros_like(carry_s)

    x1 = (x_ref[...] + _dot(ya_ref[...], wo_ref[0:half, :])
          + _dot(yb_ref[...], wo_ref[half:2 * half, :]))
    h = _rms(x1, g_ref[...]).astype(BF16)
    groups = tm // SUBLANES
    sub = lax.broadcasted_iota(jnp.int32, (1, SUBLANES, 1), 1)
    for c in range(D_FF // tf):
        cols = slice(c * tf, (c + 1) * tf)
        a = _dot(h, w1_ref[:, cols])
        lin = _dot(h, w1_ref[:, D_FF + c * tf:D_FF + (c + 1) * tf])
        a3 = a.reshape(groups, SUBLANES, tf)
        prev = carry_s[:, cols].reshape(1, SUBLANES, tf)
        carry_s[:, cols] = a[tm - SUBLANES:tm, :]

        def delayed(k):
            r = pltpu.roll(a3, k, 1)
            rp = jnp.concatenate([pltpu.roll(prev, k, 1), r[:groups - 1]], axis=0)
            return jnp.where(sub >= k, r, rp)

        conv = (cb_ref[:, cols] + cw_ref[2:3, cols] * a3
                + cw_ref[1:2, cols] * delayed(1) + cw_ref[0:1, cols] * delayed(2))
        act = jax.nn.gelu(conv).reshape(tm, tf) * lin
        act_s[:, cols] = act.astype(BF16)
    o_ref[...] = x1 + _dot(act_s[...], w2_ref[...])


def _ffn(x, ya, yb, wo, g, w_in, cw, cb, w2, seq):
    T = x.shape[0]
    tm = TM_FFN
    half = ya.shape[1]
    row = lambda i: (i, 0)
    resident = lambda shape: pl.BlockSpec(shape, lambda i: (0, 0), pipeline_mode=pl.Buffered(1))
    return pl.pallas_call(
        functools.partial(_ffn_kernel, tiles_per_seq=seq // tm),
        grid=(T // tm,),
        in_specs=[pl.BlockSpec((tm, D_MODEL), row),
                  pl.BlockSpec((tm, half), row),
                  pl.BlockSpec((tm, half), row),
                  resident((2 * half, D_MODEL)),
                  resident((1, D_MODEL)),
                  resident((D_MODEL, 2 * D_FF)),
                  resident((3, D_FF)),
                  resident((1, D_FF)),
                  resident((D_FF, D_MODEL))],
        out_specs=pl.BlockSpec((tm, D_MODEL), row),
        out_shape=jax.ShapeDtypeStruct((T, D_MODEL), F32),
        scratch_shapes=[pltpu.VMEM((tm, D_FF), BF16),
                        pltpu.VMEM((SUBLANES, D_FF), F32)],
        compiler_params=_cparams("arbitrary"),
        name="outproj_conv_mlp",
    )(x, ya, yb, wo, g, w_in, cw, cb, w2)


def _odd_in_kernel(x_ref, g_ref, w_ref, o_ref):
    h = _rms(x_ref[...], g_ref[...])
    o_ref[...] = _dot(h.astype(BF16), w_ref[...])


def _odd_in(x, g, w):
    T = x.shape[0]
    tm = TM_PROJ
    ncol = w.shape[1]
    return pl.pallas_call(
        _odd_in_kernel,
        grid=(T // tm,),
        in_specs=[pl.BlockSpec((tm, D_MODEL), lambda i: (i, 0)),
                  pl.BlockSpec((1, D_MODEL), lambda i: (0, 0)),
                  pl.BlockSpec((D_MODEL, ncol), lambda i: (0, 0))],
        out_specs=pl.BlockSpec((tm, ncol), lambda i: (i, 0)),
        out_shape=jax.ShapeDtypeStruct((T, ncol), F32),
        compiler_params=_cparams("arbitrary"),
        name="odd_in_proj",
    )(x, g, w)


def _cswap(v):
    return jnp.concatenate([v[:, LANES:], v[:, :LANES]], axis=1)


def _s5_kernel(u_ref, bm_ref, p1_ref, p2_ref, q1_ref, q2_ref, pc_ref, tri_ref, cm_ref,
               d_ref, gw_ref, gb_ref, o_ref, carry, x_s, ys_s):
    tl = u_ref.shape[0]
    L = L_S5
    blk = 2 * LANES
    slab = 2 * S5_NSTATE // (S5_WIDTH // LANES)

    @pl.when(pl.program_id(1) == 0)
    def _():
        carry[...] = jnp.zeros_like(carry)

    tri = tri_ref[...]
    for s in range(S5_WIDTH // LANES):
        u = u_ref[:, s * LANES:(s + 1) * LANES]
        u16 = u.astype(BF16)
        for jb in range(slab // blk):
            cols = slice(s * slab + jb * blk, s * slab + (jb + 1) * blk)
            bu = _dot(u16, bm_ref[s, :, jb * blk:(jb + 1) * blk])
            for c in range(tl // L):
                rows = slice(c * L, (c + 1) * L)
                v = bu[rows, :].astype(BF16)
                z = v * q1_ref[:, cols] + _cswap(v) * q2_ref[:, cols]
                w = _dot(tri, z) + carry[0:1, cols]
                w16 = w.astype(BF16)
                x_s[rows, jb * blk:(jb + 1) * blk] = (
                    w16 * p1_ref[:, cols] + _cswap(w16) * p2_ref[:, cols])
                wl = w[L - 1:L, :]
                xl = wl * pc_ref[2:3, cols] + _cswap(wl) * pc_ref[3:4, cols]
                carry[0:1, cols] = xl * pc_ref[0:1, cols] + _cswap(xl) * pc_ref[1:2, cols]
        y = _dot(x_s[...], cm_ref[s]) + d_ref[:, s * LANES:(s + 1) * LANES] * u
        ys_s[:, s * LANES:(s + 1) * LANES] = jax.nn.gelu(y)
    ys = ys_s[...]
    o_ref[...] = (ys * jax.nn.sigmoid(_dot(ys.astype(BF16), gw_ref[...]) + gb_ref[...])).astype(BF16)


def _s5(proj, bm, p1, p2, q1, q2, pc, tri, cm, d, gw, gb, batch, seq):
    T = proj.shape[0]
    tl = TL_S5
    n_seq = seq // tl
    c2 = lambda b, i: (0, 0)
    c3 = lambda b, i: (0, 0, 0)
    return pl.pallas_call(
        _s5_kernel,
        grid=(batch, n_seq),
        in_specs=[pl.BlockSpec((tl, S5_WIDTH), lambda b, i: (b * n_seq + i, 0)),
                  pl.BlockSpec(bm.shape, c3),
                  pl.BlockSpec(p1.shape, c2), pl.BlockSpec(p2.shape, c2),
                  pl.BlockSpec(q1.shape, c2), pl.BlockSpec(q2.shape, c2),
                  pl.BlockSpec(pc.shape, c2),
                  pl.BlockSpec(tri.shape, c2),
                  pl.BlockSpec(cm.shape, c3),
                  pl.BlockSpec((1, S5_WIDTH), c2),
                  pl.BlockSpec((S5_WIDTH, S5_WIDTH), c2),
                  pl.BlockSpec((1, S5_WIDTH), c2)],
        out_specs=pl.BlockSpec((tl, S5_WIDTH), lambda b, i: (b * n_seq + i, 0)),
        out_shape=jax.ShapeDtypeStruct((T, S5_WIDTH), BF16),
        scratch_shapes=[pltpu.VMEM((SUBLANES, 2 * S5_NSTATE), F32),
                        pltpu.VMEM((tl, 2 * S5_NSTATE // (S5_WIDTH // LANES)), BF16),
                        pltpu.VMEM((tl, S5_WIDTH), F32)],
        compiler_params=_cparams("arbitrary", "arbitrary"),
        name="s5_glu",
    )(proj, bm, p1, p2, q1, q2, pc, tri, cm, d, gw, gb)


def _s5_params(lam_re, lam_im, b_re, b_im, c_re, c_im, log_step):
    G, N, P = S5_GROUPS, S5_STATE, S5_GROUP
    gs = LANES // P
    ns = S5_WIDTH // LANES
    step = jnp.exp(log_step.astype(F32))[:, None]
    lr, li = lam_re.astype(F32), lam_im.astype(F32)
    mag = jnp.exp(lr * step)
    ar, ai = mag * jnp.cos(li * step), mag * jnp.sin(li * step)
    den = lr * lr + li * li
    cr = ((ar - 1.0) * lr + ai * li) / den
    ci = (ai * lr - (ar - 1.0) * li) / den
    bbr = cr[..., None] * b_re - ci[..., None] * b_im
    bbi = cr[..., None] * b_im + ci[..., None] * b_re
    eye = jnp.eye(gs, dtype=F32)

    def in_blockdiag(t):
        t = t.reshape(ns, gs, N, P).transpose(0, 1, 3, 2)
        return jnp.einsum('ab,sapn->sapbn', eye, t).reshape(ns, gs * P, gs * N)

    def out_blockdiag(t):
        t = t.reshape(ns, gs, P, N).transpose(0, 1, 3, 2)
        return jnp.einsum('ab,sanp->sanbp', eye, t).reshape(ns, gs * N, gs * P)

    def interleave(re, im, axis):
        shp = list(re.shape)
        blocked = shp[:axis] + [shp[axis] // LANES, LANES] + shp[axis + 1:]
        both = jnp.stack([re.reshape(blocked), im.reshape(blocked)], axis=axis + 1)
        return both.reshape(shp[:axis] + [2 * shp[axis]] + shp[axis + 1:])

    bm = interleave(in_blockdiag(bbr), in_blockdiag(bbi), 2).astype(BF16)
    cm = interleave(out_blockdiag(c_re), out_blockdiag(-c_im), 1).astype(BF16)
    def twice(v):
        v = v.reshape(G * N // LANES, 1, LANES)
        return jnp.broadcast_to(v, (G * N // LANES, 2, LANES)).reshape(1, 2 * G * N)

    sign = jnp.asarray(np.tile(np.repeat([-1.0, 1.0], LANES), G * N // LANES)[None, :], F32)
    j = jnp.arange(L_S5, dtype=F32)[:, None]
    la, th = twice(lr * step), twice(li * step)
    pmag, qmag = jnp.exp(j * la), jnp.exp(-(j * la))
    cs, sn = jnp.cos(j * th), jnp.sin(j * th)
    p1, p2 = pmag * cs, sign * (pmag * sn)
    q1, q2 = qmag * cs, -(sign * (qmag * sn))
    pc = jnp.concatenate([p1[1:2], p2[1:2], p1[L_S5 - 1:], p2[L_S5 - 1:]], axis=0)
    return bm, p1.astype(BF16), p2.astype(BF16), q1.astype(BF16), q2.astype(BF16), pc, cm


def _gla_cumsum_matrix(tl):
    r = np.arange(tl)[:, None]
    c = np.arange(tl)[None, :]
    same = (r // C_GLA) == (c // C_GLA)
    return np.concatenate([same & (c <= r), same & (c > r)], axis=0).astype(np.float32)


def _gla_kernel(q_ref, k_ref, v_ref, g_ref, gk_ref, gw_ref, gb_ref, nrm_ref, cum_ref, o_ref, st):
    nbatch, tl = q_ref.shape[0], q_ref.shape[1]
    C = C_GLA
    blk = 2 * C
    pair = LANES // GLA_DK

    @pl.when(pl.program_id(0) == 0)
    def _():
        st[...] = jnp.zeros_like(st)

    lane = lax.broadcasted_iota(jnp.int32, (1, LANES), 1)
    rb = lax.broadcasted_iota(jnp.int32, (blk, blk), 0)
    cb = lax.broadcasted_iota(jnp.int32, (blk, blk), 1)
    causal = (rb >= cb) & ((rb < C) | (cb >= C))
    streams = []
    for n in range(nbatch):
        z = _dot(gk_ref[n].astype(BF16), gw_ref[...]) + gb_ref[...]
        log_a = _log_sigmoid(z) * (1.0 / GLA_TAU)
        hi, lo = _split_bf16(log_a)
        sums = _dot(cum_ref[...], hi) + _dot(cum_ref[...], lo)
        bc, suffix = sums[:tl], sums[tl:]
        eb = jnp.exp(bc)
        q_dec = q_ref[n] * (GLA_DK ** -0.5) * eb
        k = k_ref[n]
        k_inv = (k * jnp.exp(-bc)).astype(BF16)
        k_dec = k * jnp.exp(suffix)
        for h in range(GLA_HEADS):
            hp, hh = divmod(h, pair)
            cols = slice(hp * LANES, (hp + 1) * LANES)
            in_head = (lane >= hh * GLA_DK) & (lane < (hh + 1) * GLA_DK)
            streams.append(dict(
                n=n, h=h, eb=eb[:, cols],
                qd=jnp.where(in_head, q_dec[:, cols], 0.0).astype(BF16),
                kd=jnp.where(in_head, k_dec[:, cols], 0.0).astype(BF16),
                ki=k_inv[:, cols],
                vh=v_ref[n, :, h * GLA_DV:(h + 1) * GLA_DV].astype(BF16),
                state=st[n * GLA_HEADS + h]))
    for b in range(tl // blk):
        rows = slice(b * blk, (b + 1) * blk)
        for sd in streams:
            n, h = sd["n"], sd["h"]
            att = jnp.where(causal, _dot_nt(sd["qd"][rows], sd["ki"][rows]), 0.0)
            o = _dot(att.astype(BF16), sd["vh"][rows])
            inter = []
            for c in range(b * blk // C, (b + 1) * blk // C):
                crow = slice(c * C, (c + 1) * C)
                inter.append(_dot_nt(sd["qd"][crow], sd["state"].astype(BF16)))
                decay = sd["eb"][(c + 1) * C - 1:(c + 1) * C, :]
                sd["state"] = sd["state"] * decay + _dot_tn(sd["vh"][crow], sd["kd"][crow])
            o = _rms(o + jnp.concatenate(inter, axis=0), nrm_ref[...])
            gh = g_ref[n, rows, h * GLA_DV:(h + 1) * GLA_DV]
            o_ref[n, rows, h * GLA_DV:(h + 1) * GLA_DV] = (o * jax.nn.silu(gh)).astype(BF16)
    for sd in streams:
        st[sd["n"] * GLA_HEADS + sd["h"]] = sd["state"]


def _gla(proj, gw, gb, nrm, tri, batch, seq):
    tl = TL_GLA
    hk = GLA_HEADS * GLA_DK
    hv = GLA_HEADS * GLA_DV
    c2 = lambda i: (0, 0)
    q0 = S5_WIDTH // hk
    v0 = (S5_WIDTH + 2 * hk) // hv
    gk0 = (S5_WIDTH + 2 * hk + 2 * hv) // GK_PAD
    proj = proj.reshape(batch, seq, proj.shape[-1])
    out = pl.pallas_call(
        _gla_kernel,
        grid=(seq // tl,),
        in_specs=[pl.BlockSpec((batch, tl, hk), lambda i: (0, i, q0)),
                  pl.BlockSpec((batch, tl, hk), lambda i: (0, i, q0 + 1)),
                  pl.BlockSpec((batch, tl, hv), lambda i: (0, i, v0)),
                  pl.BlockSpec((batch, tl, hv), lambda i: (0, i, v0 + 1)),
                  pl.BlockSpec((batch, tl, GK_PAD), lambda i: (0, i, gk0)),
                  pl.BlockSpec((GK_PAD, hk), c2),
                  pl.BlockSpec((1, hk), c2),
                  pl.BlockSpec((1, GLA_DV), c2),
                  pl.BlockSpec((2 * tl, tl), c2)],
        out_specs=pl.BlockSpec((batch, tl, hv), lambda i: (0, i, 0)),
        out_shape=jax.ShapeDtypeStruct((batch, seq, hv), BF16),
        scratch_shapes=[pltpu.VMEM((batch * GLA_HEADS, GLA_DV, LANES), F32)],
        compiler_params=_cparams("arbitrary"),
        name="gla",
    )(proj, proj, proj, proj, proj, gw, gb, nrm, tri)
    return out.reshape(batch * seq, hv)


def _block_diag(w):
    nb, a, b = w.shape
    return jnp.einsum('hk,hij->hikj', jnp.eye(nb, dtype=w.dtype), w).reshape(nb * a, nb * b)


def _rope_tables(seq):
    half = ROPE_DIM // 2
    pos = np.arange(seq, dtype=np.float64)
    inv = ROPE_THETA ** (-np.arange(0, ROPE_DIM, 2, dtype=np.float64) / ROPE_DIM)
    ang = pos[:, None] * inv[None, :]
    cos, sin = np.cos(ang), np.sin(ang)
    rest = ATT_HEAD_DIM - ROPE_DIM
    ones = np.ones((seq, rest))
    zeros = np.zeros((seq, rest))
    zh = np.zeros((seq, half))
    per_head = lambda parts: jnp.asarray(
        np.tile(np.concatenate(parts, axis=1), (1, LANES // ATT_HEAD_DIM)), F32)
    return (per_head([cos, cos, ones]), per_head([-sin, zh, zeros]), per_head([zh, sin, zeros]))


def _cast_kernel(w_ref, o_ref):
    o_ref[...] = w_ref[0].astype(BF16)


def _to_bf16(w, idx):
    _, rows, cols = w.shape
    tr = rows // 4
    return pl.pallas_call(
        _cast_kernel,
        grid=(rows // tr,),
        in_specs=[pl.BlockSpec((1, tr, cols), lambda r: (idx, r, 0))],
        out_specs=pl.BlockSpec((tr, cols), lambda r: (r, 0)),
        out_shape=jax.ShapeDtypeStruct((rows, cols), BF16),
        compiler_params=_cparams("arbitrary"),
        name="weight_to_bf16",
    )(w)


def kernel(x, e_norm, e_w_in, e_conv_w, e_conv_b, e_gate_a_w, e_gate_a_b, e_gate_x_w, e_gate_x_b, e_lambda, e_q_norm, e_k_norm, e_w_out, o_norm, o_w_in, o_lambda_re, o_lambda_im, o_b_re, o_b_im, o_c_re, o_c_im, o_d, o_log_step, o_glu_w, o_glu_b, o_gk_w, o_gk_b, o_gla_norm, o_w_out, f_norm, f_w_in, f_conv_w, f_conv_b, f_w_out):
    B, S, D = x.shape
    T = B * S
    depth = f_norm.shape[0]
    row = lambda t: t.reshape(1, -1).astype(F32)
    xt = x.reshape(T, D)

    cos_t, s1_t, s2_t = _rope_tables(S)
    head_seg = jnp.asarray(np.kron(np.eye(LANES // ATT_HEAD_DIM), np.ones((ATT_HEAD_DIM, ATT_HEAD_DIM))), BF16)
    att_bias = jnp.asarray(_attention_bias())
    tri_s5 = jnp.asarray(np.tril(np.ones((L_S5, L_S5))), BF16)
    tri_gla = jnp.asarray(_gla_cumsum_matrix(TL_GLA), BF16)
    two_heads = lambda t: jnp.tile(row(t), (1, LANES // ATT_HEAD_DIM))

    for layer in range(depth):
        i = layer // 2
        if layer % 2 == 0:
            xg, q, k, v = _even_in(xt, row(e_norm[i]), _to_bf16(e_w_in, i),
                                   two_heads(e_q_norm[i]), two_heads(e_k_norm[i]),
                                   head_seg, cos_t, s1_t, s2_t, S)
            ya = _lru(xg, e_conv_w[i], row(e_conv_b[i]),
                      _block_diag(e_gate_a_w[i]).astype(BF16), row(e_gate_a_b[i]),
                      _block_diag(e_gate_x_w[i]).astype(BF16), row(e_gate_x_b[i]),
                      row(e_lambda[i]), B, S)
            yb = _attention(q.reshape(B, S, ATT_WIDTH), k.reshape(B, S, ATT_WIDTH),
                            v.reshape(B, S, ATT_WIDTH), att_bias).reshape(T, ATT_WIDTH)
            w_out = _to_bf16(e_w_out, i)
        else:
            w_in = jnp.pad(o_w_in[i], ((0, 0), (0, GK_PAD - GLA_LOWRANK))).astype(BF16)
            proj = _odd_in(xt, row(o_norm[i]), w_in)
            bm, p1, p2, q1, q2, pc, cm = _s5_params(o_lambda_re[i], o_lambda_im[i], o_b_re[i], o_b_im[i],
                                                    o_c_re[i], o_c_im[i], o_log_step[i])
            ya = _s5(proj, bm, p1, p2, q1, q2, pc, tri_s5, cm, row(o_d[i]),
                     _to_bf16(o_glu_w, i), row(o_glu_b[i]), B, S)
            gk_w = jnp.pad(o_gk_w[i], ((0, GK_PAD - GLA_LOWRANK), (0, 0))).astype(BF16)
            yb = _gla(proj, gk_w, row(o_gk_b[i]), row(o_gla_norm[i]), tri_gla, B, S)
            w_out = _to_bf16(o_w_out, i)
        xt = _ffn(xt, ya, yb, w_out, row(f_norm[layer]), _to_bf16(f_w_in, layer),
                  f_conv_w[layer], row(f_conv_b[layer]), _to_bf16(f_w_out, layer), S)
    return xt.reshape(B, S, D)
```

```python
import functools
import math

import numpy as np
import jax
import jax.numpy as jnp
from jax import lax
from jax.experimental import pallas as pl
from jax.experimental.pallas import tpu as pltpu

F32 = jnp.float32
BF16 = jnp.bfloat16

D_MODEL = 1024
LRU_WIDTH = 512
LRU_BLOCKS = 8
LRU_CONV = 4
LRU_C = 8.0
ATT_HEADS = 8
ATT_HEAD_DIM = 64
ATT_WIDTH = 512
DILATED_PATTERNS = ((128, 1), (512, 4), (2048, 16))
ATT_SPAN = 2048
ROPE_THETA = 500000.0
ROPE_DIM = 16
S5_WIDTH = 512
S5_GROUP = 16
S5_GROUPS = 32
S5_STATE = 64
S5_NSTATE = S5_GROUPS * S5_STATE
GLA_HEADS = 4
GLA_DK = 64
GLA_DV = 128
GLA_LOWRANK = 16
GLA_TAU = 16.0
D_FF = 3 * D_MODEL
EPS = 1e-6
NEG_INF = -1e30

LANES = 128
SUBLANES = 8
VMEM_LIMIT = 56 * 1024 * 1024
VMEM_LIMIT_MLP = 60 * 1024 * 1024

TM_PROJ = 512
TM_FFN = 1024
TF_FFN = 512
TL_LRU = 512
TL_S5 = 512
L_S5 = 128
TL_GLA = 512
C_GLA = 64
GK_PAD = 128


def _cparams(*sem, vmem=VMEM_LIMIT):
    return pltpu.CompilerParams(dimension_semantics=sem, vmem_limit_bytes=vmem)


def _rms(x, g):
    return x * lax.rsqrt(jnp.mean(x * x, axis=-1, keepdims=True) + EPS) * g


def _log_sigmoid(x):
    return -(jnp.maximum(-x, 0.0) + jnp.log(1.0 + jnp.exp(-jnp.abs(x))))


def _split_bf16(x):
    hi = x.astype(BF16)
    lo = (x - hi.astype(F32)).astype(BF16)
    return hi, lo


def _dot(a, b):
    return jnp.dot(a, b, preferred_element_type=F32)


def _dot_nt(a, b):
    return lax.dot_general(a, b, (((1,), (1,)), ((), ())), preferred_element_type=F32)


def _dot_tn(a, b):
    return lax.dot_general(a, b, (((0,), (0,)), ((), ())), preferred_element_type=F32)


def _even_in_kernel(x_ref, g_ref, w_ref, qn_ref, kn_ref, seg_ref, cos_ref, s1_ref, s2_ref,
                    xg_ref, q_ref, k_ref, v_ref):
    h = _rms(x_ref[...], g_ref[...]).astype(BF16)
    seg = seg_ref[...]
    cos, s1, s2 = cos_ref[...], s1_ref[...], s2_ref[...]
    half = ROPE_DIM // 2
    for off, n_ref, dst, scale in ((2 * LRU_WIDTH, qn_ref, q_ref, ATT_HEAD_DIM ** -0.5),
                                   (2 * LRU_WIDTH + ATT_WIDTH, kn_ref, k_ref, 1.0)):
        y = _dot(h, w_ref[:, off:off + ATT_WIDTH])
        for c in range(ATT_WIDTH // LANES):
            t = y[:, c * LANES:(c + 1) * LANES]
            ms = _dot((t * t).astype(BF16), seg) * (1.0 / ATT_HEAD_DIM)
            tn = t * lax.rsqrt(ms + EPS) * n_ref[...]
            r = (tn * cos + pltpu.roll(tn, LANES - half, 1) * s1 + pltpu.roll(tn, half, 1) * s2)
            dst[:, c * LANES:(c + 1) * LANES] = r * scale
    xg_ref[...] = _dot(h, w_ref[:, :2 * LRU_WIDTH])
    v_ref[...] = _dot(h, w_ref[:, 2 * LRU_WIDTH + 2 * ATT_WIDTH:])


def _even_in(x, g, w, qn, kn, seg, cos_t, s1_t, s2_t, seq):
    T = x.shape[0]
    tm = TM_PROJ
    n_seq = seq // tm
    ncol = w.shape[1]
    full = lambda i: (0, 0)
    tab = lambda i: (i % n_seq, 0)
    row = lambda i: (i, 0)
    return pl.pallas_call(
        _even_in_kernel,
        grid=(T // tm,),
        in_specs=[pl.BlockSpec((tm, D_MODEL), row),
                  pl.BlockSpec((1, D_MODEL), full),
                  pl.BlockSpec((D_MODEL, ncol), full),
                  pl.BlockSpec((1, LANES), full),
                  pl.BlockSpec((1, LANES), full),
                  pl.BlockSpec((LANES, LANES), full),
                  pl.BlockSpec((tm, LANES), tab),
                  pl.BlockSpec((tm, LANES), tab),
                  pl.BlockSpec((tm, LANES), tab)],
        out_specs=[pl.BlockSpec((tm, 2 * LRU_WIDTH), row),
                   pl.BlockSpec((tm, ATT_WIDTH), row),
                   pl.BlockSpec((tm, ATT_WIDTH), row),
                   pl.BlockSpec((tm, ATT_WIDTH), row)],
        out_shape=[jax.ShapeDtypeStruct((T, 2 * LRU_WIDTH), F32),
                   jax.ShapeDtypeStruct((T, ATT_WIDTH), F32),
                   jax.ShapeDtypeStruct((T, ATT_WIDTH), F32),
                   jax.ShapeDtypeStruct((T, ATT_WIDTH), F32)],
        compiler_params=_cparams("arbitrary"),
        name="even_in_proj",
    )(x, g, w, qn, kn, seg, cos_t, s1_t, s2_t)


def _lru_kernel(xl_ref, gl_ref, cw_ref, cb_ref, wa_ref, ba_ref, wx_ref, bx_ref, lam_ref,
                o_ref, xbuf, hprev):
    tl = xl_ref.shape[0]

    @pl.when(pl.program_id(1) == 0)
    def _():
        xbuf[...] = jnp.zeros_like(xbuf)
        hprev[...] = jnp.zeros_like(hprev)

    groups = tl // SUBLANES
    sub = lax.broadcasted_iota(jnp.int32, (1, SUBLANES, 1), 1)
    x = xl_ref[...]
    x3 = x.reshape(groups, SUBLANES, LRU_WIDTH)
    prev = xbuf[...].reshape(1, SUBLANES, LRU_WIDTH)
    xbuf[...] = x[tl - SUBLANES:tl, :]
    conv = cb_ref[...] + cw_ref[LRU_CONV - 1:LRU_CONV, :] * x3
    for k in range(1, LRU_CONV):
        r = pltpu.roll(x3, k, 1)
        rp = jnp.concatenate([pltpu.roll(prev, k, 1), r[:groups - 1]], axis=0)
        conv = conv + cw_ref[LRU_CONV - 1 - k:LRU_CONV - k, :] * jnp.where(sub >= k, r, rp)
    conv = conv.reshape(tl, LRU_WIDTH)

    c16 = conv.astype(BF16)
    r = jax.nn.sigmoid(_dot(c16, wa_ref[...]) + ba_ref[...])
    ig = jax.nn.sigmoid(_dot(c16, wx_ref[...]) + bx_ref[...])
    log_a = (LRU_C * r) * _log_sigmoid(lam_ref[...])
    a = jnp.exp(log_a)
    b = jnp.sqrt(1.0 - a * a) * (ig * conv)

    a = a.reshape(groups, SUBLANES, LRU_WIDTH)
    b = b.reshape(groups, SUBLANES, LRU_WIDTH)
    d = 1
    while d < SUBLANES:
        keep = sub >= d
        a_sh = jnp.where(keep, pltpu.roll(a, d, 1), 1.0)
        b_sh = jnp.where(keep, pltpu.roll(b, d, 1), 0.0)
        b = a * b_sh + b
        a = a * a_sh
        d *= 2
    last = hprev[0:1, :]
    hs = []
    for t in range(groups):
        ht = b[t] + a[t] * last
        hs.append(ht)
        last = ht[SUBLANES - 1:SUBLANES, :]
    hprev[0:1, :] = last
    h = jnp.concatenate(hs, axis=0)
    o_ref[...] = (h * jax.nn.gelu(gl_ref[...])).astype(BF16)


def _lru(xg, cw, cb, wa, ba, wx, bx, lam, batch, seq):
    T = xg.shape[0]
    tl = TL_LRU
    n_seq = seq // tl
    full = lambda b, i: (0, 0)
    return pl.pallas_call(
        _lru_kernel,
        grid=(batch, n_seq),
        in_specs=[pl.BlockSpec((tl, LRU_WIDTH), lambda b, i: (b * n_seq + i, 0)),
                  pl.BlockSpec((tl, LRU_WIDTH), lambda b, i: (b * n_seq + i, 1)),
                  pl.BlockSpec((LRU_CONV, LRU_WIDTH), full),
                  pl.BlockSpec((1, LRU_WIDTH), full),
                  pl.BlockSpec((LRU_WIDTH, LRU_WIDTH), full),
                  pl.BlockSpec((1, LRU_WIDTH), full),
                  pl.BlockSpec((LRU_WIDTH, LRU_WIDTH), full),
                  pl.BlockSpec((1, LRU_WIDTH), full),
                  pl.BlockSpec((1, LRU_WIDTH), full)],
        out_specs=pl.BlockSpec((tl, LRU_WIDTH), lambda b, i: (b * n_seq + i, 0)),
        out_shape=jax.ShapeDtypeStruct((T, LRU_WIDTH), BF16),
        scratch_shapes=[pltpu.VMEM((SUBLANES, LRU_WIDTH), F32),
                        pltpu.VMEM((SUBLANES, LRU_WIDTH), F32)],
        compiler_params=_cparams("arbitrary", "arbitrary"),
        name="rg_lru",
    )(xg, xg, cw, cb, wa, ba, wx, bx, lam)


N_BACK = DILATED_PATTERNS[0][0] // DILATED_PATTERNS[0][1]
Q_BLOCKS = ATT_SPAN // N_BACK


def _attention_bias():
    qi = np.arange(N_BACK)[:, None]
    ki = np.arange(2 * N_BACK)[None, :]
    dist = N_BACK + qi - ki
    band = (dist >= 0) & (dist <= N_BACK)
    first = band & (ki >= N_BACK)
    return np.where(np.stack([band, first]), 0.0, NEG_INF).astype(np.float32)


def _attn_kernel(q_ref, k_ref, v_ref, bias_ref, o_ref, *scratch):
    kv_s = scratch[:6]
    o_s, m_s, l_s = scratch[6:]
    sb = pl.program_id(2)
    lane = lax.broadcasted_iota(jnp.int32, (1, LANES), 1)
    head0 = lane < ATT_HEAD_DIM

    for p, (window, dil) in enumerate(DILATED_PATTERNS):
        per_res = Q_BLOCKS // dil
        span = N_BACK * per_res
        for src, dst in ((k_ref, kv_s[2 * p]), (v_ref, kv_s[2 * p + 1])):
            @pl.when(sb == 0)
            def _():
                dst[:, 0:N_BACK, :] = jnp.zeros((dil, N_BACK, LANES), BF16)

            @pl.when(sb > 0)
            def _():
                dst[:, 0:N_BACK, :] = dst[:, span:span + N_BACK, :]

            for r in range(dil):
                rows = pl.ds(r, span, stride=dil) if dil > 1 else pl.ds(0, span)
                dst[r, N_BACK:N_BACK + span, :] = src[0, rows, :].astype(BF16)

    for p, (window, dil) in enumerate(DILATED_PATTERNS):
        per_res = Q_BLOCKS // dil
        k_s, v_s = kv_s[2 * p], kv_s[2 * p + 1]

        for n in range(Q_BLOCKS):
            r, m = n % dil, n // dil
            if dil > 1:
                rows = pl.ds(m * (N_BACK * dil) + r, N_BACK, stride=dil)
            else:
                rows = pl.ds(m * N_BACK, N_BACK)
            q = q_ref[0, rows, :].astype(BF16)
            kc = k_s[r, m * N_BACK:(m + 2) * N_BACK, :]
            vc = v_s[r, m * N_BACK:(m + 2) * N_BACK, :]
            bias = bias_ref[jnp.where(sb == 0, 1, 0)] if m == 0 else bias_ref[0]
            res = []
            for h in range(LANES // ATT_HEAD_DIM):
                qm = jnp.where(head0 if h == 0 else ~head0, q, jnp.zeros_like(q))
                s = _dot_nt(qm, kc) + bias
                mx = jnp.max(s, axis=-1, keepdims=True)
                e = jnp.exp(s - mx)
                res.append((_dot(e.astype(BF16), vc), mx, jnp.sum(e, axis=-1, keepdims=True)))
            for dst, idx in ((o_s, 0), (m_s, 1), (l_s, 2)):
                dst[p, rows, :] = jnp.where(head0, res[0][idx], res[1][idx])

    mx = jnp.maximum(jnp.maximum(m_s[0], m_s[1]), m_s[2])
    num = jnp.zeros_like(mx)
    den = jnp.zeros_like(mx)
    for p in range(len(DILATED_PATTERNS)):
        w = jnp.exp(m_s[p] - mx)
        num = num + w * o_s[p]
        den = den + w * l_s[p]
    o_ref[0] = (num / den).astype(BF16)


def _attention(q, k, v, bias):
    B, S, W = q.shape
    blk = pl.BlockSpec((1, ATT_SPAN, LANES), lambda b, p, i: (b, i, p))
    kv_scratch = []
    for window, dil in DILATED_PATTERNS:
        shape = (dil, N_BACK * (1 + Q_BLOCKS // dil), LANES)
        kv_scratch += [pltpu.VMEM(shape, BF16), pltpu.VMEM(shape, BF16)]
    acc = pltpu.VMEM((len(DILATED_PATTERNS), ATT_SPAN, LANES), F32)
    return pl.pallas_call(
        _attn_kernel,
        grid=(B, W // LANES, S // ATT_SPAN),
        in_specs=[blk, blk, blk, pl.BlockSpec(bias.shape, lambda b, p, i: (0, 0, 0))],
        out_specs=blk,
        out_shape=jax.ShapeDtypeStruct((B, S, W), BF16),
        scratch_shapes=kv_scratch + [acc, acc, acc],
        compiler_params=_cparams("arbitrary", "arbitrary", "arbitrary"),
        name="dilated_attention",
    )(q, k, v, bias)


def _ffn_kernel(x_ref, ya_ref, yb_ref, wo_ref, g_ref, w1_ref, cw_ref, cb_ref, w2_ref,
                o_ref, act_s, carry_s, *, tiles_per_seq):
    i = pl.program_id(0)
    tm = x_ref.shape[0]
    half = ya_ref.shape[1]
    tf = TF_FFN

    @pl.when(i % tiles_per_seq == 0)
    def _():
        carry_s[...] = jnp.zeros_like(carry_s)

    x1 = (x_ref[...] + _dot(ya_ref[...], wo_ref[0:half, :])
          + _dot(yb_ref[...], wo_ref[half:2 * half, :]))
    h = _rms(x1, g_ref[...]).astype(BF16)
    groups = tm // SUBLANES
    sub = lax.broadcasted_iota(jnp.int32, (1, SUBLANES, 1), 1)
    for c in range(D_FF // tf):
        cols = slice(c * tf, (c + 1) * tf)
        a = _dot(h, w1_ref[:, cols])
        lin = _dot(h, w1_ref[:, D_FF + c * tf:D_FF + (c + 1) * tf])
        a3 = a.reshape(groups, SUBLANES, tf)
        prev = carry_s[:, cols].reshape(1, SUBLANES, tf)
        carry_s[:, cols] = a[tm - SUBLANES:tm, :]

        def delayed(k):
            r = pltpu.roll(a3, k, 1)
            rp = jnp.concatenate([pltpu.roll(prev, k, 1), r[:groups - 1]], axis=0)
            return jnp.where(sub >= k, r, rp)

        conv = (cb_ref[:, cols] + cw_ref[2:3, cols] * a3
                + cw_ref[1:2, cols] * delayed(1) + cw_ref[0:1, cols] * delayed(2))
        act = jax.nn.gelu(conv).reshape(tm, tf) * lin
        act_s[:, cols] = act.astype(BF16)
    o_ref[...] = x1 + _dot(act_s[...], w2_ref[...])


def _ffn(x, ya, yb, wo, g, w_in, cw, cb, w2, seq):
    T = x.shape[0]
    tm = TM_FFN
    half = ya.shape[1]
    row = lambda i: (i, 0)
    resident = lambda shape: pl.BlockSpec(shape, lambda i: (0, 0), pipeline_mode=pl.Buffered(1))
    return pl.pallas_call(
        functools.partial(_ffn_kernel, tiles_per_seq=seq // tm),
        grid=(T // tm,),
        in_specs=[pl.BlockSpec((tm, D_MODEL), row),
                  pl.BlockSpec((tm, half), row),
                  pl.BlockSpec((tm, half), row),
                  resident((2 * half, D_MODEL)),
                  resident((1, D_MODEL)),
                  resident((D_MODEL, 2 * D_FF)),
                  resident((3, D_FF)),
                  resident((1, D_FF)),
                  resident((D_FF, D_MODEL))],
        out_specs=pl.BlockSpec((tm, D_MODEL), row),
        out_shape=jax.ShapeDtypeStruct((T, D_MODEL), F32),
        scratch_shapes=[pltpu.VMEM((tm, D_FF), BF16),
                        pltpu.VMEM((SUBLANES, D_FF), F32)],
        compiler_params=_cparams("arbitrary", vmem=VMEM_LIMIT_MLP),
        name="outproj_conv_mlp",
    )(x, ya, yb, wo, g, w_in, cw, cb, w2)


def _odd_in_kernel(x_ref, g_ref, w_ref, o_ref):
    h = _rms(x_ref[...], g_ref[...])
    o_ref[...] = _dot(h.astype(BF16), w_ref[...])


def _odd_in(x, g, w):
    T = x.shape[0]
    tm = TM_PROJ
    ncol = w.shape[1]
    return pl.pallas_call(
        _odd_in_kernel,
        grid=(T // tm,),
        in_specs=[pl.BlockSpec((tm, D_MODEL), lambda i: (i, 0)),
                  pl.BlockSpec((1, D_MODEL), lambda i: (0, 0)),
                  pl.BlockSpec((D_MODEL, ncol), lambda i: (0, 0))],
        out_specs=pl.BlockSpec((tm, ncol), lambda i: (i, 0)),
        out_shape=jax.ShapeDtypeStruct((T, ncol), F32),
        compiler_params=_cparams("arbitrary"),
        name="odd_in_proj",
    )(x, g, w)


def _cswap(v):
    return jnp.concatenate([v[:, LANES:], v[:, :LANES]], axis=1)


def _s5_kernel(u_ref, bm_ref, p1_ref, p2_ref, q1_ref, q2_ref, pc_ref, tri_ref, cm_ref,
               d_ref, gw_ref, gb_ref, o_ref, carry, x_s, ys_s):
    tl = u_ref.shape[0]
    L = L_S5
    blk = 2 * LANES
    slab = 2 * S5_NSTATE // (S5_WIDTH // LANES)

    @pl.when(pl.program_id(1) == 0)
    def _():
        carry[...] = jnp.zeros_like(carry)

    tri = tri_ref[...]
    for s in range(S5_WIDTH // LANES):
        u = u_ref[:, s * LANES:(s + 1) * LANES]
        u16 = u.astype(BF16)
        for jb in range(slab // blk):
            cols = slice(s * slab + jb * blk, s * slab + (jb + 1) * blk)
            bu = _dot(u16, bm_ref[s, :, jb * blk:(jb + 1) * blk])
            for c in range(tl // L):
                rows = slice(c * L, (c + 1) * L)
                v = bu[rows, :].astype(BF16)
                z = v * q1_ref[:, cols] + _cswap(v) * q2_ref[:, cols]
                w = _dot(tri, z) + carry[0:1, cols]
                w16 = w.astype(BF16)
                x_s[rows, jb * blk:(jb + 1) * blk] = (
                    w16 * p1_ref[:, cols] + _cswap(w16) * p2_ref[:, cols])
                wl = w[L - 1:L, :]
                xl = wl * pc_ref[2:3, cols] + _cswap(wl) * pc_ref[3:4, cols]
                carry[0:1, cols] = xl * pc_ref[0:1, cols] + _cswap(xl) * pc_ref[1:2, cols]
        y = _dot(x_s[...], cm_ref[s]) + d_ref[:, s * LANES:(s + 1) * LANES] * u
        ys_s[:, s * LANES:(s + 1) * LANES] = jax.nn.gelu(y)
    ys = ys_s[...]
    o_ref[...] = (ys * jax.nn.sigmoid(_dot(ys.astype(BF16), gw_ref[...]) + gb_ref[...])).astype(BF16)


def _s5(proj, bm, p1, p2, q1, q2, pc, tri, cm, d, gw, gb, batch, seq):
    T = proj.shape[0]
    tl = TL_S5
    n_seq = seq // tl
    c2 = lambda b, i: (0, 0)
    c3 = lambda b, i: (0, 0, 0)
    return pl.pallas_call(
        _s5_kernel,
        grid=(batch, n_seq),
        in_specs=[pl.BlockSpec((tl, S5_WIDTH), lambda b, i: (b * n_seq + i, 0)),
                  pl.BlockSpec(bm.shape, c3),
                  pl.BlockSpec(p1.shape, c2), pl.BlockSpec(p2.shape, c2),
                  pl.BlockSpec(q1.shape, c2), pl.BlockSpec(q2.shape, c2),
                  pl.BlockSpec(pc.shape, c2),
                  pl.BlockSpec(tri.shape, c2),
                  pl.BlockSpec(cm.shape, c3),
                  pl.BlockSpec((1, S5_WIDTH), c2),
                  pl.BlockSpec((S5_WIDTH, S5_WIDTH), c2),
                  pl.BlockSpec((1, S5_WIDTH), c2)],
        out_specs=pl.BlockSpec((tl, S5_WIDTH), lambda b, i: (b * n_seq + i, 0)),
        out_shape=jax.ShapeDtypeStruct((T, S5_WIDTH), BF16),
        scratch_shapes=[pltpu.VMEM((SUBLANES, 2 * S5_NSTATE), F32),
                        pltpu.VMEM((tl, 2 * S5_NSTATE // (S5_WIDTH // LANES)), BF16),
                        pltpu.VMEM((tl, S5_WIDTH), F32)],
        compiler_params=_cparams("arbitrary", "arbitrary"),
        name="s5_glu",
    )(proj, bm, p1, p2, q1, q2, pc, tri, cm, d, gw, gb)


def _s5_params(lam_re, lam_im, b_re, b_im, c_re, c_im, log_step):
    G, N, P = S5_GROUPS, S5_STATE, S5_GROUP
    gs = LANES // P
    ns = S5_WIDTH // LANES
    step = jnp.exp(log_step.astype(F32))[:, None]
    lr, li = lam_re.astype(F32), lam_im.astype(F32)
    mag = jnp.exp(lr * step)
    ar, ai = mag * jnp.cos(li * step), mag * jnp.sin(li * step)
    den = lr * lr + li * li
    cr = ((ar - 1.0) * lr + ai * li) / den
    ci = (ai * lr - (ar - 1.0) * li) / den
    bbr = cr[..., None] * b_re - ci[..., None] * b_im
    bbi = cr[..., None] * b_im + ci[..., None] * b_re
    eye = jnp.eye(gs, dtype=F32)

    def in_blockdiag(t):
        t = t.reshape(ns, gs, N, P).transpose(0, 1, 3, 2)
        return jnp.einsum('ab,sapn->sapbn', eye, t).reshape(ns, gs * P, gs * N)

    def out_blockdiag(t):
        t = t.reshape(ns, gs, P, N).transpose(0, 1, 3, 2)
        return jnp.einsum('ab,sanp->sanbp', eye, t).reshape(ns, gs * N, gs * P)

    def interleave(re, im, axis):
        shp = list(re.shape)
        blocked = shp[:axis] + [shp[axis] // LANES, LANES] + shp[axis + 1:]
        both = jnp.stack([re.reshape(blocked), im.reshape(blocked)], axis=axis + 1)
        return both.reshape(shp[:axis] + [2 * shp[axis]] + shp[axis + 1:])

    bm = interleave(in_blockdiag(bbr), in_blockdiag(bbi), 2).astype(BF16)
    cm = interleave(out_blockdiag(c_re), out_blockdiag(-c_im), 1).astype(BF16)
    def twice(v):
        v = v.reshape(G * N // LANES, 1, LANES)
        return jnp.broadcast_to(v, (G * N // LANES, 2, LANES)).reshape(1, 2 * G * N)

    sign = jnp.asarray(np.tile(np.repeat([-1.0, 1.0], LANES), G * N // LANES)[None, :], F32)
    j = jnp.arange(L_S5, dtype=F32)[:, None]
    la, th = twice(lr * step), twice(li * step)
    pmag, qmag = jnp.exp(j * la), jnp.exp(-(j * la))
    cs, sn = jnp.cos(j * th), jnp.sin(j * th)
    p1, p2 = pmag * cs, sign * (pmag * sn)
    q1, q2 = qmag * cs, -(sign * (qmag * sn))
    pc = jnp.concatenate([p1[1:2], p2[1:2], p1[L_S5 - 1:], p2[L_S5 - 1:]], axis=0)
    return bm, p1.astype(BF16), p2.astype(BF16), q1.astype(BF16), q2.astype(BF16), pc, cm


def _gla_cumsum_matrix(tl):
    r = np.arange(tl)[:, None]
    c = np.arange(tl)[None, :]
    same = (r // C_GLA) == (c // C_GLA)
    return np.concatenate([same & (c <= r), same & (c > r)], axis=0).astype(np.float32)


def _gla_kernel(q_ref, k_ref, v_ref, g_ref, gk_ref, gw_ref, gb_ref, nrm_ref, cum_ref, o_ref, st):
    nbatch, tl = q_ref.shape[0], q_ref.shape[1]
    C = C_GLA
    blk = 2 * C
    pair = LANES // GLA_DK

    @pl.when(pl.program_id(0) == 0)
    def _():
        st[...] = jnp.zeros_like(st)

    lane = lax.broadcasted_iota(jnp.int32, (1, LANES), 1)
    rb = lax.broadcasted_iota(jnp.int32, (blk, blk), 0)
    cb = lax.broadcasted_iota(jnp.int32, (blk, blk), 1)
    causal = (rb >= cb) & ((rb < C) | (cb >= C))
    streams = []
    for n in range(nbatch):
        z = _dot(gk_ref[n].astype(BF16), gw_ref[...]) + gb_ref[...]
        log_a = _log_sigmoid(z) * (1.0 / GLA_TAU)
        hi, lo = _split_bf16(log_a)
        sums = _dot(cum_ref[...], hi) + _dot(cum_ref[...], lo)
        bc, suffix = sums[:tl], sums[tl:]
        eb = jnp.exp(bc)
        q_dec = q_ref[n] * (GLA_DK ** -0.5) * eb
        k = k_ref[n]
        k_inv = (k * jnp.exp(-bc)).astype(BF16)
        k_dec = k * jnp.exp(suffix)
        for h in range(GLA_HEADS):
            hp, hh = divmod(h, pair)
            cols = slice(hp * LANES, (hp + 1) * LANES)
            in_head = (lane >= hh * GLA_DK) & (lane < (hh + 1) * GLA_DK)
            streams.append(dict(
                n=n, h=h, eb=eb[:, cols],
                qd=jnp.where(in_head, q_dec[:, cols], 0.0).astype(BF16),
                kd=jnp.where(in_head, k_dec[:, cols], 0.0).astype(BF16),
                ki=k_inv[:, cols],
                vh=v_ref[n, :, h * GLA_DV:(h + 1) * GLA_DV].astype(BF16),
                state=st[n * GLA_HEADS + h]))
    for b in range(tl // blk):
        rows = slice(b * blk, (b + 1) * blk)
        for sd in streams:
            n, h = sd["n"], sd["h"]
            att = jnp.where(causal, _dot_nt(sd["qd"][rows], sd["ki"][rows]), 0.0)
            o = _dot(att.astype(BF16), sd["vh"][rows])
            inter = []
            for c in range(b * blk // C, (b + 1) * blk // C):
                crow = slice(c * C, (c + 1) * C)
                inter.append(_dot_nt(sd["qd"][crow], sd["state"].astype(BF16)))
                decay = sd["eb"][(c + 1) * C - 1:(c + 1) * C, :]
                sd["state"] = sd["state"] * decay + _dot_tn(sd["vh"][crow], sd["kd"][crow])
            o = _rms(o + jnp.concatenate(inter, axis=0), nrm_ref[...])
            gh = g_ref[n, rows, h * GLA_DV:(h + 1) * GLA_DV]
            o_ref[n, rows, h * GLA_DV:(h + 1) * GLA_DV] = (o * jax.nn.silu(gh)).astype(BF16)
    for sd in streams:
        st[sd["n"] * GLA_HEADS + sd["h"]] = sd["state"]


def _gla(proj, gw, gb, nrm, tri, batch, seq):
    tl = TL_GLA
    hk = GLA_HEADS * GLA_DK
    hv = GLA_HEADS * GLA_DV
    c2 = lambda i: (0, 0)
    q0 = S5_WIDTH // hk
    v0 = (S5_WIDTH + 2 * hk) // hv
    gk0 = (S5_WIDTH + 2 * hk + 2 * hv) // GK_PAD
    proj = proj.reshape(batch, seq, proj.shape[-1])
    out = pl.pallas_call(
        _gla_kernel,
        grid=(seq // tl,),
        in_specs=[pl.BlockSpec((batch, tl, hk), lambda i: (0, i, q0)),
                  pl.BlockSpec((batch, tl, hk), lambda i: (0, i, q0 + 1)),
                  pl.BlockSpec((batch, tl, hv), lambda i: (0, i, v0)),
                  pl.BlockSpec((batch, tl, hv), lambda i: (0, i, v0 + 1)),
                  pl.BlockSpec((batch, tl, GK_PAD), lambda i: (0, i, gk0)),
                  pl.BlockSpec((GK_PAD, hk), c2),
                  pl.BlockSpec((1, hk), c2),
                  pl.BlockSpec((1, GLA_DV), c2),
                  pl.BlockSpec((2 * tl, tl), c2)],
        out_specs=pl.BlockSpec((batch, tl, hv), lambda i: (0, i, 0)),
        out_shape=jax.ShapeDtypeStruct((batch, seq, hv), BF16),
        scratch_shapes=[pltpu.VMEM((batch * GLA_HEADS, GLA_DV, LANES), F32)],
        compiler_params=_cparams("arbitrary"),
        name="gla",
    )(proj, proj, proj, proj, proj, gw, gb, nrm, tri)
    return out.reshape(batch * seq, hv)


def _block_diag(w):
    nb, a, b = w.shape
    return jnp.einsum('hk,hij->hikj', jnp.eye(nb, dtype=w.dtype), w).reshape(nb * a, nb * b)


def _rope_tables(seq):
    half = ROPE_DIM // 2
    pos = np.arange(seq, dtype=np.float64)
    inv = ROPE_THETA ** (-np.arange(0, ROPE_DIM, 2, dtype=np.float64) / ROPE_DIM)
    ang = pos[:, None] * inv[None, :]
    cos, sin = np.cos(ang), np.sin(ang)
    rest = ATT_HEAD_DIM - ROPE_DIM
    ones = np.ones((seq, rest))
    zeros = np.zeros((seq, rest))
    zh = np.zeros((seq, half))
    per_head = lambda parts: jnp.asarray(
        np.tile(np.concatenate(parts, axis=1), (1, LANES // ATT_HEAD_DIM)), F32)
    return (per_head([cos, cos, ones]), per_head([-sin, zh, zeros]), per_head([zh, sin, zeros]))


def _cast_kernel(w_ref, o_ref):
    o_ref[...] = w_ref[0].astype(BF16)


def _to_bf16(w, idx):
    _, rows, cols = w.shape
    tr = rows // 4
    return pl.pallas_call(
        _cast_kernel,
        grid=(rows // tr,),
        in_specs=[pl.BlockSpec((1, tr, cols), lambda r: (idx, r, 0))],
        out_specs=pl.BlockSpec((tr, cols), lambda r: (r, 0)),
        out_shape=jax.ShapeDtypeStruct((rows, cols), BF16),
        compiler_params=_cparams("arbitrary"),
        name="weight_to_bf16",
    )(w)


def kernel(x, e_norm, e_w_in, e_conv_w, e_conv_b, e_gate_a_w, e_gate_a_b, e_gate_x_w, e_gate_x_b, e_lambda, e_q_norm, e_k_norm, e_w_out, o_norm, o_w_in, o_lambda_re, o_lambda_im, o_b_re, o_b_im, o_c_re, o_c_im, o_d, o_log_step, o_glu_w, o_glu_b, o_gk_w, o_gk_b, o_gla_norm, o_w_out, f_norm, f_w_in, f_conv_w, f_conv_b, f_w_out):
    B, S, D = x.shape
    T = B * S
    depth = f_norm.shape[0]
    row = lambda t: t.reshape(1, -1).astype(F32)
    xt = x.reshape(T, D)

    cos_t, s1_t, s2_t = _rope_tables(S)
    head_seg = jnp.asarray(np.kron(np.eye(LANES // ATT_HEAD_DIM), np.ones((ATT_HEAD_DIM, ATT_HEAD_DIM))), BF16)
    att_bias = jnp.asarray(_attention_bias())
    tri_s5 = jnp.asarray(np.tril(np.ones((L_S5, L_S5))), BF16)
    tri_gla = jnp.asarray(_gla_cumsum_matrix(TL_GLA), BF16)
    two_heads = lambda t: jnp.tile(row(t), (1, LANES // ATT_HEAD_DIM))

    for layer in range(depth):
        i = layer // 2
        if layer % 2 == 0:
            xg, q, k, v = _even_in(xt, row(e_norm[i]), _to_bf16(e_w_in, i),
                                   two_heads(e_q_norm[i]), two_heads(e_k_norm[i]),
                                   head_seg, cos_t, s1_t, s2_t, S)
            ya = _lru(xg, e_conv_w[i], row(e_conv_b[i]),
                      _block_diag(e_gate_a_w[i]).astype(BF16), row(e_gate_a_b[i]),
                      _block_diag(e_gate_x_w[i]).astype(BF16), row(e_gate_x_b[i]),
                      row(e_lambda[i]), B, S)
            yb = _attention(q.reshape(B, S, ATT_WIDTH), k.reshape(B, S, ATT_WIDTH),
                            v.reshape(B, S, ATT_WIDTH), att_bias).reshape(T, ATT_WIDTH)
            w_out = _to_bf16(e_w_out, i)
        else:
            w_in = jnp.pad(o_w_in[i], ((0, 0), (0, GK_PAD - GLA_LOWRANK))).astype(BF16)
            proj = _odd_in(xt, row(o_norm[i]), w_in)
            bm, p1, p2, q1, q2, pc, cm = _s5_params(o_lambda_re[i], o_lambda_im[i], o_b_re[i], o_b_im[i],
                                                    o_c_re[i], o_c_im[i], o_log_step[i])
            ya = _s5(proj, bm, p1, p2, q1, q2, pc, tri_s5, cm, row(o_d[i]),
                     _to_bf16(o_glu_w, i), row(o_glu_b[i]), B, S)
            gk_w = jnp.pad(o_gk_w[i], ((0, GK_PAD - GLA_LOWRANK), (0, 0))).astype(BF16)
            yb = _gla(proj, gk_w, row(o_gk_b[i]), row(o_gla_norm[i]), tri_gla, B, S)
            w_out = _to_bf16(o_w_out, i)
        xt = _ffn(xt, ya, yb, w_out, row(f_norm[layer]), _to_bf16(f_w_in, layer),
                  f_conv_w[layer], row(f_conv_b[layer]), _to_bf16(f_w_out, layer), S)
    return xt.reshape(B, S, D)
```

```python
import functools
import math

import numpy as np
import jax
import jax.numpy as jnp
from jax import lax
from jax.experimental import pallas as pl
from jax.experimental.pallas import tpu as pltpu

F32 = jnp.float32
BF16 = jnp.bfloat16

D_MODEL = 1024
LRU_WIDTH = 512
LRU_BLOCKS = 8
LRU_CONV = 4
LRU_C = 8.0
ATT_HEADS = 8
ATT_HEAD_DIM = 64
ATT_WIDTH = 512
DILATED_PATTERNS = ((128, 1), (512, 4), (2048, 16))
ATT_SPAN = 2048
ROPE_THETA = 500000.0
ROPE_DIM = 16
S5_WIDTH = 512
S5_GROUP = 16
S5_GROUPS = 32
S5_STATE = 64
S5_NSTATE = S5_GROUPS * S5_STATE
GLA_HEADS = 4
GLA_DK = 64
GLA_DV = 128
GLA_LOWRANK = 16
GLA_TAU = 16.0
D_FF = 3 * D_MODEL
EPS = 1e-6
NEG_INF = -1e30

LANES = 128
SUBLANES = 8
VMEM_LIMIT = 60 * 1024 * 1024
VMEM_LIMIT_ATTENTION = 56 * 1024 * 1024

TM_PROJ = 512
TM_FFN = 1024
TF_FFN = 512
TL_LRU = 512
TL_S5 = 512
L_S5 = 128
TL_GLA = 512
C_GLA = 64
GK_PAD = 128


def _cparams(*sem, vmem=VMEM_LIMIT):
    return pltpu.CompilerParams(dimension_semantics=sem, vmem_limit_bytes=vmem)


def _rms(x, g):
    return x * lax.rsqrt(jnp.mean(x * x, axis=-1, keepdims=True) + EPS) * g


def _log_sigmoid(x):
    return -(jnp.maximum(-x, 0.0) + jnp.log(1.0 + jnp.exp(-jnp.abs(x))))


def _split_bf16(x):
    hi = x.astype(BF16)
    lo = (x - hi.astype(F32)).astype(BF16)
    return hi, lo


def _dot(a, b):
    return jnp.dot(a, b, preferred_element_type=F32)


def _dot_nt(a, b):
    return lax.dot_general(a, b, (((1,), (1,)), ((), ())), preferred_element_type=F32)


def _dot_tn(a, b):
    return lax.dot_general(a, b, (((0,), (0,)), ((), ())), preferred_element_type=F32)


def _even_in_kernel(x_ref, g_ref, w_ref, qn_ref, kn_ref, seg_ref, cos_ref, s1_ref, s2_ref,
                    xg_ref, q_ref, k_ref, v_ref):
    h = _rms(x_ref[...], g_ref[...]).astype(BF16)
    seg = seg_ref[...]
    cos, s1, s2 = cos_ref[...], s1_ref[...], s2_ref[...]
    half = ROPE_DIM // 2
    for off, n_ref, dst, scale in ((2 * LRU_WIDTH, qn_ref, q_ref, ATT_HEAD_DIM ** -0.5),
                                   (2 * LRU_WIDTH + ATT_WIDTH, kn_ref, k_ref, 1.0)):
        y = _dot(h, w_ref[:, off:off + ATT_WIDTH])
        for c in range(ATT_WIDTH // LANES):
            t = y[:, c * LANES:(c + 1) * LANES]
            ms = _dot((t * t).astype(BF16), seg) * (1.0 / ATT_HEAD_DIM)
            tn = t * lax.rsqrt(ms + EPS) * n_ref[...]
            r = (tn * cos + pltpu.roll(tn, LANES - half, 1) * s1 + pltpu.roll(tn, half, 1) * s2)
            dst[:, c * LANES:(c + 1) * LANES] = r * scale
    xg_ref[...] = _dot(h, w_ref[:, :2 * LRU_WIDTH])
    v_ref[...] = _dot(h, w_ref[:, 2 * LRU_WIDTH + 2 * ATT_WIDTH:])


def _even_in(x, g, w, qn, kn, seg, cos_t, s1_t, s2_t, seq):
    T = x.shape[0]
    tm = TM_PROJ
    n_seq = seq // tm
    ncol = w.shape[1]
    full = lambda i: (0, 0)
    tab = lambda i: (i % n_seq, 0)
    row = lambda i: (i, 0)
    return pl.pallas_call(
        _even_in_kernel,
        grid=(T // tm,),
        in_specs=[pl.BlockSpec((tm, D_MODEL), row),
                  pl.BlockSpec((1, D_MODEL), full),
                  pl.BlockSpec((D_MODEL, ncol), full),
                  pl.BlockSpec((1, LANES), full),
                  pl.BlockSpec((1, LANES), full),
                  pl.BlockSpec((LANES, LANES), full),
                  pl.BlockSpec((tm, LANES), tab),
                  pl.BlockSpec((tm, LANES), tab),
                  pl.BlockSpec((tm, LANES), tab)],
        out_specs=[pl.BlockSpec((tm, 2 * LRU_WIDTH), row),
                   pl.BlockSpec((tm, ATT_WIDTH), row),
                   pl.BlockSpec((tm, ATT_WIDTH), row),
                   pl.BlockSpec((tm, ATT_WIDTH), row)],
        out_shape=[jax.ShapeDtypeStruct((T, 2 * LRU_WIDTH), F32),
                   jax.ShapeDtypeStruct((T, ATT_WIDTH), F32),
                   jax.ShapeDtypeStruct((T, ATT_WIDTH), F32),
                   jax.ShapeDtypeStruct((T, ATT_WIDTH), F32)],
        compiler_params=_cparams("arbitrary"),
        name="even_in_proj",
    )(x, g, w, qn, kn, seg, cos_t, s1_t, s2_t)


def _lru_kernel(xl_ref, gl_ref, cw_ref, cb_ref, wa_ref, ba_ref, wx_ref, bx_ref, lam_ref,
                o_ref, xbuf, hprev):
    tl = xl_ref.shape[0]

    @pl.when(pl.program_id(1) == 0)
    def _():
        xbuf[...] = jnp.zeros_like(xbuf)
        hprev[...] = jnp.zeros_like(hprev)

    groups = tl // SUBLANES
    sub = lax.broadcasted_iota(jnp.int32, (1, SUBLANES, 1), 1)
    x = xl_ref[...]
    x3 = x.reshape(groups, SUBLANES, LRU_WIDTH)
    prev = xbuf[...].reshape(1, SUBLANES, LRU_WIDTH)
    xbuf[...] = x[tl - SUBLANES:tl, :]
    conv = cb_ref[...] + cw_ref[LRU_CONV - 1:LRU_CONV, :] * x3
    for k in range(1, LRU_CONV):
        r = pltpu.roll(x3, k, 1)
        rp = jnp.concatenate([pltpu.roll(prev, k, 1), r[:groups - 1]], axis=0)
        conv = conv + cw_ref[LRU_CONV - 1 - k:LRU_CONV - k, :] * jnp.where(sub >= k, r, rp)
    conv = conv.reshape(tl, LRU_WIDTH)

    c16 = conv.astype(BF16)
    r = jax.nn.sigmoid(_dot(c16, wa_ref[...]) + ba_ref[...])
    ig = jax.nn.sigmoid(_dot(c16, wx_ref[...]) + bx_ref[...])
    log_a = (LRU_C * r) * _log_sigmoid(lam_ref[...])
    a = jnp.exp(log_a)
    b = jnp.sqrt(1.0 - a * a) * (ig * conv)

    a = a.reshape(groups, SUBLANES, LRU_WIDTH)
    b = b.reshape(groups, SUBLANES, LRU_WIDTH)
    d = 1
    while d < SUBLANES:
        keep = sub >= d
        a_sh = jnp.where(keep, pltpu.roll(a, d, 1), 1.0)
        b_sh = jnp.where(keep, pltpu.roll(b, d, 1), 0.0)
        b = a * b_sh + b
        a = a * a_sh
        d *= 2
    last = hprev[0:1, :]
    hs = []
    for t in range(groups):
        ht = b[t] + a[t] * last
        hs.append(ht)
        last = ht[SUBLANES - 1:SUBLANES, :]
    hprev[0:1, :] = last
    h = jnp.concatenate(hs, axis=0)
    o_ref[...] = (h * jax.nn.gelu(gl_ref[...])).astype(BF16)


def _lru(xg, cw, cb, wa, ba, wx, bx, lam, batch, seq):
    T = xg.shape[0]
    tl = TL_LRU
    n_seq = seq // tl
    full = lambda b, i: (0, 0)
    return pl.pallas_call(
        _lru_kernel,
        grid=(batch, n_seq),
        in_specs=[pl.BlockSpec((tl, LRU_WIDTH), lambda b, i: (b * n_seq + i, 0)),
                  pl.BlockSpec((tl, LRU_WIDTH), lambda b, i: (b * n_seq + i, 1)),
                  pl.BlockSpec((LRU_CONV, LRU_WIDTH), full),
                  pl.BlockSpec((1, LRU_WIDTH), full),
                  pl.BlockSpec((LRU_WIDTH, LRU_WIDTH), full),
                  pl.BlockSpec((1, LRU_WIDTH), full),
                  pl.BlockSpec((LRU_WIDTH, LRU_WIDTH), full),
                  pl.BlockSpec((1, LRU_WIDTH), full),
                  pl.BlockSpec((1, LRU_WIDTH), full)],
        out_specs=pl.BlockSpec((tl, LRU_WIDTH), lambda b, i: (b * n_seq + i, 0)),
        out_shape=jax.ShapeDtypeStruct((T, LRU_WIDTH), BF16),
        scratch_shapes=[pltpu.VMEM((SUBLANES, LRU_WIDTH), F32),
                        pltpu.VMEM((SUBLANES, LRU_WIDTH), F32)],
        compiler_params=_cparams("arbitrary", "arbitrary"),
        name="rg_lru",
    )(xg, xg, cw, cb, wa, ba, wx, bx, lam)


N_BACK = DILATED_PATTERNS[0][0] // DILATED_PATTERNS[0][1]
Q_BLOCKS = ATT_SPAN // N_BACK


def _attention_bias():
    qi = np.arange(N_BACK)[:, None]
    ki = np.arange(2 * N_BACK)[None, :]
    dist = N_BACK + qi - ki
    band = (dist >= 0) & (dist <= N_BACK)
    first = band & (ki >= N_BACK)
    return np.where(np.stack([band, first]), 0.0, NEG_INF).astype(np.float32)


def _attn_kernel(q_ref, k_ref, v_ref, bias_ref, o_ref, *scratch):
    kv_s = scratch[:6]
    o_s, m_s, l_s = scratch[6:]
    sb = pl.program_id(2)
    lane = lax.broadcasted_iota(jnp.int32, (1, LANES), 1)
    head0 = lane < ATT_HEAD_DIM

    for p, (window, dil) in enumerate(DILATED_PATTERNS):
        per_res = Q_BLOCKS // dil
        span = N_BACK * per_res
        for src, dst in ((k_ref, kv_s[2 * p]), (v_ref, kv_s[2 * p + 1])):
            @pl.when(sb == 0)
            def _():
                dst[:, 0:N_BACK, :] = jnp.zeros((dil, N_BACK, LANES), BF16)

            @pl.when(sb > 0)
            def _():
                dst[:, 0:N_BACK, :] = dst[:, span:span + N_BACK, :]

            for r in range(dil):
                rows = pl.ds(r, span, stride=dil) if dil > 1 else pl.ds(0, span)
                dst[r, N_BACK:N_BACK + span, :] = src[0, rows, :].astype(BF16)

    for p, (window, dil) in enumerate(DILATED_PATTERNS):
        per_res = Q_BLOCKS // dil
        k_s, v_s = kv_s[2 * p], kv_s[2 * p + 1]

        for n in range(Q_BLOCKS):
            r, m = n % dil, n // dil
            if dil > 1:
                rows = pl.ds(m * (N_BACK * dil) + r, N_BACK, stride=dil)
            else:
                rows = pl.ds(m * N_BACK, N_BACK)
            q = q_ref[0, rows, :].astype(BF16)
            kc = k_s[r, m * N_BACK:(m + 2) * N_BACK, :]
            vc = v_s[r, m * N_BACK:(m + 2) * N_BACK, :]
            bias = bias_ref[jnp.where(sb == 0, 1, 0)] if m == 0 else bias_ref[0]
            res = []
            for h in range(LANES // ATT_HEAD_DIM):
                qm = jnp.where(head0 if h == 0 else ~head0, q, jnp.zeros_like(q))
                s = _dot_nt(qm, kc) + bias
                mx = jnp.max(s, axis=-1, keepdims=True)
                e = jnp.exp(s - mx)
                res.append((_dot(e.astype(BF16), vc), mx, jnp.sum(e, axis=-1, keepdims=True)))
            for dst, idx in ((o_s, 0), (m_s, 1), (l_s, 2)):
                dst[p, rows, :] = jnp.where(head0, res[0][idx], res[1][idx])

    mx = jnp.maximum(jnp.maximum(m_s[0], m_s[1]), m_s[2])
    num = jnp.zeros_like(mx)
    den = jnp.zeros_like(mx)
    for p in range(len(DILATED_PATTERNS)):
        w = jnp.exp(m_s[p] - mx)
        num = num + w * o_s[p]
        den = den + w * l_s[p]
    o_ref[0] = (num / den).astype(BF16)


def _attention(q, k, v, bias):
    B, S, W = q.shape
    blk = pl.BlockSpec((1, ATT_SPAN, LANES), lambda b, p, i: (b, i, p))
    kv_scratch = []
    for window, dil in DILATED_PATTERNS:
        shape = (dil, N_BACK * (1 + Q_BLOCKS // dil), LANES)
        kv_scratch += [pltpu.VMEM(shape, BF16), pltpu.VMEM(shape, BF16)]
    acc = pltpu.VMEM((len(DILATED_PATTERNS), ATT_SPAN, LANES), F32)
    return pl.pallas_call(
        _attn_kernel,
        grid=(B, W // LANES, S // ATT_SPAN),
        in_specs=[blk, blk, blk, pl.BlockSpec(bias.shape, lambda b, p, i: (0, 0, 0))],
        out_specs=blk,
        out_shape=jax.ShapeDtypeStruct((B, S, W), BF16),
        scratch_shapes=kv_scratch + [acc, acc, acc],
        compiler_params=_cparams("arbitrary", "arbitrary", "arbitrary", vmem=VMEM_LIMIT_ATTENTION),
        name="dilated_attention",
    )(q, k, v, bias)


def _ffn_kernel(x_ref, ya_ref, yb_ref, wo_ref, g_ref, w1_ref, cw_ref, cb_ref, w2_ref,
                o_ref, act_s, carry_s, *, tiles_per_seq):
    i = pl.program_id(0)
    tm = x_ref.shape[0]
    half = ya_ref.shape[1]
    tf = TF_FFN

    @pl.when(i % tiles_per_seq == 0)
    def _():
        carry_s[...] = jnp.zeros_like(carry_s)

    x1 = (x_ref[...] + _dot(ya_ref[...], wo_ref[0:half, :])
          + _dot(yb_ref[...], wo_ref[half:2 * half, :]))
    h = _rms(x1, g_ref[...]).astype(BF16)
    groups = tm // SUBLANES
    sub = lax.broadcasted_iota(jnp.int32, (1, SUBLANES, 1), 1)
    for c in range(D_FF // tf):
        cols = slice(c * tf, (c + 1) * tf)
        a = _dot(h, w1_ref[:, cols])
        lin = _dot(h, w1_ref[:, D_FF + c * tf:D_FF + (c + 1) * tf])
        a3 = a.reshape(groups, SUBLANES, tf)
        prev = carry_s[:, cols].reshape(1, SUBLANES, tf)
        carry_s[:, cols] = a[tm - SUBLANES:tm, :]

        def delayed(k):
            r = pltpu.roll(a3, k, 1)
            rp = jnp.concatenate([pltpu.roll(prev, k, 1), r[:groups - 1]], axis=0)
            return jnp.where(sub >= k, r, rp)

        conv = (cb_ref[:, cols] + cw_ref[2:3, cols] * a3
                + cw_ref[1:2, cols] * delayed(1) + cw_ref[0:1, cols] * delayed(2))
        act = jax.nn.gelu(conv).reshape(tm, tf) * lin
        act_s[:, cols] = act.astype(BF16)
    o_ref[...] = x1 + _dot(act_s[...], w2_ref[...])


def _ffn(x, ya, yb, wo, g, w_in, cw, cb, w2, seq):
    T = x.shape[0]
    tm = TM_FFN
    half = ya.shape[1]
    row = lambda i: (i, 0)
    resident = lambda shape: pl.BlockSpec(shape, lambda i: (0, 0), pipeline_mode=pl.Buffered(1))
    return pl.pallas_call(
        functools.partial(_ffn_kernel, tiles_per_seq=seq // tm),
        grid=(T // tm,),
        in_specs=[pl.BlockSpec((tm, D_MODEL), row),
                  pl.BlockSpec((tm, half), row),
                  pl.BlockSpec((tm, half), row),
                  resident((2 * half, D_MODEL)),
                  resident((1, D_MODEL)),
                  resident((D_MODEL, 2 * D_FF)),
                  resident((3, D_FF)),
                  resident((1, D_FF)),
                  resident((D_FF, D_MODEL))],
        out_specs=pl.BlockSpec((tm, D_MODEL), row),
        out_shape=jax.ShapeDtypeStruct((T, D_MODEL), F32),
        scratch_shapes=[pltpu.VMEM((tm, D_FF), BF16),
                        pltpu.VMEM((SUBLANES, D_FF), F32)],
        compiler_params=_cparams("arbitrary"),
        name="outproj_conv_mlp",
    )(x, ya, yb, wo, g, w_in, cw, cb, w2)


def _odd_in_kernel(x_ref, g_ref, w_ref, o_ref):
    h = _rms(x_ref[...], g_ref[...])
    o_ref[...] = _dot(h.astype(BF16), w_ref[...])


def _odd_in(x, g, w):
    T = x.shape[0]
    tm = TM_PROJ
    ncol = w.shape[1]
    return pl.pallas_call(
        _odd_in_kernel,
        grid=(T // tm,),
        in_specs=[pl.BlockSpec((tm, D_MODEL), lambda i: (i, 0)),
                  pl.BlockSpec((1, D_MODEL), lambda i: (0, 0)),
                  pl.BlockSpec((D_MODEL, ncol), lambda i: (0, 0))],
        out_specs=pl.BlockSpec((tm, ncol), lambda i: (i, 0)),
        out_shape=jax.ShapeDtypeStruct((T, ncol), F32),
        compiler_params=_cparams("arbitrary"),
        name="odd_in_proj",
    )(x, g, w)


def _cswap(v):
    return jnp.concatenate([v[:, LANES:], v[:, :LANES]], axis=1)


def _s5_kernel(u_ref, bm_ref, p1_ref, p2_ref, q1_ref, q2_ref, pc_ref, tri_ref, cm_ref,
               d_ref, gw_ref, gb_ref, o_ref, carry, x_s, ys_s):
    tl = u_ref.shape[0]
    L = L_S5
    blk = 2 * LANES
    slab = 2 * S5_NSTATE // (S5_WIDTH // LANES)

    @pl.when(pl.program_id(1) == 0)
    def _():
        carry[...] = jnp.zeros_like(carry)

    tri = tri_ref[...]
    for s in range(S5_WIDTH // LANES):
        u = u_ref[:, s * LANES:(s + 1) * LANES]
        u16 = u.astype(BF16)
        for jb in range(slab // blk):
            cols = slice(s * slab + jb * blk, s * slab + (jb + 1) * blk)
            bu = _dot(u16, bm_ref[s, :, jb * blk:(jb + 1) * blk])
            for c in range(tl // L):
                rows = slice(c * L, (c + 1) * L)
                v = bu[rows, :].astype(BF16)
                z = v * q1_ref[:, cols] + _cswap(v) * q2_ref[:, cols]
                w = _dot(tri, z) + carry[0:1, cols]
                w16 = w.astype(BF16)
                x_s[rows, jb * blk:(jb + 1) * blk] = (
                    w16 * p1_ref[:, cols] + _cswap(w16) * p2_ref[:, cols])
                wl = w[L - 1:L, :]
                xl = wl * pc_ref[2:3, cols] + _cswap(wl) * pc_ref[3:4, cols]
                carry[0:1, cols] = xl * pc_ref[0:1, cols] + _cswap(xl) * pc_ref[1:2, cols]
        y = _dot(x_s[...], cm_ref[s]) + d_ref[:, s * LANES:(s + 1) * LANES] * u
        ys_s[:, s * LANES:(s + 1) * LANES] = jax.nn.gelu(y)
    ys = ys_s[...]
    o_ref[...] = (ys * jax.nn.sigmoid(_dot(ys.astype(BF16), gw_ref[...]) + gb_ref[...])).astype(BF16)


def _s5(proj, bm, p1, p2, q1, q2, pc, tri, cm, d, gw, gb, batch, seq):
    T = proj.shape[0]
    tl = TL_S5
    n_seq = seq // tl
    c2 = lambda b, i: (0, 0)
    c3 = lambda b, i: (0, 0, 0)
    return pl.pallas_call(
        _s5_kernel,
        grid=(batch, n_seq),
        in_specs=[pl.BlockSpec((tl, S5_WIDTH), lambda b, i: (b * n_seq + i, 0)),
                  pl.BlockSpec(bm.shape, c3),
                  pl.BlockSpec(p1.shape, c2), pl.BlockSpec(p2.shape, c2),
                  pl.BlockSpec(q1.shape, c2), pl.BlockSpec(q2.shape, c2),
                  pl.BlockSpec(pc.shape, c2),
                  pl.BlockSpec(tri.shape, c2),
                  pl.BlockSpec(cm.shape, c3),
                  pl.BlockSpec((1, S5_WIDTH), c2),
                  pl.BlockSpec((S5_WIDTH, S5_WIDTH), c2),
                  pl.BlockSpec((1, S5_WIDTH), c2)],
        out_specs=pl.BlockSpec((tl, S5_WIDTH), lambda b, i: (b * n_seq + i, 0)),
        out_shape=jax.ShapeDtypeStruct((T, S5_WIDTH), BF16),
        scratch_shapes=[pltpu.VMEM((SUBLANES, 2 * S5_NSTATE), F32),
                        pltpu.VMEM((tl, 2 * S5_NSTATE // (S5_WIDTH // LANES)), BF16),
                        pltpu.VMEM((tl, S5_WIDTH), F32)],
        compiler_params=_cparams("arbitrary", "arbitrary"),
        name="s5_glu",
    )(proj, bm, p1, p2, q1, q2, pc, tri, cm, d, gw, gb)


def _s5_params(lam_re, lam_im, b_re, b_im, c_re, c_im, log_step):
    G, N, P = S5_GROUPS, S5_STATE, S5_GROUP
    gs = LANES // P
    ns = S5_WIDTH // LANES
    step = jnp.exp(log_step.astype(F32))[:, None]
    lr, li = lam_re.astype(F32), lam_im.astype(F32)
    mag = jnp.exp(lr * step)
    ar, ai = mag * jnp.cos(li * step), mag * jnp.sin(li * step)
    den = lr * lr + li * li
    cr = ((ar - 1.0) * lr + ai * li) / den
    ci = (ai * lr - (ar - 1.0) * li) / den
    bbr = cr[..., None] * b_re - ci[..., None] * b_im
    bbi = cr[..., None] * b_im + ci[..., None] * b_re
    eye = jnp.eye(gs, dtype=F32)

    def in_blockdiag(t):
        t = t.reshape(ns, gs, N, P).transpose(0, 1, 3, 2)
        return jnp.einsum('ab,sapn->sapbn', eye, t).reshape(ns, gs * P, gs * N)

    def out_blockdiag(t):
        t = t.reshape(ns, gs, P, N).transpose(0, 1, 3, 2)
        return jnp.einsum('ab,sanp->sanbp', eye, t).reshape(ns, gs * N, gs * P)

    def interleave(re, im, axis):
        shp = list(re.shape)
        blocked = shp[:axis] + [shp[axis] // LANES, LANES] + shp[axis + 1:]
        both = jnp.stack([re.reshape(blocked), im.reshape(blocked)], axis=axis + 1)
        return both.reshape(shp[:axis] + [2 * shp[axis]] + shp[axis + 1:])

    bm = interleave(in_blockdiag(bbr), in_blockdiag(bbi), 2).astype(BF16)
    cm = interleave(out_blockdiag(c_re), out_blockdiag(-c_im), 1).astype(BF16)
    def twice(v):
        v = v.reshape(G * N // LANES, 1, LANES)
        return jnp.broadcast_to(v, (G * N // LANES, 2, LANES)).reshape(1, 2 * G * N)

    sign = jnp.asarray(np.tile(np.repeat([-1.0, 1.0], LANES), G * N // LANES)[None, :], F32)
    j = jnp.arange(L_S5, dtype=F32)[:, None]
    la, th = twice(lr * step), twice(li * step)
    pmag, qmag = jnp.exp(j * la), jnp.exp(-(j * la))
    cs, sn = jnp.cos(j * th), jnp.sin(j * th)
    p1, p2 = pmag * cs, sign * (pmag * sn)
    q1, q2 = qmag * cs, -(sign * (qmag * sn))
    pc = jnp.concatenate([p1[1:2], p2[1:2], p1[L_S5 - 1:], p2[L_S5 - 1:]], axis=0)
    return bm, p1.astype(BF16), p2.astype(BF16), q1.astype(BF16), q2.astype(BF16), pc, cm


def _gla_cumsum_matrix(tl):
    r = np.arange(tl)[:, None]
    c = np.arange(tl)[None, :]
    same = (r // C_GLA) == (c // C_GLA)
    return np.concatenate([same & (c <= r), same & (c > r)], axis=0).astype(np.float32)


def _gla_kernel(q_ref, k_ref, v_ref, g_ref, gk_ref, gw_ref, gb_ref, nrm_ref, cum_ref, o_ref, st):
    nbatch, tl = q_ref.shape[0], q_ref.shape[1]
    C = C_GLA
    blk = 2 * C
    pair = LANES // GLA_DK

    @pl.when(pl.program_id(0) == 0)
    def _():
        st[...] = jnp.zeros_like(st)

    lane = lax.broadcasted_iota(jnp.int32, (1, LANES), 1)
    rb = lax.broadcasted_iota(jnp.int32, (blk, blk), 0)
    cb = lax.broadcasted_iota(jnp.int32, (blk, blk), 1)
    causal = (rb >= cb) & ((rb < C) | (cb >= C))
    streams = []
    for n in range(nbatch):
        z = _dot(gk_ref[n].astype(BF16), gw_ref[...]) + gb_ref[...]
        log_a = _log_sigmoid(z) * (1.0 / GLA_TAU)
        hi, lo = _split_bf16(log_a)
        sums = _dot(cum_ref[...], hi) + _dot(cum_ref[...], lo)
        bc, suffix = sums[:tl], sums[tl:]
        eb = jnp.exp(bc)
        q_dec = q_ref[n] * (GLA_DK ** -0.5) * eb
        k = k_ref[n]
        k_inv = (k * jnp.exp(-bc)).astype(BF16)
        k_dec = k * jnp.exp(suffix)
        for h in range(GLA_HEADS):
            hp, hh = divmod(h, pair)
            cols = slice(hp * LANES, (hp + 1) * LANES)
            in_head = (lane >= hh * GLA_DK) & (lane < (hh + 1) * GLA_DK)
            streams.append(dict(
                n=n, h=h, eb=eb[:, cols],
                qd=jnp.where(in_head, q_dec[:, cols], 0.0).astype(BF16),
                kd=jnp.where(in_head, k_dec[:, cols], 0.0).astype(BF16),
                ki=k_inv[:, cols],
                vh=v_ref[n, :, h * GLA_DV:(h + 1) * GLA_DV].astype(BF16),
                state=st[n * GLA_HEADS + h]))
    for b in range(tl // blk):
        rows = slice(b * blk, (b + 1) * blk)
        for sd in streams:
            n, h = sd["n"], sd["h"]
            att = jnp.where(causal, _dot_nt(sd["qd"][rows], sd["ki"][rows]), 0.0)
            o = _dot(att.astype(BF16), sd["vh"][rows])
            inter = []
            for c in range(b * blk // C, (b + 1) * blk // C):
                crow = slice(c * C, (c + 1) * C)
                inter.append(_dot_nt(sd["qd"][crow], sd["state"].astype(BF16)))
                decay = sd["eb"][(c + 1) * C - 1:(c + 1) * C, :]
                sd["state"] = sd["state"] * decay + _dot_tn(sd["vh"][crow], sd["kd"][crow])
            o = _rms(o + jnp.concatenate(inter, axis=0), nrm_ref[...])
            gh = g_ref[n, rows, h * GLA_DV:(h + 1) * GLA_DV]
            o_ref[n, rows, h * GLA_DV:(h + 1) * GLA_DV] = (o * jax.nn.silu(gh)).astype(BF16)
    for sd in streams:
        st[sd["n"] * GLA_HEADS + sd["h"]] = sd["state"]


def _gla(proj, gw, gb, nrm, tri, batch, seq):
    tl = TL_GLA
    hk = GLA_HEADS * GLA_DK
    hv = GLA_HEADS * GLA_DV
    c2 = lambda i: (0, 0)
    q0 = S5_WIDTH // hk
    v0 = (S5_WIDTH + 2 * hk) // hv
    gk0 = (S5_WIDTH + 2 * hk + 2 * hv) // GK_PAD
    proj = proj.reshape(batch, seq, proj.shape[-1])
    out = pl.pallas_call(
        _gla_kernel,
        grid=(seq // tl,),
        in_specs=[pl.BlockSpec((batch, tl, hk), lambda i: (0, i, q0)),
                  pl.BlockSpec((batch, tl, hk), lambda i: (0, i, q0 + 1)),
                  pl.BlockSpec((batch, tl, hv), lambda i: (0, i, v0)),
                  pl.BlockSpec((batch, tl, hv), lambda i: (0, i, v0 + 1)),
                  pl.BlockSpec((batch, tl, GK_PAD), lambda i: (0, i, gk0)),
                  pl.BlockSpec((GK_PAD, hk), c2),
                  pl.BlockSpec((1, hk), c2),
                  pl.BlockSpec((1, GLA_DV), c2),
                  pl.BlockSpec((2 * tl, tl), c2)],
        out_specs=pl.BlockSpec((batch, tl, hv), lambda i: (0, i, 0)),
        out_shape=jax.ShapeDtypeStruct((batch, seq, hv), BF16),
        scratch_shapes=[pltpu.VMEM((batch * GLA_HEADS, GLA_DV, LANES), F32)],
        compiler_params=_cparams("arbitrary"),
        name="gla",
    )(proj, proj, proj, proj, proj, gw, gb, nrm, tri)
    return out.reshape(batch * seq, hv)


def _block_diag(w):
    nb, a, b = w.shape
    return jnp.einsum('hk,hij->hikj', jnp.eye(nb, dtype=w.dtype), w).reshape(nb * a, nb * b)


def _rope_tables(seq):
    half = ROPE_DIM // 2
    pos = np.arange(seq, dtype=np.float64)
    inv = ROPE_THETA ** (-np.arange(0, ROPE_DIM, 2, dtype=np.float64) / ROPE_DIM)
    ang = pos[:, None] * inv[None, :]
    cos, sin = np.cos(ang), np.sin(ang)
    rest = ATT_HEAD_DIM - ROPE_DIM
    ones = np.ones((seq, rest))
    zeros = np.zeros((seq, rest))
    zh = np.zeros((seq, half))
    per_head = lambda parts: jnp.asarray(
        np.tile(np.concatenate(parts, axis=1), (1, LANES // ATT_HEAD_DIM)), F32)
    return (per_head([cos, cos, ones]), per_head([-sin, zh, zeros]), per_head([zh, sin, zeros]))


def _cast_kernel(w_ref, o_ref):
    o_ref[...] = w_ref[0].astype(BF16)


def _to_bf16(w, idx):
    _, rows, cols = w.shape
    tr = rows // 4
    return pl.pallas_call(
        _cast_kernel,
        grid=(rows // tr,),
        in_specs=[pl.BlockSpec((1, tr, cols), lambda r: (idx, r, 0))],
        out_specs=pl.BlockSpec((tr, cols), lambda r: (r, 0)),
        out_shape=jax.ShapeDtypeStruct((rows, cols), BF16),
        compiler_params=_cparams("arbitrary"),
        name="weight_to_bf16",
    )(w)


def kernel(x, e_norm, e_w_in, e_conv_w, e_conv_b, e_gate_a_w, e_gate_a_b, e_gate_x_w, e_gate_x_b, e_lambda, e_q_norm, e_k_norm, e_w_out, o_norm, o_w_in, o_lambda_re, o_lambda_im, o_b_re, o_b_im, o_c_re, o_c_im, o_d, o_log_step, o_glu_w, o_glu_b, o_gk_w, o_gk_b, o_gla_norm, o_w_out, f_norm, f_w_in, f_conv_w, f_conv_b, f_w_out):
    B, S, D = x.shape
    T = B * S
    depth = f_norm.shape[0]
    row = lambda t: t.reshape(1, -1).astype(F32)
    xt = x.reshape(T, D)

    cos_t, s1_t, s2_t = _rope_tables(S)
    head_seg = jnp.asarray(np.kron(np.eye(LANES // ATT_HEAD_DIM), np.ones((ATT_HEAD_DIM, ATT_HEAD_DIM))), BF16)
    att_bias = jnp.asarray(_attention_bias())
    tri_s5 = jnp.asarray(np.tril(np.ones((L_S5, L_S5))), BF16)
    tri_gla = jnp.asarray(_gla_cumsum_matrix(TL_GLA), BF16)
    two_heads = lambda t: jnp.tile(row(t), (1, LANES // ATT_HEAD_DIM))

    for layer in range(depth):
        i = layer // 2
        if layer % 2 == 0:
            xg, q, k, v = _even_in(xt, row(e_norm[i]), _to_bf16(e_w_in, i),
                                   two_heads(e_q_norm[i]), two_heads(e_k_norm[i]),
                                   head_seg, cos_t, s1_t, s2_t, S)
            ya = _lru(xg, e_conv_w[i], row(e_conv_b[i]),
                      _block_diag(e_gate_a_w[i]).astype(BF16), row(e_gate_a_b[i]),
                      _block_diag(e_gate_x_w[i]).astype(BF16), row(e_gate_x_b[i]),
                      row(e_lambda[i]), B, S)
            yb = _attention(q.reshape(B, S, ATT_WIDTH), k.reshape(B, S, ATT_WIDTH),
                            v.reshape(B, S, ATT_WIDTH), att_bias).reshape(T, ATT_WIDTH)
            w_out = _to_bf16(e_w_out, i)
        else:
            w_in = jnp.pad(o_w_in[i], ((0, 0), (0, GK_PAD - GLA_LOWRANK))).astype(BF16)
            proj = _odd_in(xt, row(o_norm[i]), w_in)
            bm, p1, p2, q1, q2, pc, cm = _s5_params(o_lambda_re[i], o_lambda_im[i], o_b_re[i], o_b_im[i],
                                                    o_c_re[i], o_c_im[i], o_log_step[i])
            ya = _s5(proj, bm, p1, p2, q1, q2, pc, tri_s5, cm, row(o_d[i]),
                     _to_bf16(o_glu_w, i), row(o_glu_b[i]), B, S)
            gk_w = jnp.pad(o_gk_w[i], ((0, GK_PAD - GLA_LOWRANK), (0, 0))).astype(BF16)
            yb = _gla(proj, gk_w, row(o_gk_b[i]), row(o_gla_norm[i]), tri_gla, B, S)
            w_out = _to_bf16(o_w_out, i)
        xt = _ffn(xt, ya, yb, w_out, row(f_norm[layer]), _to_bf16(f_w_in, layer),
                  f_conv_w[layer], row(f_conv_b[layer]), _to_bf16(f_w_out, layer), S)
    return xt.reshape(B, S, D)
```

```python
import functools
import math

import numpy as np
import jax
import jax.numpy as jnp
from jax import lax
from jax.experimental import pallas as pl
from jax.experimental.pallas import tpu as pltpu

F32 = jnp.float32
BF16 = jnp.bfloat16

D_MODEL = 1024
LRU_WIDTH = 512
LRU_BLOCKS = 8
LRU_CONV = 4
LRU_C = 8.0
ATT_HEADS = 8
ATT_HEAD_DIM = 64
ATT_WIDTH = 512
DILATED_PATTERNS = ((128, 1), (512, 4), (2048, 16))
ATT_SPAN = 2048
ROPE_THETA = 500000.0
ROPE_DIM = 16
S5_WIDTH = 512
S5_GROUP = 16
S5_GROUPS = 32
S5_STATE = 64
S5_NSTATE = S5_GROUPS * S5_STATE
GLA_HEADS = 4
GLA_DK = 64
GLA_DV = 128
GLA_LOWRANK = 16
GLA_TAU = 16.0
D_FF = 3 * D_MODEL
EPS = 1e-6
NEG_INF = -1e30

LANES = 128
SUBLANES = 8
VMEM_LIMIT = 60 * 1024 * 1024
VMEM_LIMIT_ATTENTION = 56 * 1024 * 1024

TM_PROJ = 512
TM_FFN = 1024
TF_FFN = 512
TL_LRU = 512
TL_S5 = 512
L_S5 = 128
TL_GLA = 512
C_GLA = 64
GK_PAD = 128


def _cparams(*sem, vmem=VMEM_LIMIT):
    return pltpu.CompilerParams(dimension_semantics=sem, vmem_limit_bytes=vmem)


def _rms(x, g):
    return x * lax.rsqrt(jnp.mean(x * x, axis=-1, keepdims=True) + EPS) * g


def _log_sigmoid(x):
    return -(jnp.maximum(-x, 0.0) + jnp.log(1.0 + jnp.exp(-jnp.abs(x))))


def _split_bf16(x):
    hi = x.astype(BF16)
    lo = (x - hi.astype(F32)).astype(BF16)
    return hi, lo


def _dot(a, b):
    return jnp.dot(a, b, preferred_element_type=F32)


def _dot_nt(a, b):
    return lax.dot_general(a, b, (((1,), (1,)), ((), ())), preferred_element_type=F32)


def _dot_tn(a, b):
    return lax.dot_general(a, b, (((0,), (0,)), ((), ())), preferred_element_type=F32)


def _even_in_kernel(x_ref, g_ref, w_ref, qn_ref, kn_ref, seg_ref, cos_ref, s1_ref, s2_ref,
                    xg_ref, q_ref, k_ref, v_ref):
    h = _rms(x_ref[...], g_ref[...]).astype(BF16)
    seg = seg_ref[...]
    cos, s1, s2 = cos_ref[...], s1_ref[...], s2_ref[...]
    half = ROPE_DIM // 2
    for off, n_ref, dst, scale in ((2 * LRU_WIDTH, qn_ref, q_ref, ATT_HEAD_DIM ** -0.5),
                                   (2 * LRU_WIDTH + ATT_WIDTH, kn_ref, k_ref, 1.0)):
        y = _dot(h, w_ref[:, off:off + ATT_WIDTH])
        for c in range(ATT_WIDTH // LANES):
            t = y[:, c * LANES:(c + 1) * LANES]
            ms = _dot((t * t).astype(BF16), seg) * (1.0 / ATT_HEAD_DIM)
            tn = t * lax.rsqrt(ms + EPS) * n_ref[...]
            r = (tn * cos + pltpu.roll(tn, LANES - half, 1) * s1 + pltpu.roll(tn, half, 1) * s2)
            dst[:, c * LANES:(c + 1) * LANES] = r * scale
    xg_ref[...] = _dot(h, w_ref[:, :2 * LRU_WIDTH])
    v_ref[...] = _dot(h, w_ref[:, 2 * LRU_WIDTH + 2 * ATT_WIDTH:])


def _even_in(x, g, w, qn, kn, seg, cos_t, s1_t, s2_t, seq):
    T = x.shape[0]
    tm = TM_PROJ
    n_seq = seq // tm
    ncol = w.shape[1]
    full = lambda i: (0, 0)
    tab = lambda i: (i % n_seq, 0)
    row = lambda i: (i, 0)
    return pl.pallas_call(
        _even_in_kernel,
        grid=(T // tm,),
        in_specs=[pl.BlockSpec((tm, D_MODEL), row),
                  pl.BlockSpec((1, D_MODEL), full),
                  pl.BlockSpec((D_MODEL, ncol), full),
                  pl.BlockSpec((1, LANES), full),
                  pl.BlockSpec((1, LANES), full),
                  pl.BlockSpec((LANES, LANES), full),
                  pl.BlockSpec((tm, LANES), tab),
                  pl.BlockSpec((tm, LANES), tab),
                  pl.BlockSpec((tm, LANES), tab)],
        out_specs=[pl.BlockSpec((tm, 2 * LRU_WIDTH), row),
                   pl.BlockSpec((tm, ATT_WIDTH), row),
                   pl.BlockSpec((tm, ATT_WIDTH), row),
                   pl.BlockSpec((tm, ATT_WIDTH), row)],
        out_shape=[jax.ShapeDtypeStruct((T, 2 * LRU_WIDTH), F32),
                   jax.ShapeDtypeStruct((T, ATT_WIDTH), F32),
                   jax.ShapeDtypeStruct((T, ATT_WIDTH), F32),
                   jax.ShapeDtypeStruct((T, ATT_WIDTH), F32)],
        compiler_params=_cparams("arbitrary"),
        name="even_in_proj",
    )(x, g, w, qn, kn, seg, cos_t, s1_t, s2_t)


def _lru_kernel(xl_ref, gl_ref, cw_ref, cb_ref, wa_ref, ba_ref, wx_ref, bx_ref, lam_ref,
                o_ref, xbuf, hprev):
    tl = xl_ref.shape[0]

    @pl.when(pl.program_id(1) == 0)
    def _():
        xbuf[...] = jnp.zeros_like(xbuf)
        hprev[...] = jnp.zeros_like(hprev)

    groups = tl // SUBLANES
    sub = lax.broadcasted_iota(jnp.int32, (1, SUBLANES, 1), 1)
    x = xl_ref[...]
    x3 = x.reshape(groups, SUBLANES, LRU_WIDTH)
    prev = xbuf[...].reshape(1, SUBLANES, LRU_WIDTH)
    xbuf[...] = x[tl - SUBLANES:tl, :]
    conv = cb_ref[...] + cw_ref[LRU_CONV - 1:LRU_CONV, :] * x3
    for k in range(1, LRU_CONV):
        r = pltpu.roll(x3, k, 1)
        rp = jnp.concatenate([pltpu.roll(prev, k, 1), r[:groups - 1]], axis=0)
        conv = conv + cw_ref[LRU_CONV - 1 - k:LRU_CONV - k, :] * jnp.where(sub >= k, r, rp)
    conv = conv.reshape(tl, LRU_WIDTH)

    c16 = conv.astype(BF16)
    r = jax.nn.sigmoid(_dot(c16, wa_ref[...]) + ba_ref[...])
    ig = jax.nn.sigmoid(_dot(c16, wx_ref[...]) + bx_ref[...])
    log_a = (LRU_C * r) * _log_sigmoid(lam_ref[...])
    a = jnp.exp(log_a)
    b = jnp.sqrt(1.0 - a * a) * (ig * conv)

    a = a.reshape(groups, SUBLANES, LRU_WIDTH)
    b = b.reshape(groups, SUBLANES, LRU_WIDTH)
    d = 1
    while d < SUBLANES:
        keep = sub >= d
        a_sh = jnp.where(keep, pltpu.roll(a, d, 1), 1.0)
        b_sh = jnp.where(keep, pltpu.roll(b, d, 1), 0.0)
        b = a * b_sh + b
        a = a * a_sh
        d *= 2
    last = hprev[0:1, :]
    hs = []
    for t in range(groups):
        ht = b[t] + a[t] * last
        hs.append(ht)
        last = ht[SUBLANES - 1:SUBLANES, :]
    hprev[0:1, :] = last
    h = jnp.concatenate(hs, axis=0)
    o_ref[...] = (h * jax.nn.gelu(gl_ref[...])).astype(BF16)


def _lru(xg, cw, cb, wa, ba, wx, bx, lam, batch, seq):
    T = xg.shape[0]
    tl = TL_LRU
    n_seq = seq // tl
    full = lambda b, i: (0, 0)
    return pl.pallas_call(
        _lru_kernel,
        grid=(batch, n_seq),
        in_specs=[pl.BlockSpec((tl, LRU_WIDTH), lambda b, i: (b * n_seq + i, 0)),
                  pl.BlockSpec((tl, LRU_WIDTH), lambda b, i: (b * n_seq + i, 1)),
                  pl.BlockSpec((LRU_CONV, LRU_WIDTH), full),
                  pl.BlockSpec((1, LRU_WIDTH), full),
                  pl.BlockSpec((LRU_WIDTH, LRU_WIDTH), full),
                  pl.BlockSpec((1, LRU_WIDTH), full),
                  pl.BlockSpec((LRU_WIDTH, LRU_WIDTH), full),
                  pl.BlockSpec((1, LRU_WIDTH), full),
                  pl.BlockSpec((1, LRU_WIDTH), full)],
        out_specs=pl.BlockSpec((tl, LRU_WIDTH), lambda b, i: (b * n_seq + i, 0)),
        out_shape=jax.ShapeDtypeStruct((T, LRU_WIDTH), BF16),
        scratch_shapes=[pltpu.VMEM((SUBLANES, LRU_WIDTH), F32),
                        pltpu.VMEM((SUBLANES, LRU_WIDTH), F32)],
        compiler_params=_cparams("arbitrary", "arbitrary"),
        name="rg_lru",
    )(xg, xg, cw, cb, wa, ba, wx, bx, lam)


N_BACK = DILATED_PATTERNS[0][0] // DILATED_PATTERNS[0][1]
Q_BLOCKS = ATT_SPAN // N_BACK
STAGE_DIL = DILATED_PATTERNS[1][1]


def _attention_bias():
    qi = np.arange(N_BACK)[:, None]
    ki = np.arange(2 * N_BACK)[None, :]
    dist = N_BACK + qi - ki
    band = (dist >= 0) & (dist <= N_BACK)
    first = band & (ki >= N_BACK)
    return np.where(np.stack([band, first]), 0.0, NEG_INF).astype(np.float32)


def _attn_kernel(q_ref, k_ref, v_ref, bias_ref, o_ref, *scratch):
    kv_s = scratch[:6]
    o_s, m_s, l_s = scratch[6:9]
    stages = scratch[9:]
    sb = pl.program_id(2)
    lane = lax.broadcasted_iota(jnp.int32, (1, LANES), 1)
    head0 = lane < ATT_HEAD_DIM

    for p, (window, dil) in enumerate(DILATED_PATTERNS):
        per_res = Q_BLOCKS // dil
        span = N_BACK * per_res
        for src, dst, stage in ((k_ref, kv_s[2 * p], stages[0]), (v_ref, kv_s[2 * p + 1], stages[1])):
            @pl.when(sb == 0)
            def _():
                dst[:, 0:N_BACK, :] = jnp.zeros((dil, N_BACK, LANES), BF16)

            @pl.when(sb > 0)
            def _():
                dst[:, 0:N_BACK, :] = dst[:, span:span + N_BACK, :]

            if dil <= STAGE_DIL:
                for r in range(dil):
                    rows = pl.ds(r, span, stride=dil) if dil > 1 else pl.ds(0, span)
                    part = src[0, rows, :]
                    dst[r, N_BACK:N_BACK + span, :] = part.astype(BF16)
                    if dil == STAGE_DIL:
                        stage[r] = part
            else:
                sub = dil // STAGE_DIL
                for r in range(dil):
                    rows = pl.ds(r // STAGE_DIL, span, stride=sub)
                    dst[r, N_BACK:N_BACK + span, :] = stage[r % STAGE_DIL, rows, :].astype(BF16)

    q_stage = stages[2]
    out_stages = (stages[0], stages[1], stages[3])
    for r in range(STAGE_DIL):
        q_stage[r] = q_ref[0, pl.ds(r, ATT_SPAN // STAGE_DIL, stride=STAGE_DIL), :]

    for p, (window, dil) in enumerate(DILATED_PATTERNS):
        per_res = Q_BLOCKS // dil
        k_s, v_s = kv_s[2 * p], kv_s[2 * p + 1]

        for n in range(Q_BLOCKS):
            r, m = n % dil, n // dil
            if dil > 1:
                rows = pl.ds(m * (N_BACK * dil) + r, N_BACK, stride=dil)
            else:
                rows = pl.ds(m * N_BACK, N_BACK)
            if dil < STAGE_DIL:
                q = q_ref[0, rows, :]
            else:
                sub = dil // STAGE_DIL
                staged = (pl.ds(sub * N_BACK * m + r // STAGE_DIL, N_BACK, stride=sub) if sub > 1
                          else pl.ds(N_BACK * m, N_BACK))
                q = q_stage[r % STAGE_DIL, staged, :]
            q = q.astype(BF16)
            kc = k_s[r, m * N_BACK:(m + 2) * N_BACK, :]
            vc = v_s[r, m * N_BACK:(m + 2) * N_BACK, :]
            bias = bias_ref[jnp.where(sb == 0, 1, 0)] if m == 0 else bias_ref[0]
            res = []
            for h in range(LANES // ATT_HEAD_DIM):
                qm = jnp.where(head0 if h == 0 else ~head0, q, jnp.zeros_like(q))
                s = _dot_nt(qm, kc) + bias
                mx = jnp.max(s, axis=-1, keepdims=True)
                e = jnp.exp(s - mx)
                res.append((_dot(e.astype(BF16), vc), mx, jnp.sum(e, axis=-1, keepdims=True)))
            vals = [jnp.where(head0, res[0][idx], res[1][idx]) for idx in range(3)]
            if dil > STAGE_DIL:
                for st, val in zip(out_stages, vals):
                    st[r % STAGE_DIL, staged, :] = val
            else:
                for dst, val in zip((o_s, m_s, l_s), vals):
                    dst[p, rows, :] = val
        if dil > STAGE_DIL:
            for dst, st in zip((o_s, m_s, l_s), out_stages):
                for r in range(STAGE_DIL):
                    dst[p, pl.ds(r, ATT_SPAN // STAGE_DIL, stride=STAGE_DIL), :] = st[r]

    mx = jnp.maximum(jnp.maximum(m_s[0], m_s[1]), m_s[2])
    num = jnp.zeros_like(mx)
    den = jnp.zeros_like(mx)
    for p in range(len(DILATED_PATTERNS)):
        w = jnp.exp(m_s[p] - mx)
        num = num + w * o_s[p]
        den = den + w * l_s[p]
    o_ref[0] = (num / den).astype(BF16)


def _attention(q, k, v, bias):
    B, S, W = q.shape
    blk = pl.BlockSpec((1, ATT_SPAN, LANES), lambda b, p, i: (b, i, p))
    kv_scratch = []
    for window, dil in DILATED_PATTERNS:
        shape = (dil, N_BACK * (1 + Q_BLOCKS // dil), LANES)
        kv_scratch += [pltpu.VMEM(shape, BF16), pltpu.VMEM(shape, BF16)]
    acc = pltpu.VMEM((len(DILATED_PATTERNS), ATT_SPAN, LANES), F32)
    stage = pltpu.VMEM((STAGE_DIL, ATT_SPAN // STAGE_DIL, LANES), F32)
    return pl.pallas_call(
        _attn_kernel,
        grid=(B, W // LANES, S // ATT_SPAN),
        in_specs=[blk, blk, blk, pl.BlockSpec(bias.shape, lambda b, p, i: (0, 0, 0))],
        out_specs=blk,
        out_shape=jax.ShapeDtypeStruct((B, S, W), BF16),
        scratch_shapes=kv_scratch + [acc, acc, acc, stage, stage, stage, stage],
        compiler_params=_cparams("arbitrary", "arbitrary", "arbitrary", vmem=VMEM_LIMIT_ATTENTION),
        name="dilated_attention",
    )(q, k, v, bias)


def _ffn_kernel(x_ref, ya_ref, yb_ref, wo_ref, g_ref, w1_ref, cw_ref, cb_ref, w2_ref,
                o_ref, act_s, carry_s, *, tiles_per_seq):
    i = pl.program_id(0)
    tm = x_ref.shape[0]
    half = ya_ref.shape[1]
    tf = TF_FFN

    @pl.when(i % tiles_per_seq == 0)
    def _():
        carry_s[...] = jnp.zeros_like(carry_s)

    x1 = (x_ref[...] + _dot(ya_ref[...], wo_ref[0:half, :])
          + _dot(yb_ref[...], wo_ref[half:2 * half, :]))
    h = _rms(x1, g_ref[...]).astype(BF16)
    groups = tm // SUBLANES
    sub = lax.broadcasted_iota(jnp.int32, (1, SUBLANES, 1), 1)
    for c in range(D_FF // tf):
        cols = slice(c * tf, (c + 1) * tf)
        a = _dot(h, w1_ref[:, cols])
        lin = _dot(h, w1_ref[:, D_FF + c * tf:D_FF + (c + 1) * tf])
        a3 = a.reshape(groups, SUBLANES, tf)
        prev = carry_s[:, cols].reshape(1, SUBLANES, tf)
        carry_s[:, cols] = a[tm - SUBLANES:tm, :]

        def delayed(k):
            r = pltpu.roll(a3, k, 1)
            rp = jnp.concatenate([pltpu.roll(prev, k, 1), r[:groups - 1]], axis=0)
            return jnp.where(sub >= k, r, rp)

        conv = (cb_ref[:, cols] + cw_ref[2:3, cols] * a3
                + cw_ref[1:2, cols] * delayed(1) + cw_ref[0:1, cols] * delayed(2))
        act = jax.nn.gelu(conv).reshape(tm, tf) * lin
        act_s[:, cols] = act.astype(BF16)
    o_ref[...] = x1 + _dot(act_s[...], w2_ref[...])


def _ffn(x, ya, yb, wo, g, w_in, cw, cb, w2, seq):
    T = x.shape[0]
    tm = TM_FFN
    half = ya.shape[1]
    row = lambda i: (i, 0)
    resident = lambda shape: pl.BlockSpec(shape, lambda i: (0, 0), pipeline_mode=pl.Buffered(1))
    return pl.pallas_call(
        functools.partial(_ffn_kernel, tiles_per_seq=seq // tm),
        grid=(T // tm,),
        in_specs=[pl.BlockSpec((tm, D_MODEL), row),
                  pl.BlockSpec((tm, half), row),
                  pl.BlockSpec((tm, half), row),
                  resident((2 * half, D_MODEL)),
                  resident((1, D_MODEL)),
                  resident((D_MODEL, 2 * D_FF)),
                  resident((3, D_FF)),
                  resident((1, D_FF)),
                  resident((D_FF, D_MODEL))],
        out_specs=pl.BlockSpec((tm, D_MODEL), row),
        out_shape=jax.ShapeDtypeStruct((T, D_MODEL), F32),
        scratch_shapes=[pltpu.VMEM((tm, D_FF), BF16),
                        pltpu.VMEM((SUBLANES, D_FF), F32)],
        compiler_params=_cparams("arbitrary"),
        name="outproj_conv_mlp",
    )(x, ya, yb, wo, g, w_in, cw, cb, w2)


def _odd_in_kernel(x_ref, g_ref, w_ref, o_ref):
    h = _rms(x_ref[...], g_ref[...])
    o_ref[...] = _dot(h.astype(BF16), w_ref[...])


def _odd_in(x, g, w):
    T = x.shape[0]
    tm = TM_PROJ
    ncol = w.shape[1]
    return pl.pallas_call(
        _odd_in_kernel,
        grid=(T // tm,),
        in_specs=[pl.BlockSpec((tm, D_MODEL), lambda i: (i, 0)),
                  pl.BlockSpec((1, D_MODEL), lambda i: (0, 0)),
                  pl.BlockSpec((D_MODEL, ncol), lambda i: (0, 0))],
        out_specs=pl.BlockSpec((tm, ncol), lambda i: (i, 0)),
        out_shape=jax.ShapeDtypeStruct((T, ncol), F32),
        compiler_params=_cparams("arbitrary"),
        name="odd_in_proj",
    )(x, g, w)


def _cswap(v):
    return jnp.concatenate([v[:, LANES:], v[:, :LANES]], axis=1)


def _s5_kernel(u_ref, bm_ref, p1_ref, p2_ref, q1_ref, q2_ref, pc_ref, tri_ref, cm_ref,
               d_ref, gw_ref, gb_ref, o_ref, carry, x_s, ys_s):
    tl = u_ref.shape[0]
    L = L_S5
    blk = 2 * LANES
    slab = 2 * S5_NSTATE // (S5_WIDTH // LANES)

    @pl.when(pl.program_id(1) == 0)
    def _():
        carry[...] = jnp.zeros_like(carry)

    tri = tri_ref[...]
    for s in range(S5_WIDTH // LANES):
        u = u_ref[:, s * LANES:(s + 1) * LANES]
        u16 = u.astype(BF16)
        for jb in range(slab // blk):
            cols = slice(s * slab + jb * blk, s * slab + (jb + 1) * blk)
            bu = _dot(u16, bm_ref[s, :, jb * blk:(jb + 1) * blk])
            for c in range(tl // L):
                rows = slice(c * L, (c + 1) * L)
                v = bu[rows, :].astype(BF16)
                z = v * q1_ref[:, cols] + _cswap(v) * q2_ref[:, cols]
                w = _dot(tri, z) + carry[0:1, cols]
                w16 = w.astype(BF16)
                x_s[rows, jb * blk:(jb + 1) * blk] = (
                    w16 * p1_ref[:, cols] + _cswap(w16) * p2_ref[:, cols])
                wl = w[L - 1:L, :]
                xl = wl * pc_ref[2:3, cols] + _cswap(wl) * pc_ref[3:4, cols]
                carry[0:1, cols] = xl * pc_ref[0:1, cols] + _cswap(xl) * pc_ref[1:2, cols]
        y = _dot(x_s[...], cm_ref[s]) + d_ref[:, s * LANES:(s + 1) * LANES] * u
        ys_s[:, s * LANES:(s + 1) * LANES] = jax.nn.gelu(y)
    ys = ys_s[...]
    o_ref[...] = (ys * jax.nn.sigmoid(_dot(ys.astype(BF16), gw_ref[...]) + gb_ref[...])).astype(BF16)


def _s5(proj, bm, p1, p2, q1, q2, pc, tri, cm, d, gw, gb, batch, seq):
    T = proj.shape[0]
    tl = TL_S5
    n_seq = seq // tl
    c2 = lambda b, i: (0, 0)
    c3 = lambda b, i: (0, 0, 0)
    return pl.pallas_call(
        _s5_kernel,
        grid=(batch, n_seq),
        in_specs=[pl.BlockSpec((tl, S5_WIDTH), lambda b, i: (b * n_seq + i, 0)),
                  pl.BlockSpec(bm.shape, c3),
                  pl.BlockSpec(p1.shape, c2), pl.BlockSpec(p2.shape, c2),
                  pl.BlockSpec(q1.shape, c2), pl.BlockSpec(q2.shape, c2),
                  pl.BlockSpec(pc.shape, c2),
                  pl.BlockSpec(tri.shape, c2),
                  pl.BlockSpec(cm.shape, c3),
                  pl.BlockSpec((1, S5_WIDTH), c2),
                  pl.BlockSpec((S5_WIDTH, S5_WIDTH), c2),
                  pl.BlockSpec((1, S5_WIDTH), c2)],
        out_specs=pl.BlockSpec((tl, S5_WIDTH), lambda b, i: (b * n_seq + i, 0)),
        out_shape=jax.ShapeDtypeStruct((T, S5_WIDTH), BF16),
        scratch_shapes=[pltpu.VMEM((SUBLANES, 2 * S5_NSTATE), F32),
                        pltpu.VMEM((tl, 2 * S5_NSTATE // (S5_WIDTH // LANES)), BF16),
                        pltpu.VMEM((tl, S5_WIDTH), F32)],
        compiler_params=_cparams("arbitrary", "arbitrary"),
        name="s5_glu",
    )(proj, bm, p1, p2, q1, q2, pc, tri, cm, d, gw, gb)


def _s5_params(lam_re, lam_im, b_re, b_im, c_re, c_im, log_step):
    G, N, P = S5_GROUPS, S5_STATE, S5_GROUP
    gs = LANES // P
    ns = S5_WIDTH // LANES
    step = jnp.exp(log_step.astype(F32))[:, None]
    lr, li = lam_re.astype(F32), lam_im.astype(F32)
    mag = jnp.exp(lr * step)
    ar, ai = mag * jnp.cos(li * step), mag * jnp.sin(li * step)
    den = lr * lr + li * li
    cr = ((ar - 1.0) * lr + ai * li) / den
    ci = (ai * lr - (ar - 1.0) * li) / den
    bbr = cr[..., None] * b_re - ci[..., None] * b_im
    bbi = cr[..., None] * b_im + ci[..., None] * b_re
    eye = jnp.eye(gs, dtype=F32)

    def in_blockdiag(t):
        t = t.reshape(ns, gs, N, P).transpose(0, 1, 3, 2)
        return jnp.einsum('ab,sapn->sapbn', eye, t).reshape(ns, gs * P, gs * N)

    def out_blockdiag(t):
        t = t.reshape(ns, gs, P, N).transpose(0, 1, 3, 2)
        return jnp.einsum('ab,sanp->sanbp', eye, t).reshape(ns, gs * N, gs * P)

    def interleave(re, im, axis):
        shp = list(re.shape)
        blocked = shp[:axis] + [shp[axis] // LANES, LANES] + shp[axis + 1:]
        both = jnp.stack([re.reshape(blocked), im.reshape(blocked)], axis=axis + 1)
        return both.reshape(shp[:axis] + [2 * shp[axis]] + shp[axis + 1:])

    bm = interleave(in_blockdiag(bbr), in_blockdiag(bbi), 2).astype(BF16)
    cm = interleave(out_blockdiag(c_re), out_blockdiag(-c_im), 1).astype(BF16)
    def twice(v):
        v = v.reshape(G * N // LANES, 1, LANES)
        return jnp.broadcast_to(v, (G * N // LANES, 2, LANES)).reshape(1, 2 * G * N)

    sign = jnp.asarray(np.tile(np.repeat([-1.0, 1.0], LANES), G * N // LANES)[None, :], F32)
    j = jnp.arange(L_S5, dtype=F32)[:, None]
    la, th = twice(lr * step), twice(li * step)
    pmag, qmag = jnp.exp(j * la), jnp.exp(-(j * la))
    cs, sn = jnp.cos(j * th), jnp.sin(j * th)
    p1, p2 = pmag * cs, sign * (pmag * sn)
    q1, q2 = qmag * cs, -(sign * (qmag * sn))
    pc = jnp.concatenate([p1[1:2], p2[1:2], p1[L_S5 - 1:], p2[L_S5 - 1:]], axis=0)
    return bm, p1.astype(BF16), p2.astype(BF16), q1.astype(BF16), q2.astype(BF16), pc, cm


def _gla_cumsum_matrix(tl):
    r = np.arange(tl)[:, None]
    c = np.arange(tl)[None, :]
    same = (r // C_GLA) == (c // C_GLA)
    return np.concatenate([same & (c <= r), same & (c > r)], axis=0).astype(np.float32)


def _gla_kernel(q_ref, k_ref, v_ref, g_ref, gk_ref, gw_ref, gb_ref, nrm_ref, cum_ref, o_ref, st):
    nbatch, tl = q_ref.shape[0], q_ref.shape[1]
    C = C_GLA
    blk = 2 * C
    pair = LANES // GLA_DK

    @pl.when(pl.program_id(0) == 0)
    def _():
        st[...] = jnp.zeros_like(st)

    lane = lax.broadcasted_iota(jnp.int32, (1, LANES), 1)
    rb = lax.broadcasted_iota(jnp.int32, (blk, blk), 0)
    cb = lax.broadcasted_iota(jnp.int32, (blk, blk), 1)
    causal = (rb >= cb) & ((rb < C) | (cb >= C))
    streams = []
    for n in range(nbatch):
        z = _dot(gk_ref[n].astype(BF16), gw_ref[...]) + gb_ref[...]
        log_a = _log_sigmoid(z) * (1.0 / GLA_TAU)
        hi, lo = _split_bf16(log_a)
        sums = _dot(cum_ref[...], hi) + _dot(cum_ref[...], lo)
        bc, suffix = sums[:tl], sums[tl:]
        eb = jnp.exp(bc)
        q_dec = q_ref[n] * (GLA_DK ** -0.5) * eb
        k = k_ref[n]
        k_inv = (k * jnp.exp(-bc)).astype(BF16)
        k_dec = k * jnp.exp(suffix)
        for h in range(GLA_HEADS):
            hp, hh = divmod(h, pair)
            cols = slice(hp * LANES, (hp + 1) * LANES)
            in_head = (lane >= hh * GLA_DK) & (lane < (hh + 1) * GLA_DK)
            streams.append(dict(
                n=n, h=h, eb=eb[:, cols],
                qd=jnp.where(in_head, q_dec[:, cols], 0.0).astype(BF16),
                kd=jnp.where(in_head, k_dec[:, cols], 0.0).astype(BF16),
                ki=k_inv[:, cols],
                vh=v_ref[n, :, h * GLA_DV:(h + 1) * GLA_DV].astype(BF16),
                state=st[n * GLA_HEADS + h]))
    for b in range(tl // blk):
        rows = slice(b * blk, (b + 1) * blk)
        for sd in streams:
            n, h = sd["n"], sd["h"]
            att = jnp.where(causal, _dot_nt(sd["qd"][rows], sd["ki"][rows]), 0.0)
            o = _dot(att.astype(BF16), sd["vh"][rows])
            inter = []
            for c in range(b * blk // C, (b + 1) * blk // C):
                crow = slice(c * C, (c + 1) * C)
                inter.append(_dot_nt(sd["qd"][crow], sd["state"].astype(BF16)))
                decay = sd["eb"][(c + 1) * C - 1:(c + 1) * C, :]
                sd["state"] = sd["state"] * decay + _dot_tn(sd["vh"][crow], sd["kd"][crow])
            o = _rms(o + jnp.concatenate(inter, axis=0), nrm_ref[...])
            gh = g_ref[n, rows, h * GLA_DV:(h + 1) * GLA_DV]
            o_ref[n, rows, h * GLA_DV:(h + 1) * GLA_DV] = (o * jax.nn.silu(gh)).astype(BF16)
    for sd in streams:
        st[sd["n"] * GLA_HEADS + sd["h"]] = sd["state"]


def _gla(proj, gw, gb, nrm, tri, batch, seq):
    tl = TL_GLA
    hk = GLA_HEADS * GLA_DK
    hv = GLA_HEADS * GLA_DV
    c2 = lambda i: (0, 0)
    q0 = S5_WIDTH // hk
    v0 = (S5_WIDTH + 2 * hk) // hv
    gk0 = (S5_WIDTH + 2 * hk + 2 * hv) // GK_PAD
    proj = proj.reshape(batch, seq, proj.shape[-1])
    out = pl.pallas_call(
        _gla_kernel,
        grid=(seq // tl,),
        in_specs=[pl.BlockSpec((batch, tl, hk), lambda i: (0, i, q0)),
                  pl.BlockSpec((batch, tl, hk), lambda i: (0, i, q0 + 1)),
                  pl.BlockSpec((batch, tl, hv), lambda i: (0, i, v0)),
                  pl.BlockSpec((batch, tl, hv), lambda i: (0, i, v0 + 1)),
                  pl.BlockSpec((batch, tl, GK_PAD), lambda i: (0, i, gk0)),
                  pl.BlockSpec((GK_PAD, hk), c2),
                  pl.BlockSpec((1, hk), c2),
                  pl.BlockSpec((1, GLA_DV), c2),
                  pl.BlockSpec((2 * tl, tl), c2)],
        out_specs=pl.BlockSpec((batch, tl, hv), lambda i: (0, i, 0)),
        out_shape=jax.ShapeDtypeStruct((batch, seq, hv), BF16),
        scratch_shapes=[pltpu.VMEM((batch * GLA_HEADS, GLA_DV, LANES), F32)],
        compiler_params=_cparams("arbitrary"),
        name="gla",
    )(proj, proj, proj, proj, proj, gw, gb, nrm, tri)
    return out.reshape(batch * seq, hv)


def _block_diag(w):
    nb, a, b = w.shape
    return jnp.einsum('hk,hij->hikj', jnp.eye(nb, dtype=w.dtype), w).reshape(nb * a, nb * b)


def _rope_tables(seq):
    half = ROPE_DIM // 2
    pos = np.arange(seq, dtype=np.float64)
    inv = ROPE_THETA ** (-np.arange(0, ROPE_DIM, 2, dtype=np.float64) / ROPE_DIM)
    ang = pos[:, None] * inv[None, :]
    cos, sin = np.cos(ang), np.sin(ang)
    rest = ATT_HEAD_DIM - ROPE_DIM
    ones = np.ones((seq, rest))
    zeros = np.zeros((seq, rest))
    zh = np.zeros((seq, half))
    per_head = lambda parts: jnp.asarray(
        np.tile(np.concatenate(parts, axis=1), (1, LANES // ATT_HEAD_DIM)), F32)
    return (per_head([cos, cos, ones]), per_head([-sin, zh, zeros]), per_head([zh, sin, zeros]))


def _cast_kernel(w_ref, o_ref):
    o_ref[...] = w_ref[0].astype(BF16)


def _to_bf16(w, idx):
    _, rows, cols = w.shape
    tr = rows // 4
    return pl.pallas_call(
        _cast_kernel,
        grid=(rows // tr,),
        in_specs=[pl.BlockSpec((1, tr, cols), lambda r: (idx, r, 0))],
        out_specs=pl.BlockSpec((tr, cols), lambda r: (r, 0)),
        out_shape=jax.ShapeDtypeStruct((rows, cols), BF16),
        compiler_params=_cparams("arbitrary"),
        name="weight_to_bf16",
    )(w)


def kernel(x, e_norm, e_w_in, e_conv_w, e_conv_b, e_gate_a_w, e_gate_a_b, e_gate_x_w, e_gate_x_b, e_lambda, e_q_norm, e_k_norm, e_w_out, o_norm, o_w_in, o_lambda_re, o_lambda_im, o_b_re, o_b_im, o_c_re, o_c_im, o_d, o_log_step, o_glu_w, o_glu_b, o_gk_w, o_gk_b, o_gla_norm, o_w_out, f_norm, f_w_in, f_conv_w, f_conv_b, f_w_out):
    B, S, D = x.shape
    T = B * S
    depth = f_norm.shape[0]
    row = lambda t: t.reshape(1, -1).astype(F32)
    xt = x.reshape(T, D)

    cos_t, s1_t, s2_t = _rope_tables(S)
    head_seg = jnp.asarray(np.kron(np.eye(LANES // ATT_HEAD_DIM), np.ones((ATT_HEAD_DIM, ATT_HEAD_DIM))), BF16)
    att_bias = jnp.asarray(_attention_bias())
    tri_s5 = jnp.asarray(np.tril(np.ones((L_S5, L_S5))), BF16)
    tri_gla = jnp.asarray(_gla_cumsum_matrix(TL_GLA), BF16)
    two_heads = lambda t: jnp.tile(row(t), (1, LANES // ATT_HEAD_DIM))

    for layer in range(depth):
        i = layer // 2
        if layer % 2 == 0:
            xg, q, k, v = _even_in(xt, row(e_norm[i]), _to_bf16(e_w_in, i),
                                   two_heads(e_q_norm[i]), two_heads(e_k_norm[i]),
                                   head_seg, cos_t, s1_t, s2_t, S)
            ya = _lru(xg, e_conv_w[i], row(e_conv_b[i]),
                      _block_diag(e_gate_a_w[i]).astype(BF16), row(e_gate_a_b[i]),
                      _block_diag(e_gate_x_w[i]).astype(BF16), row(e_gate_x_b[i]),
                      row(e_lambda[i]), B, S)
            yb = _attention(q.reshape(B, S, ATT_WIDTH), k.reshape(B, S, ATT_WIDTH),
                            v.reshape(B, S, ATT_WIDTH), att_bias).reshape(T, ATT_WIDTH)
            w_out = _to_bf16(e_w_out, i)
        else:
            w_in = jnp.pad(o_w_in[i], ((0, 0), (0, GK_PAD - GLA_LOWRANK))).astype(BF16)
            proj = _odd_in(xt, row(o_norm[i]), w_in)
            bm, p1, p2, q1, q2, pc, cm = _s5_params(o_lambda_re[i], o_lambda_im[i], o_b_re[i], o_b_im[i],
                                                    o_c_re[i], o_c_im[i], o_log_step[i])
            ya = _s5(proj, bm, p1, p2, q1, q2, pc, tri_s5, cm, row(o_d[i]),
                     _to_bf16(o_glu_w, i), row(o_glu_b[i]), B, S)
            gk_w = jnp.pad(o_gk_w[i], ((0, GK_PAD - GLA_LOWRANK), (0, 0))).astype(BF16)
            yb = _gla(proj, gk_w, row(o_gk_b[i]), row(o_gla_norm[i]), tri_gla, B, S)
            w_out = _to_bf16(o_w_out, i)
        xt = _ffn(xt, ya, yb, w_out, row(f_norm[layer]), _to_bf16(f_w_in, layer),
                  f_conv_w[layer], row(f_conv_b[layer]), _to_bf16(f_w_out, layer), S)
    return xt.reshape(B, S, D)
```

```python
import functools

import numpy as np
import jax
import jax.numpy as jnp
from jax import lax
from jax.experimental import pallas as pl
from jax.experimental.pallas import tpu as pltpu

F32 = jnp.float32
BF16 = jnp.bfloat16

D_MODEL = 1024
LRU_WIDTH = 512
LRU_BLOCKS = 8
LRU_CONV = 4
LRU_C = 8.0
ATT_HEADS = 8
ATT_HEAD_DIM = 64
ATT_WIDTH = 512
DILATED_PATTERNS = ((128, 1), (512, 4), (2048, 16))
ATT_SPAN = 2048
ROPE_THETA = 500000.0
ROPE_DIM = 16
S5_WIDTH = 512
S5_GROUP = 16
S5_GROUPS = 32
S5_STATE = 64
S5_NSTATE = S5_GROUPS * S5_STATE
GLA_HEADS = 4
GLA_DK = 64
GLA_DV = 128
GLA_LOWRANK = 16
GLA_TAU = 16.0
D_FF = 3 * D_MODEL
EPS = 1e-6
NEG_INF = -1e30

LANES = 128
SUBLANES = 8
VMEM_PHYSICAL = 64 * 1024 * 1024
VMEM_LIMIT = VMEM_PHYSICAL - 4 * 1024 * 1024
VMEM_LIMIT_ATTENTION = VMEM_PHYSICAL - 8 * 1024 * 1024

TM_PROJ = 1024
TM_FFN = 1024
TF_FFN = 512
TL_LRU = 512
TL_S5 = 512
L_S5 = 128
TL_GLA = 512
C_GLA = 64
GK_PAD = 128


def _cparams(*sem, vmem=VMEM_LIMIT):
    return pltpu.CompilerParams(dimension_semantics=sem, vmem_limit_bytes=vmem)


def _rms(x, g):
    return x * lax.rsqrt(jnp.mean(x * x, axis=-1, keepdims=True) + EPS) * g


def _log_sigmoid(x):
    return -(jnp.maximum(-x, 0.0) + jnp.log(1.0 + jnp.exp(-jnp.abs(x))))


def _split_bf16(x):
    hi = x.astype(BF16)
    lo = (x - hi.astype(F32)).astype(BF16)
    return hi, lo


def _dot(a, b):
    return jnp.dot(a, b, preferred_element_type=F32)


def _dot_nt(a, b):
    return lax.dot_general(a, b, (((1,), (1,)), ((), ())), preferred_element_type=F32)


def _dot_tn(a, b):
    return lax.dot_general(a, b, (((0,), (0,)), ((), ())), preferred_element_type=F32)


def _even_in_kernel(x_ref, g_ref, w_ref, qn_ref, kn_ref, seg_ref, cos_ref, s1_ref, s2_ref,
                    xg_ref, q_ref, k_ref, v_ref):
    h = _rms(x_ref[...], g_ref[...]).astype(BF16)
    seg = seg_ref[...]
    cos, s1, s2 = cos_ref[...], s1_ref[...], s2_ref[...]
    half = ROPE_DIM // 2
    for off, n_ref, dst, scale in ((2 * LRU_WIDTH, qn_ref, q_ref, ATT_HEAD_DIM ** -0.5),
                                   (2 * LRU_WIDTH + ATT_WIDTH, kn_ref, k_ref, 1.0)):
        y = _dot(h, w_ref[:, off:off + ATT_WIDTH])
        for c in range(ATT_WIDTH // LANES):
            t = y[:, c * LANES:(c + 1) * LANES]
            ms = _dot((t * t).astype(BF16), seg) * (1.0 / ATT_HEAD_DIM)
            tn = t * lax.rsqrt(ms + EPS) * n_ref[...]
            r = (tn * cos + pltpu.roll(tn, LANES - half, 1) * s1 + pltpu.roll(tn, half, 1) * s2)
            dst[:, c * LANES:(c + 1) * LANES] = r * scale
    xg_ref[...] = _dot(h, w_ref[:, :2 * LRU_WIDTH])
    v_ref[...] = _dot(h, w_ref[:, 2 * LRU_WIDTH + 2 * ATT_WIDTH:])


def _even_in(x, g, w, qn, kn, seg, cos_t, s1_t, s2_t, seq):
    T = x.shape[0]
    tm = TM_PROJ
    n_seq = seq // tm
    ncol = w.shape[1]
    full = lambda i: (0, 0)
    tab = lambda i: (i % n_seq, 0)
    row = lambda i: (i, 0)
    return pl.pallas_call(
        _even_in_kernel,
        grid=(T // tm,),
        in_specs=[pl.BlockSpec((tm, D_MODEL), row),
                  pl.BlockSpec((1, D_MODEL), full),
                  pl.BlockSpec((D_MODEL, ncol), full),
                  pl.BlockSpec((1, LANES), full),
                  pl.BlockSpec((1, LANES), full),
                  pl.BlockSpec((LANES, LANES), full),
                  pl.BlockSpec((tm, LANES), tab),
                  pl.BlockSpec((tm, LANES), tab),
                  pl.BlockSpec((tm, LANES), tab)],
        out_specs=[pl.BlockSpec((tm, 2 * LRU_WIDTH), row),
                   pl.BlockSpec((tm, ATT_WIDTH), row),
                   pl.BlockSpec((tm, ATT_WIDTH), row),
                   pl.BlockSpec((tm, ATT_WIDTH), row)],
        out_shape=[jax.ShapeDtypeStruct((T, 2 * LRU_WIDTH), F32),
                   jax.ShapeDtypeStruct((T, ATT_WIDTH), F32),
                   jax.ShapeDtypeStruct((T, ATT_WIDTH), F32),
                   jax.ShapeDtypeStruct((T, ATT_WIDTH), F32)],
        compiler_params=_cparams("arbitrary"),
        name="even_in_proj",
    )(x, g, w, qn, kn, seg, cos_t, s1_t, s2_t)


def _lru_kernel(xl_ref, gl_ref, cw_ref, cb_ref, wa_ref, ba_ref, wx_ref, bx_ref, lam_ref,
                o_ref, xbuf, hprev):
    tl = xl_ref.shape[0]

    @pl.when(pl.program_id(1) == 0)
    def _():
        xbuf[...] = jnp.zeros_like(xbuf)
        hprev[...] = jnp.zeros_like(hprev)

    groups = tl // SUBLANES
    sub = lax.broadcasted_iota(jnp.int32, (1, SUBLANES, 1), 1)
    x = xl_ref[...]
    x3 = x.reshape(groups, SUBLANES, LRU_WIDTH)
    prev = xbuf[...].reshape(1, SUBLANES, LRU_WIDTH)
    xbuf[...] = x[tl - SUBLANES:tl, :]
    conv = cb_ref[...] + cw_ref[LRU_CONV - 1:LRU_CONV, :] * x3
    for k in range(1, LRU_CONV):
        r = pltpu.roll(x3, k, 1)
        rp = jnp.concatenate([pltpu.roll(prev, k, 1), r[:groups - 1]], axis=0)
        conv = conv + cw_ref[LRU_CONV - 1 - k:LRU_CONV - k, :] * jnp.where(sub >= k, r, rp)
    conv = conv.reshape(tl, LRU_WIDTH)

    c16 = conv.astype(BF16)
    r = jax.nn.sigmoid(_dot(c16, wa_ref[...]) + ba_ref[...])
    ig = jax.nn.sigmoid(_dot(c16, wx_ref[...]) + bx_ref[...])
    log_a = (LRU_C * r) * _log_sigmoid(lam_ref[...])
    a = jnp.exp(log_a)
    b = jnp.sqrt(1.0 - a * a) * (ig * conv)

    a = a.reshape(groups, SUBLANES, LRU_WIDTH)
    b = b.reshape(groups, SUBLANES, LRU_WIDTH)
    d = 1
    while d < SUBLANES:
        keep = sub >= d
        a_sh = jnp.where(keep, pltpu.roll(a, d, 1), 1.0)
        b_sh = jnp.where(keep, pltpu.roll(b, d, 1), 0.0)
        b = a * b_sh + b
        a = a * a_sh
        d *= 2
    last = hprev[0:1, :]
    hs = []
    for t in range(groups):
        ht = b[t] + a[t] * last
        hs.append(ht)
        last = ht[SUBLANES - 1:SUBLANES, :]
    hprev[0:1, :] = last
    h = jnp.concatenate(hs, axis=0)
    o_ref[...] = (h * jax.nn.gelu(gl_ref[...])).astype(BF16)


def _lru(xg, cw, cb, wa, ba, wx, bx, lam, batch, seq):
    T = xg.shape[0]
    tl = TL_LRU
    n_seq = seq // tl
    full = lambda b, i: (0, 0)
    return pl.pallas_call(
        _lru_kernel,
        grid=(batch, n_seq),
        in_specs=[pl.BlockSpec((tl, LRU_WIDTH), lambda b, i: (b * n_seq + i, 0)),
                  pl.BlockSpec((tl, LRU_WIDTH), lambda b, i: (b * n_seq + i, 1)),
                  pl.BlockSpec((LRU_CONV, LRU_WIDTH), full),
                  pl.BlockSpec((1, LRU_WIDTH), full),
                  pl.BlockSpec((LRU_WIDTH, LRU_WIDTH), full),
                  pl.BlockSpec((1, LRU_WIDTH), full),
                  pl.BlockSpec((LRU_WIDTH, LRU_WIDTH), full),
                  pl.BlockSpec((1, LRU_WIDTH), full),
                  pl.BlockSpec((1, LRU_WIDTH), full)],
        out_specs=pl.BlockSpec((tl, LRU_WIDTH), lambda b, i: (b * n_seq + i, 0)),
        out_shape=jax.ShapeDtypeStruct((T, LRU_WIDTH), BF16),
        scratch_shapes=[pltpu.VMEM((SUBLANES, LRU_WIDTH), F32),
                        pltpu.VMEM((SUBLANES, LRU_WIDTH), F32)],
        compiler_params=_cparams("arbitrary", "arbitrary"),
        name="rg_lru",
    )(xg, xg, cw, cb, wa, ba, wx, bx, lam)


N_BACK = DILATED_PATTERNS[0][0] // DILATED_PATTERNS[0][1]
Q_BLOCKS = ATT_SPAN // N_BACK
STAGE_DIL = DILATED_PATTERNS[1][1]


def _attention_bias():
    qi = np.arange(N_BACK)[:, None]
    ki = np.arange(2 * N_BACK)[None, :]
    dist = N_BACK + qi - ki
    band = (dist >= 0) & (dist <= N_BACK)
    first = band & (ki >= N_BACK)
    return np.where(np.stack([band, first]), 0.0, NEG_INF).astype(np.float32)


def _attn_kernel(q_ref, k_ref, v_ref, bias_ref, o_ref, *scratch):
    kv_s = scratch[:6]
    o_s, m_s, l_s = scratch[6:9]
    stages = scratch[9:]
    sb = pl.program_id(2)
    lane = lax.broadcasted_iota(jnp.int32, (1, LANES), 1)
    head0 = lane < ATT_HEAD_DIM

    for p, (window, dil) in enumerate(DILATED_PATTERNS):
        per_res = Q_BLOCKS // dil
        span = N_BACK * per_res
        for src, dst, stage in ((k_ref, kv_s[2 * p], stages[0]), (v_ref, kv_s[2 * p + 1], stages[1])):
            @pl.when(sb == 0)
            def _():
                dst[:, 0:N_BACK, :] = jnp.zeros((dil, N_BACK, LANES), BF16)

            @pl.when(sb > 0)
            def _():
                dst[:, 0:N_BACK, :] = dst[:, span:span + N_BACK, :]

            if dil <= STAGE_DIL:
                for r in range(dil):
                    rows = pl.ds(r, span, stride=dil) if dil > 1 else pl.ds(0, span)
                    part = src[0, rows, :]
                    dst[r, N_BACK:N_BACK + span, :] = part.astype(BF16)
                    if dil == STAGE_DIL:
                        stage[r] = part
            else:
                sub = dil // STAGE_DIL
                for r in range(dil):
                    rows = pl.ds(r // STAGE_DIL, span, stride=sub)
                    dst[r, N_BACK:N_BACK + span, :] = stage[r % STAGE_DIL, rows, :].astype(BF16)

    q_stage = stages[2]
    out_stages = (stages[0], stages[1], stages[3])
    for r in range(STAGE_DIL):
        q_stage[r] = q_ref[0, pl.ds(r, ATT_SPAN // STAGE_DIL, stride=STAGE_DIL), :]

    for p, (window, dil) in enumerate(DILATED_PATTERNS):
        per_res = Q_BLOCKS // dil
        k_s, v_s = kv_s[2 * p], kv_s[2 * p + 1]

        for n in range(Q_BLOCKS):
            r, m = n % dil, n // dil
            if dil > 1:
                rows = pl.ds(m * (N_BACK * dil) + r, N_BACK, stride=dil)
            else:
                rows = pl.ds(m * N_BACK, N_BACK)
            if dil < STAGE_DIL:
                q = q_ref[0, rows, :]
            else:
                sub = dil // STAGE_DIL
                staged = (pl.ds(sub * N_BACK * m + r // STAGE_DIL, N_BACK, stride=sub) if sub > 1
                          else pl.ds(N_BACK * m, N_BACK))
                q = q_stage[r % STAGE_DIL, staged, :]
            q = q.astype(BF16)
            kc = k_s[r, m * N_BACK:(m + 2) * N_BACK, :]
            vc = v_s[r, m * N_BACK:(m + 2) * N_BACK, :]
            bias = bias_ref[jnp.where(sb == 0, 1, 0)] if m == 0 else bias_ref[0]
            res = []
            for h in range(LANES // ATT_HEAD_DIM):
                qm = jnp.where(head0 if h == 0 else ~head0, q, jnp.zeros_like(q))
                s = _dot_nt(qm, kc) + bias
                mx = jnp.max(s, axis=-1, keepdims=True)
                e = jnp.exp(s - mx)
                res.append((_dot(e.astype(BF16), vc), mx, jnp.sum(e, axis=-1, keepdims=True)))
            vals = [jnp.where(head0, res[0][idx], res[1][idx]) for idx in range(3)]
            if dil > STAGE_DIL:
                for st, val in zip(out_stages, vals):
                    st[r % STAGE_DIL, staged, :] = val
            else:
                for dst, val in zip((o_s, m_s, l_s), vals):
                    dst[p, rows, :] = val
        if dil > STAGE_DIL:
            for dst, st in zip((o_s, m_s, l_s), out_stages):
                for r in range(STAGE_DIL):
                    dst[p, pl.ds(r, ATT_SPAN // STAGE_DIL, stride=STAGE_DIL), :] = st[r]

    mx = jnp.maximum(jnp.maximum(m_s[0], m_s[1]), m_s[2])
    num = jnp.zeros_like(mx)
    den = jnp.zeros_like(mx)
    for p in range(len(DILATED_PATTERNS)):
        w = jnp.exp(m_s[p] - mx)
        num = num + w * o_s[p]
        den = den + w * l_s[p]
    o_ref[0] = (num / den).astype(BF16)


def _attention(q, k, v, bias):
    B, S, W = q.shape
    blk = pl.BlockSpec((1, ATT_SPAN, LANES), lambda b, p, i: (b, i, p))
    kv_scratch = []
    for window, dil in DILATED_PATTERNS:
        shape = (dil, N_BACK * (1 + Q_BLOCKS // dil), LANES)
        kv_scratch += [pltpu.VMEM(shape, BF16), pltpu.VMEM(shape, BF16)]
    acc = pltpu.VMEM((len(DILATED_PATTERNS), ATT_SPAN, LANES), F32)
    stage = pltpu.VMEM((STAGE_DIL, ATT_SPAN // STAGE_DIL, LANES), F32)
    return pl.pallas_call(
        _attn_kernel,
        grid=(B, W // LANES, S // ATT_SPAN),
        in_specs=[blk, blk, blk, pl.BlockSpec(bias.shape, lambda b, p, i: (0, 0, 0))],
        out_specs=blk,
        out_shape=jax.ShapeDtypeStruct((B, S, W), BF16),
        scratch_shapes=kv_scratch + [acc, acc, acc, stage, stage, stage, stage],
        compiler_params=_cparams("arbitrary", "arbitrary", "arbitrary", vmem=VMEM_LIMIT_ATTENTION),
        name="dilated_attention",
    )(q, k, v, bias)


def _ffn_kernel(x_ref, ya_ref, yb_ref, wo_ref, g_ref, w1_ref, cw_ref, cb_ref, w2_ref,
                o_ref, act_s, carry_s, *, tiles_per_seq):
    i = pl.program_id(0)
    tm = x_ref.shape[0]
    half = ya_ref.shape[1]
    tf = TF_FFN

    @pl.when(i % tiles_per_seq == 0)
    def _():
        carry_s[...] = jnp.zeros_like(carry_s)

    x1 = (x_ref[...] + _dot(ya_ref[...], wo_ref[0:half, :])
          + _dot(yb_ref[...], wo_ref[half:2 * half, :]))
    h = _rms(x1, g_ref[...]).astype(BF16)
    groups = tm // SUBLANES
    sub = lax.broadcasted_iota(jnp.int32, (1, SUBLANES, 1), 1)
    for c in range(D_FF // tf):
        cols = slice(c * tf, (c + 1) * tf)
        a = _dot(h, w1_ref[:, cols])
        lin = _dot(h, w1_ref[:, D_FF + c * tf:D_FF + (c + 1) * tf])
        a3 = a.reshape(groups, SUBLANES, tf)
        prev = carry_s[:, cols].reshape(1, SUBLANES, tf)
        carry_s[:, cols] = a[tm - SUBLANES:tm, :]

        def delayed(k):
            r = pltpu.roll(a3, k, 1)
            rp = jnp.concatenate([pltpu.roll(prev, k, 1), r[:groups - 1]], axis=0)
            return jnp.where(sub >= k, r, rp)

        conv = (cb_ref[:, cols] + cw_ref[2:3, cols] * a3
                + cw_ref[1:2, cols] * delayed(1) + cw_ref[0:1, cols] * delayed(2))
        act = jax.nn.gelu(conv).reshape(tm, tf) * lin
        act_s[:, cols] = act.astype(BF16)
    o_ref[...] = x1 + _dot(act_s[...], w2_ref[...])


def _ffn(x, ya, yb, wo, g, w_in, cw, cb, w2, seq):
    T = x.shape[0]
    tm = TM_FFN
    half = ya.shape[1]
    row = lambda i: (i, 0)
    resident = lambda shape: pl.BlockSpec(shape, lambda i: (0, 0), pipeline_mode=pl.Buffered(1))
    return pl.pallas_call(
        functools.partial(_ffn_kernel, tiles_per_seq=seq // tm),
        grid=(T // tm,),
        in_specs=[pl.BlockSpec((tm, D_MODEL), row),
                  pl.BlockSpec((tm, half), row),
                  pl.BlockSpec((tm, half), row),
                  resident((2 * half, D_MODEL)),
                  resident((1, D_MODEL)),
                  resident((D_MODEL, 2 * D_FF)),
                  resident((3, D_FF)),
                  resident((1, D_FF)),
                  resident((D_FF, D_MODEL))],
        out_specs=pl.BlockSpec((tm, D_MODEL), row),
        out_shape=jax.ShapeDtypeStruct((T, D_MODEL), F32),
        scratch_shapes=[pltpu.VMEM((tm, D_FF), BF16),
                        pltpu.VMEM((SUBLANES, D_FF), F32)],
        compiler_params=_cparams("arbitrary"),
        name="outproj_conv_mlp",
    )(x, ya, yb, wo, g, w_in, cw, cb, w2)


def _odd_in_kernel(x_ref, g_ref, w_ref, o_ref):
    h = _rms(x_ref[...], g_ref[...])
    o_ref[...] = _dot(h.astype(BF16), w_ref[...])


def _odd_in(x, g, w):
    T = x.shape[0]
    tm = TM_PROJ
    ncol = w.shape[1]
    return pl.pallas_call(
        _odd_in_kernel,
        grid=(T // tm,),
        in_specs=[pl.BlockSpec((tm, D_MODEL), lambda i: (i, 0)),
                  pl.BlockSpec((1, D_MODEL), lambda i: (0, 0)),
                  pl.BlockSpec((D_MODEL, ncol), lambda i: (0, 0))],
        out_specs=pl.BlockSpec((tm, ncol), lambda i: (i, 0)),
        out_shape=jax.ShapeDtypeStruct((T, ncol), F32),
        compiler_params=_cparams("arbitrary"),
        name="odd_in_proj",
    )(x, g, w)


def _cswap(v):
    return jnp.concatenate([v[:, LANES:], v[:, :LANES]], axis=1)


def _s5_kernel(u_ref, bm_ref, p1_ref, p2_ref, q1_ref, q2_ref, pc_ref, tri_ref, cm_ref,
               d_ref, gw_ref, gb_ref, o_ref, carry, x_s, ys_s):
    tl = u_ref.shape[0]
    L = L_S5
    blk = 2 * LANES
    slab = 2 * S5_NSTATE // (S5_WIDTH // LANES)

    @pl.when(pl.program_id(1) == 0)
    def _():
        carry[...] = jnp.zeros_like(carry)

    tri = tri_ref[...]
    for s in range(S5_WIDTH // LANES):
        u = u_ref[:, s * LANES:(s + 1) * LANES]
        u16 = u.astype(BF16)
        for jb in range(slab // blk):
            cols = slice(s * slab + jb * blk, s * slab + (jb + 1) * blk)
            bu = _dot(u16, bm_ref[s, :, jb * blk:(jb + 1) * blk])
            for c in range(tl // L):
                rows = slice(c * L, (c + 1) * L)
                v = bu[rows, :].astype(BF16)
                z = v * q1_ref[:, cols] + _cswap(v) * q2_ref[:, cols]
                w = _dot(tri, z) + carry[0:1, cols]
                w16 = w.astype(BF16)
                x_s[rows, jb * blk:(jb + 1) * blk] = (
                    w16 * p1_ref[:, cols] + _cswap(w16) * p2_ref[:, cols])
                wl = w[L - 1:L, :]
                xl = wl * pc_ref[2:3, cols] + _cswap(wl) * pc_ref[3:4, cols]
                carry[0:1, cols] = xl * pc_ref[0:1, cols] + _cswap(xl) * pc_ref[1:2, cols]
        y = _dot(x_s[...], cm_ref[s]) + d_ref[:, s * LANES:(s + 1) * LANES] * u
        ys_s[:, s * LANES:(s + 1) * LANES] = jax.nn.gelu(y)
    ys = ys_s[...]
    o_ref[...] = (ys * jax.nn.sigmoid(_dot(ys.astype(BF16), gw_ref[...]) + gb_ref[...])).astype(BF16)


def _s5(proj, bm, p1, p2, q1, q2, pc, tri, cm, d, gw, gb, batch, seq):
    T = proj.shape[0]
    tl = TL_S5
    n_seq = seq // tl
    c2 = lambda b, i: (0, 0)
    c3 = lambda b, i: (0, 0, 0)
    return pl.pallas_call(
        _s5_kernel,
        grid=(batch, n_seq),
        in_specs=[pl.BlockSpec((tl, S5_WIDTH), lambda b, i: (b * n_seq + i, 0)),
                  pl.BlockSpec(bm.shape, c3),
                  pl.BlockSpec(p1.shape, c2), pl.BlockSpec(p2.shape, c2),
                  pl.BlockSpec(q1.shape, c2), pl.BlockSpec(q2.shape, c2),
                  pl.BlockSpec(pc.shape, c2),
                  pl.BlockSpec(tri.shape, c2),
                  pl.BlockSpec(cm.shape, c3),
                  pl.BlockSpec((1, S5_WIDTH), c2),
                  pl.BlockSpec((S5_WIDTH, S5_WIDTH), c2),
                  pl.BlockSpec((1, S5_WIDTH), c2)],
        out_specs=pl.BlockSpec((tl, S5_WIDTH), lambda b, i: (b * n_seq + i, 0)),
        out_shape=jax.ShapeDtypeStruct((T, S5_WIDTH), BF16),
        scratch_shapes=[pltpu.VMEM((SUBLANES, 2 * S5_NSTATE), F32),
                        pltpu.VMEM((tl, 2 * S5_NSTATE // (S5_WIDTH // LANES)), BF16),
                        pltpu.VMEM((tl, S5_WIDTH), F32)],
        compiler_params=_cparams("arbitrary", "arbitrary"),
        name="s5_glu",
    )(proj, bm, p1, p2, q1, q2, pc, tri, cm, d, gw, gb)


def _s5_params(lam_re, lam_im, b_re, b_im, c_re, c_im, log_step):
    G, N, P = S5_GROUPS, S5_STATE, S5_GROUP
    gs = LANES // P
    ns = S5_WIDTH // LANES
    step = jnp.exp(log_step.astype(F32))[:, None]
    lr, li = lam_re.astype(F32), lam_im.astype(F32)
    mag = jnp.exp(lr * step)
    ar, ai = mag * jnp.cos(li * step), mag * jnp.sin(li * step)
    den = lr * lr + li * li
    cr = ((ar - 1.0) * lr + ai * li) / den
    ci = (ai * lr - (ar - 1.0) * li) / den
    bbr = cr[..., None] * b_re - ci[..., None] * b_im
    bbi = cr[..., None] * b_im + ci[..., None] * b_re
    eye = jnp.eye(gs, dtype=F32)

    def in_blockdiag(t):
        t = t.reshape(ns, gs, N, P).transpose(0, 1, 3, 2)
        return jnp.einsum('ab,sapn->sapbn', eye, t).reshape(ns, gs * P, gs * N)

    def out_blockdiag(t):
        t = t.reshape(ns, gs, P, N).transpose(0, 1, 3, 2)
        return jnp.einsum('ab,sanp->sanbp', eye, t).reshape(ns, gs * N, gs * P)

    def interleave(re, im, axis):
        shp = list(re.shape)
        blocked = shp[:axis] + [shp[axis] // LANES, LANES] + shp[axis + 1:]
        both = jnp.stack([re.reshape(blocked), im.reshape(blocked)], axis=axis + 1)
        return both.reshape(shp[:axis] + [2 * shp[axis]] + shp[axis + 1:])

    bm = interleave(in_blockdiag(bbr), in_blockdiag(bbi), 2).astype(BF16)
    cm = interleave(out_blockdiag(c_re), out_blockdiag(-c_im), 1).astype(BF16)
    def twice(v):
        v = v.reshape(G * N // LANES, 1, LANES)
        return jnp.broadcast_to(v, (G * N // LANES, 2, LANES)).reshape(1, 2 * G * N)

    sign = jnp.asarray(np.tile(np.repeat([-1.0, 1.0], LANES), G * N // LANES)[None, :], F32)
    j = jnp.arange(L_S5, dtype=F32)[:, None]
    la, th = twice(lr * step), twice(li * step)
    pmag, qmag = jnp.exp(j * la), jnp.exp(-(j * la))
    cs, sn = jnp.cos(j * th), jnp.sin(j * th)
    p1, p2 = pmag * cs, sign * (pmag * sn)
    q1, q2 = qmag * cs, -(sign * (qmag * sn))
    pc = jnp.concatenate([p1[1:2], p2[1:2], p1[L_S5 - 1:], p2[L_S5 - 1:]], axis=0)
    return bm, p1.astype(BF16), p2.astype(BF16), q1.astype(BF16), q2.astype(BF16), pc, cm


def _gla_cumsum_matrix(tl):
    r = np.arange(tl)[:, None]
    c = np.arange(tl)[None, :]
    same = (r // C_GLA) == (c // C_GLA)
    return np.concatenate([same & (c <= r), same & (c > r)], axis=0).astype(np.float32)


def _gla_kernel(q_ref, k_ref, v_ref, g_ref, gk_ref, gw_ref, gb_ref, nrm_ref, cum_ref, o_ref, st):
    nbatch, tl = q_ref.shape[0], q_ref.shape[1]
    C = C_GLA
    blk = 2 * C
    pair = LANES // GLA_DK

    @pl.when(pl.program_id(0) == 0)
    def _():
        st[...] = jnp.zeros_like(st)

    lane = lax.broadcasted_iota(jnp.int32, (1, LANES), 1)
    rb = lax.broadcasted_iota(jnp.int32, (blk, blk), 0)
    cb = lax.broadcasted_iota(jnp.int32, (blk, blk), 1)
    causal = (rb >= cb) & ((rb < C) | (cb >= C))
    streams = []
    for n in range(nbatch):
        z = _dot(gk_ref[n].astype(BF16), gw_ref[...]) + gb_ref[...]
        log_a = _log_sigmoid(z) * (1.0 / GLA_TAU)
        hi, lo = _split_bf16(log_a)
        sums = _dot(cum_ref[...], hi) + _dot(cum_ref[...], lo)
        bc, suffix = sums[:tl], sums[tl:]
        eb = jnp.exp(bc)
        q_dec = q_ref[n] * (GLA_DK ** -0.5) * eb
        k = k_ref[n]
        k_inv = (k * jnp.exp(-bc)).astype(BF16)
        k_dec = k * jnp.exp(suffix)
        for h in range(GLA_HEADS):
            hp, hh = divmod(h, pair)
            cols = slice(hp * LANES, (hp + 1) * LANES)
            in_head = (lane >= hh * GLA_DK) & (lane < (hh + 1) * GLA_DK)
            streams.append(dict(
                n=n, h=h, eb=eb[:, cols],
                qd=jnp.where(in_head, q_dec[:, cols], 0.0).astype(BF16),
                kd=jnp.where(in_head, k_dec[:, cols], 0.0).astype(BF16),
                ki=k_inv[:, cols],
                vh=v_ref[n, :, h * GLA_DV:(h + 1) * GLA_DV].astype(BF16),
                state=st[n * GLA_HEADS + h]))
    for b in range(tl // blk):
        rows = slice(b * blk, (b + 1) * blk)
        for sd in streams:
            n, h = sd["n"], sd["h"]
            att = jnp.where(causal, _dot_nt(sd["qd"][rows], sd["ki"][rows]), 0.0)
            o = _dot(att.astype(BF16), sd["vh"][rows])
            inter = []
            for c in range(b * blk // C, (b + 1) * blk // C):
                crow = slice(c * C, (c + 1) * C)
                inter.append(_dot_nt(sd["qd"][crow], sd["state"].astype(BF16)))
                decay = sd["eb"][(c + 1) * C - 1:(c + 1) * C, :]
                sd["state"] = sd["state"] * decay + _dot_tn(sd["vh"][crow], sd["kd"][crow])
            o = _rms(o + jnp.concatenate(inter, axis=0), nrm_ref[...])
            gh = g_ref[n, rows, h * GLA_DV:(h + 1) * GLA_DV]
            o_ref[n, rows, h * GLA_DV:(h + 1) * GLA_DV] = (o * jax.nn.silu(gh)).astype(BF16)
    for sd in streams:
        st[sd["n"] * GLA_HEADS + sd["h"]] = sd["state"]


def _gla(proj, gw, gb, nrm, tri, batch, seq):
    tl = TL_GLA
    hk = GLA_HEADS * GLA_DK
    hv = GLA_HEADS * GLA_DV
    c2 = lambda i: (0, 0)
    q0 = S5_WIDTH // hk
    v0 = (S5_WIDTH + 2 * hk) // hv
    gk0 = (S5_WIDTH + 2 * hk + 2 * hv) // GK_PAD
    proj = proj.reshape(batch, seq, proj.shape[-1])
    out = pl.pallas_call(
        _gla_kernel,
        grid=(seq // tl,),
        in_specs=[pl.BlockSpec((batch, tl, hk), lambda i: (0, i, q0)),
                  pl.BlockSpec((batch, tl, hk), lambda i: (0, i, q0 + 1)),
                  pl.BlockSpec((batch, tl, hv), lambda i: (0, i, v0)),
                  pl.BlockSpec((batch, tl, hv), lambda i: (0, i, v0 + 1)),
                  pl.BlockSpec((batch, tl, GK_PAD), lambda i: (0, i, gk0)),
                  pl.BlockSpec((GK_PAD, hk), c2),
                  pl.BlockSpec((1, hk), c2),
                  pl.BlockSpec((1, GLA_DV), c2),
                  pl.BlockSpec((2 * tl, tl), c2)],
        out_specs=pl.BlockSpec((batch, tl, hv), lambda i: (0, i, 0)),
        out_shape=jax.ShapeDtypeStruct((batch, seq, hv), BF16),
        scratch_shapes=[pltpu.VMEM((batch * GLA_HEADS, GLA_DV, LANES), F32)],
        compiler_params=_cparams("arbitrary"),
        name="gla",
    )(proj, proj, proj, proj, proj, gw, gb, nrm, tri)
    return out.reshape(batch * seq, hv)


def _block_diag(w):
    nb, a, b = w.shape
    return jnp.einsum('hk,hij->hikj', jnp.eye(nb, dtype=w.dtype), w).reshape(nb * a, nb * b)


def _rope_tables(seq):
    half = ROPE_DIM // 2
    pos = np.arange(seq, dtype=np.float64)
    inv = ROPE_THETA ** (-np.arange(0, ROPE_DIM, 2, dtype=np.float64) / ROPE_DIM)
    ang = pos[:, None] * inv[None, :]
    cos, sin = np.cos(ang), np.sin(ang)
    rest = ATT_HEAD_DIM - ROPE_DIM
    ones = np.ones((seq, rest))
    zeros = np.zeros((seq, rest))
    zh = np.zeros((seq, half))
    per_head = lambda parts: jnp.asarray(
        np.tile(np.concatenate(parts, axis=1), (1, LANES // ATT_HEAD_DIM)), F32)
    return (per_head([cos, cos, ones]), per_head([-sin, zh, zeros]), per_head([zh, sin, zeros]))


def _cast_kernel(w_ref, o_ref):
    o_ref[...] = w_ref[0].astype(BF16)


def _to_bf16(w, idx):
    _, rows, cols = w.shape
    tr = rows // 4
    return pl.pallas_call(
        _cast_kernel,
        grid=(rows // tr,),
        in_specs=[pl.BlockSpec((1, tr, cols), lambda r: (idx, r, 0))],
        out_specs=pl.BlockSpec((tr, cols), lambda r: (r, 0)),
        out_shape=jax.ShapeDtypeStruct((rows, cols), BF16),
        compiler_params=_cparams("arbitrary"),
        name="weight_to_bf16",
    )(w)


def kernel(x, e_norm, e_w_in, e_conv_w, e_conv_b, e_gate_a_w, e_gate_a_b, e_gate_x_w, e_gate_x_b, e_lambda, e_q_norm, e_k_norm, e_w_out, o_norm, o_w_in, o_lambda_re, o_lambda_im, o_b_re, o_b_im, o_c_re, o_c_im, o_d, o_log_step, o_glu_w, o_glu_b, o_gk_w, o_gk_b, o_gla_norm, o_w_out, f_norm, f_w_in, f_conv_w, f_conv_b, f_w_out):
    B, S, D = x.shape
    T = B * S
    depth = f_norm.shape[0]
    row = lambda t: t.reshape(1, -1).astype(F32)
    xt = x.reshape(T, D)

    cos_t, s1_t, s2_t = _rope_tables(S)
    head_seg = jnp.asarray(np.kron(np.eye(LANES // ATT_HEAD_DIM), np.ones((ATT_HEAD_DIM, ATT_HEAD_DIM))), BF16)
    att_bias = jnp.asarray(_attention_bias())
    tri_s5 = jnp.asarray(np.tril(np.ones((L_S5, L_S5))), BF16)
    tri_gla = jnp.asarray(_gla_cumsum_matrix(TL_GLA), BF16)
    two_heads = lambda t: jnp.tile(row(t), (1, LANES // ATT_HEAD_DIM))

    for layer in range(depth):
        i = layer // 2
        if layer % 2 == 0:
            xg, q, k, v = _even_in(xt, row(e_norm[i]), _to_bf16(e_w_in, i),
                                   two_heads(e_q_norm[i]), two_heads(e_k_norm[i]),
                                   head_seg, cos_t, s1_t, s2_t, S)
            ya = _lru(xg, e_conv_w[i], row(e_conv_b[i]),
                      _block_diag(e_gate_a_w[i]).astype(BF16), row(e_gate_a_b[i]),
                      _block_diag(e_gate_x_w[i]).astype(BF16), row(e_gate_x_b[i]),
                      row(e_lambda[i]), B, S)
            yb = _attention(q.reshape(B, S, ATT_WIDTH), k.reshape(B, S, ATT_WIDTH),
                            v.reshape(B, S, ATT_WIDTH), att_bias).reshape(T, ATT_WIDTH)
            w_out = _to_bf16(e_w_out, i)
        else:
            w_in = jnp.pad(o_w_in[i], ((0, 0), (0, GK_PAD - GLA_LOWRANK))).astype(BF16)
            proj = _odd_in(xt, row(o_norm[i]), w_in)
            bm, p1, p2, q1, q2, pc, cm = _s5_params(o_lambda_re[i], o_lambda_im[i], o_b_re[i], o_b_im[i],
                                                    o_c_re[i], o_c_im[i], o_log_step[i])
            ya = _s5(proj, bm, p1, p2, q1, q2, pc, tri_s5, cm, row(o_d[i]),
                     _to_bf16(o_glu_w, i), row(o_glu_b[i]), B, S)
            gk_w = jnp.pad(o_gk_w[i], ((0, GK_PAD - GLA_LOWRANK), (0, 0))).astype(BF16)
            yb = _gla(proj, gk_w, row(o_gk_b[i]), row(o_gla_norm[i]), tri_gla, B, S)
            w_out = _to_bf16(o_w_out, i)
        xt = _ffn(xt, ya, yb, w_out, row(f_norm[layer]), _to_bf16(f_w_in, layer),
                  f_conv_w[layer], row(f_conv_b[layer]), _to_bf16(f_w_out, layer), S)
    return xt.reshape(B, S, D)
```

```python
import functools

import numpy as np
import jax
import jax.numpy as jnp
from jax import lax
from jax.experimental import pallas as pl
from jax.experimental.pallas import tpu as pltpu

F32 = jnp.float32
BF16 = jnp.bfloat16

D_MODEL = 1024
LRU_WIDTH = 512
LRU_BLOCKS = 8
LRU_CONV = 4
LRU_C = 8.0
ATT_HEADS = 8
ATT_HEAD_DIM = 64
ATT_WIDTH = 512
DILATED_PATTERNS = ((128, 1), (512, 4), (2048, 16))
ATT_SPAN = 2048
ROPE_THETA = 500000.0
ROPE_DIM = 16
S5_WIDTH = 512
S5_GROUP = 16
S5_GROUPS = 32
S5_STATE = 64
S5_NSTATE = S5_GROUPS * S5_STATE
GLA_HEADS = 4
GLA_DK = 64
GLA_DV = 128
GLA_LOWRANK = 16
GLA_TAU = 16.0
D_FF = 3 * D_MODEL
EPS = 1e-6
NEG_INF = -1e30

LANES = 128
SUBLANES = 8
VMEM_PHYSICAL = 64 * 1024 * 1024
VMEM_LIMIT = VMEM_PHYSICAL - 4 * 1024 * 1024

TM_PROJ = 1024
TM_FFN = 1024
TF_FFN = 512
TL_LRU = 512
TL_S5 = 512
L_S5 = 128
TL_GLA = 512
C_GLA = 64
GK_PAD = 128


def _cparams(*sem):
    return pltpu.CompilerParams(dimension_semantics=sem, vmem_limit_bytes=VMEM_LIMIT)


def _rms(x, g):
    return x * lax.rsqrt(jnp.mean(x * x, axis=-1, keepdims=True) + EPS) * g


def _log_sigmoid(x):
    return -(jnp.maximum(-x, 0.0) + jnp.log(1.0 + jnp.exp(-jnp.abs(x))))


def _split_bf16(x):
    hi = x.astype(BF16)
    lo = (x - hi.astype(F32)).astype(BF16)
    return hi, lo


def _dot(a, b):
    return jnp.dot(a, b, preferred_element_type=F32)


def _dot_nt(a, b):
    return lax.dot_general(a, b, (((1,), (1,)), ((), ())), preferred_element_type=F32)


def _dot_tn(a, b):
    return lax.dot_general(a, b, (((0,), (0,)), ((), ())), preferred_element_type=F32)


def _even_in_kernel(x_ref, g_ref, w_ref, qn_ref, kn_ref, seg_ref, cos_ref, s1_ref, s2_ref,
                    xg_ref, q_ref, k_ref, v_ref):
    h = _rms(x_ref[...], g_ref[...]).astype(BF16)
    seg = seg_ref[...]
    cos, s1, s2 = cos_ref[...], s1_ref[...], s2_ref[...]
    half = ROPE_DIM // 2
    for off, n_ref, dst, scale in ((2 * LRU_WIDTH, qn_ref, q_ref, ATT_HEAD_DIM ** -0.5),
                                   (2 * LRU_WIDTH + ATT_WIDTH, kn_ref, k_ref, 1.0)):
        y = _dot(h, w_ref[:, off:off + ATT_WIDTH])
        for c in range(ATT_WIDTH // LANES):
            t = y[:, c * LANES:(c + 1) * LANES]
            ms = _dot((t * t).astype(BF16), seg) * (1.0 / ATT_HEAD_DIM)
            tn = t * lax.rsqrt(ms + EPS) * n_ref[...]
            r = (tn * cos + pltpu.roll(tn, LANES - half, 1) * s1 + pltpu.roll(tn, half, 1) * s2)
            dst[:, c * LANES:(c + 1) * LANES] = r * scale
    xg_ref[...] = _dot(h, w_ref[:, :2 * LRU_WIDTH])
    v_ref[...] = _dot(h, w_ref[:, 2 * LRU_WIDTH + 2 * ATT_WIDTH:])


def _even_in(x, g, w, qn, kn, seg, cos_t, s1_t, s2_t, seq):
    T = x.shape[0]
    tm = TM_PROJ
    n_seq = seq // tm
    ncol = w.shape[1]
    full = lambda i: (0, 0)
    tab = lambda i: (i % n_seq, 0)
    row = lambda i: (i, 0)
    return pl.pallas_call(
        _even_in_kernel,
        grid=(T // tm,),
        in_specs=[pl.BlockSpec((tm, D_MODEL), row),
                  pl.BlockSpec((1, D_MODEL), full),
                  pl.BlockSpec((D_MODEL, ncol), full),
                  pl.BlockSpec((1, LANES), full),
                  pl.BlockSpec((1, LANES), full),
                  pl.BlockSpec((LANES, LANES), full),
                  pl.BlockSpec((tm, LANES), tab),
                  pl.BlockSpec((tm, LANES), tab),
                  pl.BlockSpec((tm, LANES), tab)],
        out_specs=[pl.BlockSpec((tm, 2 * LRU_WIDTH), row),
                   pl.BlockSpec((tm, ATT_WIDTH), row),
                   pl.BlockSpec((tm, ATT_WIDTH), row),
                   pl.BlockSpec((tm, ATT_WIDTH), row)],
        out_shape=[jax.ShapeDtypeStruct((T, 2 * LRU_WIDTH), F32),
                   jax.ShapeDtypeStruct((T, ATT_WIDTH), F32),
                   jax.ShapeDtypeStruct((T, ATT_WIDTH), F32),
                   jax.ShapeDtypeStruct((T, ATT_WIDTH), F32)],
        compiler_params=_cparams("arbitrary"),
        name="even_in_proj",
    )(x, g, w, qn, kn, seg, cos_t, s1_t, s2_t)


def _lru_kernel(xl_ref, gl_ref, cw_ref, cb_ref, wa_ref, ba_ref, wx_ref, bx_ref, lam_ref,
                o_ref, xbuf, hprev):
    tl = xl_ref.shape[0]

    @pl.when(pl.program_id(1) == 0)
    def _():
        xbuf[...] = jnp.zeros_like(xbuf)
        hprev[...] = jnp.zeros_like(hprev)

    groups = tl // SUBLANES
    sub = lax.broadcasted_iota(jnp.int32, (1, SUBLANES, 1), 1)
    x = xl_ref[...]
    x3 = x.reshape(groups, SUBLANES, LRU_WIDTH)
    prev = xbuf[...].reshape(1, SUBLANES, LRU_WIDTH)
    xbuf[...] = x[tl - SUBLANES:tl, :]
    conv = cb_ref[...] + cw_ref[LRU_CONV - 1:LRU_CONV, :] * x3
    for k in range(1, LRU_CONV):
        r = pltpu.roll(x3, k, 1)
        rp = jnp.concatenate([pltpu.roll(prev, k, 1), r[:groups - 1]], axis=0)
        conv = conv + cw_ref[LRU_CONV - 1 - k:LRU_CONV - k, :] * jnp.where(sub >= k, r, rp)
    conv = conv.reshape(tl, LRU_WIDTH)

    c16 = conv.astype(BF16)
    r = jax.nn.sigmoid(_dot(c16, wa_ref[...]) + ba_ref[...])
    ig = jax.nn.sigmoid(_dot(c16, wx_ref[...]) + bx_ref[...])
    log_a = (LRU_C * r) * _log_sigmoid(lam_ref[...])
    a = jnp.exp(log_a)
    b = jnp.sqrt(1.0 - a * a) * (ig * conv)

    a = a.reshape(groups, SUBLANES, LRU_WIDTH)
    b = b.reshape(groups, SUBLANES, LRU_WIDTH)
    d = 1
    while d < SUBLANES:
        keep = sub >= d
        a_sh = jnp.where(keep, pltpu.roll(a, d, 1), 1.0)
        b_sh = jnp.where(keep, pltpu.roll(b, d, 1), 0.0)
        b = a * b_sh + b
        a = a * a_sh
        d *= 2
    last = hprev[0:1, :]
    hs = []
    for t in range(groups):
        ht = b[t] + a[t] * last
        hs.append(ht)
        last = ht[SUBLANES - 1:SUBLANES, :]
    hprev[0:1, :] = last
    h = jnp.concatenate(hs, axis=0)
    o_ref[...] = (h * jax.nn.gelu(gl_ref[...])).astype(BF16)


def _lru(xg, cw, cb, wa, ba, wx, bx, lam, batch, seq):
    T = xg.shape[0]
    tl = TL_LRU
    n_seq = seq // tl
    full = lambda b, i: (0, 0)
    return pl.pallas_call(
        _lru_kernel,
        grid=(batch, n_seq),
        in_specs=[pl.BlockSpec((tl, LRU_WIDTH), lambda b, i: (b * n_seq + i, 0)),
                  pl.BlockSpec((tl, LRU_WIDTH), lambda b, i: (b * n_seq + i, 1)),
                  pl.BlockSpec((LRU_CONV, LRU_WIDTH), full),
                  pl.BlockSpec((1, LRU_WIDTH), full),
                  pl.BlockSpec((LRU_WIDTH, LRU_WIDTH), full),
                  pl.BlockSpec((1, LRU_WIDTH), full),
                  pl.BlockSpec((LRU_WIDTH, LRU_WIDTH), full),
                  pl.BlockSpec((1, LRU_WIDTH), full),
                  pl.BlockSpec((1, LRU_WIDTH), full)],
        out_specs=pl.BlockSpec((tl, LRU_WIDTH), lambda b, i: (b * n_seq + i, 0)),
        out_shape=jax.ShapeDtypeStruct((T, LRU_WIDTH), BF16),
        scratch_shapes=[pltpu.VMEM((SUBLANES, LRU_WIDTH), F32),
                        pltpu.VMEM((SUBLANES, LRU_WIDTH), F32)],
        compiler_params=_cparams("arbitrary", "arbitrary"),
        name="rg_lru",
    )(xg, xg, cw, cb, wa, ba, wx, bx, lam)


N_BACK = DILATED_PATTERNS[0][0] // DILATED_PATTERNS[0][1]
Q_BLOCKS = ATT_SPAN // N_BACK
STAGE_DIL = DILATED_PATTERNS[1][1]


def _attention_bias():
    qi = np.arange(N_BACK)[:, None]
    ki = np.arange(2 * N_BACK)[None, :]
    dist = N_BACK + qi - ki
    band = (dist >= 0) & (dist <= N_BACK)
    first = band & (ki >= N_BACK)
    return np.where(np.stack([band, first]), 0.0, NEG_INF).astype(np.float32)


def _attn_kernel(q_ref, k_ref, v_ref, bias_ref, o_ref, *scratch):
    kv_s = scratch[:6]
    o_s, m_s, l_s = scratch[6:9]
    stages = scratch[9:]
    sb = pl.program_id(2)
    lane = lax.broadcasted_iota(jnp.int32, (1, LANES), 1)
    head0 = lane < ATT_HEAD_DIM

    for p, (window, dil) in enumerate(DILATED_PATTERNS):
        per_res = Q_BLOCKS // dil
        span = N_BACK * per_res
        for src, dst, stage in ((k_ref, kv_s[2 * p], stages[0]), (v_ref, kv_s[2 * p + 1], stages[1])):
            @pl.when(sb == 0)
            def _():
                dst[:, 0:N_BACK, :] = jnp.zeros((dil, N_BACK, LANES), BF16)

            @pl.when(sb > 0)
            def _():
                dst[:, 0:N_BACK, :] = dst[:, span:span + N_BACK, :]

            if dil <= STAGE_DIL:
                for r in range(dil):
                    rows = pl.ds(r, span, stride=dil) if dil > 1 else pl.ds(0, span)
                    part = src[0, rows, :]
                    dst[r, N_BACK:N_BACK + span, :] = part.astype(BF16)
                    if dil == STAGE_DIL:
                        stage[r] = part
            else:
                sub = dil // STAGE_DIL
                for r in range(dil):
                    rows = pl.ds(r // STAGE_DIL, span, stride=sub)
                    dst[r, N_BACK:N_BACK + span, :] = stage[r % STAGE_DIL, rows, :].astype(BF16)

    q_stage = stages[2]
    out_stages = (stages[0], stages[1], stages[3])
    for r in range(STAGE_DIL):
        q_stage[r] = q_ref[0, pl.ds(r, ATT_SPAN // STAGE_DIL, stride=STAGE_DIL), :]

    for p, (window, dil) in enumerate(DILATED_PATTERNS):
        per_res = Q_BLOCKS // dil
        k_s, v_s = kv_s[2 * p], kv_s[2 * p + 1]

        for n in range(Q_BLOCKS):
            r, m = n % dil, n // dil
            if dil > 1:
                rows = pl.ds(m * (N_BACK * dil) + r, N_BACK, stride=dil)
            else:
                rows = pl.ds(m * N_BACK, N_BACK)
            if dil < STAGE_DIL:
                q = q_ref[0, rows, :]
            else:
                sub = dil // STAGE_DIL
                staged = (pl.ds(sub * N_BACK * m + r // STAGE_DIL, N_BACK, stride=sub) if sub > 1
                          else pl.ds(N_BACK * m, N_BACK))
                q = q_stage[r % STAGE_DIL, staged, :]
            q = q.astype(BF16)
            kc = k_s[r, m * N_BACK:(m + 2) * N_BACK, :]
            vc = v_s[r, m * N_BACK:(m + 2) * N_BACK, :]
            bias = bias_ref[jnp.where(sb == 0, 1, 0)] if m == 0 else bias_ref[0]
            res = []
            for h in range(LANES // ATT_HEAD_DIM):
                qm = jnp.where(head0 if h == 0 else ~head0, q, jnp.zeros_like(q))
                s = _dot_nt(qm, kc) + bias
                mx = jnp.max(s, axis=-1, keepdims=True)
                e = jnp.exp(s - mx)
                res.append((_dot(e.astype(BF16), vc), mx, jnp.sum(e, axis=-1, keepdims=True)))
            vals = [jnp.where(head0, res[0][idx], res[1][idx]) for idx in range(3)]
            if dil > STAGE_DIL:
                for st, val in zip(out_stages, vals):
                    st[r % STAGE_DIL, staged, :] = val
            else:
                for dst, val in zip((o_s, m_s, l_s), vals):
                    dst[p, rows, :] = val
        if dil > STAGE_DIL:
            for dst, st in zip((o_s, m_s, l_s), out_stages):
                for r in range(STAGE_DIL):
                    dst[p, pl.ds(r, ATT_SPAN // STAGE_DIL, stride=STAGE_DIL), :] = st[r]

    mx = jnp.maximum(jnp.maximum(m_s[0], m_s[1]), m_s[2])
    num = jnp.zeros_like(mx)
    den = jnp.zeros_like(mx)
    for p in range(len(DILATED_PATTERNS)):
        w = jnp.exp(m_s[p] - mx)
        num = num + w * o_s[p]
        den = den + w * l_s[p]
    o_ref[0] = (num / den).astype(BF16)


def _attention(q, k, v, bias):
    B, S, W = q.shape
    blk = pl.BlockSpec((1, ATT_SPAN, LANES), lambda b, p, i: (b, i, p))
    kv_scratch = []
    for window, dil in DILATED_PATTERNS:
        shape = (dil, N_BACK * (1 + Q_BLOCKS // dil), LANES)
        kv_scratch += [pltpu.VMEM(shape, BF16), pltpu.VMEM(shape, BF16)]
    acc = pltpu.VMEM((len(DILATED_PATTERNS), ATT_SPAN, LANES), F32)
    stage = pltpu.VMEM((STAGE_DIL, ATT_SPAN // STAGE_DIL, LANES), F32)
    return pl.pallas_call(
        _attn_kernel,
        grid=(B, W // LANES, S // ATT_SPAN),
        in_specs=[blk, blk, blk, pl.BlockSpec(bias.shape, lambda b, p, i: (0, 0, 0))],
        out_specs=blk,
        out_shape=jax.ShapeDtypeStruct((B, S, W), BF16),
        scratch_shapes=kv_scratch + [acc, acc, acc, stage, stage, stage, stage],
        compiler_params=_cparams("arbitrary", "arbitrary", "arbitrary"),
        name="dilated_attention",
    )(q, k, v, bias)


def _ffn_kernel(x_ref, ya_ref, yb_ref, wo_ref, g_ref, w1_ref, cw_ref, cb_ref, w2_ref,
                o_ref, act_s, carry_s, *, tiles_per_seq):
    i = pl.program_id(0)
    tm = x_ref.shape[0]
    half = ya_ref.shape[1]
    tf = TF_FFN

    @pl.when(i % tiles_per_seq == 0)
    def _():
        carry_s[...] = jnp.zeros_like(carry_s)

    x1 = (x_ref[...] + _dot(ya_ref[...], wo_ref[0:half, :])
          + _dot(yb_ref[...], wo_ref[half:2 * half, :]))
    h = _rms(x1, g_ref[...]).astype(BF16)
    groups = tm // SUBLANES
    sub = lax.broadcasted_iota(jnp.int32, (1, SUBLANES, 1), 1)
    for c in range(D_FF // tf):
        cols = slice(c * tf, (c + 1) * tf)
        a = _dot(h, w1_ref[:, cols])
        lin = _dot(h, w1_ref[:, D_FF + c * tf:D_FF + (c + 1) * tf])
        a3 = a.reshape(groups, SUBLANES, tf)
        prev = carry_s[:, cols].reshape(1, SUBLANES, tf)
        carry_s[:, cols] = a[tm - SUBLANES:tm, :]

        def delayed(k):
            r = pltpu.roll(a3, k, 1)
            rp = jnp.concatenate([pltpu.roll(prev, k, 1), r[:groups - 1]], axis=0)
            return jnp.where(sub >= k, r, rp)

        conv = (cb_ref[:, cols] + cw_ref[2:3, cols] * a3
                + cw_ref[1:2, cols] * delayed(1) + cw_ref[0:1, cols] * delayed(2))
        act = jax.nn.gelu(conv).reshape(tm, tf) * lin
        act_s[:, cols] = act.astype(BF16)
    o_ref[...] = x1 + _dot(act_s[...], w2_ref[...])


def _ffn(x, ya, yb, wo, g, w_in, cw, cb, w2, seq):
    T = x.shape[0]
    tm = TM_FFN
    half = ya.shape[1]
    row = lambda i: (i, 0)
    resident = lambda shape: pl.BlockSpec(shape, lambda i: (0, 0), pipeline_mode=pl.Buffered(1))
    return pl.pallas_call(
        functools.partial(_ffn_kernel, tiles_per_seq=seq // tm),
        grid=(T // tm,),
        in_specs=[pl.BlockSpec((tm, D_MODEL), row),
                  pl.BlockSpec((tm, half), row),
                  pl.BlockSpec((tm, half), row),
                  resident((2 * half, D_MODEL)),
                  resident((1, D_MODEL)),
                  resident((D_MODEL, 2 * D_FF)),
                  resident((3, D_FF)),
                  resident((1, D_FF)),
                  resident((D_FF, D_MODEL))],
        out_specs=pl.BlockSpec((tm, D_MODEL), row),
        out_shape=jax.ShapeDtypeStruct((T, D_MODEL), F32),
        scratch_shapes=[pltpu.VMEM((tm, D_FF), BF16),
                        pltpu.VMEM((SUBLANES, D_FF), F32)],
        compiler_params=_cparams("arbitrary"),
        name="outproj_conv_mlp",
    )(x, ya, yb, wo, g, w_in, cw, cb, w2)


def _odd_in_kernel(x_ref, g_ref, w_ref, o_ref):
    h = _rms(x_ref[...], g_ref[...])
    o_ref[...] = _dot(h.astype(BF16), w_ref[...])


def _odd_in(x, g, w):
    T = x.shape[0]
    tm = TM_PROJ
    ncol = w.shape[1]
    return pl.pallas_call(
        _odd_in_kernel,
        grid=(T // tm,),
        in_specs=[pl.BlockSpec((tm, D_MODEL), lambda i: (i, 0)),
                  pl.BlockSpec((1, D_MODEL), lambda i: (0, 0)),
                  pl.BlockSpec((D_MODEL, ncol), lambda i: (0, 0))],
        out_specs=pl.BlockSpec((tm, ncol), lambda i: (i, 0)),
        out_shape=jax.ShapeDtypeStruct((T, ncol), F32),
        compiler_params=_cparams("arbitrary"),
        name="odd_in_proj",
    )(x, g, w)


def _cswap(v):
    return jnp.concatenate([v[:, LANES:], v[:, :LANES]], axis=1)


def _s5_kernel(u_ref, bm_ref, p1_ref, p2_ref, q1_ref, q2_ref, pc_ref, tri_ref, cm_ref,
               d_ref, gw_ref, gb_ref, o_ref, carry, x_s, ys_s):
    tl = u_ref.shape[0]
    L = L_S5
    blk = 2 * LANES
    slab = 2 * S5_NSTATE // (S5_WIDTH // LANES)

    @pl.when(pl.program_id(1) == 0)
    def _():
        carry[...] = jnp.zeros_like(carry)

    tri = tri_ref[...]
    for s in range(S5_WIDTH // LANES):
        u = u_ref[:, s * LANES:(s + 1) * LANES]
        u16 = u.astype(BF16)
        for jb in range(slab // blk):
            cols = slice(s * slab + jb * blk, s * slab + (jb + 1) * blk)
            bu = _dot(u16, bm_ref[s, :, jb * blk:(jb + 1) * blk])
            for c in range(tl // L):
                rows = slice(c * L, (c + 1) * L)
                v = bu[rows, :].astype(BF16)
                z = v * q1_ref[:, cols] + _cswap(v) * q2_ref[:, cols]
                w = _dot(tri, z) + carry[0:1, cols]
                w16 = w.astype(BF16)
                x_s[rows, jb * blk:(jb + 1) * blk] = (
                    w16 * p1_ref[:, cols] + _cswap(w16) * p2_ref[:, cols])
                wl = w[L - 1:L, :]
                xl = wl * pc_ref[2:3, cols] + _cswap(wl) * pc_ref[3:4, cols]
                carry[0:1, cols] = xl * pc_ref[0:1, cols] + _cswap(xl) * pc_ref[1:2, cols]
        y = _dot(x_s[...], cm_ref[s]) + d_ref[:, s * LANES:(s + 1) * LANES] * u
        ys_s[:, s * LANES:(s + 1) * LANES] = jax.nn.gelu(y)
    ys = ys_s[...]
    o_ref[...] = (ys * jax.nn.sigmoid(_dot(ys.astype(BF16), gw_ref[...]) + gb_ref[...])).astype(BF16)


def _s5(proj, bm, p1, p2, q1, q2, pc, tri, cm, d, gw, gb, batch, seq):
    T = proj.shape[0]
    tl = TL_S5
    n_seq = seq // tl
    c2 = lambda b, i: (0, 0)
    c3 = lambda b, i: (0, 0, 0)
    return pl.pallas_call(
        _s5_kernel,
        grid=(batch, n_seq),
        in_specs=[pl.BlockSpec((tl, S5_WIDTH), lambda b, i: (b * n_seq + i, 0)),
                  pl.BlockSpec(bm.shape, c3),
                  pl.BlockSpec(p1.shape, c2), pl.BlockSpec(p2.shape, c2),
                  pl.BlockSpec(q1.shape, c2), pl.BlockSpec(q2.shape, c2),
                  pl.BlockSpec(pc.shape, c2),
                  pl.BlockSpec(tri.shape, c2),
                  pl.BlockSpec(cm.shape, c3),
                  pl.BlockSpec((1, S5_WIDTH), c2),
                  pl.BlockSpec((S5_WIDTH, S5_WIDTH), c2),
                  pl.BlockSpec((1, S5_WIDTH), c2)],
        out_specs=pl.BlockSpec((tl, S5_WIDTH), lambda b, i: (b * n_seq + i, 0)),
        out_shape=jax.ShapeDtypeStruct((T, S5_WIDTH), BF16),
        scratch_shapes=[pltpu.VMEM((SUBLANES, 2 * S5_NSTATE), F32),
                        pltpu.VMEM((tl, 2 * S5_NSTATE // (S5_WIDTH // LANES)), BF16),
                        pltpu.VMEM((tl, S5_WIDTH), F32)],
        compiler_params=_cparams("arbitrary", "arbitrary"),
        name="s5_glu",
    )(proj, bm, p1, p2, q1, q2, pc, tri, cm, d, gw, gb)


def _s5_params(lam_re, lam_im, b_re, b_im, c_re, c_im, log_step):
    G, N, P = S5_GROUPS, S5_STATE, S5_GROUP
    gs = LANES // P
    ns = S5_WIDTH // LANES
    step = jnp.exp(log_step.astype(F32))[:, None]
    lr, li = lam_re.astype(F32), lam_im.astype(F32)
    mag = jnp.exp(lr * step)
    ar, ai = mag * jnp.cos(li * step), mag * jnp.sin(li * step)
    den = lr * lr + li * li
    cr = ((ar - 1.0) * lr + ai * li) / den
    ci = (ai * lr - (ar - 1.0) * li) / den
    bbr = cr[..., None] * b_re - ci[..., None] * b_im
    bbi = cr[..., None] * b_im + ci[..., None] * b_re
    eye = jnp.eye(gs, dtype=F32)

    def in_blockdiag(t):
        t = t.reshape(ns, gs, N, P).transpose(0, 1, 3, 2)
        return jnp.einsum('ab,sapn->sapbn', eye, t).reshape(ns, gs * P, gs * N)

    def out_blockdiag(t):
        t = t.reshape(ns, gs, P, N).transpose(0, 1, 3, 2)
        return jnp.einsum('ab,sanp->sanbp', eye, t).reshape(ns, gs * N, gs * P)

    def interleave(re, im, axis):
        shp = list(re.shape)
        blocked = shp[:axis] + [shp[axis] // LANES, LANES] + shp[axis + 1:]
        both = jnp.stack([re.reshape(blocked), im.reshape(blocked)], axis=axis + 1)
        return both.reshape(shp[:axis] + [2 * shp[axis]] + shp[axis + 1:])

    bm = interleave(in_blockdiag(bbr), in_blockdiag(bbi), 2).astype(BF16)
    cm = interleave(out_blockdiag(c_re), out_blockdiag(-c_im), 1).astype(BF16)
    def twice(v):
        v = v.reshape(G * N // LANES, 1, LANES)
        return jnp.broadcast_to(v, (G * N // LANES, 2, LANES)).reshape(1, 2 * G * N)

    sign = jnp.asarray(np.tile(np.repeat([-1.0, 1.0], LANES), G * N // LANES)[None, :], F32)
    j = jnp.arange(L_S5, dtype=F32)[:, None]
    la, th = twice(lr * step), twice(li * step)
    pmag, qmag = jnp.exp(j * la), jnp.exp(-(j * la))
    cs, sn = jnp.cos(j * th), jnp.sin(j * th)
    p1, p2 = pmag * cs, sign * (pmag * sn)
    q1, q2 = qmag * cs, -(sign * (qmag * sn))
    pc = jnp.concatenate([p1[1:2], p2[1:2], p1[L_S5 - 1:], p2[L_S5 - 1:]], axis=0)
    return bm, p1.astype(BF16), p2.astype(BF16), q1.astype(BF16), q2.astype(BF16), pc, cm


def _gla_cumsum_matrix(tl):
    r = np.arange(tl)[:, None]
    c = np.arange(tl)[None, :]
    same = (r // C_GLA) == (c // C_GLA)
    return np.concatenate([same & (c <= r), same & (c > r)], axis=0).astype(np.float32)


def _gla_kernel(q_ref, k_ref, v_ref, g_ref, gk_ref, gw_ref, gb_ref, nrm_ref, cum_ref, o_ref, st):
    nbatch, tl = q_ref.shape[0], q_ref.shape[1]
    C = C_GLA
    blk = 2 * C
    pair = LANES // GLA_DK

    @pl.when(pl.program_id(0) == 0)
    def _():
        st[...] = jnp.zeros_like(st)

    lane = lax.broadcasted_iota(jnp.int32, (1, LANES), 1)
    rb = lax.broadcasted_iota(jnp.int32, (blk, blk), 0)
    cb = lax.broadcasted_iota(jnp.int32, (blk, blk), 1)
    causal = (rb >= cb) & ((rb < C) | (cb >= C))
    streams = []
    for n in range(nbatch):
        z = _dot(gk_ref[n].astype(BF16), gw_ref[...]) + gb_ref[...]
        log_a = _log_sigmoid(z) * (1.0 / GLA_TAU)
        hi, lo = _split_bf16(log_a)
        sums = _dot(cum_ref[...], hi) + _dot(cum_ref[...], lo)
        bc, suffix = sums[:tl], sums[tl:]
        eb = jnp.exp(bc)
        q_dec = q_ref[n] * (GLA_DK ** -0.5) * eb
        k = k_ref[n]
        k_inv = (k * jnp.exp(-bc)).astype(BF16)
        k_dec = k * jnp.exp(suffix)
        for h in range(GLA_HEADS):
            hp, hh = divmod(h, pair)
            cols = slice(hp * LANES, (hp + 1) * LANES)
            in_head = (lane >= hh * GLA_DK) & (lane < (hh + 1) * GLA_DK)
            streams.append(dict(
                n=n, h=h, eb=eb[:, cols],
                qd=jnp.where(in_head, q_dec[:, cols], 0.0).astype(BF16),
                kd=jnp.where(in_head, k_dec[:, cols], 0.0).astype(BF16),
                ki=k_inv[:, cols],
                vh=v_ref[n, :, h * GLA_DV:(h + 1) * GLA_DV].astype(BF16),
                state=st[n * GLA_HEADS + h]))
    for b in range(tl // blk):
        rows = slice(b * blk, (b + 1) * blk)
        for sd in streams:
            n, h = sd["n"], sd["h"]
            att = jnp.where(causal, _dot_nt(sd["qd"][rows], sd["ki"][rows]), 0.0)
            o = _dot(att.astype(BF16), sd["vh"][rows])
            inter = []
            for c in range(b * blk // C, (b + 1) * blk // C):
                crow = slice(c * C, (c + 1) * C)
                inter.append(_dot_nt(sd["qd"][crow], sd["state"].astype(BF16)))
                decay = sd["eb"][(c + 1) * C - 1:(c + 1) * C, :]
                sd["state"] = sd["state"] * decay + _dot_tn(sd["vh"][crow], sd["kd"][crow])
            o = _rms(o + jnp.concatenate(inter, axis=0), nrm_ref[...])
            gh = g_ref[n, rows, h * GLA_DV:(h + 1) * GLA_DV]
            o_ref[n, rows, h * GLA_DV:(h + 1) * GLA_DV] = (o * jax.nn.silu(gh)).astype(BF16)
    for sd in streams:
        st[sd["n"] * GLA_HEADS + sd["h"]] = sd["state"]


def _gla(proj, gw, gb, nrm, tri, batch, seq):
    tl = TL_GLA
    hk = GLA_HEADS * GLA_DK
    hv = GLA_HEADS * GLA_DV
    c2 = lambda i: (0, 0)
    q0 = S5_WIDTH // hk
    v0 = (S5_WIDTH + 2 * hk) // hv
    gk0 = (S5_WIDTH + 2 * hk + 2 * hv) // GK_PAD
    proj = proj.reshape(batch, seq, proj.shape[-1])
    out = pl.pallas_call(
        _gla_kernel,
        grid=(seq // tl,),
        in_specs=[pl.BlockSpec((batch, tl, hk), lambda i: (0, i, q0)),
                  pl.BlockSpec((batch, tl, hk), lambda i: (0, i, q0 + 1)),
                  pl.BlockSpec((batch, tl, hv), lambda i: (0, i, v0)),
                  pl.BlockSpec((batch, tl, hv), lambda i: (0, i, v0 + 1)),
                  pl.BlockSpec((batch, tl, GK_PAD), lambda i: (0, i, gk0)),
                  pl.BlockSpec((GK_PAD, hk), c2),
                  pl.BlockSpec((1, hk), c2),
                  pl.BlockSpec((1, GLA_DV), c2),
                  pl.BlockSpec((2 * tl, tl), c2)],
        out_specs=pl.BlockSpec((batch, tl, hv), lambda i: (0, i, 0)),
        out_shape=jax.ShapeDtypeStruct((batch, seq, hv), BF16),
        scratch_shapes=[pltpu.VMEM((batch * GLA_HEADS, GLA_DV, LANES), F32)],
        compiler_params=_cparams("arbitrary"),
        name="gla",
    )(proj, proj, proj, proj, proj, gw, gb, nrm, tri)
    return out.reshape(batch * seq, hv)


def _block_diag(w):
    nb, a, b = w.shape
    return jnp.einsum('hk,hij->hikj', jnp.eye(nb, dtype=w.dtype), w).reshape(nb * a, nb * b)


def _rope_tables(seq):
    half = ROPE_DIM // 2
    pos = np.arange(seq, dtype=np.float64)
    inv = ROPE_THETA ** (-np.arange(0, ROPE_DIM, 2, dtype=np.float64) / ROPE_DIM)
    ang = pos[:, None] * inv[None, :]
    cos, sin = np.cos(ang), np.sin(ang)
    rest = ATT_HEAD_DIM - ROPE_DIM
    ones = np.ones((seq, rest))
    zeros = np.zeros((seq, rest))
    zh = np.zeros((seq, half))
    per_head = lambda parts: jnp.asarray(
        np.tile(np.concatenate(parts, axis=1), (1, LANES // ATT_HEAD_DIM)), F32)
    return (per_head([cos, cos, ones]), per_head([-sin, zh, zeros]), per_head([zh, sin, zeros]))


def _cast_kernel(w_ref, o_ref):
    o_ref[...] = w_ref[0].astype(BF16)


def _to_bf16(w, idx):
    _, rows, cols = w.shape
    tr = rows // 4
    return pl.pallas_call(
        _cast_kernel,
        grid=(rows // tr,),
        in_specs=[pl.BlockSpec((1, tr, cols), lambda r: (idx, r, 0))],
        out_specs=pl.BlockSpec((tr, cols), lambda r: (r, 0)),
        out_shape=jax.ShapeDtypeStruct((rows, cols), BF16),
        compiler_params=_cparams("arbitrary"),
        name="weight_to_bf16",
    )(w)


def kernel(x, e_norm, e_w_in, e_conv_w, e_conv_b, e_gate_a_w, e_gate_a_b, e_gate_x_w, e_gate_x_b, e_lambda, e_q_norm, e_k_norm, e_w_out, o_norm, o_w_in, o_lambda_re, o_lambda_im, o_b_re, o_b_im, o_c_re, o_c_im, o_d, o_log_step, o_glu_w, o_glu_b, o_gk_w, o_gk_b, o_gla_norm, o_w_out, f_norm, f_w_in, f_conv_w, f_conv_b, f_w_out):
    B, S, D = x.shape
    T = B * S
    depth = f_norm.shape[0]
    row = lambda t: t.reshape(1, -1).astype(F32)
    xt = x.reshape(T, D)

    cos_t, s1_t, s2_t = _rope_tables(S)
    head_seg = jnp.asarray(np.kron(np.eye(LANES // ATT_HEAD_DIM), np.ones((ATT_HEAD_DIM, ATT_HEAD_DIM))), BF16)
    att_bias = jnp.asarray(_attention_bias())
    tri_s5 = jnp.asarray(np.tril(np.ones((L_S5, L_S5))), BF16)
    tri_gla = jnp.asarray(_gla_cumsum_matrix(TL_GLA), BF16)
    two_heads = lambda t: jnp.tile(row(t), (1, LANES // ATT_HEAD_DIM))

    for layer in range(depth):
        i = layer // 2
        if layer % 2 == 0:
            xg, q, k, v = _even_in(xt, row(e_norm[i]), _to_bf16(e_w_in, i),
                                   two_heads(e_q_norm[i]), two_heads(e_k_norm[i]),
                                   head_seg, cos_t, s1_t, s2_t, S)
            ya = _lru(xg, e_conv_w[i], row(e_conv_b[i]),
                      _block_diag(e_gate_a_w[i]).astype(BF16), row(e_gate_a_b[i]),
                      _block_diag(e_gate_x_w[i]).astype(BF16), row(e_gate_x_b[i]),
                      row(e_lambda[i]), B, S)
            yb = _attention(q.reshape(B, S, ATT_WIDTH), k.reshape(B, S, ATT_WIDTH),
                            v.reshape(B, S, ATT_WIDTH), att_bias).reshape(T, ATT_WIDTH)
            w_out = _to_bf16(e_w_out, i)
        else:
            w_in = jnp.pad(o_w_in[i], ((0, 0), (0, GK_PAD - GLA_LOWRANK))).astype(BF16)
            proj = _odd_in(xt, row(o_norm[i]), w_in)
            bm, p1, p2, q1, q2, pc, cm = _s5_params(o_lambda_re[i], o_lambda_im[i], o_b_re[i], o_b_im[i],
                                                    o_c_re[i], o_c_im[i], o_log_step[i])
            ya = _s5(proj, bm, p1, p2, q1, q2, pc, tri_s5, cm, row(o_d[i]),
                     _to_bf16(o_glu_w, i), row(o_glu_b[i]), B, S)
            gk_w = jnp.pad(o_gk_w[i], ((0, GK_PAD - GLA_LOWRANK), (0, 0))).astype(BF16)
            yb = _gla(proj, gk_w, row(o_gk_b[i]), row(o_gla_norm[i]), tri_gla, B, S)
            w_out = _to_bf16(o_w_out, i)
        xt = _ffn(xt, ya, yb, w_out, row(f_norm[layer]), _to_bf16(f_w_in, layer),
                  f_conv_w[layer], row(f_conv_b[layer]), _to_bf16(f_w_out, layer), S)
    return xt.reshape(B, S, D)
```

```python
import functools

import numpy as np
import jax
import jax.numpy as jnp
from jax import lax
from jax.experimental import pallas as pl
from jax.experimental.pallas import tpu as pltpu

F32 = jnp.float32
BF16 = jnp.bfloat16

D_MODEL = 1024
LRU_WIDTH = 512
LRU_BLOCKS = 8
LRU_CONV = 4
LRU_C = 8.0
ATT_HEADS = 8
ATT_HEAD_DIM = 64
ATT_WIDTH = 512
DILATED_PATTERNS = ((128, 1), (512, 4), (2048, 16))
ATT_SPAN = 2048
ROPE_THETA = 500000.0
ROPE_DIM = 16
S5_WIDTH = 512
S5_GROUP = 16
S5_GROUPS = 32
S5_STATE = 64
S5_NSTATE = S5_GROUPS * S5_STATE
GLA_HEADS = 4
GLA_DK = 64
GLA_DV = 128
GLA_LOWRANK = 16
GLA_TAU = 16.0
D_FF = 3 * D_MODEL
EPS = 1e-6
NEG_INF = -1e30

LANES = 128
SUBLANES = 8
VMEM_PHYSICAL = 64 * 1024 * 1024
VMEM_LIMIT = VMEM_PHYSICAL - 4 * 1024 * 1024

TM_PROJ = 1024
TM_FFN = 1024
TF_FFN = 512
TL_LRU = 512
TL_S5 = 1024
L_S5 = 128
TL_GLA = 256
C_GLA = 64
GK_PAD = 128


def _cparams(*sem):
    return pltpu.CompilerParams(dimension_semantics=sem, vmem_limit_bytes=VMEM_LIMIT)


def _rms(x, g):
    return x * lax.rsqrt(jnp.mean(x * x, axis=-1, keepdims=True) + EPS) * g


def _log_sigmoid(x):
    return -(jnp.maximum(-x, 0.0) + jnp.log(1.0 + jnp.exp(-jnp.abs(x))))


def _split_bf16(x):
    hi = x.astype(BF16)
    lo = (x - hi.astype(F32)).astype(BF16)
    return hi, lo


def _dot(a, b):
    return jnp.dot(a, b, preferred_element_type=F32)


def _dot_nt(a, b):
    return lax.dot_general(a, b, (((1,), (1,)), ((), ())), preferred_element_type=F32)


def _dot_tn(a, b):
    return lax.dot_general(a, b, (((0,), (0,)), ((), ())), preferred_element_type=F32)


def _even_in_kernel(x_ref, g_ref, w_ref, qn_ref, kn_ref, seg_ref, cos_ref, s1_ref, s2_ref,
                    xg_ref, q_ref, k_ref, v_ref):
    h = _rms(x_ref[...], g_ref[...]).astype(BF16)
    seg = seg_ref[...]
    cos, s1, s2 = cos_ref[...], s1_ref[...], s2_ref[...]
    half = ROPE_DIM // 2
    for off, n_ref, dst, scale in ((2 * LRU_WIDTH, qn_ref, q_ref, ATT_HEAD_DIM ** -0.5),
                                   (2 * LRU_WIDTH + ATT_WIDTH, kn_ref, k_ref, 1.0)):
        y = _dot(h, w_ref[:, off:off + ATT_WIDTH])
        for c in range(ATT_WIDTH // LANES):
            t = y[:, c * LANES:(c + 1) * LANES]
            ms = _dot((t * t).astype(BF16), seg) * (1.0 / ATT_HEAD_DIM)
            tn = t * lax.rsqrt(ms + EPS) * n_ref[...]
            r = (tn * cos + pltpu.roll(tn, LANES - half, 1) * s1 + pltpu.roll(tn, half, 1) * s2)
            dst[:, c * LANES:(c + 1) * LANES] = r * scale
    xg_ref[...] = _dot(h, w_ref[:, :2 * LRU_WIDTH])
    v_ref[...] = _dot(h, w_ref[:, 2 * LRU_WIDTH + 2 * ATT_WIDTH:])


def _even_in(x, g, w, qn, kn, seg, cos_t, s1_t, s2_t, seq):
    T = x.shape[0]
    tm = TM_PROJ
    n_seq = seq // tm
    ncol = w.shape[1]
    full = lambda i: (0, 0)
    tab = lambda i: (i % n_seq, 0)
    row = lambda i: (i, 0)
    return pl.pallas_call(
        _even_in_kernel,
        grid=(T // tm,),
        in_specs=[pl.BlockSpec((tm, D_MODEL), row),
                  pl.BlockSpec((1, D_MODEL), full),
                  pl.BlockSpec((D_MODEL, ncol), full),
                  pl.BlockSpec((1, LANES), full),
                  pl.BlockSpec((1, LANES), full),
                  pl.BlockSpec((LANES, LANES), full),
                  pl.BlockSpec((tm, LANES), tab),
                  pl.BlockSpec((tm, LANES), tab),
                  pl.BlockSpec((tm, LANES), tab)],
        out_specs=[pl.BlockSpec((tm, 2 * LRU_WIDTH), row),
                   pl.BlockSpec((tm, ATT_WIDTH), row),
                   pl.BlockSpec((tm, ATT_WIDTH), row),
                   pl.BlockSpec((tm, ATT_WIDTH), row)],
        out_shape=[jax.ShapeDtypeStruct((T, 2 * LRU_WIDTH), F32),
                   jax.ShapeDtypeStruct((T, ATT_WIDTH), F32),
                   jax.ShapeDtypeStruct((T, ATT_WIDTH), F32),
                   jax.ShapeDtypeStruct((T, ATT_WIDTH), F32)],
        compiler_params=_cparams("arbitrary"),
        name="even_in_proj",
    )(x, g, w, qn, kn, seg, cos_t, s1_t, s2_t)


def _lru_kernel(xl_ref, gl_ref, cw_ref, cb_ref, wa_ref, ba_ref, wx_ref, bx_ref, lam_ref,
                o_ref, xbuf, hprev):
    tl = xl_ref.shape[0]

    @pl.when(pl.program_id(1) == 0)
    def _():
        xbuf[...] = jnp.zeros_like(xbuf)
        hprev[...] = jnp.zeros_like(hprev)

    groups = tl // SUBLANES
    sub = lax.broadcasted_iota(jnp.int32, (1, SUBLANES, 1), 1)
    x = xl_ref[...]
    x3 = x.reshape(groups, SUBLANES, LRU_WIDTH)
    prev = xbuf[...].reshape(1, SUBLANES, LRU_WIDTH)
    xbuf[...] = x[tl - SUBLANES:tl, :]
    conv = cb_ref[...] + cw_ref[LRU_CONV - 1:LRU_CONV, :] * x3
    for k in range(1, LRU_CONV):
        r = pltpu.roll(x3, k, 1)
        rp = jnp.concatenate([pltpu.roll(prev, k, 1), r[:groups - 1]], axis=0)
        conv = conv + cw_ref[LRU_CONV - 1 - k:LRU_CONV - k, :] * jnp.where(sub >= k, r, rp)
    conv = conv.reshape(tl, LRU_WIDTH)

    c16 = conv.astype(BF16)
    r = jax.nn.sigmoid(_dot(c16, wa_ref[...]) + ba_ref[...])
    ig = jax.nn.sigmoid(_dot(c16, wx_ref[...]) + bx_ref[...])
    log_a = (LRU_C * r) * _log_sigmoid(lam_ref[...])
    a = jnp.exp(log_a)
    b = jnp.sqrt(1.0 - a * a) * (ig * conv)

    a = a.reshape(groups, SUBLANES, LRU_WIDTH)
    b = b.reshape(groups, SUBLANES, LRU_WIDTH)
    d = 1
    while d < SUBLANES:
        keep = sub >= d
        a_sh = jnp.where(keep, pltpu.roll(a, d, 1), 1.0)
        b_sh = jnp.where(keep, pltpu.roll(b, d, 1), 0.0)
        b = a * b_sh + b
        a = a * a_sh
        d *= 2
    last = hprev[0:1, :]
    hs = []
    for t in range(groups):
        ht = b[t] + a[t] * last
        hs.append(ht)
        last = ht[SUBLANES - 1:SUBLANES, :]
    hprev[0:1, :] = last
    h = jnp.concatenate(hs, axis=0)
    o_ref[...] = (h * jax.nn.gelu(gl_ref[...])).astype(BF16)


def _lru(xg, cw, cb, wa, ba, wx, bx, lam, batch, seq):
    T = xg.shape[0]
    tl = TL_LRU
    n_seq = seq // tl
    full = lambda b, i: (0, 0)
    return pl.pallas_call(
        _lru_kernel,
        grid=(batch, n_seq),
        in_specs=[pl.BlockSpec((tl, LRU_WIDTH), lambda b, i: (b * n_seq + i, 0)),
                  pl.BlockSpec((tl, LRU_WIDTH), lambda b, i: (b * n_seq + i, 1)),
                  pl.BlockSpec((LRU_CONV, LRU_WIDTH), full),
                  pl.BlockSpec((1, LRU_WIDTH), full),
                  pl.BlockSpec((LRU_WIDTH, LRU_WIDTH), full),
                  pl.BlockSpec((1, LRU_WIDTH), full),
                  pl.BlockSpec((LRU_WIDTH, LRU_WIDTH), full),
                  pl.BlockSpec((1, LRU_WIDTH), full),
                  pl.BlockSpec((1, LRU_WIDTH), full)],
        out_specs=pl.BlockSpec((tl, LRU_WIDTH), lambda b, i: (b * n_seq + i, 0)),
        out_shape=jax.ShapeDtypeStruct((T, LRU_WIDTH), BF16),
        scratch_shapes=[pltpu.VMEM((SUBLANES, LRU_WIDTH), F32),
                        pltpu.VMEM((SUBLANES, LRU_WIDTH), F32)],
        compiler_params=_cparams("arbitrary", "arbitrary"),
        name="rg_lru",
    )(xg, xg, cw, cb, wa, ba, wx, bx, lam)


N_BACK = DILATED_PATTERNS[0][0] // DILATED_PATTERNS[0][1]
Q_BLOCKS = ATT_SPAN // N_BACK
STAGE_DIL = DILATED_PATTERNS[1][1]


def _attention_bias():
    qi = np.arange(N_BACK)[:, None]
    ki = np.arange(2 * N_BACK)[None, :]
    dist = N_BACK + qi - ki
    band = (dist >= 0) & (dist <= N_BACK)
    first = band & (ki >= N_BACK)
    return np.where(np.stack([band, first]), 0.0, NEG_INF).astype(np.float32)


def _attn_kernel(q_ref, k_ref, v_ref, bias_ref, o_ref, *scratch):
    kv_s = scratch[:6]
    o_s, m_s, l_s = scratch[6:9]
    stages = scratch[9:]
    sb = pl.program_id(2)
    lane = lax.broadcasted_iota(jnp.int32, (1, LANES), 1)
    head0 = lane < ATT_HEAD_DIM

    for p, (window, dil) in enumerate(DILATED_PATTERNS):
        per_res = Q_BLOCKS // dil
        span = N_BACK * per_res
        for src, dst, stage in ((k_ref, kv_s[2 * p], stages[0]), (v_ref, kv_s[2 * p + 1], stages[1])):
            @pl.when(sb == 0)
            def _():
                dst[:, 0:N_BACK, :] = jnp.zeros((dil, N_BACK, LANES), BF16)

            @pl.when(sb > 0)
            def _():
                dst[:, 0:N_BACK, :] = dst[:, span:span + N_BACK, :]

            if dil <= STAGE_DIL:
                for r in range(dil):
                    rows = pl.ds(r, span, stride=dil) if dil > 1 else pl.ds(0, span)
                    part = src[0, rows, :]
                    dst[r, N_BACK:N_BACK + span, :] = part.astype(BF16)
                    if dil == STAGE_DIL:
                        stage[r] = part
            else:
                sub = dil // STAGE_DIL
                for r in range(dil):
                    rows = pl.ds(r // STAGE_DIL, span, stride=sub)
                    dst[r, N_BACK:N_BACK + span, :] = stage[r % STAGE_DIL, rows, :].astype(BF16)

    q_stage = stages[2]
    out_stages = (stages[0], stages[1], stages[3])
    for r in range(STAGE_DIL):
        q_stage[r] = q_ref[0, pl.ds(r, ATT_SPAN // STAGE_DIL, stride=STAGE_DIL), :]

    for p, (window, dil) in enumerate(DILATED_PATTERNS):
        per_res = Q_BLOCKS // dil
        k_s, v_s = kv_s[2 * p], kv_s[2 * p + 1]

        for n in range(Q_BLOCKS):
            r, m = n % dil, n // dil
            if dil > 1:
                rows = pl.ds(m * (N_BACK * dil) + r, N_BACK, stride=dil)
            else:
                rows = pl.ds(m * N_BACK, N_BACK)
            if dil < STAGE_DIL:
                q = q_ref[0, rows, :]
            else:
                sub = dil // STAGE_DIL
                staged = (pl.ds(sub * N_BACK * m + r // STAGE_DIL, N_BACK, stride=sub) if sub > 1
                          else pl.ds(N_BACK * m, N_BACK))
                q = q_stage[r % STAGE_DIL, staged, :]
            q = q.astype(BF16)
            kc = k_s[r, m * N_BACK:(m + 2) * N_BACK, :]
            vc = v_s[r, m * N_BACK:(m + 2) * N_BACK, :]
            bias = bias_ref[jnp.where(sb == 0, 1, 0)] if m == 0 else bias_ref[0]
            res = []
            for h in range(LANES // ATT_HEAD_DIM):
                qm = jnp.where(head0 if h == 0 else ~head0, q, jnp.zeros_like(q))
                s = _dot_nt(qm, kc) + bias
                mx = jnp.max(s, axis=-1, keepdims=True)
                e = jnp.exp(s - mx)
                res.append((_dot(e.astype(BF16), vc), mx, jnp.sum(e, axis=-1, keepdims=True)))
            vals = [jnp.where(head0, res[0][idx], res[1][idx]) for idx in range(3)]
            if dil > STAGE_DIL:
                for st, val in zip(out_stages, vals):
                    st[r % STAGE_DIL, staged, :] = val
            else:
                for dst, val in zip((o_s, m_s, l_s), vals):
                    dst[p, rows, :] = val
        if dil > STAGE_DIL:
            for dst, st in zip((o_s, m_s, l_s), out_stages):
                for r in range(STAGE_DIL):
                    dst[p, pl.ds(r, ATT_SPAN // STAGE_DIL, stride=STAGE_DIL), :] = st[r]

    mx = jnp.maximum(jnp.maximum(m_s[0], m_s[1]), m_s[2])
    num = jnp.zeros_like(mx)
    den = jnp.zeros_like(mx)
    for p in range(len(DILATED_PATTERNS)):
        w = jnp.exp(m_s[p] - mx)
        num = num + w * o_s[p]
        den = den + w * l_s[p]
    o_ref[0] = (num / den).astype(BF16)


def _attention(q, k, v, bias):
    B, S, W = q.shape
    blk = pl.BlockSpec((1, ATT_SPAN, LANES), lambda b, p, i: (b, i, p))
    kv_scratch = []
    for window, dil in DILATED_PATTERNS:
        shape = (dil, N_BACK * (1 + Q_BLOCKS // dil), LANES)
        kv_scratch += [pltpu.VMEM(shape, BF16), pltpu.VMEM(shape, BF16)]
    acc = pltpu.VMEM((len(DILATED_PATTERNS), ATT_SPAN, LANES), F32)
    stage = pltpu.VMEM((STAGE_DIL, ATT_SPAN // STAGE_DIL, LANES), F32)
    return pl.pallas_call(
        _attn_kernel,
        grid=(B, W // LANES, S // ATT_SPAN),
        in_specs=[blk, blk, blk, pl.BlockSpec(bias.shape, lambda b, p, i: (0, 0, 0))],
        out_specs=blk,
        out_shape=jax.ShapeDtypeStruct((B, S, W), BF16),
        scratch_shapes=kv_scratch + [acc, acc, acc, stage, stage, stage, stage],
        compiler_params=_cparams("arbitrary", "arbitrary", "arbitrary"),
        name="dilated_attention",
    )(q, k, v, bias)


def _ffn_kernel(x_ref, ya_ref, yb_ref, wo_ref, g_ref, w1_ref, cw_ref, cb_ref, w2_ref,
                o_ref, act_s, carry_s, *, tiles_per_seq):
    i = pl.program_id(0)
    tm = x_ref.shape[0]
    half = ya_ref.shape[1]
    tf = TF_FFN

    @pl.when(i % tiles_per_seq == 0)
    def _():
        carry_s[...] = jnp.zeros_like(carry_s)

    x1 = (x_ref[...] + _dot(ya_ref[...], wo_ref[0:half, :])
          + _dot(yb_ref[...], wo_ref[half:2 * half, :]))
    h = _rms(x1, g_ref[...]).astype(BF16)
    groups = tm // SUBLANES
    sub = lax.broadcasted_iota(jnp.int32, (1, SUBLANES, 1), 1)
    for c in range(D_FF // tf):
        cols = slice(c * tf, (c + 1) * tf)
        a = _dot(h, w1_ref[:, cols])
        lin = _dot(h, w1_ref[:, D_FF + c * tf:D_FF + (c + 1) * tf])
        a3 = a.reshape(groups, SUBLANES, tf)
        prev = carry_s[:, cols].reshape(1, SUBLANES, tf)
        carry_s[:, cols] = a[tm - SUBLANES:tm, :]

        def delayed(k):
            r = pltpu.roll(a3, k, 1)
            rp = jnp.concatenate([pltpu.roll(prev, k, 1), r[:groups - 1]], axis=0)
            return jnp.where(sub >= k, r, rp)

        conv = (cb_ref[:, cols] + cw_ref[2:3, cols] * a3
                + cw_ref[1:2, cols] * delayed(1) + cw_ref[0:1, cols] * delayed(2))
        act = jax.nn.gelu(conv).reshape(tm, tf) * lin
        act_s[:, cols] = act.astype(BF16)
    o_ref[...] = x1 + _dot(act_s[...], w2_ref[...])


def _ffn(x, ya, yb, wo, g, w_in, cw, cb, w2, seq):
    T = x.shape[0]
    tm = TM_FFN
    half = ya.shape[1]
    row = lambda i: (i, 0)
    resident = lambda shape: pl.BlockSpec(shape, lambda i: (0, 0), pipeline_mode=pl.Buffered(1))
    return pl.pallas_call(
        functools.partial(_ffn_kernel, tiles_per_seq=seq // tm),
        grid=(T // tm,),
        in_specs=[pl.BlockSpec((tm, D_MODEL), row),
                  pl.BlockSpec((tm, half), row),
                  pl.BlockSpec((tm, half), row),
                  resident((2 * half, D_MODEL)),
                  resident((1, D_MODEL)),
                  resident((D_MODEL, 2 * D_FF)),
                  resident((3, D_FF)),
                  resident((1, D_FF)),
                  resident((D_FF, D_MODEL))],
        out_specs=pl.BlockSpec((tm, D_MODEL), row),
        out_shape=jax.ShapeDtypeStruct((T, D_MODEL), F32),
        scratch_shapes=[pltpu.VMEM((tm, D_FF), BF16),
                        pltpu.VMEM((SUBLANES, D_FF), F32)],
        compiler_params=_cparams("arbitrary"),
        name="outproj_conv_mlp",
    )(x, ya, yb, wo, g, w_in, cw, cb, w2)


def _odd_in_kernel(x_ref, g_ref, w_ref, o_ref):
    h = _rms(x_ref[...], g_ref[...])
    o_ref[...] = _dot(h.astype(BF16), w_ref[...])


def _odd_in(x, g, w):
    T = x.shape[0]
    tm = TM_PROJ
    ncol = w.shape[1]
    return pl.pallas_call(
        _odd_in_kernel,
        grid=(T // tm,),
        in_specs=[pl.BlockSpec((tm, D_MODEL), lambda i: (i, 0)),
                  pl.BlockSpec((1, D_MODEL), lambda i: (0, 0)),
                  pl.BlockSpec((D_MODEL, ncol), lambda i: (0, 0))],
        out_specs=pl.BlockSpec((tm, ncol), lambda i: (i, 0)),
        out_shape=jax.ShapeDtypeStruct((T, ncol), F32),
        compiler_params=_cparams("arbitrary"),
        name="odd_in_proj",
    )(x, g, w)


def _cswap(v):
    return jnp.concatenate([v[:, LANES:], v[:, :LANES]], axis=1)


def _s5_kernel(u_ref, bm_ref, p1_ref, p2_ref, q1_ref, q2_ref, pc_ref, tri_ref, cm_ref,
               d_ref, gw_ref, gb_ref, o_ref, carry, x_s, ys_s):
    tl = u_ref.shape[0]
    L = L_S5
    blk = 2 * LANES
    slab = 2 * S5_NSTATE // (S5_WIDTH // LANES)

    @pl.when(pl.program_id(1) == 0)
    def _():
        carry[...] = jnp.zeros_like(carry)

    tri = tri_ref[...]
    for s in range(S5_WIDTH // LANES):
        u = u_ref[:, s * LANES:(s + 1) * LANES]
        u16 = u.astype(BF16)
        for jb in range(slab // blk):
            cols = slice(s * slab + jb * blk, s * slab + (jb + 1) * blk)
            bu = _dot(u16, bm_ref[s, :, jb * blk:(jb + 1) * blk])
            for c in range(tl // L):
                rows = slice(c * L, (c + 1) * L)
                v = bu[rows, :].astype(BF16)
                z = v * q1_ref[:, cols] + _cswap(v) * q2_ref[:, cols]
                w = _dot(tri, z) + carry[0:1, cols]
                w16 = w.astype(BF16)
                x_s[rows, jb * blk:(jb + 1) * blk] = (
                    w16 * p1_ref[:, cols] + _cswap(w16) * p2_ref[:, cols])
                wl = w[L - 1:L, :]
                xl = wl * pc_ref[2:3, cols] + _cswap(wl) * pc_ref[3:4, cols]
                carry[0:1, cols] = xl * pc_ref[0:1, cols] + _cswap(xl) * pc_ref[1:2, cols]
        y = _dot(x_s[...], cm_ref[s]) + d_ref[:, s * LANES:(s + 1) * LANES] * u
        ys_s[:, s * LANES:(s + 1) * LANES] = jax.nn.gelu(y)
    ys = ys_s[...]
    o_ref[...] = (ys * jax.nn.sigmoid(_dot(ys.astype(BF16), gw_ref[...]) + gb_ref[...])).astype(BF16)


def _s5(proj, bm, p1, p2, q1, q2, pc, tri, cm, d, gw, gb, batch, seq):
    T = proj.shape[0]
    tl = TL_S5
    n_seq = seq // tl
    c2 = lambda b, i: (0, 0)
    c3 = lambda b, i: (0, 0, 0)
    return pl.pallas_call(
        _s5_kernel,
        grid=(batch, n_seq),
        in_specs=[pl.BlockSpec((tl, S5_WIDTH), lambda b, i: (b * n_seq + i, 0)),
                  pl.BlockSpec(bm.shape, c3),
                  pl.BlockSpec(p1.shape, c2), pl.BlockSpec(p2.shape, c2),
                  pl.BlockSpec(q1.shape, c2), pl.BlockSpec(q2.shape, c2),
                  pl.BlockSpec(pc.shape, c2),
                  pl.BlockSpec(tri.shape, c2),
                  pl.BlockSpec(cm.shape, c3),
                  pl.BlockSpec((1, S5_WIDTH), c2),
                  pl.BlockSpec((S5_WIDTH, S5_WIDTH), c2),
                  pl.BlockSpec((1, S5_WIDTH), c2)],
        out_specs=pl.BlockSpec((tl, S5_WIDTH), lambda b, i: (b * n_seq + i, 0)),
        out_shape=jax.ShapeDtypeStruct((T, S5_WIDTH), BF16),
        scratch_shapes=[pltpu.VMEM((SUBLANES, 2 * S5_NSTATE), F32),
                        pltpu.VMEM((tl, 2 * S5_NSTATE // (S5_WIDTH // LANES)), BF16),
                        pltpu.VMEM((tl, S5_WIDTH), F32)],
        compiler_params=_cparams("arbitrary", "arbitrary"),
        name="s5_glu",
    )(proj, bm, p1, p2, q1, q2, pc, tri, cm, d, gw, gb)


def _s5_params(lam_re, lam_im, b_re, b_im, c_re, c_im, log_step):
    G, N, P = S5_GROUPS, S5_STATE, S5_GROUP
    gs = LANES // P
    ns = S5_WIDTH // LANES
    step = jnp.exp(log_step.astype(F32))[:, None]
    lr, li = lam_re.astype(F32), lam_im.astype(F32)
    mag = jnp.exp(lr * step)
    ar, ai = mag * jnp.cos(li * step), mag * jnp.sin(li * step)
    den = lr * lr + li * li
    cr = ((ar - 1.0) * lr + ai * li) / den
    ci = (ai * lr - (ar - 1.0) * li) / den
    bbr = cr[..., None] * b_re - ci[..., None] * b_im
    bbi = cr[..., None] * b_im + ci[..., None] * b_re
    eye = jnp.eye(gs, dtype=F32)

    def in_blockdiag(t):
        t = t.reshape(ns, gs, N, P).transpose(0, 1, 3, 2)
        return jnp.einsum('ab,sapn->sapbn', eye, t).reshape(ns, gs * P, gs * N)

    def out_blockdiag(t):
        t = t.reshape(ns, gs, P, N).transpose(0, 1, 3, 2)
        return jnp.einsum('ab,sanp->sanbp', eye, t).reshape(ns, gs * N, gs * P)

    def interleave(re, im, axis):
        shp = list(re.shape)
        blocked = shp[:axis] + [shp[axis] // LANES, LANES] + shp[axis + 1:]
        both = jnp.stack([re.reshape(blocked), im.reshape(blocked)], axis=axis + 1)
        return both.reshape(shp[:axis] + [2 * shp[axis]] + shp[axis + 1:])

    bm = interleave(in_blockdiag(bbr), in_blockdiag(bbi), 2).astype(BF16)
    cm = interleave(out_blockdiag(c_re), out_blockdiag(-c_im), 1).astype(BF16)
    def twice(v):
        v = v.reshape(G * N // LANES, 1, LANES)
        return jnp.broadcast_to(v, (G * N // LANES, 2, LANES)).reshape(1, 2 * G * N)

    sign = jnp.asarray(np.tile(np.repeat([-1.0, 1.0], LANES), G * N // LANES)[None, :], F32)
    j = jnp.arange(L_S5, dtype=F32)[:, None]
    la, th = twice(lr * step), twice(li * step)
    pmag, qmag = jnp.exp(j * la), jnp.exp(-(j * la))
    cs, sn = jnp.cos(j * th), jnp.sin(j * th)
    p1, p2 = pmag * cs, sign * (pmag * sn)
    q1, q2 = qmag * cs, -(sign * (qmag * sn))
    pc = jnp.concatenate([p1[1:2], p2[1:2], p1[L_S5 - 1:], p2[L_S5 - 1:]], axis=0)
    return bm, p1.astype(BF16), p2.astype(BF16), q1.astype(BF16), q2.astype(BF16), pc, cm


def _gla_cumsum_matrix(tl):
    r = np.arange(tl)[:, None]
    c = np.arange(tl)[None, :]
    same = (r // C_GLA) == (c // C_GLA)
    return np.concatenate([same & (c <= r), same & (c > r)], axis=0).astype(np.float32)


def _gla_kernel(q_ref, k_ref, v_ref, g_ref, gk_ref, gw_ref, gb_ref, nrm_ref, cum_ref, o_ref, st):
    nbatch, tl = q_ref.shape[0], q_ref.shape[1]
    C = C_GLA
    blk = 2 * C
    pair = LANES // GLA_DK

    @pl.when(pl.program_id(0) == 0)
    def _():
        st[...] = jnp.zeros_like(st)

    lane = lax.broadcasted_iota(jnp.int32, (1, LANES), 1)
    rb = lax.broadcasted_iota(jnp.int32, (blk, blk), 0)
    cb = lax.broadcasted_iota(jnp.int32, (blk, blk), 1)
    causal = (rb >= cb) & ((rb < C) | (cb >= C))
    streams = []
    for n in range(nbatch):
        z = _dot(gk_ref[n].astype(BF16), gw_ref[...]) + gb_ref[...]
        log_a = _log_sigmoid(z) * (1.0 / GLA_TAU)
        hi, lo = _split_bf16(log_a)
        sums = _dot(cum_ref[...], hi) + _dot(cum_ref[...], lo)
        bc, suffix = sums[:tl], sums[tl:]
        eb = jnp.exp(bc)
        q_dec = q_ref[n] * (GLA_DK ** -0.5) * eb
        k = k_ref[n]
        k_inv = (k * jnp.exp(-bc)).astype(BF16)
        k_dec = k * jnp.exp(suffix)
        for h in range(GLA_HEADS):
            hp, hh = divmod(h, pair)
            cols = slice(hp * LANES, (hp + 1) * LANES)
            in_head = (lane >= hh * GLA_DK) & (lane < (hh + 1) * GLA_DK)
            streams.append(dict(
                n=n, h=h, eb=eb[:, cols],
                qd=jnp.where(in_head, q_dec[:, cols], 0.0).astype(BF16),
                kd=jnp.where(in_head, k_dec[:, cols], 0.0).astype(BF16),
                ki=k_inv[:, cols],
                vh=v_ref[n, :, h * GLA_DV:(h + 1) * GLA_DV].astype(BF16),
                state=st[n * GLA_HEADS + h]))
    for b in range(tl // blk):
        rows = slice(b * blk, (b + 1) * blk)
        for sd in streams:
            n, h = sd["n"], sd["h"]
            att = jnp.where(causal, _dot_nt(sd["qd"][rows], sd["ki"][rows]), 0.0)
            o = _dot(att.astype(BF16), sd["vh"][rows])
            inter = []
            for c in range(b * blk // C, (b + 1) * blk // C):
                crow = slice(c * C, (c + 1) * C)
                inter.append(_dot_nt(sd["qd"][crow], sd["state"].astype(BF16)))
                decay = sd["eb"][(c + 1) * C - 1:(c + 1) * C, :]
                sd["state"] = sd["state"] * decay + _dot_tn(sd["vh"][crow], sd["kd"][crow])
            o = _rms(o + jnp.concatenate(inter, axis=0), nrm_ref[...])
            gh = g_ref[n, rows, h * GLA_DV:(h + 1) * GLA_DV]
            o_ref[n, rows, h * GLA_DV:(h + 1) * GLA_DV] = (o * jax.nn.silu(gh)).astype(BF16)
    for sd in streams:
        st[sd["n"] * GLA_HEADS + sd["h"]] = sd["state"]


def _gla(proj, gw, gb, nrm, tri, batch, seq):
    tl = TL_GLA
    hk = GLA_HEADS * GLA_DK
    hv = GLA_HEADS * GLA_DV
    c2 = lambda i: (0, 0)
    q0 = S5_WIDTH // hk
    v0 = (S5_WIDTH + 2 * hk) // hv
    gk0 = (S5_WIDTH + 2 * hk + 2 * hv) // GK_PAD
    proj = proj.reshape(batch, seq, proj.shape[-1])
    out = pl.pallas_call(
        _gla_kernel,
        grid=(seq // tl,),
        in_specs=[pl.BlockSpec((batch, tl, hk), lambda i: (0, i, q0)),
                  pl.BlockSpec((batch, tl, hk), lambda i: (0, i, q0 + 1)),
                  pl.BlockSpec((batch, tl, hv), lambda i: (0, i, v0)),
                  pl.BlockSpec((batch, tl, hv), lambda i: (0, i, v0 + 1)),
                  pl.BlockSpec((batch, tl, GK_PAD), lambda i: (0, i, gk0)),
                  pl.BlockSpec((GK_PAD, hk), c2),
                  pl.BlockSpec((1, hk), c2),
                  pl.BlockSpec((1, GLA_DV), c2),
                  pl.BlockSpec((2 * tl, tl), c2)],
        out_specs=pl.BlockSpec((batch, tl, hv), lambda i: (0, i, 0)),
        out_shape=jax.ShapeDtypeStruct((batch, seq, hv), BF16),
        scratch_shapes=[pltpu.VMEM((batch * GLA_HEADS, GLA_DV, LANES), F32)],
        compiler_params=_cparams("arbitrary"),
        name="gla",
    )(proj, proj, proj, proj, proj, gw, gb, nrm, tri)
    return out.reshape(batch * seq, hv)


def _block_diag(w):
    nb, a, b = w.shape
    return jnp.einsum('hk,hij->hikj', jnp.eye(nb, dtype=w.dtype), w).reshape(nb * a, nb * b)


def _rope_tables(seq):
    half = ROPE_DIM // 2
    pos = np.arange(seq, dtype=np.float64)
    inv = ROPE_THETA ** (-np.arange(0, ROPE_DIM, 2, dtype=np.float64) / ROPE_DIM)
    ang = pos[:, None] * inv[None, :]
    cos, sin = np.cos(ang), np.sin(ang)
    rest = ATT_HEAD_DIM - ROPE_DIM
    ones = np.ones((seq, rest))
    zeros = np.zeros((seq, rest))
    zh = np.zeros((seq, half))
    per_head = lambda parts: jnp.asarray(
        np.tile(np.concatenate(parts, axis=1), (1, LANES // ATT_HEAD_DIM)), F32)
    return (per_head([cos, cos, ones]), per_head([-sin, zh, zeros]), per_head([zh, sin, zeros]))


def _cast_kernel(w_ref, o_ref):
    o_ref[...] = w_ref[0].astype(BF16)


def _to_bf16(w, idx):
    _, rows, cols = w.shape
    tr = rows // 4
    return pl.pallas_call(
        _cast_kernel,
        grid=(rows // tr,),
        in_specs=[pl.BlockSpec((1, tr, cols), lambda r: (idx, r, 0))],
        out_specs=pl.BlockSpec((tr, cols), lambda r: (r, 0)),
        out_shape=jax.ShapeDtypeStruct((rows, cols), BF16),
        compiler_params=_cparams("arbitrary"),
        name="weight_to_bf16",
    )(w)


def kernel(x, e_norm, e_w_in, e_conv_w, e_conv_b, e_gate_a_w, e_gate_a_b, e_gate_x_w, e_gate_x_b, e_lambda, e_q_norm, e_k_norm, e_w_out, o_norm, o_w_in, o_lambda_re, o_lambda_im, o_b_re, o_b_im, o_c_re, o_c_im, o_d, o_log_step, o_glu_w, o_glu_b, o_gk_w, o_gk_b, o_gla_norm, o_w_out, f_norm, f_w_in, f_conv_w, f_conv_b, f_w_out):
    B, S, D = x.shape
    T = B * S
    depth = f_norm.shape[0]
    row = lambda t: t.reshape(1, -1).astype(F32)
    xt = x.reshape(T, D)

    cos_t, s1_t, s2_t = _rope_tables(S)
    head_seg = jnp.asarray(np.kron(np.eye(LANES // ATT_HEAD_DIM), np.ones((ATT_HEAD_DIM, ATT_HEAD_DIM))), BF16)
    att_bias = jnp.asarray(_attention_bias())
    tri_s5 = jnp.asarray(np.tril(np.ones((L_S5, L_S5))), BF16)
    tri_gla = jnp.asarray(_gla_cumsum_matrix(TL_GLA), BF16)
    two_heads = lambda t: jnp.tile(row(t), (1, LANES // ATT_HEAD_DIM))

    for layer in range(depth):
        i = layer // 2
        if layer % 2 == 0:
            xg, q, k, v = _even_in(xt, row(e_norm[i]), _to_bf16(e_w_in, i),
                                   two_heads(e_q_norm[i]), two_heads(e_k_norm[i]),
                                   head_seg, cos_t, s1_t, s2_t, S)
            ya = _lru(xg, e_conv_w[i], row(e_conv_b[i]),
                      _block_diag(e_gate_a_w[i]).astype(BF16), row(e_gate_a_b[i]),
                      _block_diag(e_gate_x_w[i]).astype(BF16), row(e_gate_x_b[i]),
                      row(e_lambda[i]), B, S)
            yb = _attention(q.reshape(B, S, ATT_WIDTH), k.reshape(B, S, ATT_WIDTH),
                            v.reshape(B, S, ATT_WIDTH), att_bias).reshape(T, ATT_WIDTH)
            w_out = _to_bf16(e_w_out, i)
        else:
            w_in = jnp.pad(o_w_in[i], ((0, 0), (0, GK_PAD - GLA_LOWRANK))).astype(BF16)
            proj = _odd_in(xt, row(o_norm[i]), w_in)
            bm, p1, p2, q1, q2, pc, cm = _s5_params(o_lambda_re[i], o_lambda_im[i], o_b_re[i], o_b_im[i],
                                                    o_c_re[i], o_c_im[i], o_log_step[i])
            ya = _s5(proj, bm, p1, p2, q1, q2, pc, tri_s5, cm, row(o_d[i]),
                     _to_bf16(o_glu_w, i), row(o_glu_b[i]), B, S)
            gk_w = jnp.pad(o_gk_w[i], ((0, GK_PAD - GLA_LOWRANK), (0, 0))).astype(BF16)
            yb = _gla(proj, gk_w, row(o_gk_b[i]), row(o_gla_norm[i]), tri_gla, B, S)
            w_out = _to_bf16(o_w_out, i)
        xt = _ffn(xt, ya, yb, w_out, row(f_norm[layer]), _to_bf16(f_w_in, layer),
                  f_conv_w[layer], row(f_conv_b[layer]), _to_bf16(f_w_out, layer), S)
    return xt.reshape(B, S, D)
```

```python
import functools

import numpy as np
import jax
import jax.numpy as jnp
from jax import lax
from jax.experimental import pallas as pl
from jax.experimental.pallas import tpu as pltpu

F32 = jnp.float32
BF16 = jnp.bfloat16

D_MODEL = 1024
LRU_WIDTH = 512
LRU_BLOCKS = 8
LRU_CONV = 4
LRU_C = 8.0
ATT_HEADS = 8
ATT_HEAD_DIM = 64
ATT_WIDTH = 512
DILATED_PATTERNS = ((128, 1), (512, 4), (2048, 16))
ATT_SPAN = 2048
ROPE_THETA = 500000.0
ROPE_DIM = 16
S5_WIDTH = 512
S5_GROUP = 16
S5_GROUPS = 32
S5_STATE = 64
S5_NSTATE = S5_GROUPS * S5_STATE
GLA_HEADS = 4
GLA_DK = 64
GLA_DV = 128
GLA_LOWRANK = 16
GLA_TAU = 16.0
D_FF = 3 * D_MODEL
EPS = 1e-6
NEG_INF = -1e30

LANES = 128
SUBLANES = 8
VMEM_PHYSICAL = 64 * 1024 * 1024
VMEM_LIMIT = VMEM_PHYSICAL - 4 * 1024 * 1024

TM_PROJ = 1024
TM_FFN = 1024
TF_FFN = 512
TL_LRU = 1024
TL_S5 = 1024
L_S5 = 128
TL_GLA = 256
C_GLA = 64
GK_PAD = 128


def _cparams(*sem):
    return pltpu.CompilerParams(dimension_semantics=sem, vmem_limit_bytes=VMEM_LIMIT)


def _rms(x, g):
    return x * lax.rsqrt(jnp.mean(x * x, axis=-1, keepdims=True) + EPS) * g


def _log_sigmoid(x):
    return -(jnp.maximum(-x, 0.0) + jnp.log(1.0 + jnp.exp(-jnp.abs(x))))


def _split_bf16(x):
    hi = x.astype(BF16)
    lo = (x - hi.astype(F32)).astype(BF16)
    return hi, lo


def _dot(a, b):
    return jnp.dot(a, b, preferred_element_type=F32)


def _dot_nt(a, b):
    return lax.dot_general(a, b, (((1,), (1,)), ((), ())), preferred_element_type=F32)


def _dot_tn(a, b):
    return lax.dot_general(a, b, (((0,), (0,)), ((), ())), preferred_element_type=F32)


def _even_in_kernel(x_ref, g_ref, w_ref, qn_ref, kn_ref, seg_ref, cos_ref, s1_ref, s2_ref,
                    xg_ref, q_ref, k_ref, v_ref):
    h = _rms(x_ref[...], g_ref[...]).astype(BF16)
    seg = seg_ref[...]
    cos, s1, s2 = cos_ref[...], s1_ref[...], s2_ref[...]
    half = ROPE_DIM // 2
    for off, n_ref, dst, scale in ((2 * LRU_WIDTH, qn_ref, q_ref, ATT_HEAD_DIM ** -0.5),
                                   (2 * LRU_WIDTH + ATT_WIDTH, kn_ref, k_ref, 1.0)):
        y = _dot(h, w_ref[:, off:off + ATT_WIDTH])
        for c in range(ATT_WIDTH // LANES):
            t = y[:, c * LANES:(c + 1) * LANES]
            ms = _dot((t * t).astype(BF16), seg) * (1.0 / ATT_HEAD_DIM)
            tn = t * lax.rsqrt(ms + EPS) * n_ref[...]
            r = (tn * cos + pltpu.roll(tn, LANES - half, 1) * s1 + pltpu.roll(tn, half, 1) * s2)
            dst[:, c * LANES:(c + 1) * LANES] = r * scale
    xg_ref[...] = _dot(h, w_ref[:, :2 * LRU_WIDTH])
    v_ref[...] = _dot(h, w_ref[:, 2 * LRU_WIDTH + 2 * ATT_WIDTH:])


def _even_in(x, g, w, qn, kn, seg, cos_t, s1_t, s2_t, seq):
    T = x.shape[0]
    tm = TM_PROJ
    n_seq = seq // tm
    ncol = w.shape[1]
    full = lambda i: (0, 0)
    tab = lambda i: (i % n_seq, 0)
    row = lambda i: (i, 0)
    return pl.pallas_call(
        _even_in_kernel,
        grid=(T // tm,),
        in_specs=[pl.BlockSpec((tm, D_MODEL), row),
                  pl.BlockSpec((1, D_MODEL), full),
                  pl.BlockSpec((D_MODEL, ncol), full),
                  pl.BlockSpec((1, LANES), full),
                  pl.BlockSpec((1, LANES), full),
                  pl.BlockSpec((LANES, LANES), full),
                  pl.BlockSpec((tm, LANES), tab),
                  pl.BlockSpec((tm, LANES), tab),
                  pl.BlockSpec((tm, LANES), tab)],
        out_specs=[pl.BlockSpec((tm, 2 * LRU_WIDTH), row),
                   pl.BlockSpec((tm, ATT_WIDTH), row),
                   pl.BlockSpec((tm, ATT_WIDTH), row),
                   pl.BlockSpec((tm, ATT_WIDTH), row)],
        out_shape=[jax.ShapeDtypeStruct((T, 2 * LRU_WIDTH), F32),
                   jax.ShapeDtypeStruct((T, ATT_WIDTH), F32),
                   jax.ShapeDtypeStruct((T, ATT_WIDTH), F32),
                   jax.ShapeDtypeStruct((T, ATT_WIDTH), F32)],
        compiler_params=_cparams("arbitrary"),
        name="even_in_proj",
    )(x, g, w, qn, kn, seg, cos_t, s1_t, s2_t)


def _lru_kernel(xl_ref, gl_ref, cw_ref, cb_ref, wa_ref, ba_ref, wx_ref, bx_ref, lam_ref,
                o_ref, xbuf, hprev):
    tl = xl_ref.shape[0]

    @pl.when(pl.program_id(1) == 0)
    def _():
        xbuf[...] = jnp.zeros_like(xbuf)
        hprev[...] = jnp.zeros_like(hprev)

    groups = tl // SUBLANES
    sub = lax.broadcasted_iota(jnp.int32, (1, SUBLANES, 1), 1)
    x = xl_ref[...]
    x3 = x.reshape(groups, SUBLANES, LRU_WIDTH)
    prev = xbuf[...].reshape(1, SUBLANES, LRU_WIDTH)
    xbuf[...] = x[tl - SUBLANES:tl, :]
    conv = cb_ref[...] + cw_ref[LRU_CONV - 1:LRU_CONV, :] * x3
    for k in range(1, LRU_CONV):
        r = pltpu.roll(x3, k, 1)
        rp = jnp.concatenate([pltpu.roll(prev, k, 1), r[:groups - 1]], axis=0)
        conv = conv + cw_ref[LRU_CONV - 1 - k:LRU_CONV - k, :] * jnp.where(sub >= k, r, rp)
    conv = conv.reshape(tl, LRU_WIDTH)

    c16 = conv.astype(BF16)
    r = jax.nn.sigmoid(_dot(c16, wa_ref[...]) + ba_ref[...])
    ig = jax.nn.sigmoid(_dot(c16, wx_ref[...]) + bx_ref[...])
    log_a = (LRU_C * r) * _log_sigmoid(lam_ref[...])
    a = jnp.exp(log_a)
    b = jnp.sqrt(1.0 - a * a) * (ig * conv)

    a = a.reshape(groups, SUBLANES, LRU_WIDTH)
    b = b.reshape(groups, SUBLANES, LRU_WIDTH)
    d = 1
    while d < SUBLANES:
        keep = sub >= d
        a_sh = jnp.where(keep, pltpu.roll(a, d, 1), 1.0)
        b_sh = jnp.where(keep, pltpu.roll(b, d, 1), 0.0)
        b = a * b_sh + b
        a = a * a_sh
        d *= 2
    last = hprev[0:1, :]
    hs = []
    for t in range(groups):
        ht = b[t] + a[t] * last
        hs.append(ht)
        last = ht[SUBLANES - 1:SUBLANES, :]
    hprev[0:1, :] = last
    h = jnp.concatenate(hs, axis=0)
    o_ref[...] = (h * jax.nn.gelu(gl_ref[...])).astype(BF16)


def _lru(xg, cw, cb, wa, ba, wx, bx, lam, batch, seq):
    T = xg.shape[0]
    tl = TL_LRU
    n_seq = seq // tl
    full = lambda b, i: (0, 0)
    return pl.pallas_call(
        _lru_kernel,
        grid=(batch, n_seq),
        in_specs=[pl.BlockSpec((tl, LRU_WIDTH), lambda b, i: (b * n_seq + i, 0)),
                  pl.BlockSpec((tl, LRU_WIDTH), lambda b, i: (b * n_seq + i, 1)),
                  pl.BlockSpec((LRU_CONV, LRU_WIDTH), full),
                  pl.BlockSpec((1, LRU_WIDTH), full),
                  pl.BlockSpec((LRU_WIDTH, LRU_WIDTH), full),
                  pl.BlockSpec((1, LRU_WIDTH), full),
                  pl.BlockSpec((LRU_WIDTH, LRU_WIDTH), full),
                  pl.BlockSpec((1, LRU_WIDTH), full),
                  pl.BlockSpec((1, LRU_WIDTH), full)],
        out_specs=pl.BlockSpec((tl, LRU_WIDTH), lambda b, i: (b * n_seq + i, 0)),
        out_shape=jax.ShapeDtypeStruct((T, LRU_WIDTH), BF16),
        scratch_shapes=[pltpu.VMEM((SUBLANES, LRU_WIDTH), F32),
                        pltpu.VMEM((SUBLANES, LRU_WIDTH), F32)],
        compiler_params=_cparams("arbitrary", "arbitrary"),
        name="rg_lru",
    )(xg, xg, cw, cb, wa, ba, wx, bx, lam)


N_BACK = DILATED_PATTERNS[0][0] // DILATED_PATTERNS[0][1]
Q_BLOCKS = ATT_SPAN // N_BACK
STAGE_DIL = DILATED_PATTERNS[1][1]


def _attention_bias():
    qi = np.arange(N_BACK)[:, None]
    ki = np.arange(2 * N_BACK)[None, :]
    dist = N_BACK + qi - ki
    band = (dist >= 0) & (dist <= N_BACK)
    first = band & (ki >= N_BACK)
    return np.where(np.stack([band, first]), 0.0, NEG_INF).astype(np.float32)


def _attn_kernel(q_ref, k_ref, v_ref, bias_ref, o_ref, *scratch):
    kv_s = scratch[:6]
    o_s, m_s, l_s = scratch[6:9]
    stages = scratch[9:]
    sb = pl.program_id(2)
    lane = lax.broadcasted_iota(jnp.int32, (1, LANES), 1)
    head0 = lane < ATT_HEAD_DIM

    for p, (window, dil) in enumerate(DILATED_PATTERNS):
        per_res = Q_BLOCKS // dil
        span = N_BACK * per_res
        for src, dst, stage in ((k_ref, kv_s[2 * p], stages[0]), (v_ref, kv_s[2 * p + 1], stages[1])):
            @pl.when(sb == 0)
            def _():
                dst[:, 0:N_BACK, :] = jnp.zeros((dil, N_BACK, LANES), BF16)

            @pl.when(sb > 0)
            def _():
                dst[:, 0:N_BACK, :] = dst[:, span:span + N_BACK, :]

            if dil <= STAGE_DIL:
                for r in range(dil):
                    rows = pl.ds(r, span, stride=dil) if dil > 1 else pl.ds(0, span)
                    part = src[0, rows, :]
                    dst[r, N_BACK:N_BACK + span, :] = part.astype(BF16)
                    if dil == STAGE_DIL:
                        stage[r] = part
            else:
                sub = dil // STAGE_DIL
                for r in range(dil):
                    rows = pl.ds(r // STAGE_DIL, span, stride=sub)
                    dst[r, N_BACK:N_BACK + span, :] = stage[r % STAGE_DIL, rows, :].astype(BF16)

    q_stage = stages[2]
    out_stages = (stages[0], stages[1], stages[3])
    for r in range(STAGE_DIL):
        q_stage[r] = q_ref[0, pl.ds(r, ATT_SPAN // STAGE_DIL, stride=STAGE_DIL), :]

    for p, (window, dil) in enumerate(DILATED_PATTERNS):
        per_res = Q_BLOCKS // dil
        k_s, v_s = kv_s[2 * p], kv_s[2 * p + 1]

        for n in range(Q_BLOCKS):
            r, m = n % dil, n // dil
            if dil > 1:
                rows = pl.ds(m * (N_BACK * dil) + r, N_BACK, stride=dil)
            else:
                rows = pl.ds(m * N_BACK, N_BACK)
            if dil < STAGE_DIL:
                q = q_ref[0, rows, :]
            else:
                sub = dil // STAGE_DIL
                staged = (pl.ds(sub * N_BACK * m + r // STAGE_DIL, N_BACK, stride=sub) if sub > 1
                          else pl.ds(N_BACK * m, N_BACK))
                q = q_stage[r % STAGE_DIL, staged, :]
            q = q.astype(BF16)
            kc = k_s[r, m * N_BACK:(m + 2) * N_BACK, :]
            vc = v_s[r, m * N_BACK:(m + 2) * N_BACK, :]
            bias = bias_ref[jnp.where(sb == 0, 1, 0)] if m == 0 else bias_ref[0]
            res = []
            for h in range(LANES // ATT_HEAD_DIM):
                qm = jnp.where(head0 if h == 0 else ~head0, q, jnp.zeros_like(q))
                s = _dot_nt(qm, kc) + bias
                mx = jnp.max(s, axis=-1, keepdims=True)
                e = jnp.exp(s - mx)
                res.append((_dot(e.astype(BF16), vc), mx, jnp.sum(e, axis=-1, keepdims=True)))
            vals = [jnp.where(head0, res[0][idx], res[1][idx]) for idx in range(3)]
            if dil > STAGE_DIL:
                for st, val in zip(out_stages, vals):
                    st[r % STAGE_DIL, staged, :] = val
            else:
                for dst, val in zip((o_s, m_s, l_s), vals):
                    dst[p, rows, :] = val
        if dil > STAGE_DIL:
            for dst, st in zip((o_s, m_s, l_s), out_stages):
                for r in range(STAGE_DIL):
                    dst[p, pl.ds(r, ATT_SPAN // STAGE_DIL, stride=STAGE_DIL), :] = st[r]

    mx = jnp.maximum(jnp.maximum(m_s[0], m_s[1]), m_s[2])
    num = jnp.zeros_like(mx)
    den = jnp.zeros_like(mx)
    for p in range(len(DILATED_PATTERNS)):
        w = jnp.exp(m_s[p] - mx)
        num = num + w * o_s[p]
        den = den + w * l_s[p]
    o_ref[0] = (num / den).astype(BF16)


def _attention(q, k, v, bias):
    B, S, W = q.shape
    blk = pl.BlockSpec((1, ATT_SPAN, LANES), lambda b, p, i: (b, i, p))
    kv_scratch = []
    for window, dil in DILATED_PATTERNS:
        shape = (dil, N_BACK * (1 + Q_BLOCKS // dil), LANES)
        kv_scratch += [pltpu.VMEM(shape, BF16), pltpu.VMEM(shape, BF16)]
    acc = pltpu.VMEM((len(DILATED_PATTERNS), ATT_SPAN, LANES), F32)
    stage = pltpu.VMEM((STAGE_DIL, ATT_SPAN // STAGE_DIL, LANES), F32)
    return pl.pallas_call(
        _attn_kernel,
        grid=(B, W // LANES, S // ATT_SPAN),
        in_specs=[blk, blk, blk, pl.BlockSpec(bias.shape, lambda b, p, i: (0, 0, 0))],
        out_specs=blk,
        out_shape=jax.ShapeDtypeStruct((B, S, W), BF16),
        scratch_shapes=kv_scratch + [acc, acc, acc, stage, stage, stage, stage],
        compiler_params=_cparams("arbitrary", "arbitrary", "arbitrary"),
        name="dilated_attention",
    )(q, k, v, bias)


def _ffn_kernel(x_ref, ya_ref, yb_ref, wo_ref, g_ref, w1_ref, cw_ref, cb_ref, w2_ref,
                o_ref, act_s, carry_s, *, tiles_per_seq):
    i = pl.program_id(0)
    tm = x_ref.shape[0]
    half = ya_ref.shape[1]
    tf = TF_FFN

    @pl.when(i % tiles_per_seq == 0)
    def _():
        carry_s[...] = jnp.zeros_like(carry_s)

    x1 = (x_ref[...] + _dot(ya_ref[...], wo_ref[0:half, :])
          + _dot(yb_ref[...], wo_ref[half:2 * half, :]))
    h = _rms(x1, g_ref[...]).astype(BF16)
    groups = tm // SUBLANES
    sub = lax.broadcasted_iota(jnp.int32, (1, SUBLANES, 1), 1)
    for c in range(D_FF // tf):
        cols = slice(c * tf, (c + 1) * tf)
        a = _dot(h, w1_ref[:, cols])
        lin = _dot(h, w1_ref[:, D_FF + c * tf:D_FF + (c + 1) * tf])
        a3 = a.reshape(groups, SUBLANES, tf)
        prev = carry_s[:, cols].reshape(1, SUBLANES, tf)
        carry_s[:, cols] = a[tm - SUBLANES:tm, :]

        def delayed(k):
            r = pltpu.roll(a3, k, 1)
            rp = jnp.concatenate([pltpu.roll(prev, k, 1), r[:groups - 1]], axis=0)
            return jnp.where(sub >= k, r, rp)

        conv = (cb_ref[:, cols] + cw_ref[2:3, cols] * a3
                + cw_ref[1:2, cols] * delayed(1) + cw_ref[0:1, cols] * delayed(2))
        act = jax.nn.gelu(conv).reshape(tm, tf) * lin
        act_s[:, cols] = act.astype(BF16)
    o_ref[...] = x1 + _dot(act_s[...], w2_ref[...])


def _ffn(x, ya, yb, wo, g, w_in, cw, cb, w2, seq):
    T = x.shape[0]
    tm = TM_FFN
    half = ya.shape[1]
    row = lambda i: (i, 0)
    resident = lambda shape: pl.BlockSpec(shape, lambda i: (0, 0), pipeline_mode=pl.Buffered(1))
    return pl.pallas_call(
        functools.partial(_ffn_kernel, tiles_per_seq=seq // tm),
        grid=(T // tm,),
        in_specs=[pl.BlockSpec((tm, D_MODEL), row),
                  pl.BlockSpec((tm, half), row),
                  pl.BlockSpec((tm, half), row),
                  resident((2 * half, D_MODEL)),
                  resident((1, D_MODEL)),
                  resident((D_MODEL, 2 * D_FF)),
                  resident((3, D_FF)),
                  resident((1, D_FF)),
                  resident((D_FF, D_MODEL))],
        out_specs=pl.BlockSpec((tm, D_MODEL), row),
        out_shape=jax.ShapeDtypeStruct((T, D_MODEL), F32),
        scratch_shapes=[pltpu.VMEM((tm, D_FF), BF16),
                        pltpu.VMEM((SUBLANES, D_FF), F32)],
        compiler_params=_cparams("arbitrary"),
        name="outproj_conv_mlp",
    )(x, ya, yb, wo, g, w_in, cw, cb, w2)


def _odd_in_kernel(x_ref, g_ref, w_ref, o_ref):
    h = _rms(x_ref[...], g_ref[...])
    o_ref[...] = _dot(h.astype(BF16), w_ref[...])


def _odd_in(x, g, w):
    T = x.shape[0]
    tm = TM_PROJ
    ncol = w.shape[1]
    return pl.pallas_call(
        _odd_in_kernel,
        grid=(T // tm,),
        in_specs=[pl.BlockSpec((tm, D_MODEL), lambda i: (i, 0)),
                  pl.BlockSpec((1, D_MODEL), lambda i: (0, 0)),
                  pl.BlockSpec((D_MODEL, ncol), lambda i: (0, 0))],
        out_specs=pl.BlockSpec((tm, ncol), lambda i: (i, 0)),
        out_shape=jax.ShapeDtypeStruct((T, ncol), F32),
        compiler_params=_cparams("arbitrary"),
        name="odd_in_proj",
    )(x, g, w)


def _cswap(v):
    return jnp.concatenate([v[:, LANES:], v[:, :LANES]], axis=1)


def _s5_kernel(u_ref, bm_ref, p1_ref, p2_ref, q1_ref, q2_ref, pc_ref, tri_ref, cm_ref,
               d_ref, gw_ref, gb_ref, o_ref, carry, x_s, ys_s):
    tl = u_ref.shape[0]
    L = L_S5
    blk = 2 * LANES
    slab = 2 * S5_NSTATE // (S5_WIDTH // LANES)

    @pl.when(pl.program_id(1) == 0)
    def _():
        carry[...] = jnp.zeros_like(carry)

    tri = tri_ref[...]
    for s in range(S5_WIDTH // LANES):
        u = u_ref[:, s * LANES:(s + 1) * LANES]
        u16 = u.astype(BF16)
        for jb in range(slab // blk):
            cols = slice(s * slab + jb * blk, s * slab + (jb + 1) * blk)
            bu = _dot(u16, bm_ref[s, :, jb * blk:(jb + 1) * blk])
            for c in range(tl // L):
                rows = slice(c * L, (c + 1) * L)
                v = bu[rows, :].astype(BF16)
                z = v * q1_ref[:, cols] + _cswap(v) * q2_ref[:, cols]
                w = _dot(tri, z) + carry[0:1, cols]
                w16 = w.astype(BF16)
                x_s[rows, jb * blk:(jb + 1) * blk] = (
                    w16 * p1_ref[:, cols] + _cswap(w16) * p2_ref[:, cols])
                wl = w[L - 1:L, :]
                xl = wl * pc_ref[2:3, cols] + _cswap(wl) * pc_ref[3:4, cols]
                carry[0:1, cols] = xl * pc_ref[0:1, cols] + _cswap(xl) * pc_ref[1:2, cols]
        y = _dot(x_s[...], cm_ref[s]) + d_ref[:, s * LANES:(s + 1) * LANES] * u
        ys_s[:, s * LANES:(s + 1) * LANES] = jax.nn.gelu(y)
    ys = ys_s[...]
    o_ref[...] = (ys * jax.nn.sigmoid(_dot(ys.astype(BF16), gw_ref[...]) + gb_ref[...])).astype(BF16)


def _s5(proj, bm, p1, p2, q1, q2, pc, tri, cm, d, gw, gb, batch, seq):
    T = proj.shape[0]
    tl = TL_S5
    n_seq = seq // tl
    c2 = lambda b, i: (0, 0)
    c3 = lambda b, i: (0, 0, 0)
    return pl.pallas_call(
        _s5_kernel,
        grid=(batch, n_seq),
        in_specs=[pl.BlockSpec((tl, S5_WIDTH), lambda b, i: (b * n_seq + i, 0)),
                  pl.BlockSpec(bm.shape, c3),
                  pl.BlockSpec(p1.shape, c2), pl.BlockSpec(p2.shape, c2),
                  pl.BlockSpec(q1.shape, c2), pl.BlockSpec(q2.shape, c2),
                  pl.BlockSpec(pc.shape, c2),
                  pl.BlockSpec(tri.shape, c2),
                  pl.BlockSpec(cm.shape, c3),
                  pl.BlockSpec((1, S5_WIDTH), c2),
                  pl.BlockSpec((S5_WIDTH, S5_WIDTH), c2),
                  pl.BlockSpec((1, S5_WIDTH), c2)],
        out_specs=pl.BlockSpec((tl, S5_WIDTH), lambda b, i: (b * n_seq + i, 0)),
        out_shape=jax.ShapeDtypeStruct((T, S5_WIDTH), BF16),
        scratch_shapes=[pltpu.VMEM((SUBLANES, 2 * S5_NSTATE), F32),
                        pltpu.VMEM((tl, 2 * S5_NSTATE // (S5_WIDTH // LANES)), BF16),
                        pltpu.VMEM((tl, S5_WIDTH), F32)],
        compiler_params=_cparams("arbitrary", "arbitrary"),
        name="s5_glu",
    )(proj, bm, p1, p2, q1, q2, pc, tri, cm, d, gw, gb)


def _s5_params(lam_re, lam_im, b_re, b_im, c_re, c_im, log_step):
    G, N, P = S5_GROUPS, S5_STATE, S5_GROUP
    gs = LANES // P
    ns = S5_WIDTH // LANES
    step = jnp.exp(log_step.astype(F32))[:, None]
    lr, li = lam_re.astype(F32), lam_im.astype(F32)
    mag = jnp.exp(lr * step)
    ar, ai = mag * jnp.cos(li * step), mag * jnp.sin(li * step)
    den = lr * lr + li * li
    cr = ((ar - 1.0) * lr + ai * li) / den
    ci = (ai * lr - (ar - 1.0) * li) / den
    bbr = cr[..., None] * b_re - ci[..., None] * b_im
    bbi = cr[..., None] * b_im + ci[..., None] * b_re
    eye = jnp.eye(gs, dtype=F32)

    def in_blockdiag(t):
        t = t.reshape(ns, gs, N, P).transpose(0, 1, 3, 2)
        return jnp.einsum('ab,sapn->sapbn', eye, t).reshape(ns, gs * P, gs * N)

    def out_blockdiag(t):
        t = t.reshape(ns, gs, P, N).transpose(0, 1, 3, 2)
        return jnp.einsum('ab,sanp->sanbp', eye, t).reshape(ns, gs * N, gs * P)

    def interleave(re, im, axis):
        shp = list(re.shape)
        blocked = shp[:axis] + [shp[axis] // LANES, LANES] + shp[axis + 1:]
        both = jnp.stack([re.reshape(blocked), im.reshape(blocked)], axis=axis + 1)
        return both.reshape(shp[:axis] + [2 * shp[axis]] + shp[axis + 1:])

    bm = interleave(in_blockdiag(bbr), in_blockdiag(bbi), 2).astype(BF16)
    cm = interleave(out_blockdiag(c_re), out_blockdiag(-c_im), 1).astype(BF16)
    def twice(v):
        v = v.reshape(G * N // LANES, 1, LANES)
        return jnp.broadcast_to(v, (G * N // LANES, 2, LANES)).reshape(1, 2 * G * N)

    sign = jnp.asarray(np.tile(np.repeat([-1.0, 1.0], LANES), G * N // LANES)[None, :], F32)
    j = jnp.arange(L_S5, dtype=F32)[:, None]
    la, th = twice(lr * step), twice(li * step)
    pmag, qmag = jnp.exp(j * la), jnp.exp(-(j * la))
    cs, sn = jnp.cos(j * th), jnp.sin(j * th)
    p1, p2 = pmag * cs, sign * (pmag * sn)
    q1, q2 = qmag * cs, -(sign * (qmag * sn))
    pc = jnp.concatenate([p1[1:2], p2[1:2], p1[L_S5 - 1:], p2[L_S5 - 1:]], axis=0)
    return bm, p1.astype(BF16), p2.astype(BF16), q1.astype(BF16), q2.astype(BF16), pc, cm


def _gla_cumsum_matrix(tl):
    r = np.arange(tl)[:, None]
    c = np.arange(tl)[None, :]
    same = (r // C_GLA) == (c // C_GLA)
    return np.concatenate([same & (c <= r), same & (c > r)], axis=0).astype(np.float32)


def _gla_kernel(q_ref, k_ref, v_ref, g_ref, gk_ref, gw_ref, gb_ref, nrm_ref, cum_ref, o_ref, st):
    nbatch, tl = q_ref.shape[0], q_ref.shape[1]
    C = C_GLA
    blk = 2 * C
    pair = LANES // GLA_DK

    @pl.when(pl.program_id(0) == 0)
    def _():
        st[...] = jnp.zeros_like(st)

    lane = lax.broadcasted_iota(jnp.int32, (1, LANES), 1)
    rb = lax.broadcasted_iota(jnp.int32, (blk, blk), 0)
    cb = lax.broadcasted_iota(jnp.int32, (blk, blk), 1)
    causal = (rb >= cb) & ((rb < C) | (cb >= C))
    streams = []
    for n in range(nbatch):
        z = _dot(gk_ref[n].astype(BF16), gw_ref[...]) + gb_ref[...]
        log_a = _log_sigmoid(z) * (1.0 / GLA_TAU)
        hi, lo = _split_bf16(log_a)
        sums = _dot(cum_ref[...], hi) + _dot(cum_ref[...], lo)
        bc, suffix = sums[:tl], sums[tl:]
        eb = jnp.exp(bc)
        q_dec = q_ref[n] * (GLA_DK ** -0.5) * eb
        k = k_ref[n]
        k_inv = (k * jnp.exp(-bc)).astype(BF16)
        k_dec = k * jnp.exp(suffix)
        for h in range(GLA_HEADS):
            hp, hh = divmod(h, pair)
            cols = slice(hp * LANES, (hp + 1) * LANES)
            in_head = (lane >= hh * GLA_DK) & (lane < (hh + 1) * GLA_DK)
            streams.append(dict(
                n=n, h=h, eb=eb[:, cols],
                qd=jnp.where(in_head, q_dec[:, cols], 0.0).astype(BF16),
                kd=jnp.where(in_head, k_dec[:, cols], 0.0).astype(BF16),
                ki=k_inv[:, cols],
                vh=v_ref[n, :, h * GLA_DV:(h + 1) * GLA_DV].astype(BF16),
                state=st[n * GLA_HEADS + h]))
    for b in range(tl // blk):
        rows = slice(b * blk, (b + 1) * blk)
        for sd in streams:
            n, h = sd["n"], sd["h"]
            att = jnp.where(causal, _dot_nt(sd["qd"][rows], sd["ki"][rows]), 0.0)
            o = _dot(att.astype(BF16), sd["vh"][rows])
            inter = []
            for c in range(b * blk // C, (b + 1) * blk // C):
                crow = slice(c * C, (c + 1) * C)
                inter.append(_dot_nt(sd["qd"][crow], sd["state"].astype(BF16)))
                decay = sd["eb"][(c + 1) * C - 1:(c + 1) * C, :]
                sd["state"] = sd["state"] * decay + _dot_tn(sd["vh"][crow], sd["kd"][crow])
            o = _rms(o + jnp.concatenate(inter, axis=0), nrm_ref[...])
            gh = g_ref[n, rows, h * GLA_DV:(h + 1) * GLA_DV]
            o_ref[n, rows, h * GLA_DV:(h + 1) * GLA_DV] = (o * jax.nn.silu(gh)).astype(BF16)
    for sd in streams:
        st[sd["n"] * GLA_HEADS + sd["h"]] = sd["state"]


def _gla(proj, gw, gb, nrm, tri, batch, seq):
    tl = TL_GLA
    hk = GLA_HEADS * GLA_DK
    hv = GLA_HEADS * GLA_DV
    c2 = lambda i: (0, 0)
    q0 = S5_WIDTH // hk
    v0 = (S5_WIDTH + 2 * hk) // hv
    gk0 = (S5_WIDTH + 2 * hk + 2 * hv) // GK_PAD
    proj = proj.reshape(batch, seq, proj.shape[-1])
    out = pl.pallas_call(
        _gla_kernel,
        grid=(seq // tl,),
        in_specs=[pl.BlockSpec((batch, tl, hk), lambda i: (0, i, q0)),
                  pl.BlockSpec((batch, tl, hk), lambda i: (0, i, q0 + 1)),
                  pl.BlockSpec((batch, tl, hv), lambda i: (0, i, v0)),
                  pl.BlockSpec((batch, tl, hv), lambda i: (0, i, v0 + 1)),
                  pl.BlockSpec((batch, tl, GK_PAD), lambda i: (0, i, gk0)),
                  pl.BlockSpec((GK_PAD, hk), c2),
                  pl.BlockSpec((1, hk), c2),
                  pl.BlockSpec((1, GLA_DV), c2),
                  pl.BlockSpec((2 * tl, tl), c2)],
        out_specs=pl.BlockSpec((batch, tl, hv), lambda i: (0, i, 0)),
        out_shape=jax.ShapeDtypeStruct((batch, seq, hv), BF16),
        scratch_shapes=[pltpu.VMEM((batch * GLA_HEADS, GLA_DV, LANES), F32)],
        compiler_params=_cparams("arbitrary"),
        name="gla",
    )(proj, proj, proj, proj, proj, gw, gb, nrm, tri)
    return out.reshape(batch * seq, hv)


def _block_diag(w):
    nb, a, b = w.shape
    return jnp.einsum('hk,hij->hikj', jnp.eye(nb, dtype=w.dtype), w).reshape(nb * a, nb * b)


def _rope_tables(seq):
    half = ROPE_DIM // 2
    pos = np.arange(seq, dtype=np.float64)
    inv = ROPE_THETA ** (-np.arange(0, ROPE_DIM, 2, dtype=np.float64) / ROPE_DIM)
    ang = pos[:, None] * inv[None, :]
    cos, sin = np.cos(ang), np.sin(ang)
    rest = ATT_HEAD_DIM - ROPE_DIM
    ones = np.ones((seq, rest))
    zeros = np.zeros((seq, rest))
    zh = np.zeros((seq, half))
    per_head = lambda parts: jnp.asarray(
        np.tile(np.concatenate(parts, axis=1), (1, LANES // ATT_HEAD_DIM)), F32)
    return (per_head([cos, cos, ones]), per_head([-sin, zh, zeros]), per_head([zh, sin, zeros]))


def _cast_kernel(w_ref, o_ref):
    o_ref[...] = w_ref[0].astype(BF16)


def _to_bf16(w, idx):
    _, rows, cols = w.shape
    tr = rows // 4
    return pl.pallas_call(
        _cast_kernel,
        grid=(rows // tr,),
        in_specs=[pl.BlockSpec((1, tr, cols), lambda r: (idx, r, 0))],
        out_specs=pl.BlockSpec((tr, cols), lambda r: (r, 0)),
        out_shape=jax.ShapeDtypeStruct((rows, cols), BF16),
        compiler_params=_cparams("arbitrary"),
        name="weight_to_bf16",
    )(w)


def kernel(x, e_norm, e_w_in, e_conv_w, e_conv_b, e_gate_a_w, e_gate_a_b, e_gate_x_w, e_gate_x_b, e_lambda, e_q_norm, e_k_norm, e_w_out, o_norm, o_w_in, o_lambda_re, o_lambda_im, o_b_re, o_b_im, o_c_re, o_c_im, o_d, o_log_step, o_glu_w, o_glu_b, o_gk_w, o_gk_b, o_gla_norm, o_w_out, f_norm, f_w_in, f_conv_w, f_conv_b, f_w_out):
    B, S, D = x.shape
    T = B * S
    depth = f_norm.shape[0]
    row = lambda t: t.reshape(1, -1).astype(F32)
    xt = x.reshape(T, D)

    cos_t, s1_t, s2_t = _rope_tables(S)
    head_seg = jnp.asarray(np.kron(np.eye(LANES // ATT_HEAD_DIM), np.ones((ATT_HEAD_DIM, ATT_HEAD_DIM))), BF16)
    att_bias = jnp.asarray(_attention_bias())
    tri_s5 = jnp.asarray(np.tril(np.ones((L_S5, L_S5))), BF16)
    tri_gla = jnp.asarray(_gla_cumsum_matrix(TL_GLA), BF16)
    two_heads = lambda t: jnp.tile(row(t), (1, LANES // ATT_HEAD_DIM))

    for layer in range(depth):
        i = layer // 2
        if layer % 2 == 0:
            xg, q, k, v = _even_in(xt, row(e_norm[i]), _to_bf16(e_w_in, i),
                                   two_heads(e_q_norm[i]), two_heads(e_k_norm[i]),
                                   head_seg, cos_t, s1_t, s2_t, S)
            ya = _lru(xg, e_conv_w[i], row(e_conv_b[i]),
                      _block_diag(e_gate_a_w[i]).astype(BF16), row(e_gate_a_b[i]),
                      _block_diag(e_gate_x_w[i]).astype(BF16), row(e_gate_x_b[i]),
                      row(e_lambda[i]), B, S)
            yb = _attention(q.reshape(B, S, ATT_WIDTH), k.reshape(B, S, ATT_WIDTH),
                            v.reshape(B, S, ATT_WIDTH), att_bias).reshape(T, ATT_WIDTH)
            w_out = _to_bf16(e_w_out, i)
        else:
            w_in = jnp.pad(o_w_in[i], ((0, 0), (0, GK_PAD - GLA_LOWRANK))).astype(BF16)
            proj = _odd_in(xt, row(o_norm[i]), w_in)
            bm, p1, p2, q1, q2, pc, cm = _s5_params(o_lambda_re[i], o_lambda_im[i], o_b_re[i], o_b_im[i],
                                                    o_c_re[i], o_c_im[i], o_log_step[i])
            ya = _s5(proj, bm, p1, p2, q1, q2, pc, tri_s5, cm, row(o_d[i]),
                     _to_bf16(o_glu_w, i), row(o_glu_b[i]), B, S)
            gk_w = jnp.pad(o_gk_w[i], ((0, GK_PAD - GLA_LOWRANK), (0, 0))).astype(BF16)
            yb = _gla(proj, gk_w, row(o_gk_b[i]), row(o_gla_norm[i]), tri_gla, B, S)
            w_out = _to_bf16(o_w_out, i)
        xt = _ffn(xt, ya, yb, w_out, row(f_norm[layer]), _to_bf16(f_w_in, layer),
                  f_conv_w[layer], row(f_conv_b[layer]), _to_bf16(f_w_out, layer), S)
    return xt.reshape(B, S, D)
```
